```python
import math
import jax
import jax.numpy as jnp
from jax import lax
import numpy as np

D_MODEL = 2048
BATCH = 16
SEQ = 2048
DEPTH = 2

N_MIXERS = 2
N_ATTN_LAYERS = (DEPTH + 1) // 2
N_MLSTM_LAYERS = DEPTH // 2

DILATED_PATTERNS = ((128, 1), (512, 4), (2048, 16))
N_DIL_GROUPS = 3
ATTN_HEADS = 8
HEAD_DIM = 128
ROT_DIM = HEAD_DIM // 4
ROPE_THETA = 500000.0
ATTN_BLOCK = 128
ATTN_QKV_COLS = N_DIL_GROUPS * 3 * ATTN_HEADS * HEAD_DIM

MLSTM_HEADS = 8
QK_DIM = 128
V_DIM = D_MODEL // MLSTM_HEADS
CONV_K = 4
MLSTM_CHUNK = 64
MLSTM_QK_COLS = 2 * MLSTM_HEADS * QK_DIM
MLSTM_V_COLS = MLSTM_HEADS * V_DIM
MLSTM_IN_COLS = MLSTM_QK_COLS + 2 * MLSTM_V_COLS + 2 * MLSTM_HEADS

N_EXPERTS = 16
N_EXPERT_GROUPS = 4
EXPERTS_PER_GROUP = N_EXPERTS // N_EXPERT_GROUPS
TOP_K = 2
D_EXPERT = 768
MOE_BLOCK = 128

ALPHA = (2 * DEPTH) ** 0.25
BETA = (8 * DEPTH) ** -0.25
LN_EPS = 1e-5
NEG = -1e30

kernel_name = 'hybrid_dilated_attn_mlstm_grouped_moe'


def layer_norm(x, g, b):
    xf = x.astype(jnp.float32)
    mu = xf.mean(-1, keepdims=True)
    var = jnp.square(xf - mu).mean(-1, keepdims=True)
    return ((xf - mu) * lax.rsqrt(var + LN_EPS) * g + b).astype(x.dtype)


def rope_tables(seq):
    inv_freq = ROPE_THETA ** (-jnp.arange(0, ROT_DIM, 2, dtype=jnp.float32) / ROT_DIM)
    ang = jnp.arange(seq, dtype=jnp.float32)[:, None] * inv_freq[None, :]
    ang = jnp.concatenate([ang, ang], -1)
    return jnp.cos(ang)[:, None, :], jnp.sin(ang)[:, None, :]


def apply_partial_rope(t, cos, sin):
    tr, tp = t[..., :ROT_DIM], t[..., ROT_DIM:]
    x1, x2 = tr[..., :ROT_DIM // 2], tr[..., ROT_DIM // 2:]
    rot = jnp.concatenate([-x2, x1], -1)
    tr = (tr * cos + rot * sin).astype(t.dtype)
    return jnp.concatenate([tr, tp], -1)


def banded_causal_attention(q, k, v, n_back):
    n, h, l, hd = q.shape
    nb = -(-l // ATTN_BLOCK)
    pad = nb * ATTN_BLOCK - l
    padw = ((0, 0), (0, 0), (0, pad), (0, 0))
    qb, kb, vb = (jnp.pad(t, padw).reshape(n, h, nb, ATTN_BLOCK, hd) for t in (q, k, v))

    def with_prev(t):
        prev = jnp.concatenate([jnp.zeros_like(t[:, :, :1]), t[:, :, :-1]], axis=2)
        return jnp.concatenate([prev, t], axis=3)

    kc, vc = with_prev(kb), with_prev(vb)
    s = jnp.einsum('nhbqd,nhbkd->nhbqk', qb, kc).astype(jnp.float32) * (hd ** -0.5)
    blk = jnp.arange(nb)[:, None, None]
    qpos = blk * ATTN_BLOCK + jnp.arange(ATTN_BLOCK)[None, :, None]
    kpos = (blk - 1) * ATTN_BLOCK + jnp.arange(2 * ATTN_BLOCK)[None, None, :]
    dist = qpos - kpos
    mask = (dist >= 0) & (dist <= n_back) & (kpos >= 0)
    s = jnp.where(mask, s, NEG)
    mx = s.max(-1, keepdims=True)
    p = jnp.exp(s - mx)
    den = p.sum(-1)
    o = jnp.einsum('nhbqk,nhbkd->nhbqd', p.astype(v.dtype), vc).astype(jnp.float32) / den[..., None]
    lse = mx[..., 0] + jnp.log(den)
    o = o.reshape(n, h, nb * ATTN_BLOCK, hd)[:, :, :l]
    lse = lse.reshape(n, h, nb * ATTN_BLOCK)[:, :, :l]
    return o, lse


def strided_window_attention(q, k, v, n_back, dilation):
    b, s, h, hd = q.shape
    l = s // dilation

    def gather(t):
        return t.reshape(b, l, dilation, h, hd).transpose(0, 2, 3, 1, 4).reshape(b * dilation, h, l, hd)

    o, lse = banded_causal_attention(gather(q), gather(k), gather(v), n_back)
    o = o.reshape(b, dilation, h, l, hd).transpose(0, 3, 1, 2, 4).reshape(b, s, h, hd)
    lse = lse.reshape(b, dilation, h, l).transpose(0, 3, 1, 2).reshape(b, s, h)
    return o, lse


def dilated_attention(x, w_qkv, w_o, cos, sin):
    b, s, _ = x.shape
    qkv = (x @ w_qkv).reshape(b, s, N_DIL_GROUPS, 3, ATTN_HEADS, HEAD_DIM)
    outs, lses = [], []
    for g, (window, dil) in enumerate(DILATED_PATTERNS):
        q = apply_partial_rope(qkv[:, :, g, 0], cos, sin)
        k = apply_partial_rope(qkv[:, :, g, 1], cos, sin)
        o, lse = strided_window_attention(q, k, qkv[:, :, g, 2], window // dil, dil)
        outs.append(o)
        lses.append(lse)
    wts = jax.nn.softmax(jnp.stack(lses), axis=0)
    o = jnp.einsum('gbsh,gbshd->bshd', wts, jnp.stack(outs))
    return o.reshape(b, s, ATTN_HEADS * HEAD_DIM).astype(x.dtype) @ w_o


def causal_depthwise_conv(t, w, bias):
    s = t.shape[1]
    tp = jnp.pad(t, ((0, 0), (CONV_K - 1, 0), (0, 0)))
    out = bias
    for j in range(CONV_K):
        out = out + tp[:, j:j + s] * w[j]
    return out


def mlstm_cell(q, k, v, log_i, log_f):
    b, s, h, dqk = q.shape
    dv = v.shape[-1]
    nc = s // MLSTM_CHUNK
    lc = MLSTM_CHUNK

    def chunks(t):
        return t.astype(jnp.float32).reshape((b, nc, lc, h) + t.shape[3:]).swapaxes(0, 1).swapaxes(2, 3)

    xs = (chunks(q), chunks(k), chunks(v), chunks(log_i), chunks(log_f))
    causal = jnp.tril(jnp.ones((lc, lc), dtype=bool))

    def step(carry, inp):
        c_state, n_state, m_state = carry
        qc, kc, vc, ic, fc = inp
        bcum = jnp.cumsum(fc, axis=-1)
        dmat = jnp.where(causal, bcum[..., :, None] - bcum[..., None, :] + ic[..., None, :], NEG)
        inter = bcum + m_state[..., None]
        m_t = jnp.maximum(inter, dmat.max(-1))
        a = jnp.einsum('bhtd,bhsd->bhts', qc, kc) * jnp.exp(dmat - m_t[..., None])
        w_inter = jnp.exp(inter - m_t)
        num = jnp.einsum('bhts,bhsv->bhtv', a, vc) + w_inter[..., None] * jnp.einsum('bhtd,bhdv->bhtv', qc, c_state)
        den = a.sum(-1) + w_inter * jnp.einsum('bhtd,bhd->bht', qc, n_state)
        h_out = num / jnp.maximum(jnp.abs(den), jnp.exp(-m_t))[..., None]
        b_last = bcum[..., -1]
        g = b_last[..., None] - bcum + ic
        m_new = jnp.maximum(b_last + m_state, g.max(-1))
        wk = jnp.exp(g - m_new[..., None])
        decay = jnp.exp(b_last + m_state - m_new)
        c_new = decay[..., None, None] * c_state + jnp.einsum('bhs,bhsd,bhsv->bhdv', wk, kc, vc)
        n_new = decay[..., None] * n_state + jnp.einsum('bhs,bhsd->bhd', wk, kc)
        return (c_new, n_new, m_new), h_out

    init = (jnp.zeros((b, h, dqk, dv), jnp.float32),
            jnp.zeros((b, h, dqk), jnp.float32),
            jnp.full((b, h), NEG, jnp.float32))
    _, hs = lax.scan(step, init, xs)
    return hs.swapaxes(2, 3).swapaxes(0, 1).reshape(b, s, h, dv)


def mlstm_mixer(x, w_in, b_gates, conv_w, conv_b, norm_g, w_out):
    b, s, _ = x.shape
    z = x @ w_in
    qk = jax.nn.silu(causal_depthwise_conv(z[..., :MLSTM_QK_COLS], conv_w, conv_b))
    half = MLSTM_QK_COLS // 2
    q = qk[..., :half].reshape(b, s, MLSTM_HEADS, QK_DIM)
    k = qk[..., half:].reshape(b, s, MLSTM_HEADS, QK_DIM) * (QK_DIM ** -0.5)
    v = z[..., MLSTM_QK_COLS:MLSTM_QK_COLS + MLSTM_V_COLS].reshape(b, s, MLSTM_HEADS, V_DIM)
    o_pre = z[..., MLSTM_QK_COLS + MLSTM_V_COLS:MLSTM_QK_COLS + 2 * MLSTM_V_COLS]
    gates = z[..., MLSTM_QK_COLS + 2 * MLSTM_V_COLS:].astype(jnp.float32) + b_gates
    log_i = gates[..., :MLSTM_HEADS]
    log_f = jax.nn.log_sigmoid(gates[..., MLSTM_HEADS:])
    hc = mlstm_cell(q, k, v, log_i, log_f)
    mu = hc.mean(-1, keepdims=True)
    var = jnp.square(hc - mu).mean(-1, keepdims=True)
    hn = (hc - mu) * lax.rsqrt(var + LN_EPS) * norm_g.reshape(MLSTM_HEADS, V_DIM)
    y = hn.reshape(b, s, MLSTM_V_COLS).astype(x.dtype) * jax.nn.sigmoid(o_pre)
    return y @ w_out


def grouped_moe(x, w_router, b_router, w_gate, w_up, w_down):
    b, s, d = x.shape
    xt = x.reshape(-1, d)
    t = xt.shape[0]
    logits = (xt @ w_router).astype(jnp.float32) + b_router
    probs = jax.nn.softmax(logits, axis=-1).reshape(t, N_EXPERT_GROUPS, EXPERTS_PER_GROUP)
    top_p, top_i = lax.top_k(probs, TOP_K)
    g_sel = jnp.argmax(top_p.sum(-1), axis=-1)
    sel_p = jnp.take_along_axis(top_p, g_sel[:, None, None], axis=1)[:, 0]
    sel_i = jnp.take_along_axis(top_i, g_sel[:, None, None], axis=1)[:, 0]
    gate = sel_p / sel_p.sum(-1, keepdims=True)
    expert = g_sel[:, None] * EXPERTS_PER_GROUP + sel_i
    a = t * TOP_K
    e_flat = expert.reshape(a)
    tok = jnp.repeat(jnp.arange(t), TOP_K)
    counts = jnp.zeros((N_EXPERTS,), jnp.int32).at[e_flat].add(1)
    padded = (counts + MOE_BLOCK - 1) // MOE_BLOCK * MOE_BLOCK
    ends = jnp.cumsum(padded)
    starts = ends - padded
    order = jnp.argsort(e_flat)
    e_sorted = e_flat[order]
    rank = jnp.arange(a) - (jnp.cumsum(counts) - counts)[e_sorted]
    dest = jnp.zeros((a,), jnp.int32).at[order].set(starts[e_sorted] + rank)
    rows = a + N_EXPERTS * MOE_BLOCK
    buf = jnp.zeros((rows, d), x.dtype).at[dest].set(xt[tok])
    n_blocks = rows // MOE_BLOCK
    block_expert = jnp.minimum(jnp.searchsorted(ends, jnp.arange(n_blocks) * MOE_BLOCK, side='right'), N_EXPERTS - 1)

    def expert_block(args):
        xb, e = args
        hb = jax.nn.silu(xb @ w_gate[e]) * (xb @ w_up[e])
        return hb @ w_down[e]

    out = lax.map(expert_block, (buf.reshape(n_blocks, MOE_BLOCK, d), block_expert)).reshape(rows, d)
    y = (out[dest].reshape(t, TOP_K, d) * gate[..., None].astype(x.dtype)).sum(1)
    return y.reshape(b, s, d)


def setup_inputs(seed: int = 0) -> dict:
    key = jax.random.key(seed)
    ks = jax.random.split(key, 24)

    def nrm(k, shape, scale):
        return jax.random.normal(k, shape, jnp.float32) * scale

    x = nrm(ks[0], (BATCH, SEQ, D_MODEL), 1.0)
    qkv_scale = jnp.array([1.0, 1.0, BETA], jnp.float32)[None, None, None, :, None]
    attn_w_qkv = (nrm(ks[1], (N_ATTN_LAYERS, D_MODEL, N_DIL_GROUPS, 3, ATTN_HEADS * HEAD_DIM), D_MODEL ** -0.5)
                  * qkv_scale).reshape(N_ATTN_LAYERS, D_MODEL, ATTN_QKV_COLS)
    attn_w_o = nrm(ks[2], (N_ATTN_LAYERS, ATTN_HEADS * HEAD_DIM, D_MODEL), (ATTN_HEADS * HEAD_DIM) ** -0.5 * BETA)
    col_scale = jnp.concatenate([jnp.ones((MLSTM_QK_COLS,), jnp.float32),
                                 jnp.full((MLSTM_V_COLS,), BETA, jnp.float32),
                                 jnp.ones((MLSTM_V_COLS + 2 * MLSTM_HEADS,), jnp.float32)])
    mlstm_w_in = nrm(ks[3], (N_MLSTM_LAYERS, D_MODEL, MLSTM_IN_COLS), D_MODEL ** -0.5) * col_scale
    i_bias = nrm(ks[4], (N_MLSTM_LAYERS, MLSTM_HEADS), 0.1)
    f_bias = jnp.linspace(3.0, 6.0, MLSTM_HEADS, dtype=jnp.float32)[None, :] + nrm(ks[5], (N_MLSTM_LAYERS, MLSTM_HEADS), 0.1)
    mlstm_b_gates = jnp.concatenate([i_bias, f_bias], axis=-1)
    mlstm_conv_w = nrm(ks[6], (N_MLSTM_LAYERS, CONV_K, MLSTM_QK_COLS), CONV_K ** -0.5)
    mlstm_conv_b = nrm(ks[7], (N_MLSTM_LAYERS, MLSTM_QK_COLS), 0.02)
    mlstm_norm_g = 1.0 + nrm(ks[8], (N_MLSTM_LAYERS, MLSTM_V_COLS), 0.02)
    mlstm_w_out = nrm(ks[9], (N_MLSTM_LAYERS, MLSTM_V_COLS, D_MODEL), MLSTM_V_COLS ** -0.5 * BETA)
    ln_mix_g = 1.0 + nrm(ks[10], (DEPTH, D_MODEL), 0.02)
    ln_mix_b = nrm(ks[11], (DEPTH, D_MODEL), 0.02)
    ln_ffn_g = 1.0 + nrm(ks[12], (DEPTH, D_MODEL), 0.02)
    ln_ffn_b = nrm(ks[13], (DEPTH, D_MODEL), 0.02)
    router_w = nrm(ks[14], (D_MODEL, N_EXPERTS), D_MODEL ** -0.5)
    router_b = nrm(ks[15], (N_EXPERTS,), 0.01)
    moe_w_gate = nrm(ks[16], (DEPTH, N_EXPERTS, D_MODEL, D_EXPERT), D_MODEL ** -0.5)
    moe_w_up = nrm(ks[17], (DEPTH, N_EXPERTS, D_MODEL, D_EXPERT), D_MODEL ** -0.5 * BETA)
    moe_w_down = nrm(ks[18], (DEPTH, N_EXPERTS, D_EXPERT, D_MODEL), D_EXPERT ** -0.5 * BETA)
    return {'x': x, 'attn_w_qkv': attn_w_qkv, 'attn_w_o': attn_w_o,
            'mlstm_w_in': mlstm_w_in, 'mlstm_b_gates': mlstm_b_gates,
            'mlstm_conv_w': mlstm_conv_w, 'mlstm_conv_b': mlstm_conv_b,
            'mlstm_norm_g': mlstm_norm_g, 'mlstm_w_out': mlstm_w_out,
            'ln_mix_g': ln_mix_g, 'ln_mix_b': ln_mix_b, 'ln_ffn_g': ln_ffn_g, 'ln_ffn_b': ln_ffn_b,
            'router_w': router_w, 'router_b': router_b,
            'moe_w_gate': moe_w_gate, 'moe_w_up': moe_w_up, 'moe_w_down': moe_w_down}


def reference(x, attn_w_qkv, attn_w_o, mlstm_w_in, mlstm_b_gates, mlstm_conv_w, mlstm_conv_b,
              mlstm_norm_g, mlstm_w_out, ln_mix_g, ln_mix_b, ln_ffn_g, ln_ffn_b,
              router_w, router_b, moe_w_gate, moe_w_up, moe_w_down):
    cos, sin = rope_tables(x.shape[1])
    cos, sin = cos.astype(x.dtype), sin.astype(x.dtype)
    for i in range(DEPTH):
        j = i // N_MIXERS
        if i % N_MIXERS == 0:
            mixed = dilated_attention(x, attn_w_qkv[j], attn_w_o[j], cos, sin)
        else:
            mixed = mlstm_mixer(x, mlstm_w_in[j], mlstm_b_gates[j], mlstm_conv_w[j], mlstm_conv_b[j],
                                mlstm_norm_g[j], mlstm_w_out[j])
        x = layer_norm(ALPHA * x + mixed, ln_mix_g[i], ln_mix_b[i])
        ffn = grouped_moe(x, router_w, router_b, moe_w_gate[i], moe_w_up[i], moe_w_down[i])
        x = layer_norm(ALPHA * x + ffn, ln_ffn_g[i], ln_ffn_b[i])
    return x
```

```python
import functools

import numpy as np
import jax
import jax.numpy as jnp
from jax import lax
from jax.experimental import pallas as pl
from jax.experimental.pallas import tpu as pltpu

F32 = jnp.float32
BF16 = jnp.bfloat16
I32 = jnp.int32
HIGHEST = lax.Precision.HIGHEST

D_MODEL = 2048
SEQ = 2048
DEPTH = 2
DILATIONS = (1, 4, 16)
N_BACK = 128
ATTN_HEADS = 8
HEAD_DIM = 128
ROT_DIM = HEAD_DIM // 4
ROPE_THETA = 500000.0
ATTN_BLOCK = 128
ATTN_COLS = ATTN_HEADS * HEAD_DIM

MLSTM_HEADS = 8
QK_DIM = 128
V_DIM = D_MODEL // MLSTM_HEADS
CONV_K = 4
MLSTM_QK_COLS = 2 * MLSTM_HEADS * QK_DIM
MLSTM_V_COLS = MLSTM_HEADS * V_DIM
MLSTM_MAIN_COLS = MLSTM_QK_COLS + 2 * MLSTM_V_COLS

N_EXPERTS = 16
N_EXPERT_GROUPS = 4
EXPERTS_PER_GROUP = 4
D_EXPERT = 768
PAIRS = ((0, 1), (0, 2), (0, 3), (1, 2), (1, 3), (2, 3))
N_CLASSES = N_EXPERT_GROUPS * len(PAIRS)

ALPHA = (2 * DEPTH) ** 0.25
LN_EPS = 1e-5
NEG = -1e30

LANES = 128
V7X_VMEM_BYTES = 64 * 1024 * 1024
MIB = 1024 * 1024

ATTN_STEP_ROWS = 512
ROW_TILE = 512
INPROJ_TM = 1024
INPROJ_TN = 1024
MLSTM_CHUNK = 128
MOE_BLOCK = 256
DISPATCH_TM = 1024
GATHER_TM = 256


def _dot(a, b, **kw):
    return jnp.dot(a, b, preferred_element_type=F32, **kw)


def _dot_nt(a, b):
    return lax.dot_general(a, b, (((1,), (1,)), ((), ())), preferred_element_type=F32)


def _layer_norm_rows(y, g, b):
    mu = jnp.mean(y, axis=-1, keepdims=True)
    yc = y - mu
    var = jnp.mean(yc * yc, axis=-1, keepdims=True)
    return yc * lax.rsqrt(var + LN_EPS) * g + b


def _attn_group_kernel(x_ref, w_ref, tab_ref, o_ref, lse_ref, q_s, k_s, v_s, kp_s, vp_s,
                       *, n_cls, lc, carry):
    step = pl.program_id(1)
    rows_total = n_cls * lc
    blocks_per_class = lc // ATTN_BLOCK

    if n_cls == 1:
        xs = x_ref[0]
    else:
        xs = jnp.concatenate([x_ref[0, :, c * D_MODEL:(c + 1) * D_MODEL] for c in range(n_cls)], axis=0)
    qkv = _dot(xs.astype(BF16), w_ref[...])

    cosf = tab_ref[:, 0:LANES]
    sina = tab_ref[:, LANES:2 * LANES]
    sinb = tab_ref[:, 2 * LANES:3 * LANES]

    def rope(t):
        return (t * cosf + pltpu.roll(t, ROT_DIM // 2, 1) * sina
                + pltpu.roll(t, LANES - ROT_DIM // 2, 1) * sinb)

    for h in range(ATTN_HEADS):
        cs = slice(h * HEAD_DIM, (h + 1) * HEAD_DIM)
        q_s[:, cs] = rope(qkv[:, h * HEAD_DIM:(h + 1) * HEAD_DIM]).astype(BF16)
        k_s[:, cs] = rope(qkv[:, ATTN_COLS + h * HEAD_DIM:ATTN_COLS + (h + 1) * HEAD_DIM]).astype(BF16)
    v_s[...] = qkv[:, 2 * ATTN_COLS:3 * ATTN_COLS].astype(BF16)

    if carry:
        @pl.when(step == 0)
        def _():
            kp_s[...] = jnp.zeros_like(kp_s)
            vp_s[...] = jnp.zeros_like(vp_s)

    row = lax.broadcasted_iota(I32, (ATTN_BLOCK, ATTN_BLOCK), 0)
    col = lax.broadcasted_iota(I32, (ATTN_BLOCK, ATTN_BLOCK), 1)
    lane = lax.broadcasted_iota(I32, (ATTN_BLOCK, LANES), 1)
    mask_cur = col <= row
    mask_prev = col >= row
    scale = HEAD_DIM ** -0.5

    for c in range(n_cls):
        for bi in range(blocks_per_class):
            j = c * blocks_per_class + bi
            rs = slice(j * ATTN_BLOCK, (j + 1) * ATTN_BLOCK)
            ors = slice(bi * ATTN_BLOCK, (bi + 1) * ATTN_BLOCK)
            lse_tile = jnp.zeros((ATTN_BLOCK, LANES), F32)
            for h in range(ATTN_HEADS):
                cs = slice(h * HEAD_DIM, (h + 1) * HEAD_DIM)
                qj = q_s[rs, cs]
                s_c = jnp.where(mask_cur, _dot_nt(qj, k_s[rs, cs]) * scale, NEG)
                mx = jnp.max(s_c, axis=1, keepdims=True)
                has_prev = bi > 0 or carry
                if has_prev:
                    if bi > 0:
                        ps = slice((j - 1) * ATTN_BLOCK, j * ATTN_BLOCK)
                        kp, vp = k_s[ps, cs], v_s[ps, cs]
                        pm = mask_prev
                    else:
                        kp, vp = kp_s[:, cs], vp_s[:, cs]
                        pm = jnp.logical_and(mask_prev, step > 0)
                    s_p = jnp.where(pm, _dot_nt(qj, kp) * scale, NEG)
                    mx = jnp.maximum(mx, jnp.max(s_p, axis=1, keepdims=True))
                p_c = jnp.exp(s_c - mx)
                den = jnp.sum(p_c, axis=1, keepdims=True)
                acc = _dot(p_c.astype(BF16), v_s[rs, cs])
                if has_prev:
                    p_p = jnp.exp(s_p - mx)
                    den = den + jnp.sum(p_p, axis=1, keepdims=True)
                    acc = acc + _dot(p_p.astype(BF16), vp)
                o_ref[0, ors, c * ATTN_COLS + h * HEAD_DIM:c * ATTN_COLS + (h + 1) * HEAD_DIM] = (
                    acc / den).astype(o_ref.dtype)
                lse_tile = jnp.where(lane == h, mx + jnp.log(den), lse_tile)
            lse_ref[0, ors, c * LANES:(c + 1) * LANES] = lse_tile

    if carry:
        kp_s[...] = k_s[rows_total - ATTN_BLOCK:rows_total, :]
        vp_s[...] = v_s[rows_total - ATTN_BLOCK:rows_total, :]


def _attn_group(x3, w_qkv_bf16, tab, group, dil):
    batch = x3.shape[0]
    per_class = SEQ // dil
    lc = min(ATTN_STEP_ROWS, per_class)
    n_cls = ATTN_STEP_ROWS // lc
    steps = SEQ // ATTN_STEP_ROWS
    carry = dil == 1
    if dil == 1:
        imap = lambda b, s: (b, s, 0)
    else:
        imap = lambda b, s: (b, 0, s)
    xg = x3.reshape(batch, per_class, dil * D_MODEL)
    kern = functools.partial(_attn_group_kernel, n_cls=n_cls, lc=lc, carry=carry)
    o, lse = pl.pallas_call(
        kern,
        grid=(batch, steps),
        in_specs=[
            pl.BlockSpec((1, lc, n_cls * D_MODEL), imap),
            pl.BlockSpec((D_MODEL, 3 * ATTN_COLS), lambda b, s: (0, group), pipeline_mode=pl.Buffered(1)),
            pl.BlockSpec((ATTN_STEP_ROWS, 3 * LANES), lambda b, s: (s, 0)),
        ],
        out_specs=[
            pl.BlockSpec((1, lc, n_cls * ATTN_COLS), imap),
            pl.BlockSpec((1, lc, n_cls * LANES), imap),
        ],
        out_shape=[
            jax.ShapeDtypeStruct((batch, per_class, dil * ATTN_COLS), BF16),
            jax.ShapeDtypeStruct((batch, per_class, dil * LANES), F32),
        ],
        scratch_shapes=[
            pltpu.VMEM((ATTN_STEP_ROWS, ATTN_COLS), BF16),
            pltpu.VMEM((ATTN_STEP_ROWS, ATTN_COLS), BF16),
            pltpu.VMEM((ATTN_STEP_ROWS, ATTN_COLS), BF16),
            pltpu.VMEM((ATTN_BLOCK, ATTN_COLS), BF16),
            pltpu.VMEM((ATTN_BLOCK, ATTN_COLS), BF16),
        ],
        compiler_params=pltpu.CompilerParams(
            dimension_semantics=("parallel", "arbitrary"), vmem_limit_bytes=56 * MIB),
        name=f"attn_group{group}",
    )(xg, w_qkv_bf16, tab)
    return o.reshape(batch * SEQ, ATTN_COLS), lse.reshape(batch * SEQ, LANES)


def _rope_table(dil):
    inv_freq = ROPE_THETA ** (-jnp.arange(0, ROT_DIM, 2, dtype=F32) / ROT_DIM)
    ang = jnp.arange(SEQ, dtype=F32)[:, None] * inv_freq[None, :]
    ang = jnp.concatenate([ang, ang], -1)
    cos, sin = jnp.cos(ang), jnp.sin(ang)
    half = ROT_DIM // 2
    cosf = jnp.concatenate([cos, jnp.ones((SEQ, LANES - ROT_DIM), F32)], 1)
    sina = jnp.concatenate([jnp.zeros((SEQ, half), F32), sin[:, half:], jnp.zeros((SEQ, LANES - ROT_DIM), F32)], 1)
    sinb = jnp.concatenate([-sin[:, :half], jnp.zeros((SEQ, LANES - half), F32)], 1)
    tab = jnp.concatenate([cosf, sina, sinb], 1)
    return tab.reshape(SEQ // dil, dil, 3 * LANES).transpose(1, 0, 2).reshape(SEQ, 3 * LANES)


def _mix_out_kernel(*refs, combine):
    if combine:
        o_refs, l_refs = refs[0:3], refs[3:6]
        x_ref, w_ref, g_ref, b_ref, out_ref = refs[6:]
        ls = [l[...] for l in l_refs]
        mx = jnp.maximum(jnp.maximum(ls[0], ls[1]), ls[2])
        es = [jnp.exp(l - mx) for l in ls]
        den = es[0] + es[1] + es[2]
        ws = [e / den for e in es]
        parts = []
        for h in range(ATTN_HEADS):
            cs = slice(h * HEAD_DIM, (h + 1) * HEAD_DIM)
            acc = ws[0][:, h:h + 1] * o_refs[0][:, cs].astype(F32)
            for g in (1, 2):
                acc = acc + ws[g][:, h:h + 1] * o_refs[g][:, cs].astype(F32)
            parts.append(acc)
        mixed_in = jnp.concatenate(parts, axis=1).astype(BF16)
    else:
        y_ref, x_ref, w_ref, g_ref, b_ref, out_ref = refs
        mixed_in = y_ref[...]
    y = ALPHA * x_ref[...] + _dot(mixed_in, w_ref[...])
    out_ref[...] = _layer_norm_rows(y, g_ref[...], b_ref[...])


def _mix_out(mix_inputs, lses, x2, w_bf16, g, b):
    t = x2.shape[0]
    combine = lses is not None
    k = w_bf16.shape[0]
    row_spec = lambda width: pl.BlockSpec((ROW_TILE, width), lambda i: (i, 0))
    const_spec = lambda shape: pl.BlockSpec(shape, lambda i: (0, 0))
    in_specs = [row_spec(k) for _ in mix_inputs]
    args = list(mix_inputs)
    if combine:
        in_specs += [row_spec(LANES) for _ in lses]
        args += list(lses)
    in_specs += [row_spec(D_MODEL), pl.BlockSpec((k, D_MODEL), lambda i: (0, 0), pipeline_mode=pl.Buffered(1)),
                 const_spec((1, D_MODEL)), const_spec((1, D_MODEL))]
    args += [x2, w_bf16, g.reshape(1, D_MODEL), b.reshape(1, D_MODEL)]
    return pl.pallas_call(
        functools.partial(_mix_out_kernel, combine=combine),
        grid=(t // ROW_TILE,),
        in_specs=in_specs,
        out_specs=row_spec(D_MODEL),
        out_shape=jax.ShapeDtypeStruct((t, D_MODEL), F32),
        compiler_params=pltpu.CompilerParams(dimension_semantics=("parallel",), vmem_limit_bytes=48 * MIB),
        name="mix_out_combine" if combine else "mix_out",
    )(*args)


def _router_kernel(x_ref, wr_ref, br_ref, mi_ref, mf_ref, cnt_ref, carry_s, *, tm):
    i = pl.program_id(0)

    @pl.when(i == 0)
    def _():
        carry_s[...] = jnp.zeros_like(carry_s)

    logits = _dot(x_ref[...], wr_ref[...], precision=HIGHEST) + br_ref[...]
    lt = logits.T
    l = [lt[e:e + 1, :] for e in range(N_EXPERTS)]
    mx = l[0]
    for e in range(1, N_EXPERTS):
        mx = jnp.maximum(mx, l[e])
    ex = [jnp.exp(v - mx) for v in l]
    tot = ex[0]
    for e in range(1, N_EXPERTS):
        tot = tot + ex[e]
    p = [v / tot for v in ex]

    def first_index_of(vals, target):
        idx = jnp.full_like(target, len(vals) - 1).astype(I32)
        for k in range(len(vals) - 2, -1, -1):
            idx = jnp.where(vals[k] == target, k, idx)
        return idx

    best = None
    for g in range(N_EXPERT_GROUPS):
        pg = p[g * EXPERTS_PER_GROUP:(g + 1) * EXPERTS_PER_GROUP]
        top1 = jnp.maximum(jnp.maximum(pg[0], pg[1]), jnp.maximum(pg[2], pg[3]))
        i1 = first_index_of(pg, top1)
        rest = [jnp.where(i1 == k, -1.0, pg[k]) for k in range(EXPERTS_PER_GROUP)]
        top2 = jnp.maximum(jnp.maximum(rest[0], rest[1]), jnp.maximum(rest[2], rest[3]))
        i2 = first_index_of(rest, top2)
        score = top1 + top2
        if best is None:
            best = (score, jnp.zeros_like(i1), top1, top2, i1, i2)
        else:
            better = score > best[0]
            cand = (score, jnp.full_like(i1, g), top1, top2, i1, i2)
            best = tuple(jnp.where(better, cv, bv) for cv, bv in zip(cand, best))
    _, g_sel, p1, p2, i1, i2 = best
    psum = p1 + p2
    gate1, gate2 = p1 / psum, p2 / psum
    first_low = i1 < i2
    lo = jnp.where(first_low, i1, i2)
    hi = jnp.where(first_low, i2, i1)
    gate_lo = jnp.where(first_low, gate1, gate2)
    gate_hi = jnp.where(first_low, gate2, gate1)
    pair = jnp.where(lo == 0, 0, jnp.where(lo == 1, 3, 5)) + hi - lo - 1
    cls = g_sel * len(PAIRS) + pair

    n_rows = carry_s.shape[0]
    sub = lax.broadcasted_iota(I32, (n_rows, tm), 0)
    onehot = sub == cls
    oh = jnp.where(onehot, 1.0, 0.0)
    upper = (lax.broadcasted_iota(I32, (tm, tm), 0) <= lax.broadcasted_iota(I32, (tm, tm), 1))
    cum = _dot(oh.astype(BF16), jnp.where(upper, 1.0, 0.0).astype(BF16))
    carry = carry_s[:, 0:1]
    rank = jnp.sum(jnp.where(onehot, cum - 1.0 + carry, 0.0), axis=0, keepdims=True)
    carry_new = carry + jnp.sum(oh, axis=1, keepdims=True)
    carry_s[...] = jnp.broadcast_to(carry_new, carry_s.shape)
    cnt_ref[...] = jnp.broadcast_to(carry_new, cnt_ref.shape)

    sub8 = lax.broadcasted_iota(I32, (8, tm), 0)
    mi_ref[...] = jnp.where(sub8 == 0, cls, jnp.where(sub8 == 1, rank.astype(I32), 0))
    mf_ref[...] = jnp.where(sub8 == 0, gate_lo, jnp.where(sub8 == 1, gate_hi, 0.0))


def _router(x2, wr_pad, br_pad):
    t = x2.shape[0]
    tm = ROW_TILE
    return pl.pallas_call(
        functools.partial(_router_kernel, tm=tm),
        grid=(t // tm,),
        in_specs=[
            pl.BlockSpec((tm, D_MODEL), lambda i: (i, 0)),
            pl.BlockSpec((D_MODEL, LANES), lambda i: (0, 0)),
            pl.BlockSpec((1, LANES), lambda i: (0, 0)),
        ],
        out_specs=[
            pl.BlockSpec((8, tm), lambda i: (0, i)),
            pl.BlockSpec((8, tm), lambda i: (0, i)),
            pl.BlockSpec((32, LANES), lambda i: (0, 0)),
        ],
        out_shape=[
            jax.ShapeDtypeStruct((8, t), I32),
            jax.ShapeDtypeStruct((8, t), F32),
            jax.ShapeDtypeStruct((32, LANES), F32),
        ],
        scratch_shapes=[pltpu.VMEM((32, LANES), F32)],
        compiler_params=pltpu.CompilerParams(dimension_semantics=("arbitrary",), vmem_limit_bytes=32 * MIB),
        name="moe_router",
    )(x2, wr_pad, br_pad)


def _dispatch_kernel(dest_ref, x_hbm, buf_in, buf_hbm, sem, *, tm):
    del buf_in
    base = pl.program_id(0) * tm

    def issue(t, c):
        pltpu.make_async_copy(x_hbm.at[pl.ds(base + t, 1)], buf_hbm.at[pl.ds(dest_ref[0, 0, t], 1)], sem).start()
        return c

    lax.fori_loop(0, tm, issue, 0)
    pltpu.make_async_copy(x_hbm.at[pl.ds(0, tm)], buf_hbm.at[pl.ds(0, tm)], sem).wait()


def _dispatch(x2, dest, n_rows):
    t = x2.shape[0]
    tm = DISPATCH_TM
    buf0 = jnp.zeros((n_rows, D_MODEL), F32)
    return pl.pallas_call(
        functools.partial(_dispatch_kernel, tm=tm),
        grid=(t // tm,),
        in_specs=[
            pl.BlockSpec((1, 1, tm), lambda i: (i, 0, 0), memory_space=pltpu.SMEM),
            pl.BlockSpec(memory_space=pl.ANY),
            pl.BlockSpec(memory_space=pl.ANY),
        ],
        out_specs=pl.BlockSpec(memory_space=pl.ANY),
        out_shape=jax.ShapeDtypeStruct((n_rows, D_MODEL), F32),
        scratch_shapes=[pltpu.SemaphoreType.DMA(())],
        input_output_aliases={2: 0},
        compiler_params=pltpu.CompilerParams(dimension_semantics=("arbitrary",), has_side_effects=True),
        name="moe_dispatch",
    )(dest.reshape(t // tm, 1, tm), x2, buf0)


def _expert_kernel(elo_ref, ehi_ref, valid_ref, xb_ref, gs_ref, wg0, wu0, wd0, wg1, wu1, wd1, out_ref):
    del elo_ref, ehi_ref
    blk = pl.program_id(0)

    @pl.when(valid_ref[blk] > 0)
    def _():
        xb = xb_ref[...].astype(BF16)

        def ffn(wg, wu, wd):
            hg = _dot(xb, wg[0])
            hu = _dot(xb, wu[0])
            hidden = (hg * jax.nn.sigmoid(hg)) * hu
            return _dot(hidden.astype(BF16), wd[0])

        gs = gs_ref[...]
        out_ref[...] = gs[:, 0:1] * ffn(wg0, wu0, wd0) + gs[:, 1:2] * ffn(wg1, wu1, wd1)

    @pl.when(valid_ref[blk] == 0)
    def _():
        out_ref[...] = jnp.zeros_like(out_ref)


def _experts(buf, gates_sorted, blk_elo, blk_ehi, blk_valid, wg, wu, wd):
    n_rows = buf.shape[0]
    nb = n_rows // MOE_BLOCK
    lo_map = lambda b, elo, ehi, val: (elo[b], 0, 0)
    hi_map = lambda b, elo, ehi, val: (ehi[b], 0, 0)
    row_map = lambda b, elo, ehi, val: (b, 0)
    up_shape = (1, D_MODEL, D_EXPERT)
    down_shape = (1, D_EXPERT, D_MODEL)
    grid_spec = pltpu.PrefetchScalarGridSpec(
        num_scalar_prefetch=3,
        grid=(nb,),
        in_specs=[
            pl.BlockSpec((MOE_BLOCK, D_MODEL), row_map),
            pl.BlockSpec((MOE_BLOCK, LANES), row_map),
            pl.BlockSpec(up_shape, lo_map), pl.BlockSpec(up_shape, lo_map), pl.BlockSpec(down_shape, lo_map),
            pl.BlockSpec(up_shape, hi_map), pl.BlockSpec(up_shape, hi_map), pl.BlockSpec(down_shape, hi_map),
        ],
        out_specs=pl.BlockSpec((MOE_BLOCK, D_MODEL), row_map),
    )
    return pl.pallas_call(
        _expert_kernel,
        grid_spec=grid_spec,
        out_shape=jax.ShapeDtypeStruct((n_rows, D_MODEL), F32),
        compiler_params=pltpu.CompilerParams(dimension_semantics=("arbitrary",), vmem_limit_bytes=58 * MIB),
        name="moe_experts",
    )(blk_elo, blk_ehi, blk_valid, buf, gates_sorted, wg, wu, wd, wg, wu, wd)


def _gather_ln_kernel(dest_ref, x_ref, src_hbm, g_ref, b_ref, y_ref, rows_s, sem, *, tm):
    def issue(t, c):
        pltpu.make_async_copy(src_hbm.at[pl.ds(dest_ref[0, 0, t], 1)], rows_s.at[pl.ds(t, 1)], sem).start()
        return c

    lax.fori_loop(0, tm, issue, 0)
    pltpu.make_async_copy(src_hbm.at[pl.ds(0, tm)], rows_s, sem).wait()
    y = ALPHA * x_ref[...] + rows_s[...]
    y_ref[...] = _layer_norm_rows(y, g_ref[...], b_ref[...])


def _gather_ln(x2, expert_out, dest, g, b):
    t = x2.shape[0]
    tm = GATHER_TM
    return pl.pallas_call(
        functools.partial(_gather_ln_kernel, tm=tm),
        grid=(t // tm,),
        in_specs=[
            pl.BlockSpec((1, 1, tm), lambda i: (i, 0, 0), memory_space=pltpu.SMEM),
            pl.BlockSpec((tm, D_MODEL), lambda i: (i, 0)),
            pl.BlockSpec(memory_space=pl.ANY),
            pl.BlockSpec((1, D_MODEL), lambda i: (0, 0)),
            pl.BlockSpec((1, D_MODEL), lambda i: (0, 0)),
        ],
        out_specs=pl.BlockSpec((tm, D_MODEL), lambda i: (i, 0)),
        out_shape=jax.ShapeDtypeStruct((t, D_MODEL), F32),
        scratch_shapes=[pltpu.VMEM((tm, D_MODEL), F32), pltpu.SemaphoreType.DMA(())],
        compiler_params=pltpu.CompilerParams(dimension_semantics=("arbitrary",), vmem_limit_bytes=32 * MIB),
        name="moe_gather_ln",
    )(dest.reshape(t // tm, 1, tm), x2, expert_out, g.reshape(1, D_MODEL), b.reshape(1, D_MODEL))


_CLASS_LO = np.array([4 * (c // 6) + PAIRS[c % 6][0] for c in range(N_CLASSES)], np.int32)
_CLASS_HI = np.array([4 * (c // 6) + PAIRS[c % 6][1] for c in range(N_CLASSES)], np.int32)


def _moe_layer(x2, wr_pad, br_pad, wg, wu, wd, ln_g, ln_b):
    t = x2.shape[0]
    nb = -(-(t + N_CLASSES * (MOE_BLOCK - 1)) // MOE_BLOCK)
    n_rows = nb * MOE_BLOCK
    meta_i, meta_f, cnt = _router(x2, wr_pad, br_pad)
    cls, rank = meta_i[0], meta_i[1]
    counts = cnt[:N_CLASSES, 0].astype(I32)
    padded = (counts + MOE_BLOCK - 1) // MOE_BLOCK * MOE_BLOCK
    ends = jnp.cumsum(padded)
    starts = ends - padded
    dest = starts[cls] + rank
    blk_start = jnp.arange(nb, dtype=I32) * MOE_BLOCK
    blk_valid = (blk_start < ends[-1]).astype(I32)
    n_valid = jnp.sum(blk_valid)
    blk_cls = jnp.minimum(jnp.sum((ends[None, :] <= blk_start[:, None]).astype(I32), axis=1), N_CLASSES - 1)
    blk_cls = blk_cls[jnp.minimum(jnp.arange(nb), n_valid - 1)]
    blk_elo = jnp.asarray(_CLASS_LO)[blk_cls]
    blk_ehi = jnp.asarray(_CLASS_HI)[blk_cls]
    gates_sorted = jnp.zeros((n_rows, LANES), F32).at[dest, 0].set(meta_f[0]).at[dest, 1].set(meta_f[1])
    buf = _dispatch(x2, dest, n_rows)
    out = _experts(buf, gates_sorted, blk_elo, blk_ehi, blk_valid, wg, wu, wd)
    return _gather_ln(x2, out, dest, ln_g, ln_b)


def _inproj_kernel(x_ref, w_ref, wgate_ref, z_ref, gates_ref, xb_s):
    @pl.when(pl.program_id(1) == 0)
    def _():
        xb_s[...] = x_ref[...].astype(BF16)
        gates_ref[...] = _dot(xb_s[...], wgate_ref[...])

    z_ref[...] = _dot(xb_s[...], w_ref[...]).astype(z_ref.dtype)


def _inproj(x2, w_main_bf16, w_gate_bf16):
    t = x2.shape[0]
    tm, tn = INPROJ_TM, INPROJ_TN
    n_main = w_main_bf16.shape[1]
    return pl.pallas_call(
        _inproj_kernel,
        grid=(t // tm, n_main // tn),
        in_specs=[
            pl.BlockSpec((tm, D_MODEL), lambda m, n: (m, 0)),
            pl.BlockSpec((D_MODEL, tn), lambda m, n: (0, n)),
            pl.BlockSpec((D_MODEL, LANES), lambda m, n: (0, 0)),
        ],
        out_specs=[
            pl.BlockSpec((tm, tn), lambda m, n: (m, n)),
            pl.BlockSpec((tm, LANES), lambda m, n: (m, 0)),
        ],
        out_shape=[
            jax.ShapeDtypeStruct((t, n_main), BF16),
            jax.ShapeDtypeStruct((t, LANES), F32),
        ],
        scratch_shapes=[pltpu.VMEM((tm, D_MODEL), BF16)],
        compiler_params=pltpu.CompilerParams(
            dimension_semantics=("parallel", "arbitrary"), vmem_limit_bytes=48 * MIB),
        name="mlstm_inproj",
    )(x2, w_main_bf16, w_gate_bf16)


def _mlstm_cell_kernel(zq_ref, zk_ref, v_ref, op_ref, gt_ref, bg_ref, cwq_ref, cbq_ref, cwk_ref, cbk_ref,
                       ng_ref, y_ref, q_s, k_s, *, chunk):
    head = pl.program_id(1)
    seq = zq_ref.shape[0]

    def conv_silu(z_ref, cw_ref, cb_ref):
        z = z_ref[...].astype(F32)
        rowi = lax.broadcasted_iota(I32, z.shape, 0)
        out = cb_ref[...] + cw_ref[0:1, :] * jnp.where(rowi >= CONV_K - 1, pltpu.roll(z, CONV_K - 1, 0), 0.0)
        for j in range(1, CONV_K - 1):
            shift = CONV_K - 1 - j
            out = out + cw_ref[j:j + 1, :] * jnp.where(rowi >= shift, pltpu.roll(z, shift, 0), 0.0)
        out = out + cw_ref[CONV_K - 1:CONV_K, :] * z
        return out * jax.nn.sigmoid(out)

    q_s[...] = conv_silu(zq_ref, cwq_ref, cbq_ref).astype(BF16)
    k_s[...] = conv_silu(zk_ref, cwk_ref, cbk_ref) * (QK_DIM ** -0.5)

    lane = lax.broadcasted_iota(I32, (chunk, LANES), 1)
    sub = lax.broadcasted_iota(I32, (LANES, chunk), 0)
    ti = lax.broadcasted_iota(I32, (chunk, chunk), 0)
    si = lax.broadcasted_iota(I32, (chunk, chunk), 1)
    causal = ti >= si
    tri = jnp.where(causal, 1.0, 0.0)
    bias = bg_ref[...]
    ones_cols = jnp.ones((chunk, LANES), BF16)

    def pick_col(a, idx):
        return jnp.sum(jnp.where(lane == idx, a, 0.0), axis=1, keepdims=True)

    def pick_row(a, idx):
        return jnp.sum(jnp.where(sub == idx, a, 0.0), axis=0, keepdims=True)

    c_state = jnp.zeros((QK_DIM, V_DIM + LANES), F32)
    m_state = jnp.full((1, 1), NEG, F32)
    for c in range(seq // chunk):
        rs = slice(c * chunk, (c + 1) * chunk)
        gb = gt_ref[rs, :] + bias
        log_f = -(jnp.maximum(-gb, 0.0) + jnp.log1p(jnp.exp(-jnp.abs(gb))))
        bcum = _dot(tri, log_f, precision=HIGHEST)
        li_col = pick_col(gb, head)
        bc_col = pick_col(bcum, head + MLSTM_HEADS)
        li_row = pick_row(gb.T, head)
        bc_row = pick_row(bcum.T, head + MLSTM_HEADS)
        dmat = jnp.where(causal, bc_col + (li_row - bc_row), NEG)
        inter = bc_col + m_state
        m_t = jnp.maximum(inter, jnp.max(dmat, axis=1, keepdims=True))
        qc = q_s[rs, :]
        kf = k_s[rs, :]
        vc = v_ref[rs, :]
        a = _dot_nt(qc, kf.astype(BF16)) * jnp.exp(dmat - m_t)
        w_inter = jnp.exp(inter - m_t)
        q_state = _dot(qc, c_state.astype(BF16))
        num = _dot(a.astype(BF16), vc) + w_inter * q_state[:, :V_DIM]
        den = jnp.sum(a, axis=1, keepdims=True) + w_inter * q_state[:, V_DIM:V_DIM + 1]
        h_out = num / jnp.maximum(jnp.abs(den), jnp.exp(-m_t))
        mu = jnp.mean(h_out, axis=1, keepdims=True)
        hc = h_out - mu
        var = jnp.mean(hc * hc, axis=1, keepdims=True)
        hn = hc * lax.rsqrt(var + LN_EPS) * ng_ref[...]
        y_ref[rs, :] = (hn * jax.nn.sigmoid(op_ref[rs, :].astype(F32))).astype(y_ref.dtype)
        b_last = bc_col[chunk - 1:chunk, :]
        g = b_last - bc_col + li_col
        m_new = jnp.maximum(b_last + m_state, jnp.max(g, axis=0, keepdims=True))
        wk = jnp.exp(g - m_new)
        decay = jnp.exp(b_last + m_state - m_new)
        kw_t = (wk * kf).T.astype(BF16)
        v_ext = jnp.concatenate([vc, ones_cols], axis=1)
        c_state = decay * c_state + _dot(kw_t, v_ext)
        m_state = m_new


def _mlstm_cell(z, gates, b_gates_pad, conv_w, conv_b, norm_g, batch):
    t = z.shape[0]
    hq = MLSTM_HEADS
    v_blk0 = MLSTM_QK_COLS // V_DIM
    o_blk0 = (MLSTM_QK_COLS + MLSTM_V_COLS) // V_DIM
    return pl.pallas_call(
        functools.partial(_mlstm_cell_kernel, chunk=MLSTM_CHUNK),
        grid=(batch, MLSTM_HEADS),
        in_specs=[
            pl.BlockSpec((SEQ, QK_DIM), lambda b, h: (b, h)),
            pl.BlockSpec((SEQ, QK_DIM), lambda b, h: (b, hq + h)),
            pl.BlockSpec((SEQ, V_DIM), lambda b, h: (b, v_blk0 + h)),
            pl.BlockSpec((SEQ, V_DIM), lambda b, h: (b, o_blk0 + h)),
            pl.BlockSpec((SEQ, LANES), lambda b, h: (b, 0)),
            pl.BlockSpec((1, LANES), lambda b, h: (0, 0)),
            pl.BlockSpec((CONV_K, QK_DIM), lambda b, h: (0, h)),
            pl.BlockSpec((1, QK_DIM), lambda b, h: (0, h)),
            pl.BlockSpec((CONV_K, QK_DIM), lambda b, h: (0, hq + h)),
            pl.BlockSpec((1, QK_DIM), lambda b, h: (0, hq + h)),
            pl.BlockSpec((1, V_DIM), lambda b, h: (0, h)),
        ],
        out_specs=pl.BlockSpec((SEQ, V_DIM), lambda b, h: (b, h)),
        out_shape=jax.ShapeDtypeStruct((t, MLSTM_V_COLS), BF16),
        scratch_shapes=[pltpu.VMEM((SEQ, QK_DIM), BF16), pltpu.VMEM((SEQ, QK_DIM), F32)],
        compiler_params=pltpu.CompilerParams(
            dimension_semantics=("parallel", "parallel"), vmem_limit_bytes=40 * MIB),
        name="mlstm_cell",
    )(z, z, z, z, gates, b_gates_pad, conv_w, conv_b.reshape(1, -1), conv_w, conv_b.reshape(1, -1),
      norm_g.reshape(1, -1))


def kernel(x, attn_w_qkv, attn_w_o, mlstm_w_in, mlstm_b_gates, mlstm_conv_w, mlstm_conv_b, mlstm_norm_g,
           mlstm_w_out, ln_mix_g, ln_mix_b, ln_ffn_g, ln_ffn_b, router_w, router_b, moe_w_gate, moe_w_up,
           moe_w_down):
    batch, seq, d = x.shape
    assert (seq, d) == (SEQ, D_MODEL)
    t = batch * seq
    x2 = x.reshape(t, d)

    wr_pad = jnp.zeros((D_MODEL, LANES), F32).at[:, :N_EXPERTS].set(router_w)
    br_pad = jnp.zeros((1, LANES), F32).at[0, :N_EXPERTS].set(router_b)
    wg_bf, wu_bf, wd_bf = moe_w_gate.astype(BF16), moe_w_up.astype(BF16), moe_w_down.astype(BF16)

    w_qkv = attn_w_qkv[0].astype(BF16)
    outs, lses = [], []
    for group, dil in enumerate(DILATIONS):
        o, lse = _attn_group(x, w_qkv, _rope_table(dil), group, dil)
        outs.append(o)
        lses.append(lse)
    x2 = _mix_out(outs, lses, x2, attn_w_o[0].astype(BF16), ln_mix_g[0], ln_mix_b[0])
    x2 = _moe_layer(x2, wr_pad, br_pad, wg_bf[0], wu_bf[0], wd_bf[0], ln_ffn_g[0], ln_ffn_b[0])

    w_in = mlstm_w_in[0]
    w_main = w_in[:, :MLSTM_MAIN_COLS].astype(BF16)
    w_gate = jnp.zeros((D_MODEL, LANES), F32).at[:, :2 * MLSTM_HEADS].set(w_in[:, MLSTM_MAIN_COLS:]).astype(BF16)
    bg_pad = jnp.zeros((1, LANES), F32).at[0, :2 * MLSTM_HEADS].set(mlstm_b_gates[0])
    z, gates = _inproj(x2, w_main, w_gate)
    y = _mlstm_cell(z, gates, bg_pad, mlstm_conv_w[0], mlstm_conv_b[0], mlstm_norm_g[0], batch)
    x2 = _mix_out([y], None, x2, mlstm_w_out[0].astype(BF16), ln_mix_g[1], ln_mix_b[1])
    x2 = _moe_layer(x2, wr_pad, br_pad, wg_bf[1], wu_bf[1], wd_bf[1], ln_ffn_g[1], ln_ffn_b[1])
    return x2.reshape(batch, seq, d)
```

```python
import functools

import numpy as np
import jax
import jax.numpy as jnp
from jax import lax
from jax.experimental import pallas as pl
from jax.experimental.pallas import tpu as pltpu

F32 = jnp.float32
BF16 = jnp.bfloat16
I32 = jnp.int32
HIGHEST = lax.Precision.HIGHEST

D_MODEL = 2048
SEQ = 2048
DEPTH = 2
DILATIONS = (1, 4, 16)
N_BACK = 128
ATTN_HEADS = 8
HEAD_DIM = 128
ROT_DIM = HEAD_DIM // 4
ROPE_THETA = 500000.0
ATTN_BLOCK = 128
ATTN_COLS = ATTN_HEADS * HEAD_DIM

MLSTM_HEADS = 8
QK_DIM = 128
V_DIM = D_MODEL // MLSTM_HEADS
CONV_K = 4
MLSTM_QK_COLS = 2 * MLSTM_HEADS * QK_DIM
MLSTM_V_COLS = MLSTM_HEADS * V_DIM
MLSTM_MAIN_COLS = MLSTM_QK_COLS + 2 * MLSTM_V_COLS

N_EXPERTS = 16
N_EXPERT_GROUPS = 4
EXPERTS_PER_GROUP = 4
D_EXPERT = 768
PAIRS = ((0, 1), (0, 2), (0, 3), (1, 2), (1, 3), (2, 3))
N_CLASSES = N_EXPERT_GROUPS * len(PAIRS)

ALPHA = (2 * DEPTH) ** 0.25
LN_EPS = 1e-5
NEG = -1e30

LANES = 128
V7X_VMEM_BYTES = 64 * 1024 * 1024
MIB = 1024 * 1024

ATTN_STEP_ROWS = 512
ATTN_HEAD_SET = 8
ROPE_PAIR_SHIFT = 64
ROW_TILE = 512
INPROJ_TM = 1024
INPROJ_TN = 1024
MLSTM_CHUNK = 128
MOE_BLOCK = 256
DISPATCH_TM = 512
GATHER_TM = 256


def _dot(a, b, **kw):
    return jnp.dot(a, b, preferred_element_type=F32, **kw)


def _dot_nt(a, b):
    return lax.dot_general(a, b, (((1,), (1,)), ((), ())), preferred_element_type=F32)


def _layer_norm_rows(y, g, b):
    mu = jnp.mean(y, axis=-1, keepdims=True)
    yc = y - mu
    var = jnp.mean(yc * yc, axis=-1, keepdims=True)
    return yc * lax.rsqrt(var + LN_EPS) * g + b


def _attn_group_kernel(x_ref, w_ref, tab_ref, o_ref, lse_ref, q_s, k_s, v_s, *, n_cls, lc, carry):
    step = pl.program_id(1)
    rows_total = n_cls * lc
    blocks_per_class = lc // ATTN_BLOCK

    if n_cls == 1:
        xs = x_ref[0]
    else:
        xs = jnp.concatenate([x_ref[0, :, c * D_MODEL:(c + 1) * D_MODEL] for c in range(n_cls)], axis=0)
    qkv = _dot(xs, w_ref[...])

    if carry:
        @pl.when(step == 0)
        def _():
            k_s[0:ATTN_BLOCK, :] = jnp.zeros((ATTN_BLOCK, ATTN_COLS), BF16)
            v_s[0:ATTN_BLOCK, :] = jnp.zeros((ATTN_BLOCK, ATTN_COLS), BF16)

        @pl.when(step > 0)
        def _():
            k_s[0:ATTN_BLOCK, :] = k_s[rows_total:rows_total + ATTN_BLOCK, :]
            v_s[0:ATTN_BLOCK, :] = v_s[rows_total:rows_total + ATTN_BLOCK, :]

    cosf = tab_ref[:, 0:LANES]
    sinr = tab_ref[:, LANES:2 * LANES]

    def rope(t):
        return t * cosf + pltpu.roll(t, ROPE_PAIR_SHIFT, 1) * sinr

    for h in range(ATTN_HEADS):
        cs = slice(h * HEAD_DIM, (h + 1) * HEAD_DIM)
        q_s[:, cs] = rope(qkv[:, h * HEAD_DIM:(h + 1) * HEAD_DIM]).astype(BF16)
        k_s[ATTN_BLOCK:, cs] = rope(qkv[:, ATTN_COLS + h * HEAD_DIM:ATTN_COLS + (h + 1) * HEAD_DIM]).astype(BF16)
    v_s[ATTN_BLOCK:, :] = qkv[:, 2 * ATTN_COLS:3 * ATTN_COLS].astype(BF16)

    row = lax.broadcasted_iota(I32, (ATTN_BLOCK, ATTN_BLOCK), 0)
    col = lax.broadcasted_iota(I32, (ATTN_BLOCK, ATTN_BLOCK), 1)
    row2 = lax.broadcasted_iota(I32, (ATTN_BLOCK, 2 * ATTN_BLOCK), 0)
    col2 = lax.broadcasted_iota(I32, (ATTN_BLOCK, 2 * ATTN_BLOCK), 1)
    lane = lax.broadcasted_iota(I32, (ATTN_BLOCK, LANES), 1)
    mask_cur = col <= row
    mask_both = jnp.logical_and(col2 >= row2, col2 <= row2 + N_BACK)
    if carry:
        mask_first = jnp.logical_and(mask_both, jnp.logical_or(col2 >= ATTN_BLOCK, step > 0))
    scale = HEAD_DIM ** -0.5

    for c in range(n_cls):
        for bi in range(blocks_per_class):
            j = c * blocks_per_class + bi
            ors = slice(bi * ATTN_BLOCK, (bi + 1) * ATTN_BLOCK)
            with_prev = bi > 0 or carry
            if with_prev:
                krows = slice(j * ATTN_BLOCK, (j + 2) * ATTN_BLOCK)
                mask = mask_first if bi == 0 else mask_both
            else:
                krows = slice((j + 1) * ATTN_BLOCK, (j + 2) * ATTN_BLOCK)
                mask = mask_cur
            lse_tile = jnp.zeros((ATTN_BLOCK, LANES), F32)
            for h0 in range(0, ATTN_HEADS, ATTN_HEAD_SET):
                heads = range(h0, h0 + ATTN_HEAD_SET)
                cols = [slice(h * HEAD_DIM, (h + 1) * HEAD_DIM) for h in heads]
                scores = [jnp.where(mask, _dot_nt(q_s[j * ATTN_BLOCK:(j + 1) * ATTN_BLOCK, cs], k_s[krows, cs])
                                    * scale, NEG) for cs in cols]
                maxes = [jnp.max(s, axis=1, keepdims=True) for s in scores]
                probs = [jnp.exp(s - m) for s, m in zip(scores, maxes)]
                dens = [jnp.sum(p, axis=1, keepdims=True) for p in probs]
                accs = [_dot(p.astype(BF16), v_s[krows, cs]) for p, cs in zip(probs, cols)]
                for h, acc, den, m in zip(heads, accs, dens, maxes):
                    o_ref[0, ors, c * ATTN_COLS + h * HEAD_DIM:c * ATTN_COLS + (h + 1) * HEAD_DIM] = (
                        acc / den).astype(o_ref.dtype)
                    lse_tile = jnp.where(lane == h, m + jnp.log(den), lse_tile)
            lse_ref[0, ors, c * LANES:(c + 1) * LANES] = lse_tile


def _attn_group(x3, w_qkv_bf16, tab, group, dil):
    batch = x3.shape[0]
    per_class = SEQ // dil
    lc = min(ATTN_STEP_ROWS, per_class)
    n_cls = ATTN_STEP_ROWS // lc
    steps = SEQ // ATTN_STEP_ROWS
    carry = dil == 1
    if dil == 1:
        imap = lambda b, s: (b, s, 0)
    else:
        imap = lambda b, s: (b, 0, s)
    xg = x3.reshape(batch, per_class, dil * D_MODEL)
    kern = functools.partial(_attn_group_kernel, n_cls=n_cls, lc=lc, carry=carry)
    o, lse = pl.pallas_call(
        kern,
        grid=(batch, steps),
        in_specs=[
            pl.BlockSpec((1, lc, n_cls * D_MODEL), imap),
            pl.BlockSpec((D_MODEL, 3 * ATTN_COLS), lambda b, s: (0, group), pipeline_mode=pl.Buffered(1)),
            pl.BlockSpec((ATTN_STEP_ROWS, 2 * LANES), lambda b, s: (s, 0)),
        ],
        out_specs=[
            pl.BlockSpec((1, lc, n_cls * ATTN_COLS), imap),
            pl.BlockSpec((1, lc, n_cls * LANES), imap),
        ],
        out_shape=[
            jax.ShapeDtypeStruct((batch, per_class, dil * ATTN_COLS), BF16),
            jax.ShapeDtypeStruct((batch, per_class, dil * LANES), F32),
        ],
        scratch_shapes=[
            pltpu.VMEM((ATTN_STEP_ROWS, ATTN_COLS), BF16),
            pltpu.VMEM((ATTN_BLOCK + ATTN_STEP_ROWS, ATTN_COLS), BF16),
            pltpu.VMEM((ATTN_BLOCK + ATTN_STEP_ROWS, ATTN_COLS), BF16),
        ],
        compiler_params=pltpu.CompilerParams(
            dimension_semantics=("parallel", "arbitrary"), vmem_limit_bytes=56 * MIB),
        name=f"attn_group{group}",
    )(xg, w_qkv_bf16, tab)
    return o.reshape(batch * SEQ, ATTN_COLS), lse.reshape(batch * SEQ, LANES)


def _rope_permutation():
    half = ROT_DIM // 2
    head = np.arange(HEAD_DIM)
    head[half:ROT_DIM] = np.arange(ROPE_PAIR_SHIFT, ROPE_PAIR_SHIFT + half)
    head[ROPE_PAIR_SHIFT:ROPE_PAIR_SHIFT + half] = np.arange(half, ROT_DIM)
    cols = np.arange(len(DILATIONS) * 3 * ATTN_COLS).reshape(len(DILATIONS), 3, ATTN_HEADS, HEAD_DIM)
    cols[:, 0:2] = cols[:, 0:2][..., head]
    return cols.reshape(-1)


def _rope_table(dil):
    inv_freq = ROPE_THETA ** (-jnp.arange(0, ROT_DIM, 2, dtype=F32) / ROT_DIM)
    ang = jnp.arange(SEQ, dtype=F32)[:, None] * inv_freq[None, :]
    ang = jnp.concatenate([ang, ang], -1)
    cos, sin = jnp.cos(ang), jnp.sin(ang)
    half = ROT_DIM // 2
    gap = ROPE_PAIR_SHIFT - half
    tail = LANES - ROPE_PAIR_SHIFT - half
    cosf = jnp.concatenate([cos[:, :half], jnp.ones((SEQ, gap), F32), cos[:, half:], jnp.ones((SEQ, tail), F32)], 1)
    sinr = jnp.concatenate([-sin[:, :half], jnp.zeros((SEQ, gap), F32), sin[:, half:], jnp.zeros((SEQ, tail), F32)], 1)
    tab = jnp.concatenate([cosf, sinr], 1)
    return tab.reshape(SEQ // dil, dil, 2 * LANES).transpose(1, 0, 2).reshape(SEQ, 2 * LANES)


def _mix_out_kernel(*refs, combine):
    if combine:
        o_refs, l_refs = refs[0:3], refs[3:6]
        x_ref, w_ref, g_ref, b_ref, out_ref = refs[6:]
        ls = [l[...] for l in l_refs]
        mx = jnp.maximum(jnp.maximum(ls[0], ls[1]), ls[2])
        es = [jnp.exp(l - mx) for l in ls]
        den = es[0] + es[1] + es[2]
        ws = [e / den for e in es]
        parts = []
        for h in range(ATTN_HEADS):
            cs = slice(h * HEAD_DIM, (h + 1) * HEAD_DIM)
            acc = ws[0][:, h:h + 1] * o_refs[0][:, cs].astype(F32)
            for g in (1, 2):
                acc = acc + ws[g][:, h:h + 1] * o_refs[g][:, cs].astype(F32)
            parts.append(acc)
        mixed_in = jnp.concatenate(parts, axis=1).astype(BF16)
    else:
        y_ref, x_ref, w_ref, g_ref, b_ref, out_ref = refs
        mixed_in = y_ref[...]
    y = ALPHA * x_ref[...] + _dot(mixed_in, w_ref[...])
    out_ref[...] = _layer_norm_rows(y, g_ref[...], b_ref[...])


def _mix_out(mix_inputs, lses, x2, w_bf16, g, b):
    t = x2.shape[0]
    combine = lses is not None
    k = w_bf16.shape[0]
    row_spec = lambda width: pl.BlockSpec((ROW_TILE, width), lambda i: (i, 0))
    const_spec = lambda shape: pl.BlockSpec(shape, lambda i: (0, 0))
    in_specs = [row_spec(k) for _ in mix_inputs]
    args = list(mix_inputs)
    if combine:
        in_specs += [row_spec(LANES) for _ in lses]
        args += list(lses)
    in_specs += [row_spec(D_MODEL), pl.BlockSpec((k, D_MODEL), lambda i: (0, 0), pipeline_mode=pl.Buffered(1)),
                 const_spec((1, D_MODEL)), const_spec((1, D_MODEL))]
    args += [x2, w_bf16, g.reshape(1, D_MODEL), b.reshape(1, D_MODEL)]
    return pl.pallas_call(
        functools.partial(_mix_out_kernel, combine=combine),
        grid=(t // ROW_TILE,),
        in_specs=in_specs,
        out_specs=row_spec(D_MODEL),
        out_shape=jax.ShapeDtypeStruct((t, D_MODEL), F32),
        compiler_params=pltpu.CompilerParams(dimension_semantics=("parallel",), vmem_limit_bytes=48 * MIB),
        name="mix_out_combine" if combine else "mix_out",
    )(*args)


def _router_kernel(x_ref, wr_ref, br_ref, mi_ref, mf_ref, cnt_ref, carry_s, *, tm):
    i = pl.program_id(0)

    @pl.when(i == 0)
    def _():
        carry_s[...] = jnp.zeros_like(carry_s)

    logits = _dot(x_ref[...], wr_ref[...], precision=HIGHEST) + br_ref[...]
    lt = logits.T
    l = [lt[e:e + 1, :] for e in range(N_EXPERTS)]
    mx = l[0]
    for e in range(1, N_EXPERTS):
        mx = jnp.maximum(mx, l[e])
    ex = [jnp.exp(v - mx) for v in l]
    tot = ex[0]
    for e in range(1, N_EXPERTS):
        tot = tot + ex[e]
    p = [v / tot for v in ex]

    def first_index_of(vals, target):
        idx = jnp.full_like(target, len(vals) - 1).astype(I32)
        for k in range(len(vals) - 2, -1, -1):
            idx = jnp.where(vals[k] == target, k, idx)
        return idx

    best = None
    for g in range(N_EXPERT_GROUPS):
        pg = p[g * EXPERTS_PER_GROUP:(g + 1) * EXPERTS_PER_GROUP]
        top1 = jnp.maximum(jnp.maximum(pg[0], pg[1]), jnp.maximum(pg[2], pg[3]))
        i1 = first_index_of(pg, top1)
        rest = [jnp.where(i1 == k, -1.0, pg[k]) for k in range(EXPERTS_PER_GROUP)]
        top2 = jnp.maximum(jnp.maximum(rest[0], rest[1]), jnp.maximum(rest[2], rest[3]))
        i2 = first_index_of(rest, top2)
        score = top1 + top2
        if best is None:
            best = (score, jnp.zeros_like(i1), top1, top2, i1, i2)
        else:
            better = score > best[0]
            cand = (score, jnp.full_like(i1, g), top1, top2, i1, i2)
            best = tuple(jnp.where(better, cv, bv) for cv, bv in zip(cand, best))
    _, g_sel, p1, p2, i1, i2 = best
    psum = p1 + p2
    gate1, gate2 = p1 / psum, p2 / psum
    first_low = i1 < i2
    lo = jnp.where(first_low, i1, i2)
    hi = jnp.where(first_low, i2, i1)
    gate_lo = jnp.where(first_low, gate1, gate2)
    gate_hi = jnp.where(first_low, gate2, gate1)
    pair = jnp.where(lo == 0, 0, jnp.where(lo == 1, 3, 5)) + hi - lo - 1
    cls = g_sel * len(PAIRS) + pair

    n_rows = carry_s.shape[0]
    sub = lax.broadcasted_iota(I32, (n_rows, tm), 0)
    onehot = sub == cls
    oh = jnp.where(onehot, 1.0, 0.0)
    upper = (lax.broadcasted_iota(I32, (tm, tm), 0) <= lax.broadcasted_iota(I32, (tm, tm), 1))
    cum = _dot(oh.astype(BF16), jnp.where(upper, 1.0, 0.0).astype(BF16))
    carry = carry_s[:, 0:1]
    rank = jnp.sum(jnp.where(onehot, cum - 1.0 + carry, 0.0), axis=0, keepdims=True)
    carry_new = carry + jnp.sum(oh, axis=1, keepdims=True)
    carry_s[...] = jnp.broadcast_to(carry_new, carry_s.shape)
    cnt_ref[...] = jnp.broadcast_to(carry_new, cnt_ref.shape)

    sub8 = lax.broadcasted_iota(I32, (8, tm), 0)
    mi_ref[...] = jnp.where(sub8 == 0, cls, jnp.where(sub8 == 1, rank.astype(I32), 0))
    mf_ref[...] = jnp.where(sub8 == 0, gate_lo, jnp.where(sub8 == 1, gate_hi, 0.0))


def _router(x2, wr_pad, br_pad):
    t = x2.shape[0]
    tm = ROW_TILE
    return pl.pallas_call(
        functools.partial(_router_kernel, tm=tm),
        grid=(t // tm,),
        in_specs=[
            pl.BlockSpec((tm, D_MODEL), lambda i: (i, 0)),
            pl.BlockSpec((D_MODEL, LANES), lambda i: (0, 0)),
            pl.BlockSpec((1, LANES), lambda i: (0, 0)),
        ],
        out_specs=[
            pl.BlockSpec((8, tm), lambda i: (0, i)),
            pl.BlockSpec((8, tm), lambda i: (0, i)),
            pl.BlockSpec((32, LANES), lambda i: (0, 0)),
        ],
        out_shape=[
            jax.ShapeDtypeStruct((8, t), I32),
            jax.ShapeDtypeStruct((8, t), F32),
            jax.ShapeDtypeStruct((32, LANES), F32),
        ],
        scratch_shapes=[pltpu.VMEM((32, LANES), F32)],
        compiler_params=pltpu.CompilerParams(dimension_semantics=("arbitrary",), vmem_limit_bytes=32 * MIB),
        name="moe_router",
    )(x2, wr_pad, br_pad)


def _dispatch_kernel(dest_ref, x_ref, buf_in, buf_hbm, sem, *, tm):
    del buf_in

    def issue(t, c):
        pltpu.make_async_copy(x_ref.at[pl.ds(t, 1)], buf_hbm.at[pl.ds(dest_ref[0, 0, t], 1)], sem).start()
        return c

    lax.fori_loop(0, tm, issue, 0)
    pltpu.make_async_copy(x_ref, buf_hbm.at[pl.ds(0, tm)], sem).wait()


def _dispatch(x2, dest, n_rows):
    t = x2.shape[0]
    tm = DISPATCH_TM
    buf0 = jnp.zeros((n_rows, D_MODEL), F32)
    return pl.pallas_call(
        functools.partial(_dispatch_kernel, tm=tm),
        grid=(t // tm,),
        in_specs=[
            pl.BlockSpec((1, 1, tm), lambda i: (i, 0, 0), memory_space=pltpu.SMEM),
            pl.BlockSpec((tm, D_MODEL), lambda i: (i, 0)),
            pl.BlockSpec(memory_space=pl.ANY),
        ],
        out_specs=pl.BlockSpec(memory_space=pl.ANY),
        out_shape=jax.ShapeDtypeStruct((n_rows, D_MODEL), F32),
        scratch_shapes=[pltpu.SemaphoreType.DMA(())],
        input_output_aliases={2: 0},
        compiler_params=pltpu.CompilerParams(
            dimension_semantics=("arbitrary",), has_side_effects=True, vmem_limit_bytes=32 * MIB),
        name="moe_dispatch",
    )(dest.reshape(t // tm, 1, tm), x2, buf0)


def _expert_kernel(elo_ref, ehi_ref, valid_ref, xb_ref, gs_ref, wg0, wu0, wd0, wg1, wu1, wd1, out_ref):
    del elo_ref, ehi_ref
    blk = pl.program_id(0)

    @pl.when(valid_ref[blk] > 0)
    def _():
        xb = xb_ref[...].astype(BF16)

        def ffn(wg, wu, wd):
            hg = _dot(xb, wg[0])
            hu = _dot(xb, wu[0])
            hidden = (hg * jax.nn.sigmoid(hg)) * hu
            return _dot(hidden.astype(BF16), wd[0])

        gs = gs_ref[...]
        out_ref[...] = gs[:, 0:1] * ffn(wg0, wu0, wd0) + gs[:, 1:2] * ffn(wg1, wu1, wd1)

    @pl.when(valid_ref[blk] == 0)
    def _():
        out_ref[...] = jnp.zeros_like(out_ref)


def _experts(buf, gates_sorted, blk_elo, blk_ehi, blk_valid, wg, wu, wd):
    n_rows = buf.shape[0]
    nb = n_rows // MOE_BLOCK
    lo_map = lambda b, elo, ehi, val: (elo[b], 0, 0)
    hi_map = lambda b, elo, ehi, val: (ehi[b], 0, 0)
    row_map = lambda b, elo, ehi, val: (b, 0)
    up_shape = (1, D_MODEL, D_EXPERT)
    down_shape = (1, D_EXPERT, D_MODEL)
    grid_spec = pltpu.PrefetchScalarGridSpec(
        num_scalar_prefetch=3,
        grid=(nb,),
        in_specs=[
            pl.BlockSpec((MOE_BLOCK, D_MODEL), row_map),
            pl.BlockSpec((MOE_BLOCK, LANES), row_map),
            pl.BlockSpec(up_shape, lo_map), pl.BlockSpec(up_shape, lo_map), pl.BlockSpec(down_shape, lo_map),
            pl.BlockSpec(up_shape, hi_map), pl.BlockSpec(up_shape, hi_map), pl.BlockSpec(down_shape, hi_map),
        ],
        out_specs=pl.BlockSpec((MOE_BLOCK, D_MODEL), row_map),
    )
    return pl.pallas_call(
        _expert_kernel,
        grid_spec=grid_spec,
        out_shape=jax.ShapeDtypeStruct((n_rows, D_MODEL), F32),
        compiler_params=pltpu.CompilerParams(dimension_semantics=("arbitrary",), vmem_limit_bytes=58 * MIB),
        name="moe_experts",
    )(blk_elo, blk_ehi, blk_valid, buf, gates_sorted, wg, wu, wd, wg, wu, wd)


def _gather_ln_kernel(dest_ref, x_ref, src_hbm, g_ref, b_ref, y_ref, rows_s, sem, *, tm):
    def issue(t, c):
        pltpu.make_async_copy(src_hbm.at[pl.ds(dest_ref[0, 0, t], 1)], rows_s.at[pl.ds(t, 1)], sem).start()
        return c

    lax.fori_loop(0, tm, issue, 0)
    pltpu.make_async_copy(src_hbm.at[pl.ds(0, tm)], rows_s, sem).wait()
    y = ALPHA * x_ref[...] + rows_s[...]
    y_ref[...] = _layer_norm_rows(y, g_ref[...], b_ref[...])


def _gather_ln(x2, expert_out, dest, g, b):
    t = x2.shape[0]
    tm = GATHER_TM
    return pl.pallas_call(
        functools.partial(_gather_ln_kernel, tm=tm),
        grid=(t // tm,),
        in_specs=[
            pl.BlockSpec((1, 1, tm), lambda i: (i, 0, 0), memory_space=pltpu.SMEM),
            pl.BlockSpec((tm, D_MODEL), lambda i: (i, 0)),
            pl.BlockSpec(memory_space=pl.ANY),
            pl.BlockSpec((1, D_MODEL), lambda i: (0, 0)),
            pl.BlockSpec((1, D_MODEL), lambda i: (0, 0)),
        ],
        out_specs=pl.BlockSpec((tm, D_MODEL), lambda i: (i, 0)),
        out_shape=jax.ShapeDtypeStruct((t, D_MODEL), F32),
        scratch_shapes=[pltpu.VMEM((tm, D_MODEL), F32), pltpu.SemaphoreType.DMA(())],
        compiler_params=pltpu.CompilerParams(dimension_semantics=("arbitrary",), vmem_limit_bytes=32 * MIB),
        name="moe_gather_ln",
    )(dest.reshape(t // tm, 1, tm), x2, expert_out, g.reshape(1, D_MODEL), b.reshape(1, D_MODEL))


_CLASS_LO = np.array([4 * (c // 6) + PAIRS[c % 6][0] for c in range(N_CLASSES)], np.int32)
_CLASS_HI = np.array([4 * (c // 6) + PAIRS[c % 6][1] for c in range(N_CLASSES)], np.int32)


def _moe_layer(x2, wr_pad, br_pad, wg, wu, wd, ln_g, ln_b):
    t = x2.shape[0]
    nb = -(-(t + N_CLASSES * (MOE_BLOCK - 1)) // MOE_BLOCK)
    n_rows = nb * MOE_BLOCK
    meta_i, meta_f, cnt = _router(x2, wr_pad, br_pad)
    cls, rank = meta_i[0], meta_i[1]
    counts = cnt[:N_CLASSES, 0].astype(I32)
    padded = (counts + MOE_BLOCK - 1) // MOE_BLOCK * MOE_BLOCK
    ends = jnp.cumsum(padded)
    starts = ends - padded
    dest = starts[cls] + rank
    blk_start = jnp.arange(nb, dtype=I32) * MOE_BLOCK
    blk_valid = (blk_start < ends[-1]).astype(I32)
    n_valid = jnp.sum(blk_valid)
    blk_cls = jnp.minimum(jnp.sum((ends[None, :] <= blk_start[:, None]).astype(I32), axis=1), N_CLASSES - 1)
    blk_cls = blk_cls[jnp.minimum(jnp.arange(nb), n_valid - 1)]
    blk_elo = jnp.asarray(_CLASS_LO)[blk_cls]
    blk_ehi = jnp.asarray(_CLASS_HI)[blk_cls]
    gates_tok = jnp.pad(meta_f[:2].T, ((0, 0), (0, LANES - 2)))
    gates_sorted = jnp.zeros((n_rows, LANES), F32).at[dest].set(gates_tok)
    buf = _dispatch(x2, dest, n_rows)
    out = _experts(buf, gates_sorted, blk_elo, blk_ehi, blk_valid, wg, wu, wd)
    return _gather_ln(x2, out, dest, ln_g, ln_b)


def _inproj_kernel(x_ref, w_ref, wgate_ref, z_ref, gates_ref, xb_s):
    @pl.when(pl.program_id(1) == 0)
    def _():
        xb_s[...] = x_ref[...].astype(BF16)
        gates_ref[...] = _dot(xb_s[...], wgate_ref[...])

    z_ref[...] = _dot(xb_s[...], w_ref[...]).astype(z_ref.dtype)


def _inproj(x2, w_main_bf16, w_gate_bf16):
    t = x2.shape[0]
    tm, tn = INPROJ_TM, INPROJ_TN
    n_main = w_main_bf16.shape[1]
    return pl.pallas_call(
        _inproj_kernel,
        grid=(t // tm, n_main // tn),
        in_specs=[
            pl.BlockSpec((tm, D_MODEL), lambda m, n: (m, 0)),
            pl.BlockSpec((D_MODEL, tn), lambda m, n: (0, n)),
            pl.BlockSpec((D_MODEL, LANES), lambda m, n: (0, 0)),
        ],
        out_specs=[
            pl.BlockSpec((tm, tn), lambda m, n: (m, n)),
            pl.BlockSpec((tm, LANES), lambda m, n: (m, 0)),
        ],
        out_shape=[
            jax.ShapeDtypeStruct((t, n_main), BF16),
            jax.ShapeDtypeStruct((t, LANES), F32),
        ],
        scratch_shapes=[pltpu.VMEM((tm, D_MODEL), BF16)],
        compiler_params=pltpu.CompilerParams(
            dimension_semantics=("parallel", "arbitrary"), vmem_limit_bytes=48 * MIB),
        name="mlstm_inproj",
    )(x2, w_main_bf16, w_gate_bf16)


def _mlstm_cell_kernel(zq_ref, zk_ref, v_ref, op_ref, gt_ref, bg_ref, cwq_ref, cbq_ref, cwk_ref, cbk_ref,
                       ng_ref, y_ref, q_s, k_s, gb_s, bc_s, gbt_s, bct_s, *, chunk):
    head = pl.program_id(1)
    seq = zq_ref.shape[0]

    def conv_silu(z_ref, cw_ref, cb_ref):
        z = z_ref[...].astype(F32)
        rowi = lax.broadcasted_iota(I32, z.shape, 0)
        out = cb_ref[...] + cw_ref[0:1, :] * jnp.where(rowi >= CONV_K - 1, pltpu.roll(z, CONV_K - 1, 0), 0.0)
        for j in range(1, CONV_K - 1):
            shift = CONV_K - 1 - j
            out = out + cw_ref[j:j + 1, :] * jnp.where(rowi >= shift, pltpu.roll(z, shift, 0), 0.0)
        out = out + cw_ref[CONV_K - 1:CONV_K, :] * z
        return out * jax.nn.sigmoid(out)

    q_s[...] = conv_silu(zq_ref, cwq_ref, cbq_ref).astype(BF16)
    k_s[...] = conv_silu(zk_ref, cwk_ref, cbk_ref) * (QK_DIM ** -0.5)

    lane = lax.broadcasted_iota(I32, (chunk, LANES), 1)
    sub = lax.broadcasted_iota(I32, (MLSTM_HEADS, chunk), 0)
    ti = lax.broadcasted_iota(I32, (chunk, chunk), 0)
    si = lax.broadcasted_iota(I32, (chunk, chunk), 1)
    causal = ti >= si
    ones_cols = jnp.ones((chunk, LANES), BF16)

    @pl.when(head == 0)
    def _():
        tri = jnp.where(causal, 1.0, 0.0)
        bias = bg_ref[...]
        for c in range(seq // chunk):
            rs = slice(c * chunk, (c + 1) * chunk)
            gb = gt_ref[rs, :] + bias
            log_f = -(jnp.maximum(-gb, 0.0) + jnp.log1p(jnp.exp(-jnp.abs(gb))))
            bcum = _dot(tri, log_f, precision=HIGHEST)
            gb_s[rs, :] = gb
            bc_s[rs, :] = bcum
            gbt_s[:, rs] = gb.T
            bct_s[:, rs] = bcum.T

    def pick_col(a, idx):
        return jnp.sum(jnp.where(lane == idx, a, 0.0), axis=1, keepdims=True)

    def pick_row(a):
        return jnp.sum(jnp.where(sub == head, a, 0.0), axis=0, keepdims=True)

    c_state = jnp.zeros((QK_DIM, V_DIM + LANES), F32)
    m_state = jnp.full((1, 1), NEG, F32)
    for c in range(seq // chunk):
        rs = slice(c * chunk, (c + 1) * chunk)
        li_col = pick_col(gb_s[rs, :], head)
        bc_col = pick_col(bc_s[rs, :], head + MLSTM_HEADS)
        li_row = pick_row(gbt_s[0:MLSTM_HEADS, rs])
        bc_row = pick_row(bct_s[MLSTM_HEADS:2 * MLSTM_HEADS, rs])
        dmat = jnp.where(causal, bc_col + (li_row - bc_row), NEG)
        inter = bc_col + m_state
        m_t = jnp.maximum(inter, jnp.max(dmat, axis=1, keepdims=True))
        qc = q_s[rs, :]
        kf = k_s[rs, :]
        vc = v_ref[rs, :]
        a = _dot_nt(qc, kf.astype(BF16)) * jnp.exp(dmat - m_t)
        w_inter = jnp.exp(inter - m_t)
        q_state = _dot(qc, c_state.astype(BF16))
        num = _dot(a.astype(BF16), vc) + w_inter * q_state[:, :V_DIM]
        den = jnp.sum(a, axis=1, keepdims=True) + w_inter * q_state[:, V_DIM:V_DIM + 1]
        h_out = num / jnp.maximum(jnp.abs(den), jnp.exp(-m_t))
        mu = jnp.mean(h_out, axis=1, keepdims=True)
        hc = h_out - mu
        var = jnp.mean(hc * hc, axis=1, keepdims=True)
        hn = hc * lax.rsqrt(var + LN_EPS) * ng_ref[...]
        y_ref[rs, :] = (hn * jax.nn.sigmoid(op_ref[rs, :].astype(F32))).astype(y_ref.dtype)
        b_last = bc_col[chunk - 1:chunk, :]
        g = b_last - bc_col + li_col
        m_new = jnp.maximum(b_last + m_state, jnp.max(g, axis=0, keepdims=True))
        wk = jnp.exp(g - m_new)
        decay = jnp.exp(b_last + m_state - m_new)
        kw_t = (wk * kf).T.astype(BF16)
        v_ext = jnp.concatenate([vc, ones_cols], axis=1)
        c_state = decay * c_state + _dot(kw_t, v_ext)
        m_state = m_new


def _mlstm_cell(z, gates, b_gates_pad, conv_w, conv_b, norm_g, batch):
    t = z.shape[0]
    hq = MLSTM_HEADS
    v_blk0 = MLSTM_QK_COLS // V_DIM
    o_blk0 = (MLSTM_QK_COLS + MLSTM_V_COLS) // V_DIM
    return pl.pallas_call(
        functools.partial(_mlstm_cell_kernel, chunk=MLSTM_CHUNK),
        grid=(batch, MLSTM_HEADS),
        in_specs=[
            pl.BlockSpec((SEQ, QK_DIM), lambda b, h: (b, h)),
            pl.BlockSpec((SEQ, QK_DIM), lambda b, h: (b, hq + h)),
            pl.BlockSpec((SEQ, V_DIM), lambda b, h: (b, v_blk0 + h)),
            pl.BlockSpec((SEQ, V_DIM), lambda b, h: (b, o_blk0 + h)),
            pl.BlockSpec((SEQ, LANES), lambda b, h: (b, 0)),
            pl.BlockSpec((1, LANES), lambda b, h: (0, 0)),
            pl.BlockSpec((CONV_K, QK_DIM), lambda b, h: (0, h)),
            pl.BlockSpec((1, QK_DIM), lambda b, h: (0, h)),
            pl.BlockSpec((CONV_K, QK_DIM), lambda b, h: (0, hq + h)),
            pl.BlockSpec((1, QK_DIM), lambda b, h: (0, hq + h)),
            pl.BlockSpec((1, V_DIM), lambda b, h: (0, h)),
        ],
        out_specs=pl.BlockSpec((SEQ, V_DIM), lambda b, h: (b, h)),
        out_shape=jax.ShapeDtypeStruct((t, MLSTM_V_COLS), BF16),
        scratch_shapes=[pltpu.VMEM((SEQ, QK_DIM), BF16), pltpu.VMEM((SEQ, QK_DIM), F32),
                        pltpu.VMEM((SEQ, LANES), F32), pltpu.VMEM((SEQ, LANES), F32),
                        pltpu.VMEM((LANES, SEQ), F32), pltpu.VMEM((LANES, SEQ), F32)],
        compiler_params=pltpu.CompilerParams(
            dimension_semantics=("parallel", "arbitrary"), vmem_limit_bytes=40 * MIB),
        name="mlstm_cell",
    )(z, z, z, z, gates, b_gates_pad, conv_w, conv_b.reshape(1, -1), conv_w, conv_b.reshape(1, -1),
      norm_g.reshape(1, -1))


def kernel(x, attn_w_qkv, attn_w_o, mlstm_w_in, mlstm_b_gates, mlstm_conv_w, mlstm_conv_b, mlstm_norm_g,
           mlstm_w_out, ln_mix_g, ln_mix_b, ln_ffn_g, ln_ffn_b, router_w, router_b, moe_w_gate, moe_w_up,
           moe_w_down):
    batch, seq, d = x.shape
    assert (seq, d) == (SEQ, D_MODEL)
    t = batch * seq
    x2 = x.reshape(t, d)

    wr_pad = jnp.zeros((D_MODEL, LANES), F32).at[:, :N_EXPERTS].set(router_w)
    br_pad = jnp.zeros((1, LANES), F32).at[0, :N_EXPERTS].set(router_b)
    wg_bf, wu_bf, wd_bf = moe_w_gate.astype(BF16), moe_w_up.astype(BF16), moe_w_down.astype(BF16)

    w_qkv = attn_w_qkv[0].astype(BF16)[:, _rope_permutation()]
    x_bf = x.astype(BF16)
    outs, lses = [], []
    for group, dil in enumerate(DILATIONS):
        o, lse = _attn_group(x_bf, w_qkv, _rope_table(dil), group, dil)
        outs.append(o)
        lses.append(lse)
    x2 = _mix_out(outs, lses, x2, attn_w_o[0].astype(BF16), ln_mix_g[0], ln_mix_b[0])
    x2 = _moe_layer(x2, wr_pad, br_pad, wg_bf[0], wu_bf[0], wd_bf[0], ln_ffn_g[0], ln_ffn_b[0])

    w_in = mlstm_w_in[0]
    w_main = w_in[:, :MLSTM_MAIN_COLS].astype(BF16)
    w_gate = jnp.zeros((D_MODEL, LANES), F32).at[:, :2 * MLSTM_HEADS].set(w_in[:, MLSTM_MAIN_COLS:]).astype(BF16)
    bg_pad = jnp.zeros((1, LANES), F32).at[0, :2 * MLSTM_HEADS].set(mlstm_b_gates[0])
    z, gates = _inproj(x2, w_main, w_gate)
    y = _mlstm_cell(z, gates, bg_pad, mlstm_conv_w[0], mlstm_conv_b[0], mlstm_norm_g[0], batch)
    x2 = _mix_out([y], None, x2, mlstm_w_out[0].astype(BF16), ln_mix_g[1], ln_mix_b[1])
    x2 = _moe_layer(x2, wr_pad, br_pad, wg_bf[1], wu_bf[1], wd_bf[1], ln_ffn_g[1], ln_ffn_b[1])
    return x2.reshape(batch, seq, d)
```

```python
import functools

import numpy as np
import jax
import jax.numpy as jnp
from jax import lax
from jax.experimental import pallas as pl
from jax.experimental.pallas import tpu as pltpu

F32 = jnp.float32
BF16 = jnp.bfloat16
I32 = jnp.int32
HIGHEST = lax.Precision.HIGHEST

D_MODEL = 2048
SEQ = 2048
DEPTH = 2
DILATIONS = (1, 4, 16)
N_BACK = 128
ATTN_HEADS = 8
HEAD_DIM = 128
ROT_DIM = HEAD_DIM // 4
ROPE_THETA = 500000.0
ATTN_BLOCK = 128
ATTN_COLS = ATTN_HEADS * HEAD_DIM

MLSTM_HEADS = 8
QK_DIM = 128
V_DIM = D_MODEL // MLSTM_HEADS
CONV_K = 4
MLSTM_QK_COLS = 2 * MLSTM_HEADS * QK_DIM
MLSTM_V_COLS = MLSTM_HEADS * V_DIM
MLSTM_MAIN_COLS = MLSTM_QK_COLS + 2 * MLSTM_V_COLS

N_EXPERTS = 16
N_EXPERT_GROUPS = 4
EXPERTS_PER_GROUP = 4
D_EXPERT = 768
PAIRS = ((0, 1), (0, 2), (0, 3), (1, 2), (1, 3), (2, 3))
N_CLASSES = N_EXPERT_GROUPS * len(PAIRS)

ALPHA = (2 * DEPTH) ** 0.25
LN_EPS = 1e-5
NEG = -1e30

LANES = 128
V7X_VMEM_BYTES = 64 * 1024 * 1024
MIB = 1024 * 1024

ATTN_STEP_ROWS = 512
ATTN_HEAD_SET = 8
ROPE_PAIR_SHIFT = 64
ROW_TILE = 512
INPROJ_TM = 1024
INPROJ_TN = 1024
MLSTM_CHUNK = 128
MLSTM_HEADS_PER_STEP = 4
MOE_BLOCK = 256
DISPATCH_TM = 512
GATHER_TM = 256


def _dot(a, b, **kw):
    return jnp.dot(a, b, preferred_element_type=F32, **kw)


def _dot_nt(a, b):
    return lax.dot_general(a, b, (((1,), (1,)), ((), ())), preferred_element_type=F32)


def _layer_norm_rows(y, g, b):
    mu = jnp.mean(y, axis=-1, keepdims=True)
    yc = y - mu
    var = jnp.mean(yc * yc, axis=-1, keepdims=True)
    return yc * lax.rsqrt(var + LN_EPS) * g + b


def _attn_group_kernel(x_ref, w_ref, tab_ref, o_ref, lse_ref, q_s, k_s, v_s, *, n_cls, lc, carry):
    step = pl.program_id(1)
    rows_total = n_cls * lc
    blocks_per_class = lc // ATTN_BLOCK

    if n_cls == 1:
        xs = x_ref[0]
    else:
        xs = jnp.concatenate([x_ref[0, :, c * D_MODEL:(c + 1) * D_MODEL] for c in range(n_cls)], axis=0)
    qkv = _dot(xs.astype(BF16), w_ref[...])

    if carry:
        @pl.when(step == 0)
        def _():
            k_s[0:ATTN_BLOCK, :] = jnp.zeros((ATTN_BLOCK, ATTN_COLS), BF16)
            v_s[0:ATTN_BLOCK, :] = jnp.zeros((ATTN_BLOCK, ATTN_COLS), BF16)

        @pl.when(step > 0)
        def _():
            k_s[0:ATTN_BLOCK, :] = k_s[rows_total:rows_total + ATTN_BLOCK, :]
            v_s[0:ATTN_BLOCK, :] = v_s[rows_total:rows_total + ATTN_BLOCK, :]

    cosf = tab_ref[:, 0:LANES]
    sinr = tab_ref[:, LANES:2 * LANES]

    def rope(t):
        return t * cosf + pltpu.roll(t, ROPE_PAIR_SHIFT, 1) * sinr

    for h in range(ATTN_HEADS):
        cs = slice(h * HEAD_DIM, (h + 1) * HEAD_DIM)
        q_s[:, cs] = rope(qkv[:, h * HEAD_DIM:(h + 1) * HEAD_DIM]).astype(BF16)
        k_s[ATTN_BLOCK:, cs] = rope(qkv[:, ATTN_COLS + h * HEAD_DIM:ATTN_COLS + (h + 1) * HEAD_DIM]).astype(BF16)
    v_s[ATTN_BLOCK:, :] = qkv[:, 2 * ATTN_COLS:3 * ATTN_COLS].astype(BF16)

    row = lax.broadcasted_iota(I32, (ATTN_BLOCK, ATTN_BLOCK), 0)
    col = lax.broadcasted_iota(I32, (ATTN_BLOCK, ATTN_BLOCK), 1)
    row2 = lax.broadcasted_iota(I32, (ATTN_BLOCK, 2 * ATTN_BLOCK), 0)
    col2 = lax.broadcasted_iota(I32, (ATTN_BLOCK, 2 * ATTN_BLOCK), 1)
    lane = lax.broadcasted_iota(I32, (ATTN_BLOCK, LANES), 1)
    mask_cur = col <= row
    mask_both = jnp.logical_and(col2 >= row2, col2 <= row2 + N_BACK)
    if carry:
        mask_first = jnp.logical_and(mask_both, jnp.logical_or(col2 >= ATTN_BLOCK, step > 0))
    scale = HEAD_DIM ** -0.5

    for c in range(n_cls):
        for bi in range(blocks_per_class):
            j = c * blocks_per_class + bi
            ors = slice(bi * ATTN_BLOCK, (bi + 1) * ATTN_BLOCK)
            with_prev = bi > 0 or carry
            if with_prev:
                krows = slice(j * ATTN_BLOCK, (j + 2) * ATTN_BLOCK)
                mask = mask_first if bi == 0 else mask_both
            else:
                krows = slice((j + 1) * ATTN_BLOCK, (j + 2) * ATTN_BLOCK)
                mask = mask_cur
            lse_tile = jnp.zeros((ATTN_BLOCK, LANES), F32)
            for h0 in range(0, ATTN_HEADS, ATTN_HEAD_SET):
                heads = range(h0, h0 + ATTN_HEAD_SET)
                cols = [slice(h * HEAD_DIM, (h + 1) * HEAD_DIM) for h in heads]
                scores = [jnp.where(mask, _dot_nt(q_s[j * ATTN_BLOCK:(j + 1) * ATTN_BLOCK, cs], k_s[krows, cs])
                                    * scale, NEG) for cs in cols]
                maxes = [jnp.max(s, axis=1, keepdims=True) for s in scores]
                probs = [jnp.exp(s - m) for s, m in zip(scores, maxes)]
                dens = [jnp.sum(p, axis=1, keepdims=True) for p in probs]
                accs = [_dot(p.astype(BF16), v_s[krows, cs]) for p, cs in zip(probs, cols)]
                for h, acc, den, m in zip(heads, accs, dens, maxes):
                    o_ref[0, ors, c * ATTN_COLS + h * HEAD_DIM:c * ATTN_COLS + (h + 1) * HEAD_DIM] = (
                        acc / den).astype(o_ref.dtype)
                    lse_tile = jnp.where(lane == h, m + jnp.log(den), lse_tile)
            lse_ref[0, ors, c * LANES:(c + 1) * LANES] = lse_tile


def _class_major_perm(dil):
    width = ATTN_STEP_ROWS // dil
    perm = np.zeros((ATTN_STEP_ROWS, ATTN_STEP_ROWS), np.float32)
    for r in range(dil):
        for m in range(width):
            perm[r * width + m, m * dil + r] = 1.0
    return perm


def _class_major_kernel(x_ref, *refs):
    n = len(DILATIONS) - 1
    xb = x_ref[0].astype(BF16)
    for p_ref, out_ref, dil in zip(refs[:n], refs[n:], DILATIONS[1:]):
        width = ATTN_STEP_ROWS // dil
        rows = _dot(p_ref[...], xb).astype(BF16)
        for r in range(dil):
            out_ref[0, :, r * D_MODEL:(r + 1) * D_MODEL] = rows[r * width:(r + 1) * width, :]


def _class_major_views(x3):
    batch = x3.shape[0]
    perms = [jnp.asarray(_class_major_perm(d), BF16) for d in DILATIONS[1:]]
    return pl.pallas_call(
        _class_major_kernel,
        grid=(batch, SEQ // ATTN_STEP_ROWS),
        in_specs=[pl.BlockSpec((1, ATTN_STEP_ROWS, D_MODEL), lambda b, s: (b, s, 0))] + [
            pl.BlockSpec((ATTN_STEP_ROWS, ATTN_STEP_ROWS), lambda b, s: (0, 0)) for _ in perms],
        out_specs=[pl.BlockSpec((1, ATTN_STEP_ROWS // d, d * D_MODEL), lambda b, s: (b, s, 0)) for d in DILATIONS[1:]],
        out_shape=[jax.ShapeDtypeStruct((batch, SEQ // d, d * D_MODEL), BF16) for d in DILATIONS[1:]],
        compiler_params=pltpu.CompilerParams(dimension_semantics=("parallel", "parallel"), vmem_limit_bytes=40 * MIB),
        name="class_major_views",
    )(x3, *perms)


def _attn_group(xg, w_qkv_bf16, tab, group, dil):
    batch = xg.shape[0]
    per_class = SEQ // dil
    lc = min(ATTN_STEP_ROWS, per_class)
    n_cls = ATTN_STEP_ROWS // lc
    steps = SEQ // ATTN_STEP_ROWS
    carry = dil == 1
    if dil == 1:
        imap = lambda b, s: (b, s, 0)
    else:
        imap = lambda b, s: (b, 0, s)
    kern = functools.partial(_attn_group_kernel, n_cls=n_cls, lc=lc, carry=carry)
    o, lse = pl.pallas_call(
        kern,
        grid=(batch, steps),
        in_specs=[
            pl.BlockSpec((1, lc, n_cls * D_MODEL), imap),
            pl.BlockSpec((D_MODEL, 3 * ATTN_COLS), lambda b, s: (0, group), pipeline_mode=pl.Buffered(1)),
            pl.BlockSpec((ATTN_STEP_ROWS, 2 * LANES), lambda b, s: (s, 0)),
        ],
        out_specs=[
            pl.BlockSpec((1, lc, n_cls * ATTN_COLS), imap),
            pl.BlockSpec((1, lc, n_cls * LANES), imap),
        ],
        out_shape=[
            jax.ShapeDtypeStruct((batch, per_class, dil * ATTN_COLS), BF16),
            jax.ShapeDtypeStruct((batch, per_class, dil * LANES), F32),
        ],
        scratch_shapes=[
            pltpu.VMEM((ATTN_STEP_ROWS, ATTN_COLS), BF16),
            pltpu.VMEM((ATTN_BLOCK + ATTN_STEP_ROWS, ATTN_COLS), BF16),
            pltpu.VMEM((ATTN_BLOCK + ATTN_STEP_ROWS, ATTN_COLS), BF16),
        ],
        compiler_params=pltpu.CompilerParams(
            dimension_semantics=("parallel", "arbitrary"), vmem_limit_bytes=56 * MIB),
        name=f"attn_group{group}",
    )(xg, w_qkv_bf16, tab)
    return o, lse


def _rope_permute_qk(w_qkv):
    half = ROT_DIM // 2
    lo, hi = ROPE_PAIR_SHIFT, ROPE_PAIR_SHIFT + half
    w = w_qkv.reshape(D_MODEL, len(DILATIONS), 3, ATTN_HEADS, HEAD_DIM)
    qk = w[:, :, 0:2]
    qk = jnp.concatenate([qk[..., :half], qk[..., lo:hi], qk[..., ROT_DIM:lo], qk[..., half:ROT_DIM], qk[..., hi:]], -1)
    return jnp.concatenate([qk, w[:, :, 2:3]], axis=2).reshape(w_qkv.shape)


def _rope_table(dil):
    inv_freq = ROPE_THETA ** (-jnp.arange(0, ROT_DIM, 2, dtype=F32) / ROT_DIM)
    ang = jnp.arange(SEQ, dtype=F32)[:, None] * inv_freq[None, :]
    ang = jnp.concatenate([ang, ang], -1)
    cos, sin = jnp.cos(ang), jnp.sin(ang)
    half = ROT_DIM // 2
    gap = ROPE_PAIR_SHIFT - half
    tail = LANES - ROPE_PAIR_SHIFT - half
    cosf = jnp.concatenate([cos[:, :half], jnp.ones((SEQ, gap), F32), cos[:, half:], jnp.ones((SEQ, tail), F32)], 1)
    sinr = jnp.concatenate([-sin[:, :half], jnp.zeros((SEQ, gap), F32), sin[:, half:], jnp.zeros((SEQ, tail), F32)], 1)
    tab = jnp.concatenate([cosf, sinr], 1)
    return tab.reshape(SEQ // dil, dil, 2 * LANES).transpose(1, 0, 2).reshape(SEQ, 2 * LANES)


def _attn_out_kernel(o0_ref, o1_ref, o2_ref, l0_ref, l1_ref, l2_ref, pt1_ref, pt2_ref, x_ref, w_ref, g_ref, b_ref,
                     out_ref, l1_s, l2_s):
    o_nat = [None]
    for o_ref, l_ref, pt_ref, l_s, dil in ((o1_ref, l1_ref, pt1_ref, l1_s, DILATIONS[1]),
                                          (o2_ref, l2_ref, pt2_ref, l2_s, DILATIONS[2])):
        width = ATTN_STEP_ROWS // dil
        o_cm = jnp.concatenate([o_ref[0, :, r * ATTN_COLS:(r + 1) * ATTN_COLS] for r in range(dil)], axis=0)
        o_nat.append(_dot(pt_ref[...], o_cm))
        for r in range(dil):
            l_s[pl.ds(r, width, stride=dil), :] = l_ref[0, :, r * LANES:(r + 1) * LANES]
    ls = [l0_ref[0], l1_s[...], l2_s[...]]
    mx = jnp.maximum(jnp.maximum(ls[0], ls[1]), ls[2])
    es = [jnp.exp(l - mx) for l in ls]
    den = es[0] + es[1] + es[2]
    ws = [e / den for e in es]
    parts = []
    for h in range(ATTN_HEADS):
        cs = slice(h * HEAD_DIM, (h + 1) * HEAD_DIM)
        acc = ws[0][:, h:h + 1] * o0_ref[0, :, cs].astype(F32)
        acc = acc + ws[1][:, h:h + 1] * o_nat[1][:, cs]
        acc = acc + ws[2][:, h:h + 1] * o_nat[2][:, cs]
        parts.append(acc)
    mixed_in = jnp.concatenate(parts, axis=1).astype(BF16)
    y = ALPHA * x_ref[...] + _dot(mixed_in, w_ref[...])
    out_ref[...] = _layer_norm_rows(y, g_ref[...], b_ref[...])


def _attn_out(outs, lses, x2, w_bf16, g, b):
    t = x2.shape[0]
    steps = SEQ // ATTN_STEP_ROWS
    view_map = lambda i: (i // steps, i % steps, 0)
    view_spec = lambda dil, width: pl.BlockSpec((1, ATTN_STEP_ROWS // dil, dil * width), view_map)
    const_spec = lambda shape: pl.BlockSpec(shape, lambda i: (0, 0))
    row_spec = pl.BlockSpec((ATTN_STEP_ROWS, D_MODEL), lambda i: (i, 0))
    perms_t = [jnp.asarray(_class_major_perm(d).T, BF16) for d in DILATIONS[1:]]
    return pl.pallas_call(
        _attn_out_kernel,
        grid=(t // ATTN_STEP_ROWS,),
        in_specs=[view_spec(d, ATTN_COLS) for d in DILATIONS] + [view_spec(d, LANES) for d in DILATIONS] + [
            const_spec((ATTN_STEP_ROWS, ATTN_STEP_ROWS)) for _ in perms_t] + [
            row_spec, pl.BlockSpec((ATTN_COLS, D_MODEL), lambda i: (0, 0), pipeline_mode=pl.Buffered(1)),
            const_spec((1, D_MODEL)), const_spec((1, D_MODEL))],
        out_specs=row_spec,
        out_shape=jax.ShapeDtypeStruct((t, D_MODEL), F32),
        scratch_shapes=[pltpu.VMEM((ATTN_STEP_ROWS, LANES), F32), pltpu.VMEM((ATTN_STEP_ROWS, LANES), F32)],
        compiler_params=pltpu.CompilerParams(dimension_semantics=("parallel",), vmem_limit_bytes=48 * MIB),
        name="attn_out",
    )(*outs, *lses, *perms_t, x2, w_bf16, g.reshape(1, D_MODEL), b.reshape(1, D_MODEL))


def _mix_out_kernel(y_ref, x_ref, w_ref, g_ref, b_ref, out_ref):
    y = ALPHA * x_ref[...] + _dot(y_ref[...], w_ref[...])
    out_ref[...] = _layer_norm_rows(y, g_ref[...], b_ref[...])


def _mix_out(mixer_y, x2, w_bf16, g, b):
    t = x2.shape[0]
    k = w_bf16.shape[0]
    row_spec = lambda width: pl.BlockSpec((ROW_TILE, width), lambda i: (i, 0))
    const_spec = lambda shape: pl.BlockSpec(shape, lambda i: (0, 0))
    return pl.pallas_call(
        _mix_out_kernel,
        grid=(t // ROW_TILE,),
        in_specs=[row_spec(k), row_spec(D_MODEL),
                  pl.BlockSpec((k, D_MODEL), lambda i: (0, 0), pipeline_mode=pl.Buffered(1)),
                  const_spec((1, D_MODEL)), const_spec((1, D_MODEL))],
        out_specs=row_spec(D_MODEL),
        out_shape=jax.ShapeDtypeStruct((t, D_MODEL), F32),
        compiler_params=pltpu.CompilerParams(dimension_semantics=("parallel",), vmem_limit_bytes=48 * MIB),
        name="mix_out",
    )(mixer_y, x2, w_bf16, g.reshape(1, D_MODEL), b.reshape(1, D_MODEL))


def _router_kernel(x_ref, wr_ref, br_ref, mi_ref, mf_ref, cnt_ref, carry_s, *, tm):
    i = pl.program_id(0)

    @pl.when(i == 0)
    def _():
        carry_s[...] = jnp.zeros_like(carry_s)

    x = x_ref[...]
    x_hi = x.astype(BF16)
    x_lo = (x - x_hi.astype(F32)).astype(BF16)
    hi_terms = _dot(x_hi, wr_ref[...])
    logits = (hi_terms[:, :LANES] + _dot(x_lo, wr_ref[:, :LANES])) + hi_terms[:, LANES:] + br_ref[...]
    lt = logits.T
    l = [lt[e:e + 1, :] for e in range(N_EXPERTS)]
    mx = l[0]
    for e in range(1, N_EXPERTS):
        mx = jnp.maximum(mx, l[e])
    ex = [jnp.exp(v - mx) for v in l]
    tot = ex[0]
    for e in range(1, N_EXPERTS):
        tot = tot + ex[e]
    p = [v / tot for v in ex]

    def first_index_of(vals, target):
        idx = jnp.full_like(target, len(vals) - 1).astype(I32)
        for k in range(len(vals) - 2, -1, -1):
            idx = jnp.where(vals[k] == target, k, idx)
        return idx

    best = None
    for g in range(N_EXPERT_GROUPS):
        pg = p[g * EXPERTS_PER_GROUP:(g + 1) * EXPERTS_PER_GROUP]
        top1 = jnp.maximum(jnp.maximum(pg[0], pg[1]), jnp.maximum(pg[2], pg[3]))
        i1 = first_index_of(pg, top1)
        rest = [jnp.where(i1 == k, -1.0, pg[k]) for k in range(EXPERTS_PER_GROUP)]
        top2 = jnp.maximum(jnp.maximum(rest[0], rest[1]), jnp.maximum(rest[2], rest[3]))
        i2 = first_index_of(rest, top2)
        score = top1 + top2
        if best is None:
            best = (score, jnp.zeros_like(i1), top1, top2, i1, i2)
        else:
            better = score > best[0]
            cand = (score, jnp.full_like(i1, g), top1, top2, i1, i2)
            best = tuple(jnp.where(better, cv, bv) for cv, bv in zip(cand, best))
    _, g_sel, p1, p2, i1, i2 = best
    psum = p1 + p2
    gate1, gate2 = p1 / psum, p2 / psum
    first_low = i1 < i2
    lo = jnp.where(first_low, i1, i2)
    hi = jnp.where(first_low, i2, i1)
    gate_lo = jnp.where(first_low, gate1, gate2)
    gate_hi = jnp.where(first_low, gate2, gate1)
    pair = jnp.where(lo == 0, 0, jnp.where(lo == 1, 3, 5)) + hi - lo - 1
    cls = g_sel * len(PAIRS) + pair

    n_rows = carry_s.shape[0]
    sub = lax.broadcasted_iota(I32, (n_rows, tm), 0)
    onehot = sub == cls
    oh = jnp.where(onehot, 1.0, 0.0)
    upper = (lax.broadcasted_iota(I32, (tm, tm), 0) <= lax.broadcasted_iota(I32, (tm, tm), 1))
    cum = _dot(oh.astype(BF16), jnp.where(upper, 1.0, 0.0).astype(BF16))
    carry = carry_s[:, 0:1]
    rank = jnp.sum(jnp.where(onehot, cum - 1.0 + carry, 0.0), axis=0, keepdims=True)
    carry_new = carry + jnp.sum(oh, axis=1, keepdims=True)
    carry_s[...] = jnp.broadcast_to(carry_new, carry_s.shape)
    cnt_ref[...] = jnp.broadcast_to(carry_new, cnt_ref.shape)

    sub8 = lax.broadcasted_iota(I32, (8, tm), 0)
    mi_ref[...] = jnp.where(sub8 == 0, cls, jnp.where(sub8 == 1, rank.astype(I32), 0))
    mf_ref[...] = jnp.where(sub8 == 0, gate_lo, jnp.where(sub8 == 1, gate_hi, 0.0))


def _router(x2, wr_pad, br_pad):
    t = x2.shape[0]
    tm = ROW_TILE
    return pl.pallas_call(
        functools.partial(_router_kernel, tm=tm),
        grid=(t // tm,),
        in_specs=[
            pl.BlockSpec((tm, D_MODEL), lambda i: (i, 0)),
            pl.BlockSpec((D_MODEL, 2 * LANES), lambda i: (0, 0)),
            pl.BlockSpec((1, LANES), lambda i: (0, 0)),
        ],
        out_specs=[
            pl.BlockSpec((8, tm), lambda i: (0, i)),
            pl.BlockSpec((8, tm), lambda i: (0, i)),
            pl.BlockSpec((32, LANES), lambda i: (0, 0)),
        ],
        out_shape=[
            jax.ShapeDtypeStruct((8, t), I32),
            jax.ShapeDtypeStruct((8, t), F32),
            jax.ShapeDtypeStruct((32, LANES), F32),
        ],
        scratch_shapes=[pltpu.VMEM((32, LANES), F32)],
        compiler_params=pltpu.CompilerParams(dimension_semantics=("arbitrary",), vmem_limit_bytes=32 * MIB),
        name="moe_router",
    )(x2, wr_pad, br_pad)


def _dispatch_kernel(dest_ref, x_ref, buf_in, buf_hbm, sem, *, tm):
    del buf_in

    def issue(t, c):
        pltpu.make_async_copy(x_ref.at[pl.ds(t, 1)], buf_hbm.at[pl.ds(dest_ref[0, 0, t], 1)], sem).start()
        return c

    lax.fori_loop(0, tm, issue, 0)
    pltpu.make_async_copy(x_ref, buf_hbm.at[pl.ds(0, tm)], sem).wait()


def _dispatch(x2, dest, n_rows):
    t = x2.shape[0]
    tm = DISPATCH_TM
    buf0 = jnp.zeros((n_rows, D_MODEL), F32)
    return pl.pallas_call(
        functools.partial(_dispatch_kernel, tm=tm),
        grid=(t // tm,),
        in_specs=[
            pl.BlockSpec((1, 1, tm), lambda i: (i, 0, 0), memory_space=pltpu.SMEM),
            pl.BlockSpec((tm, D_MODEL), lambda i: (i, 0)),
            pl.BlockSpec(memory_space=pl.ANY),
        ],
        out_specs=pl.BlockSpec(memory_space=pl.ANY),
        out_shape=jax.ShapeDtypeStruct((n_rows, D_MODEL), F32),
        scratch_shapes=[pltpu.SemaphoreType.DMA(())],
        input_output_aliases={2: 0},
        compiler_params=pltpu.CompilerParams(
            dimension_semantics=("arbitrary",), has_side_effects=True, vmem_limit_bytes=32 * MIB),
        name="moe_dispatch",
    )(dest.reshape(t // tm, 1, tm), x2, buf0)


def _expert_kernel(elo_ref, ehi_ref, valid_ref, xb_ref, gs_ref, wg0, wu0, wd0, wg1, wu1, wd1, out_ref):
    del elo_ref, ehi_ref
    blk = pl.program_id(0)

    @pl.when(valid_ref[blk] > 0)
    def _():
        xb = xb_ref[...].astype(BF16)

        def ffn(wg, wu, wd):
            hg = _dot(xb, wg[0])
            hu = _dot(xb, wu[0])
            hidden = (hg * jax.nn.sigmoid(hg)) * hu
            return _dot(hidden.astype(BF16), wd[0])

        gs = gs_ref[...]
        out_ref[...] = gs[:, 0:1] * ffn(wg0, wu0, wd0) + gs[:, 1:2] * ffn(wg1, wu1, wd1)

    @pl.when(valid_ref[blk] == 0)
    def _():
        out_ref[...] = jnp.zeros_like(out_ref)


def _experts(buf, gates_sorted, blk_elo, blk_ehi, blk_valid, wg, wu, wd):
    n_rows = buf.shape[0]
    nb = n_rows // MOE_BLOCK
    lo_map = lambda b, elo, ehi, val: (elo[b], 0, 0)
    hi_map = lambda b, elo, ehi, val: (ehi[b], 0, 0)
    row_map = lambda b, elo, ehi, val: (b, 0)
    up_shape = (1, D_MODEL, D_EXPERT)
    down_shape = (1, D_EXPERT, D_MODEL)
    grid_spec = pltpu.PrefetchScalarGridSpec(
        num_scalar_prefetch=3,
        grid=(nb,),
        in_specs=[
            pl.BlockSpec((MOE_BLOCK, D_MODEL), row_map),
            pl.BlockSpec((MOE_BLOCK, LANES), row_map),
            pl.BlockSpec(up_shape, lo_map), pl.BlockSpec(up_shape, lo_map), pl.BlockSpec(down_shape, lo_map),
            pl.BlockSpec(up_shape, hi_map), pl.BlockSpec(up_shape, hi_map), pl.BlockSpec(down_shape, hi_map),
        ],
        out_specs=pl.BlockSpec((MOE_BLOCK, D_MODEL), row_map),
    )
    return pl.pallas_call(
        _expert_kernel,
        grid_spec=grid_spec,
        out_shape=jax.ShapeDtypeStruct((n_rows, D_MODEL), F32),
        compiler_params=pltpu.CompilerParams(dimension_semantics=("arbitrary",), vmem_limit_bytes=58 * MIB),
        name="moe_experts",
    )(blk_elo, blk_ehi, blk_valid, buf, gates_sorted, wg, wu, wd, wg, wu, wd)


def _gather_ln_kernel(dest_ref, dest_next_ref, x_ref, src_hbm, g_ref, b_ref, y_ref, rows_s, sems, *, tm):
    i = pl.program_id(0)
    slot = i % 2

    def issue_rows(idx_ref, to_slot):
        def issue(t, c):
            pltpu.make_async_copy(src_hbm.at[pl.ds(idx_ref[0, 0, t], 1)], rows_s.at[to_slot, pl.ds(t, 1)],
                                  sems.at[to_slot]).start()
            return c

        lax.fori_loop(0, tm, issue, 0, unroll=8)

    @pl.when(i == 0)
    def _():
        issue_rows(dest_ref, slot)

    @pl.when(i + 1 < pl.num_programs(0))
    def _():
        issue_rows(dest_next_ref, 1 - slot)

    pltpu.make_async_copy(src_hbm.at[pl.ds(0, tm)], rows_s.at[slot], sems.at[slot]).wait()
    y = ALPHA * x_ref[...] + rows_s[slot]
    y_ref[...] = _layer_norm_rows(y, g_ref[...], b_ref[...])


def _gather_ln(x2, expert_out, dest, g, b):
    t = x2.shape[0]
    tm = GATHER_TM
    n_tiles = t // tm
    dest3 = dest.reshape(n_tiles, 1, tm)
    return pl.pallas_call(
        functools.partial(_gather_ln_kernel, tm=tm),
        grid=(n_tiles,),
        in_specs=[
            pl.BlockSpec((1, 1, tm), lambda i: (i, 0, 0), memory_space=pltpu.SMEM),
            pl.BlockSpec((1, 1, tm), lambda i: (jnp.minimum(i + 1, n_tiles - 1), 0, 0), memory_space=pltpu.SMEM),
            pl.BlockSpec((tm, D_MODEL), lambda i: (i, 0)),
            pl.BlockSpec(memory_space=pl.ANY),
            pl.BlockSpec((1, D_MODEL), lambda i: (0, 0)),
            pl.BlockSpec((1, D_MODEL), lambda i: (0, 0)),
        ],
        out_specs=pl.BlockSpec((tm, D_MODEL), lambda i: (i, 0)),
        out_shape=jax.ShapeDtypeStruct((t, D_MODEL), F32),
        scratch_shapes=[pltpu.VMEM((2, tm, D_MODEL), F32), pltpu.SemaphoreType.DMA((2,))],
        compiler_params=pltpu.CompilerParams(dimension_semantics=("arbitrary",), vmem_limit_bytes=32 * MIB),
        name="moe_gather_ln",
    )(dest3, dest3, x2, expert_out, g.reshape(1, D_MODEL), b.reshape(1, D_MODEL))


_CLASS_LO = np.array([4 * (c // 6) + PAIRS[c % 6][0] for c in range(N_CLASSES)], np.int32)
_CLASS_HI = np.array([4 * (c // 6) + PAIRS[c % 6][1] for c in range(N_CLASSES)], np.int32)


def _moe_layer(x2, wr_pad, br_pad, wg, wu, wd, ln_g, ln_b):
    t = x2.shape[0]
    nb = -(-(t + N_CLASSES * (MOE_BLOCK - 1)) // MOE_BLOCK)
    n_rows = nb * MOE_BLOCK
    meta_i, meta_f, cnt = _router(x2, wr_pad, br_pad)
    cls, rank = meta_i[0], meta_i[1]
    counts = cnt[:N_CLASSES, 0].astype(I32)
    padded = (counts + MOE_BLOCK - 1) // MOE_BLOCK * MOE_BLOCK
    ends = jnp.cumsum(padded)
    starts = ends - padded
    dest = starts[cls] + rank
    blk_start = jnp.arange(nb, dtype=I32) * MOE_BLOCK
    blk_valid = (blk_start < ends[-1]).astype(I32)
    n_valid = jnp.sum(blk_valid)
    blk_cls = jnp.minimum(jnp.sum((ends[None, :] <= blk_start[:, None]).astype(I32), axis=1), N_CLASSES - 1)
    blk_cls = blk_cls[jnp.minimum(jnp.arange(nb), n_valid - 1)]
    blk_elo = jnp.asarray(_CLASS_LO)[blk_cls]
    blk_ehi = jnp.asarray(_CLASS_HI)[blk_cls]
    gates_tok = jnp.pad(meta_f[:2].T, ((0, 0), (0, LANES - 2)))
    gates_sorted = jnp.zeros((n_rows, LANES), F32).at[dest].set(gates_tok)
    buf = _dispatch(x2, dest, n_rows)
    out = _experts(buf, gates_sorted, blk_elo, blk_ehi, blk_valid, wg, wu, wd)
    return _gather_ln(x2, out, dest, ln_g, ln_b)


def _inproj_kernel(x_ref, w_ref, wgate_ref, z_ref, gates_ref, xb_s):
    @pl.when(pl.program_id(1) == 0)
    def _():
        xb_s[...] = x_ref[...].astype(BF16)
        gates_ref[...] = _dot(xb_s[...], wgate_ref[...])

    z_ref[...] = _dot(xb_s[...], w_ref[...]).astype(z_ref.dtype)


def _inproj(x2, w_main_bf16, w_gate_bf16):
    t = x2.shape[0]
    tm, tn = INPROJ_TM, INPROJ_TN
    n_main = w_main_bf16.shape[1]
    return pl.pallas_call(
        _inproj_kernel,
        grid=(t // tm, n_main // tn),
        in_specs=[
            pl.BlockSpec((tm, D_MODEL), lambda m, n: (m, 0)),
            pl.BlockSpec((D_MODEL, tn), lambda m, n: (0, n)),
            pl.BlockSpec((D_MODEL, LANES), lambda m, n: (0, 0)),
        ],
        out_specs=[
            pl.BlockSpec((tm, tn), lambda m, n: (m, n)),
            pl.BlockSpec((tm, LANES), lambda m, n: (m, 0)),
        ],
        out_shape=[
            jax.ShapeDtypeStruct((t, n_main), BF16),
            jax.ShapeDtypeStruct((t, LANES), F32),
        ],
        scratch_shapes=[pltpu.VMEM((tm, D_MODEL), BF16)],
        compiler_params=pltpu.CompilerParams(
            dimension_semantics=("parallel", "arbitrary"), vmem_limit_bytes=48 * MIB),
        name="mlstm_inproj",
    )(x2, w_main_bf16, w_gate_bf16)


def _mlstm_cell_kernel(zq_ref, zk_ref, v_ref, op_ref, gt_ref, bg_ref, cwq_ref, cbq_ref, cwk_ref, cbk_ref,
                       ng_ref, y_ref, q_s, k_s, gb_s, bc_s, gbt_s, bct_s, *, chunk, n_heads):
    hgroup = pl.program_id(1)
    seq = zq_ref.shape[0]

    def conv_silu(z_ref, cw_ref, cb_ref):
        z = z_ref[...].astype(F32)
        rowi = lax.broadcasted_iota(I32, z.shape, 0)
        out = cb_ref[...] + cw_ref[0:1, :] * jnp.where(rowi >= CONV_K - 1, pltpu.roll(z, CONV_K - 1, 0), 0.0)
        for j in range(1, CONV_K - 1):
            shift = CONV_K - 1 - j
            out = out + cw_ref[j:j + 1, :] * jnp.where(rowi >= shift, pltpu.roll(z, shift, 0), 0.0)
        out = out + cw_ref[CONV_K - 1:CONV_K, :] * z
        return out * jax.nn.sigmoid(out)

    q_s[...] = conv_silu(zq_ref, cwq_ref, cbq_ref).astype(BF16)
    k_s[...] = conv_silu(zk_ref, cwk_ref, cbk_ref) * (QK_DIM ** -0.5)

    lane = lax.broadcasted_iota(I32, (chunk, LANES), 1)
    sub = lax.broadcasted_iota(I32, (MLSTM_HEADS, chunk), 0)
    ti = lax.broadcasted_iota(I32, (chunk, chunk), 0)
    si = lax.broadcasted_iota(I32, (chunk, chunk), 1)
    causal = ti >= si
    ones_cols = jnp.ones((chunk, LANES), BF16)

    @pl.when(hgroup == 0)
    def _():
        tri = jnp.where(causal, 1.0, 0.0)
        bias = bg_ref[...]
        for c in range(seq // chunk):
            rs = slice(c * chunk, (c + 1) * chunk)
            gb = gt_ref[rs, :] + bias
            log_f = -(jnp.maximum(-gb, 0.0) + jnp.log1p(jnp.exp(-jnp.abs(gb))))
            bcum = _dot(tri, log_f, precision=HIGHEST)
            gb_s[rs, :] = gb
            bc_s[rs, :] = bcum
            gbt_s[:, rs] = gb.T
            bct_s[:, rs] = bcum.T

    def pick_col(a, idx):
        return jnp.sum(jnp.where(lane == idx, a, 0.0), axis=1, keepdims=True)

    def pick_row(a, idx):
        return jnp.sum(jnp.where(sub == idx, a, 0.0), axis=0, keepdims=True)

    heads = [hgroup * n_heads + i for i in range(n_heads)]
    qcols = [slice(i * QK_DIM, (i + 1) * QK_DIM) for i in range(n_heads)]
    vcols = [slice(i * V_DIM, (i + 1) * V_DIM) for i in range(n_heads)]
    hr = range(n_heads)
    c_state = [jnp.zeros((QK_DIM, V_DIM + LANES), F32) for _ in hr]
    m_state = [jnp.full((1, 1), NEG, F32) for _ in hr]
    for c in range(seq // chunk):
        rs = slice(c * chunk, (c + 1) * chunk)
        gb_c, bc_c = gb_s[rs, :], bc_s[rs, :]
        gbt_c, bct_c = gbt_s[0:MLSTM_HEADS, rs], bct_s[MLSTM_HEADS:2 * MLSTM_HEADS, rs]
        li_col = [pick_col(gb_c, hd) for hd in heads]
        bc_col = [pick_col(bc_c, hd + MLSTM_HEADS) for hd in heads]
        li_row = [pick_row(gbt_c, hd) for hd in heads]
        bc_row = [pick_row(bct_c, hd) for hd in heads]
        dmat = [jnp.where(causal, bc_col[i] + (li_row[i] - bc_row[i]), NEG) for i in hr]
        inter = [bc_col[i] + m_state[i] for i in hr]
        m_t = [jnp.maximum(inter[i], jnp.max(dmat[i], axis=1, keepdims=True)) for i in hr]
        qc = [q_s[rs, qcols[i]] for i in hr]
        kf = [k_s[rs, qcols[i]] for i in hr]
        vc = [v_ref[rs, vcols[i]] for i in hr]
        a = [_dot_nt(qc[i], kf[i].astype(BF16)) * jnp.exp(dmat[i] - m_t[i]) for i in hr]
        w_inter = [jnp.exp(inter[i] - m_t[i]) for i in hr]
        q_state = [_dot(qc[i], c_state[i].astype(BF16)) for i in hr]
        num = [_dot(a[i].astype(BF16), vc[i]) + w_inter[i] * q_state[i][:, :V_DIM] for i in hr]
        den = [jnp.sum(a[i], axis=1, keepdims=True) + w_inter[i] * q_state[i][:, V_DIM:V_DIM + 1] for i in hr]
        h_out = [num[i] / jnp.maximum(jnp.abs(den[i]), jnp.exp(-m_t[i])) for i in hr]
        mu = [jnp.mean(h_out[i], axis=1, keepdims=True) for i in hr]
        hc = [h_out[i] - mu[i] for i in hr]
        var = [jnp.mean(hc[i] * hc[i], axis=1, keepdims=True) for i in hr]
        for i in hr:
            hn = hc[i] * lax.rsqrt(var[i] + LN_EPS) * ng_ref[:, vcols[i]]
            y_ref[rs, vcols[i]] = (hn * jax.nn.sigmoid(op_ref[rs, vcols[i]].astype(F32))).astype(y_ref.dtype)
        b_last = [bc_col[i][chunk - 1:chunk, :] for i in hr]
        g = [b_last[i] - bc_col[i] + li_col[i] for i in hr]
        m_new = [jnp.maximum(b_last[i] + m_state[i], jnp.max(g[i], axis=0, keepdims=True)) for i in hr]
        wk = [jnp.exp(g[i] - m_new[i]) for i in hr]
        decay = [jnp.exp(b_last[i] + m_state[i] - m_new[i]) for i in hr]
        kw_t = [(wk[i] * kf[i]).T.astype(BF16) for i in hr]
        upd = [_dot(kw_t[i], jnp.concatenate([vc[i], ones_cols], axis=1)) for i in hr]
        c_state = [decay[i] * c_state[i] + upd[i] for i in hr]
        m_state = m_new


def _mlstm_cell(z, gates, b_gates_pad, conv_w, conv_b, norm_g, batch):
    t = z.shape[0]
    nh = MLSTM_HEADS_PER_STEP
    groups = MLSTM_HEADS // nh
    qw, vw = nh * QK_DIM, nh * V_DIM
    k_blk0 = (MLSTM_QK_COLS // 2) // qw
    v_blk0 = MLSTM_QK_COLS // vw
    o_blk0 = (MLSTM_QK_COLS + MLSTM_V_COLS) // vw
    return pl.pallas_call(
        functools.partial(_mlstm_cell_kernel, chunk=MLSTM_CHUNK, n_heads=nh),
        grid=(batch, groups),
        in_specs=[
            pl.BlockSpec((SEQ, qw), lambda b, h: (b, h)),
            pl.BlockSpec((SEQ, qw), lambda b, h: (b, k_blk0 + h)),
            pl.BlockSpec((SEQ, vw), lambda b, h: (b, v_blk0 + h)),
            pl.BlockSpec((SEQ, vw), lambda b, h: (b, o_blk0 + h)),
            pl.BlockSpec((SEQ, LANES), lambda b, h: (b, 0)),
            pl.BlockSpec((1, LANES), lambda b, h: (0, 0)),
            pl.BlockSpec((CONV_K, qw), lambda b, h: (0, h)),
            pl.BlockSpec((1, qw), lambda b, h: (0, h)),
            pl.BlockSpec((CONV_K, qw), lambda b, h: (0, k_blk0 + h)),
            pl.BlockSpec((1, qw), lambda b, h: (0, k_blk0 + h)),
            pl.BlockSpec((1, vw), lambda b, h: (0, h)),
        ],
        out_specs=pl.BlockSpec((SEQ, vw), lambda b, h: (b, h)),
        out_shape=jax.ShapeDtypeStruct((t, MLSTM_V_COLS), BF16),
        scratch_shapes=[pltpu.VMEM((SEQ, qw), BF16), pltpu.VMEM((SEQ, qw), F32),
                        pltpu.VMEM((SEQ, LANES), F32), pltpu.VMEM((SEQ, LANES), F32),
                        pltpu.VMEM((LANES, SEQ), F32), pltpu.VMEM((LANES, SEQ), F32)],
        compiler_params=pltpu.CompilerParams(
            dimension_semantics=("parallel", "arbitrary"), vmem_limit_bytes=56 * MIB),
        name="mlstm_cell",
    )(z, z, z, z, gates, b_gates_pad, conv_w, conv_b.reshape(1, -1), conv_w, conv_b.reshape(1, -1),
      norm_g.reshape(1, -1))


def kernel(x, attn_w_qkv, attn_w_o, mlstm_w_in, mlstm_b_gates, mlstm_conv_w, mlstm_conv_b, mlstm_norm_g,
           mlstm_w_out, ln_mix_g, ln_mix_b, ln_ffn_g, ln_ffn_b, router_w, router_b, moe_w_gate, moe_w_up,
           moe_w_down):
    batch, seq, d = x.shape
    assert (seq, d) == (SEQ, D_MODEL)
    t = batch * seq
    x2 = x.reshape(t, d)

    wr_f32 = jnp.zeros((D_MODEL, LANES), F32).at[:, :N_EXPERTS].set(router_w)
    wr_hi = wr_f32.astype(BF16)
    wr_pad = jnp.concatenate([wr_hi, (wr_f32 - wr_hi.astype(F32)).astype(BF16)], axis=1)
    br_pad = jnp.zeros((1, LANES), F32).at[0, :N_EXPERTS].set(router_b)
    wg_bf, wu_bf, wd_bf = moe_w_gate.astype(BF16), moe_w_up.astype(BF16), moe_w_down.astype(BF16)

    w_qkv = _rope_permute_qk(attn_w_qkv[0]).astype(BF16)
    views = [x] + list(_class_major_views(x))
    outs, lses = [], []
    for group, dil in enumerate(DILATIONS):
        o, lse = _attn_group(views[group], w_qkv, _rope_table(dil), group, dil)
        outs.append(o)
        lses.append(lse)
    x2 = _attn_out(outs, lses, x2, attn_w_o[0].astype(BF16), ln_mix_g[0], ln_mix_b[0])
    x2 = _moe_layer(x2, wr_pad, br_pad, wg_bf[0], wu_bf[0], wd_bf[0], ln_ffn_g[0], ln_ffn_b[0])

    w_in = mlstm_w_in[0]
    w_main = w_in[:, :MLSTM_MAIN_COLS].astype(BF16)
    w_gate = jnp.zeros((D_MODEL, LANES), F32).at[:, :2 * MLSTM_HEADS].set(w_in[:, MLSTM_MAIN_COLS:]).astype(BF16)
    bg_pad = jnp.zeros((1, LANES), F32).at[0, :2 * MLSTM_HEADS].set(mlstm_b_gates[0])
    z, gates = _inproj(x2, w_main, w_gate)
    y = _mlstm_cell(z, gates, bg_pad, mlstm_conv_w[0], mlstm_conv_b[0], mlstm_norm_g[0], batch)
    x2 = _mix_out(y, x2, mlstm_w_out[0].astype(BF16), ln_mix_g[1], ln_mix_b[1])
    x2 = _moe_layer(x2, wr_pad, br_pad, wg_bf[1], wu_bf[1], wd_bf[1], ln_ffn_g[1], ln_ffn_b[1])
    return x2.reshape(batch, seq, d)
```

```python
import functools

import numpy as np
import jax
import jax.numpy as jnp
from jax import lax
from jax.experimental import pallas as pl
from jax.experimental.pallas import tpu as pltpu

F32 = jnp.float32
BF16 = jnp.bfloat16
I32 = jnp.int32
HIGHEST = lax.Precision.HIGHEST

D_MODEL = 2048
SEQ = 2048
DEPTH = 2
DILATIONS = (1, 4, 16)
N_BACK = 128
ATTN_HEADS = 8
HEAD_DIM = 128
ROT_DIM = HEAD_DIM // 4
ROPE_THETA = 500000.0
ATTN_BLOCK = 128
ATTN_COLS = ATTN_HEADS * HEAD_DIM

MLSTM_HEADS = 8
QK_DIM = 128
V_DIM = D_MODEL // MLSTM_HEADS
CONV_K = 4
MLSTM_QK_COLS = 2 * MLSTM_HEADS * QK_DIM
MLSTM_V_COLS = MLSTM_HEADS * V_DIM
MLSTM_MAIN_COLS = MLSTM_QK_COLS + 2 * MLSTM_V_COLS

N_EXPERTS = 16
N_EXPERT_GROUPS = 4
EXPERTS_PER_GROUP = 4
D_EXPERT = 768
PAIRS = ((0, 1), (0, 2), (0, 3), (1, 2), (1, 3), (2, 3))
N_CLASSES = N_EXPERT_GROUPS * len(PAIRS)

ALPHA = (2 * DEPTH) ** 0.25
LN_EPS = 1e-5
NEG = -1e30

LANES = 128
V7X_VMEM_BYTES = 64 * 1024 * 1024
MIB = 1024 * 1024

ATTN_STEP_ROWS = 512
ATTN_HEAD_SET = 8
ROPE_PAIR_SHIFT = 64
ROW_TILE = 512
INPROJ_TM = 1024
INPROJ_TN = 1024
MLSTM_CHUNK = 128
MLSTM_HEADS_PER_STEP = 4
MOE_BLOCK = 256
GATHER_TM = 256


def _dot(a, b, **kw):
    return jnp.dot(a, b, preferred_element_type=F32, **kw)


def _dot_nt(a, b):
    return lax.dot_general(a, b, (((1,), (1,)), ((), ())), preferred_element_type=F32)


def _layer_norm_rows(y, g, b):
    mu = jnp.mean(y, axis=-1, keepdims=True)
    yc = y - mu
    var = jnp.mean(yc * yc, axis=-1, keepdims=True)
    return yc * lax.rsqrt(var + LN_EPS) * g + b


def _attn_group_kernel(x_ref, w_ref, tab_ref, o_ref, lse_ref, q_s, k_s, v_s, *, n_cls, lc, carry):
    step = pl.program_id(1)
    rows_total = n_cls * lc
    blocks_per_class = lc // ATTN_BLOCK

    if n_cls == 1:
        xs = x_ref[0]
    else:
        xs = jnp.concatenate([x_ref[0, :, c * D_MODEL:(c + 1) * D_MODEL] for c in range(n_cls)], axis=0)
    qkv = _dot(xs.astype(BF16), w_ref[...])

    if carry:
        @pl.when(step == 0)
        def _():
            k_s[0:ATTN_BLOCK, :] = jnp.zeros((ATTN_BLOCK, ATTN_COLS), BF16)
            v_s[0:ATTN_BLOCK, :] = jnp.zeros((ATTN_BLOCK, ATTN_COLS), BF16)

        @pl.when(step > 0)
        def _():
            k_s[0:ATTN_BLOCK, :] = k_s[rows_total:rows_total + ATTN_BLOCK, :]
            v_s[0:ATTN_BLOCK, :] = v_s[rows_total:rows_total + ATTN_BLOCK, :]

    cosf = tab_ref[:, 0:LANES]
    sinr = tab_ref[:, LANES:2 * LANES]

    def rope(t):
        return t * cosf + pltpu.roll(t, ROPE_PAIR_SHIFT, 1) * sinr

    for h in range(ATTN_HEADS):
        cs = slice(h * HEAD_DIM, (h + 1) * HEAD_DIM)
        q_s[:, cs] = rope(qkv[:, h * HEAD_DIM:(h + 1) * HEAD_DIM]).astype(BF16)
        k_s[ATTN_BLOCK:, cs] = rope(qkv[:, ATTN_COLS + h * HEAD_DIM:ATTN_COLS + (h + 1) * HEAD_DIM]).astype(BF16)
    v_s[ATTN_BLOCK:, :] = qkv[:, 2 * ATTN_COLS:3 * ATTN_COLS].astype(BF16)

    row = lax.broadcasted_iota(I32, (ATTN_BLOCK, ATTN_BLOCK), 0)
    col = lax.broadcasted_iota(I32, (ATTN_BLOCK, ATTN_BLOCK), 1)
    row2 = lax.broadcasted_iota(I32, (ATTN_BLOCK, 2 * ATTN_BLOCK), 0)
    col2 = lax.broadcasted_iota(I32, (ATTN_BLOCK, 2 * ATTN_BLOCK), 1)
    lane = lax.broadcasted_iota(I32, (ATTN_BLOCK, LANES), 1)
    mask_cur = col <= row
    mask_both = jnp.logical_and(col2 >= row2, col2 <= row2 + N_BACK)
    if carry:
        mask_first = jnp.logical_and(mask_both, jnp.logical_or(col2 >= ATTN_BLOCK, step > 0))
    scale = HEAD_DIM ** -0.5

    for c in range(n_cls):
        for bi in range(blocks_per_class):
            j = c * blocks_per_class + bi
            ors = slice(bi * ATTN_BLOCK, (bi + 1) * ATTN_BLOCK)
            with_prev = bi > 0 or carry
            if with_prev:
                krows = slice(j * ATTN_BLOCK, (j + 2) * ATTN_BLOCK)
                mask = mask_first if bi == 0 else mask_both
            else:
                krows = slice((j + 1) * ATTN_BLOCK, (j + 2) * ATTN_BLOCK)
                mask = mask_cur
            lse_tile = jnp.zeros((ATTN_BLOCK, LANES), F32)
            for h0 in range(0, ATTN_HEADS, ATTN_HEAD_SET):
                heads = range(h0, h0 + ATTN_HEAD_SET)
                cols = [slice(h * HEAD_DIM, (h + 1) * HEAD_DIM) for h in heads]
                scores = [jnp.where(mask, _dot_nt(q_s[j * ATTN_BLOCK:(j + 1) * ATTN_BLOCK, cs], k_s[krows, cs])
                                    * scale, NEG) for cs in cols]
                maxes = [jnp.max(s, axis=1, keepdims=True) for s in scores]
                probs = [jnp.exp(s - m) for s, m in zip(scores, maxes)]
                dens = [jnp.sum(p, axis=1, keepdims=True) for p in probs]
                accs = [_dot(p.astype(BF16), v_s[krows, cs]) for p, cs in zip(probs, cols)]
                for h, acc, den, m in zip(heads, accs, dens, maxes):
                    o_ref[0, ors, c * ATTN_COLS + h * HEAD_DIM:c * ATTN_COLS + (h + 1) * HEAD_DIM] = (
                        acc / den).astype(o_ref.dtype)
                    lse_tile = jnp.where(lane == h, m + jnp.log(den), lse_tile)
            lse_ref[0, ors, c * LANES:(c + 1) * LANES] = lse_tile


def _class_major_perm(dil):
    width = ATTN_STEP_ROWS // dil
    perm = np.zeros((ATTN_STEP_ROWS, ATTN_STEP_ROWS), np.float32)
    for r in range(dil):
        for m in range(width):
            perm[r * width + m, m * dil + r] = 1.0
    return perm


def _class_major_kernel(x_ref, *refs):
    n = len(DILATIONS) - 1
    xb = x_ref[0].astype(BF16)
    for p_ref, out_ref, dil in zip(refs[:n], refs[n:], DILATIONS[1:]):
        width = ATTN_STEP_ROWS // dil
        rows = _dot(p_ref[...], xb).astype(BF16)
        for r in range(dil):
            out_ref[0, :, r * D_MODEL:(r + 1) * D_MODEL] = rows[r * width:(r + 1) * width, :]


def _class_major_views(x3):
    batch = x3.shape[0]
    perms = [jnp.asarray(_class_major_perm(d), BF16) for d in DILATIONS[1:]]
    return pl.pallas_call(
        _class_major_kernel,
        grid=(batch, SEQ // ATTN_STEP_ROWS),
        in_specs=[pl.BlockSpec((1, ATTN_STEP_ROWS, D_MODEL), lambda b, s: (b, s, 0))] + [
            pl.BlockSpec((ATTN_STEP_ROWS, ATTN_STEP_ROWS), lambda b, s: (0, 0)) for _ in perms],
        out_specs=[pl.BlockSpec((1, ATTN_STEP_ROWS // d, d * D_MODEL), lambda b, s: (b, s, 0)) for d in DILATIONS[1:]],
        out_shape=[jax.ShapeDtypeStruct((batch, SEQ // d, d * D_MODEL), BF16) for d in DILATIONS[1:]],
        compiler_params=pltpu.CompilerParams(dimension_semantics=("parallel", "parallel"), vmem_limit_bytes=40 * MIB),
        name="class_major_views",
    )(x3, *perms)


def _attn_group(xg, w_qkv_bf16, tab, group, dil):
    batch = xg.shape[0]
    per_class = SEQ // dil
    lc = min(ATTN_STEP_ROWS, per_class)
    n_cls = ATTN_STEP_ROWS // lc
    steps = SEQ // ATTN_STEP_ROWS
    carry = dil == 1
    if dil == 1:
        imap = lambda b, s: (b, s, 0)
    else:
        imap = lambda b, s: (b, 0, s)
    kern = functools.partial(_attn_group_kernel, n_cls=n_cls, lc=lc, carry=carry)
    o, lse = pl.pallas_call(
        kern,
        grid=(batch, steps),
        in_specs=[
            pl.BlockSpec((1, lc, n_cls * D_MODEL), imap),
            pl.BlockSpec((D_MODEL, 3 * ATTN_COLS), lambda b, s: (0, group), pipeline_mode=pl.Buffered(1)),
            pl.BlockSpec((ATTN_STEP_ROWS, 2 * LANES), lambda b, s: (s, 0)),
        ],
        out_specs=[
            pl.BlockSpec((1, lc, n_cls * ATTN_COLS), imap),
            pl.BlockSpec((1, lc, n_cls * LANES), imap),
        ],
        out_shape=[
            jax.ShapeDtypeStruct((batch, per_class, dil * ATTN_COLS), BF16),
            jax.ShapeDtypeStruct((batch, per_class, dil * LANES), F32),
        ],
        scratch_shapes=[
            pltpu.VMEM((ATTN_STEP_ROWS, ATTN_COLS), BF16),
            pltpu.VMEM((ATTN_BLOCK + ATTN_STEP_ROWS, ATTN_COLS), BF16),
            pltpu.VMEM((ATTN_BLOCK + ATTN_STEP_ROWS, ATTN_COLS), BF16),
        ],
        compiler_params=pltpu.CompilerParams(
            dimension_semantics=("parallel", "arbitrary"), vmem_limit_bytes=56 * MIB),
        name=f"attn_group{group}",
    )(xg, w_qkv_bf16, tab)
    return o, lse


def _rope_permute_qk(w_qkv):
    half = ROT_DIM // 2
    lo, hi = ROPE_PAIR_SHIFT, ROPE_PAIR_SHIFT + half
    w = w_qkv.reshape(D_MODEL, len(DILATIONS), 3, ATTN_HEADS, HEAD_DIM)
    qk = w[:, :, 0:2]
    qk = jnp.concatenate([qk[..., :half], qk[..., lo:hi], qk[..., ROT_DIM:lo], qk[..., half:ROT_DIM], qk[..., hi:]], -1)
    return jnp.concatenate([qk, w[:, :, 2:3]], axis=2).reshape(w_qkv.shape)


def _rope_table(dil):
    inv_freq = ROPE_THETA ** (-np.arange(0, ROT_DIM, 2, dtype=np.float64) / ROT_DIM)
    ang = np.arange(SEQ, dtype=np.float64)[:, None] * inv_freq[None, :]
    ang = np.concatenate([ang, ang], -1)
    cos, sin = np.cos(ang), np.sin(ang)
    half = ROT_DIM // 2
    gap = ROPE_PAIR_SHIFT - half
    tail = LANES - ROPE_PAIR_SHIFT - half
    cosf = np.concatenate([cos[:, :half], np.ones((SEQ, gap)), cos[:, half:], np.ones((SEQ, tail))], 1)
    sinr = np.concatenate([-sin[:, :half], np.zeros((SEQ, gap)), sin[:, half:], np.zeros((SEQ, tail))], 1)
    tab = np.concatenate([cosf, sinr], 1)
    tab = tab.reshape(SEQ // dil, dil, 2 * LANES).transpose(1, 0, 2).reshape(SEQ, 2 * LANES)
    return jnp.asarray(tab.astype(np.float32))


def _attn_out_kernel(o0_ref, o1_ref, o2_ref, l0_ref, l1_ref, l2_ref, pt1_ref, pt2_ref, x_ref, w_ref, g_ref, b_ref,
                     out_ref, l1_s, l2_s):
    o_nat = [None]
    for o_ref, l_ref, pt_ref, l_s, dil in ((o1_ref, l1_ref, pt1_ref, l1_s, DILATIONS[1]),
                                          (o2_ref, l2_ref, pt2_ref, l2_s, DILATIONS[2])):
        width = ATTN_STEP_ROWS // dil
        o_cm = jnp.concatenate([o_ref[0, :, r * ATTN_COLS:(r + 1) * ATTN_COLS] for r in range(dil)], axis=0)
        o_nat.append(_dot(pt_ref[...], o_cm))
        for r in range(dil):
            l_s[pl.ds(r, width, stride=dil), :] = l_ref[0, :, r * LANES:(r + 1) * LANES]
    ls = [l0_ref[0], l1_s[...], l2_s[...]]
    mx = jnp.maximum(jnp.maximum(ls[0], ls[1]), ls[2])
    es = [jnp.exp(l - mx) for l in ls]
    den = es[0] + es[1] + es[2]
    ws = [e / den for e in es]
    parts = []
    for h in range(ATTN_HEADS):
        cs = slice(h * HEAD_DIM, (h + 1) * HEAD_DIM)
        acc = ws[0][:, h:h + 1] * o0_ref[0, :, cs].astype(F32)
        acc = acc + ws[1][:, h:h + 1] * o_nat[1][:, cs]
        acc = acc + ws[2][:, h:h + 1] * o_nat[2][:, cs]
        parts.append(acc)
    mixed_in = jnp.concatenate(parts, axis=1).astype(BF16)
    y = ALPHA * x_ref[...] + _dot(mixed_in, w_ref[...])
    out_ref[...] = _layer_norm_rows(y, g_ref[...], b_ref[...])


def _attn_out(outs, lses, x2, w_bf16, g, b):
    t = x2.shape[0]
    steps = SEQ // ATTN_STEP_ROWS
    view_map = lambda i: (i // steps, i % steps, 0)
    view_spec = lambda dil, width: pl.BlockSpec((1, ATTN_STEP_ROWS // dil, dil * width), view_map)
    const_spec = lambda shape: pl.BlockSpec(shape, lambda i: (0, 0))
    row_spec = pl.BlockSpec((ATTN_STEP_ROWS, D_MODEL), lambda i: (i, 0))
    perms_t = [jnp.asarray(_class_major_perm(d).T, BF16) for d in DILATIONS[1:]]
    return pl.pallas_call(
        _attn_out_kernel,
        grid=(t // ATTN_STEP_ROWS,),
        in_specs=[view_spec(d, ATTN_COLS) for d in DILATIONS] + [view_spec(d, LANES) for d in DILATIONS] + [
            const_spec((ATTN_STEP_ROWS, ATTN_STEP_ROWS)) for _ in perms_t] + [
            row_spec, pl.BlockSpec((ATTN_COLS, D_MODEL), lambda i: (0, 0), pipeline_mode=pl.Buffered(1)),
            const_spec((1, D_MODEL)), const_spec((1, D_MODEL))],
        out_specs=row_spec,
        out_shape=jax.ShapeDtypeStruct((t, D_MODEL), F32),
        scratch_shapes=[pltpu.VMEM((ATTN_STEP_ROWS, LANES), F32), pltpu.VMEM((ATTN_STEP_ROWS, LANES), F32)],
        compiler_params=pltpu.CompilerParams(dimension_semantics=("parallel",), vmem_limit_bytes=48 * MIB),
        name="attn_out",
    )(*outs, *lses, *perms_t, x2, w_bf16, g.reshape(1, D_MODEL), b.reshape(1, D_MODEL))


def _mix_out_kernel(y_ref, x_ref, w_ref, g_ref, b_ref, out_ref):
    y = ALPHA * x_ref[...] + _dot(y_ref[...], w_ref[...])
    out_ref[...] = _layer_norm_rows(y, g_ref[...], b_ref[...])


def _mix_out(mixer_y, x2, w_bf16, g, b):
    t = x2.shape[0]
    k = w_bf16.shape[0]
    row_spec = lambda width: pl.BlockSpec((ROW_TILE, width), lambda i: (i, 0))
    const_spec = lambda shape: pl.BlockSpec(shape, lambda i: (0, 0))
    return pl.pallas_call(
        _mix_out_kernel,
        grid=(t // ROW_TILE,),
        in_specs=[row_spec(k), row_spec(D_MODEL),
                  pl.BlockSpec((k, D_MODEL), lambda i: (0, 0), pipeline_mode=pl.Buffered(1)),
                  const_spec((1, D_MODEL)), const_spec((1, D_MODEL))],
        out_specs=row_spec(D_MODEL),
        out_shape=jax.ShapeDtypeStruct((t, D_MODEL), F32),
        compiler_params=pltpu.CompilerParams(dimension_semantics=("parallel",), vmem_limit_bytes=48 * MIB),
        name="mix_out",
    )(mixer_y, x2, w_bf16, g.reshape(1, D_MODEL), b.reshape(1, D_MODEL))


def _router_kernel(x_ref, wr_ref, br_ref, mi_ref, mf_ref, cnt_ref, carry_s, *, tm):
    i = pl.program_id(0)

    @pl.when(i == 0)
    def _():
        carry_s[...] = jnp.zeros_like(carry_s)

    x = x_ref[...]
    x_hi = x.astype(BF16)
    x_lo = (x - x_hi.astype(F32)).astype(BF16)
    hi_terms = _dot(x_hi, wr_ref[...])
    logits = (hi_terms[:, :LANES] + _dot(x_lo, wr_ref[:, :LANES])) + hi_terms[:, LANES:] + br_ref[...]
    lt = logits.T
    l = [lt[e:e + 1, :] for e in range(N_EXPERTS)]
    mx = l[0]
    for e in range(1, N_EXPERTS):
        mx = jnp.maximum(mx, l[e])
    ex = [jnp.exp(v - mx) for v in l]
    tot = ex[0]
    for e in range(1, N_EXPERTS):
        tot = tot + ex[e]
    p = [v / tot for v in ex]

    def first_index_of(vals, target):
        idx = jnp.full_like(target, len(vals) - 1).astype(I32)
        for k in range(len(vals) - 2, -1, -1):
            idx = jnp.where(vals[k] == target, k, idx)
        return idx

    best = None
    for g in range(N_EXPERT_GROUPS):
        pg = p[g * EXPERTS_PER_GROUP:(g + 1) * EXPERTS_PER_GROUP]
        top1 = jnp.maximum(jnp.maximum(pg[0], pg[1]), jnp.maximum(pg[2], pg[3]))
        i1 = first_index_of(pg, top1)
        rest = [jnp.where(i1 == k, -1.0, pg[k]) for k in range(EXPERTS_PER_GROUP)]
        top2 = jnp.maximum(jnp.maximum(rest[0], rest[1]), jnp.maximum(rest[2], rest[3]))
        i2 = first_index_of(rest, top2)
        score = top1 + top2
        if best is None:
            best = (score, jnp.zeros_like(i1), top1, top2, i1, i2)
        else:
            better = score > best[0]
            cand = (score, jnp.full_like(i1, g), top1, top2, i1, i2)
            best = tuple(jnp.where(better, cv, bv) for cv, bv in zip(cand, best))
    _, g_sel, p1, p2, i1, i2 = best
    psum = p1 + p2
    gate1, gate2 = p1 / psum, p2 / psum
    first_low = i1 < i2
    lo = jnp.where(first_low, i1, i2)
    hi = jnp.where(first_low, i2, i1)
    gate_lo = jnp.where(first_low, gate1, gate2)
    gate_hi = jnp.where(first_low, gate2, gate1)
    pair = jnp.where(lo == 0, 0, jnp.where(lo == 1, 3, 5)) + hi - lo - 1
    cls = g_sel * len(PAIRS) + pair

    n_rows = carry_s.shape[0]
    sub = lax.broadcasted_iota(I32, (n_rows, tm), 0)
    onehot = sub == cls
    oh = jnp.where(onehot, 1.0, 0.0)
    upper = (lax.broadcasted_iota(I32, (tm, tm), 0) <= lax.broadcasted_iota(I32, (tm, tm), 1))
    cum = _dot(oh.astype(BF16), jnp.where(upper, 1.0, 0.0).astype(BF16))
    carry = carry_s[:, 0:1]
    rank = jnp.sum(jnp.where(onehot, cum - 1.0 + carry, 0.0), axis=0, keepdims=True)
    carry_new = carry + jnp.sum(oh, axis=1, keepdims=True)
    carry_s[...] = jnp.broadcast_to(carry_new, carry_s.shape)
    cnt_ref[...] = jnp.broadcast_to(carry_new, cnt_ref.shape)

    sub8 = lax.broadcasted_iota(I32, (8, tm), 0)
    mi_ref[...] = jnp.where(sub8 == 0, cls, jnp.where(sub8 == 1, rank.astype(I32), 0))
    mf_ref[...] = jnp.where(sub8 == 0, gate_lo, jnp.where(sub8 == 1, gate_hi, 0.0))


def _router(x2, wr_pad, br_pad):
    t = x2.shape[0]
    tm = ROW_TILE
    return pl.pallas_call(
        functools.partial(_router_kernel, tm=tm),
        grid=(t // tm,),
        in_specs=[
            pl.BlockSpec((tm, D_MODEL), lambda i: (i, 0)),
            pl.BlockSpec((D_MODEL, 2 * LANES), lambda i: (0, 0)),
            pl.BlockSpec((1, LANES), lambda i: (0, 0)),
        ],
        out_specs=[
            pl.BlockSpec((8, tm), lambda i: (0, i)),
            pl.BlockSpec((8, tm), lambda i: (0, i)),
            pl.BlockSpec((32, LANES), lambda i: (0, 0)),
        ],
        out_shape=[
            jax.ShapeDtypeStruct((8, t), I32),
            jax.ShapeDtypeStruct((8, t), F32),
            jax.ShapeDtypeStruct((32, LANES), F32),
        ],
        scratch_shapes=[pltpu.VMEM((32, LANES), F32)],
        compiler_params=pltpu.CompilerParams(dimension_semantics=("arbitrary",), vmem_limit_bytes=32 * MIB),
        name="moe_router",
    )(x2, wr_pad, br_pad)


def _expert_kernel(elo_ref, ehi_ref, src_first_ref, src_next_ref, x_hbm, gs_ref, wg0, wu0, wd0, wg1, wu1, wd1,
                   out_ref, rows_s, sems):
    del elo_ref, ehi_ref
    i = pl.program_id(0)
    slot = i % 2

    def start_row(idx_ref, t, to_slot):
        pltpu.make_async_copy(x_hbm.at[pl.ds(idx_ref[0, 0, t], 1)], rows_s.at[to_slot, pl.ds(t, 1)],
                              sems.at[to_slot]).start()

    def wait_rows(of_slot):
        pltpu.make_async_copy(x_hbm.at[pl.ds(0, MOE_BLOCK)], rows_s.at[of_slot], sems.at[of_slot]).wait()

    @pl.when(i == 0)
    def _():
        def issue(t, c):
            start_row(src_first_ref, t, slot)
            return c

        lax.fori_loop(0, MOE_BLOCK, issue, 0, unroll=8)

    wait_rows(slot)
    for t in range(MOE_BLOCK):
        start_row(src_next_ref, t, 1 - slot)

    xb = rows_s[slot].astype(BF16)

    def ffn(wg, wu, wd):
        hg = _dot(xb, wg[0])
        hu = _dot(xb, wu[0])
        hidden = (hg * jax.nn.sigmoid(hg)) * hu
        return _dot(hidden.astype(BF16), wd[0])

    gs = gs_ref[...]
    out_ref[...] = gs[:, 0:1] * ffn(wg0, wu0, wd0) + gs[:, 1:2] * ffn(wg1, wu1, wd1)

    @pl.when(i == pl.num_programs(0) - 1)
    def _():
        wait_rows(1 - slot)


def _experts(x2, src_rows, gates_sorted, blk_elo, blk_ehi, wg, wu, wd):
    nb = src_rows.shape[0] - 1
    n_rows = nb * MOE_BLOCK
    lo_map = lambda b, elo, ehi: (elo[b], 0, 0)
    hi_map = lambda b, elo, ehi: (ehi[b], 0, 0)
    row_map = lambda b, elo, ehi: (b, 0)
    up_shape = (1, D_MODEL, D_EXPERT)
    down_shape = (1, D_EXPERT, D_MODEL)
    grid_spec = pltpu.PrefetchScalarGridSpec(
        num_scalar_prefetch=2,
        grid=(nb,),
        in_specs=[
            pl.BlockSpec((1, 1, MOE_BLOCK), lambda b, elo, ehi: (0, 0, 0), memory_space=pltpu.SMEM),
            pl.BlockSpec((1, 1, MOE_BLOCK), lambda b, elo, ehi: (b + 1, 0, 0), memory_space=pltpu.SMEM),
            pl.BlockSpec(memory_space=pl.ANY),
            pl.BlockSpec((MOE_BLOCK, LANES), row_map),
            pl.BlockSpec(up_shape, lo_map), pl.BlockSpec(up_shape, lo_map), pl.BlockSpec(down_shape, lo_map),
            pl.BlockSpec(up_shape, hi_map), pl.BlockSpec(up_shape, hi_map), pl.BlockSpec(down_shape, hi_map),
        ],
        out_specs=pl.BlockSpec((MOE_BLOCK, D_MODEL), row_map),
        scratch_shapes=[pltpu.VMEM((2, MOE_BLOCK, D_MODEL), F32), pltpu.SemaphoreType.DMA((2,))],
    )
    return pl.pallas_call(
        _expert_kernel,
        grid_spec=grid_spec,
        out_shape=jax.ShapeDtypeStruct((n_rows, D_MODEL), F32),
        compiler_params=pltpu.CompilerParams(dimension_semantics=("arbitrary",), vmem_limit_bytes=58 * MIB),
        name="moe_experts",
    )(blk_elo, blk_ehi, src_rows, src_rows, x2, gates_sorted, wg, wu, wd, wg, wu, wd)


def _gather_ln_kernel(dest_ref, dest_next_ref, x_ref, src_hbm, g_ref, b_ref, y_ref, rows_s, sems, *, tm):
    i = pl.program_id(0)
    slot = i % 2

    def issue_rows(idx_ref, to_slot):
        def issue(t, c):
            pltpu.make_async_copy(src_hbm.at[pl.ds(idx_ref[0, 0, t], 1)], rows_s.at[to_slot, pl.ds(t, 1)],
                                  sems.at[to_slot]).start()
            return c

        lax.fori_loop(0, tm, issue, 0, unroll=8)

    @pl.when(i == 0)
    def _():
        issue_rows(dest_ref, slot)

    @pl.when(i + 1 < pl.num_programs(0))
    def _():
        issue_rows(dest_next_ref, 1 - slot)

    pltpu.make_async_copy(src_hbm.at[pl.ds(0, tm)], rows_s.at[slot], sems.at[slot]).wait()
    y = ALPHA * x_ref[...] + rows_s[slot]
    y_ref[...] = _layer_norm_rows(y, g_ref[...], b_ref[...])


def _gather_ln(x2, expert_out, dest, g, b):
    t = x2.shape[0]
    tm = GATHER_TM
    n_tiles = t // tm
    dest3 = dest.reshape(n_tiles, 1, tm)
    return pl.pallas_call(
        functools.partial(_gather_ln_kernel, tm=tm),
        grid=(n_tiles,),
        in_specs=[
            pl.BlockSpec((1, 1, tm), lambda i: (i, 0, 0), memory_space=pltpu.SMEM),
            pl.BlockSpec((1, 1, tm), lambda i: (jnp.minimum(i + 1, n_tiles - 1), 0, 0), memory_space=pltpu.SMEM),
            pl.BlockSpec((tm, D_MODEL), lambda i: (i, 0)),
            pl.BlockSpec(memory_space=pl.ANY),
            pl.BlockSpec((1, D_MODEL), lambda i: (0, 0)),
            pl.BlockSpec((1, D_MODEL), lambda i: (0, 0)),
        ],
        out_specs=pl.BlockSpec((tm, D_MODEL), lambda i: (i, 0)),
        out_shape=jax.ShapeDtypeStruct((t, D_MODEL), F32),
        scratch_shapes=[pltpu.VMEM((2, tm, D_MODEL), F32), pltpu.SemaphoreType.DMA((2,))],
        compiler_params=pltpu.CompilerParams(dimension_semantics=("arbitrary",), vmem_limit_bytes=32 * MIB),
        name="moe_gather_ln",
    )(dest3, dest3, x2, expert_out, g.reshape(1, D_MODEL), b.reshape(1, D_MODEL))


_CLASS_LO = np.array([4 * (c // 6) + PAIRS[c % 6][0] for c in range(N_CLASSES)], np.int32)
_CLASS_HI = np.array([4 * (c // 6) + PAIRS[c % 6][1] for c in range(N_CLASSES)], np.int32)


def _moe_layer(x2, wr_pad, br_pad, wg, wu, wd, ln_g, ln_b):
    t = x2.shape[0]
    nb = -(-(t + N_CLASSES * (MOE_BLOCK - 1)) // MOE_BLOCK)
    n_rows = nb * MOE_BLOCK
    meta_i, meta_f, cnt = _router(x2, wr_pad, br_pad)
    cls, rank = meta_i[0], meta_i[1]
    counts = cnt[:N_CLASSES, 0].astype(I32)
    padded = (counts + MOE_BLOCK - 1) // MOE_BLOCK * MOE_BLOCK
    ends = jnp.cumsum(padded)
    starts = ends - padded
    dest = starts[cls] + rank
    blk_start = jnp.arange(nb, dtype=I32) * MOE_BLOCK
    n_valid = jnp.sum((blk_start < ends[-1]).astype(I32))
    blk_cls = jnp.minimum(jnp.sum((ends[None, :] <= blk_start[:, None]).astype(I32), axis=1), N_CLASSES - 1)
    blk_cls = blk_cls[jnp.minimum(jnp.arange(nb), n_valid - 1)]
    blk_elo = jnp.asarray(_CLASS_LO)[blk_cls]
    blk_ehi = jnp.asarray(_CLASS_HI)[blk_cls]
    tok_meta = jnp.concatenate([meta_f[:2].T, jnp.arange(t, dtype=F32)[:, None], jnp.zeros((t, LANES - 3), F32)], 1)
    sorted_meta = jnp.zeros((n_rows + MOE_BLOCK, LANES), F32).at[dest].set(tok_meta)
    src_rows = sorted_meta[:, 2].astype(I32).reshape(nb + 1, 1, MOE_BLOCK)
    out = _experts(x2, src_rows, sorted_meta, blk_elo, blk_ehi, wg, wu, wd)
    return _gather_ln(x2, out, dest, ln_g, ln_b)


def _inproj_kernel(x_ref, w_ref, wgate_ref, z_ref, gates_ref, xb_s):
    @pl.when(pl.program_id(1) == 0)
    def _():
        xb_s[...] = x_ref[...].astype(BF16)
        gates_ref[...] = _dot(xb_s[...], wgate_ref[...])

    z_ref[...] = _dot(xb_s[...], w_ref[...]).astype(z_ref.dtype)


def _inproj(x2, w_main_bf16, w_gate_bf16):
    t = x2.shape[0]
    tm, tn = INPROJ_TM, INPROJ_TN
    n_main = w_main_bf16.shape[1]
    return pl.pallas_call(
        _inproj_kernel,
        grid=(t // tm, n_main // tn),
        in_specs=[
            pl.BlockSpec((tm, D_MODEL), lambda m, n: (m, 0)),
            pl.BlockSpec((D_MODEL, tn), lambda m, n: (0, n)),
            pl.BlockSpec((D_MODEL, LANES), lambda m, n: (0, 0)),
        ],
        out_specs=[
            pl.BlockSpec((tm, tn), lambda m, n: (m, n)),
            pl.BlockSpec((tm, LANES), lambda m, n: (m, 0)),
        ],
        out_shape=[
            jax.ShapeDtypeStruct((t, n_main), BF16),
            jax.ShapeDtypeStruct((t, LANES), F32),
        ],
        scratch_shapes=[pltpu.VMEM((tm, D_MODEL), BF16)],
        compiler_params=pltpu.CompilerParams(
            dimension_semantics=("parallel", "arbitrary"), vmem_limit_bytes=48 * MIB),
        name="mlstm_inproj",
    )(x2, w_main_bf16, w_gate_bf16)


def _mlstm_cell_kernel(zq_ref, zk_ref, v_ref, op_ref, gt_ref, bg_ref, cwq_ref, cbq_ref, cwk_ref, cbk_ref,
                       ng_ref, y_ref, q_s, k_s, gb_s, bc_s, gbt_s, bct_s, *, chunk, n_heads):
    hgroup = pl.program_id(1)
    seq = zq_ref.shape[0]

    def conv_silu(z_ref, cw_ref, cb_ref):
        z = z_ref[...].astype(F32)
        rowi = lax.broadcasted_iota(I32, z.shape, 0)
        out = cb_ref[...] + cw_ref[0:1, :] * jnp.where(rowi >= CONV_K - 1, pltpu.roll(z, CONV_K - 1, 0), 0.0)
        for j in range(1, CONV_K - 1):
            shift = CONV_K - 1 - j
            out = out + cw_ref[j:j + 1, :] * jnp.where(rowi >= shift, pltpu.roll(z, shift, 0), 0.0)
        out = out + cw_ref[CONV_K - 1:CONV_K, :] * z
        return out * jax.nn.sigmoid(out)

    q_s[...] = conv_silu(zq_ref, cwq_ref, cbq_ref).astype(BF16)
    k_s[...] = conv_silu(zk_ref, cwk_ref, cbk_ref) * (QK_DIM ** -0.5)

    lane = lax.broadcasted_iota(I32, (chunk, LANES), 1)
    sub = lax.broadcasted_iota(I32, (MLSTM_HEADS, chunk), 0)
    ti = lax.broadcasted_iota(I32, (chunk, chunk), 0)
    si = lax.broadcasted_iota(I32, (chunk, chunk), 1)
    causal = ti >= si
    ones_cols = jnp.ones((chunk, LANES), BF16)

    @pl.when(hgroup == 0)
    def _():
        tri = jnp.where(causal, 1.0, 0.0)
        bias = bg_ref[...]
        for c in range(seq // chunk):
            rs = slice(c * chunk, (c + 1) * chunk)
            gb = gt_ref[rs, :] + bias
            log_f = -(jnp.maximum(-gb, 0.0) + jnp.log1p(jnp.exp(-jnp.abs(gb))))
            bcum = _dot(tri, log_f, precision=HIGHEST)
            gb_s[rs, :] = gb
            bc_s[rs, :] = bcum
            gbt_s[:, rs] = gb.T
            bct_s[:, rs] = bcum.T

    def pick_col(a, idx):
        return jnp.sum(jnp.where(lane == idx, a, 0.0), axis=1, keepdims=True)

    def pick_row(a, idx):
        return jnp.sum(jnp.where(sub == idx, a, 0.0), axis=0, keepdims=True)

    heads = [hgroup * n_heads + i for i in range(n_heads)]
    qcols = [slice(i * QK_DIM, (i + 1) * QK_DIM) for i in range(n_heads)]
    vcols = [slice(i * V_DIM, (i + 1) * V_DIM) for i in range(n_heads)]
    hr = range(n_heads)
    c_state = [jnp.zeros((QK_DIM, V_DIM + LANES), F32) for _ in hr]
    m_state = [jnp.full((1, 1), NEG, F32) for _ in hr]
    for c in range(seq // chunk):
        rs = slice(c * chunk, (c + 1) * chunk)
        gb_c, bc_c = gb_s[rs, :], bc_s[rs, :]
        gbt_c, bct_c = gbt_s[0:MLSTM_HEADS, rs], bct_s[MLSTM_HEADS:2 * MLSTM_HEADS, rs]
        li_col = [pick_col(gb_c, hd) for hd in heads]
        bc_col = [pick_col(bc_c, hd + MLSTM_HEADS) for hd in heads]
        li_row = [pick_row(gbt_c, hd) for hd in heads]
        bc_row = [pick_row(bct_c, hd) for hd in heads]
        dmat = [jnp.where(causal, bc_col[i] + (li_row[i] - bc_row[i]), NEG) for i in hr]
        inter = [bc_col[i] + m_state[i] for i in hr]
        m_t = [jnp.maximum(inter[i], jnp.max(dmat[i], axis=1, keepdims=True)) for i in hr]
        qc = [q_s[rs, qcols[i]] for i in hr]
        kf = [k_s[rs, qcols[i]] for i in hr]
        vc = [v_ref[rs, vcols[i]] for i in hr]
        a = [_dot_nt(qc[i], kf[i].astype(BF16)) * jnp.exp(dmat[i] - m_t[i]) for i in hr]
        w_inter = [jnp.exp(inter[i] - m_t[i]) for i in hr]
        q_state = [_dot(qc[i], c_state[i].astype(BF16)) for i in hr]
        num = [_dot(a[i].astype(BF16), vc[i]) + w_inter[i] * q_state[i][:, :V_DIM] for i in hr]
        den = [jnp.sum(a[i], axis=1, keepdims=True) + w_inter[i] * q_state[i][:, V_DIM:V_DIM + 1] for i in hr]
        h_out = [num[i] / jnp.maximum(jnp.abs(den[i]), jnp.exp(-m_t[i])) for i in hr]
        mu = [jnp.mean(h_out[i], axis=1, keepdims=True) for i in hr]
        hc = [h_out[i] - mu[i] for i in hr]
        var = [jnp.mean(hc[i] * hc[i], axis=1, keepdims=True) for i in hr]
        for i in hr:
            hn = hc[i] * lax.rsqrt(var[i] + LN_EPS) * ng_ref[:, vcols[i]]
            y_ref[rs, vcols[i]] = (hn * jax.nn.sigmoid(op_ref[rs, vcols[i]].astype(F32))).astype(y_ref.dtype)
        b_last = [bc_col[i][chunk - 1:chunk, :] for i in hr]
        g = [b_last[i] - bc_col[i] + li_col[i] for i in hr]
        m_new = [jnp.maximum(b_last[i] + m_state[i], jnp.max(g[i], axis=0, keepdims=True)) for i in hr]
        wk = [jnp.exp(g[i] - m_new[i]) for i in hr]
        decay = [jnp.exp(b_last[i] + m_state[i] - m_new[i]) for i in hr]
        kw_t = [(wk[i] * kf[i]).T.astype(BF16) for i in hr]
        upd = [_dot(kw_t[i], jnp.concatenate([vc[i], ones_cols], axis=1)) for i in hr]
        c_state = [decay[i] * c_state[i] + upd[i] for i in hr]
        m_state = m_new


def _mlstm_cell(z, gates, b_gates_pad, conv_w, conv_b, norm_g, batch):
    t = z.shape[0]
    nh = MLSTM_HEADS_PER_STEP
    groups = MLSTM_HEADS // nh
    qw, vw = nh * QK_DIM, nh * V_DIM
    k_blk0 = (MLSTM_QK_COLS // 2) // qw
    v_blk0 = MLSTM_QK_COLS // vw
    o_blk0 = (MLSTM_QK_COLS + MLSTM_V_COLS) // vw
    return pl.pallas_call(
        functools.partial(_mlstm_cell_kernel, chunk=MLSTM_CHUNK, n_heads=nh),
        grid=(batch, groups),
        in_specs=[
            pl.BlockSpec((SEQ, qw), lambda b, h: (b, h)),
            pl.BlockSpec((SEQ, qw), lambda b, h: (b, k_blk0 + h)),
            pl.BlockSpec((SEQ, vw), lambda b, h: (b, v_blk0 + h)),
            pl.BlockSpec((SEQ, vw), lambda b, h: (b, o_blk0 + h)),
            pl.BlockSpec((SEQ, LANES), lambda b, h: (b, 0)),
            pl.BlockSpec((1, LANES), lambda b, h: (0, 0)),
            pl.BlockSpec((CONV_K, qw), lambda b, h: (0, h)),
            pl.BlockSpec((1, qw), lambda b, h: (0, h)),
            pl.BlockSpec((CONV_K, qw), lambda b, h: (0, k_blk0 + h)),
            pl.BlockSpec((1, qw), lambda b, h: (0, k_blk0 + h)),
            pl.BlockSpec((1, vw), lambda b, h: (0, h)),
        ],
        out_specs=pl.BlockSpec((SEQ, vw), lambda b, h: (b, h)),
        out_shape=jax.ShapeDtypeStruct((t, MLSTM_V_COLS), BF16),
        scratch_shapes=[pltpu.VMEM((SEQ, qw), BF16), pltpu.VMEM((SEQ, qw), F32),
                        pltpu.VMEM((SEQ, LANES), F32), pltpu.VMEM((SEQ, LANES), F32),
                        pltpu.VMEM((LANES, SEQ), F32), pltpu.VMEM((LANES, SEQ), F32)],
        compiler_params=pltpu.CompilerParams(
            dimension_semantics=("parallel", "arbitrary"), vmem_limit_bytes=56 * MIB),
        name="mlstm_cell",
    )(z, z, z, z, gates, b_gates_pad, conv_w, conv_b.reshape(1, -1), conv_w, conv_b.reshape(1, -1),
      norm_g.reshape(1, -1))


def kernel(x, attn_w_qkv, attn_w_o, mlstm_w_in, mlstm_b_gates, mlstm_conv_w, mlstm_conv_b, mlstm_norm_g,
           mlstm_w_out, ln_mix_g, ln_mix_b, ln_ffn_g, ln_ffn_b, router_w, router_b, moe_w_gate, moe_w_up,
           moe_w_down):
    batch, seq, d = x.shape
    assert (seq, d) == (SEQ, D_MODEL)
    t = batch * seq
    x2 = x.reshape(t, d)

    wr_f32 = jnp.zeros((D_MODEL, LANES), F32).at[:, :N_EXPERTS].set(router_w)
    wr_hi = lax.bitcast_convert_type(lax.bitcast_convert_type(wr_f32, jnp.uint32) & jnp.uint32(0xFFFF0000), F32)
    wr_pad = jnp.concatenate([wr_hi.astype(BF16), (wr_f32 - wr_hi).astype(BF16)], axis=1)
    br_pad = jnp.zeros((1, LANES), F32).at[0, :N_EXPERTS].set(router_b)
    wg_bf, wu_bf, wd_bf = moe_w_gate.astype(BF16), moe_w_up.astype(BF16), moe_w_down.astype(BF16)

    w_qkv = _rope_permute_qk(attn_w_qkv[0]).astype(BF16)
    views = [x] + list(_class_major_views(x))
    outs, lses = [], []
    for group, dil in enumerate(DILATIONS):
        o, lse = _attn_group(views[group], w_qkv, _rope_table(dil), group, dil)
        outs.append(o)
        lses.append(lse)
    x2 = _attn_out(outs, lses, x2, attn_w_o[0].astype(BF16), ln_mix_g[0], ln_mix_b[0])
    x2 = _moe_layer(x2, wr_pad, br_pad, wg_bf[0], wu_bf[0], wd_bf[0], ln_ffn_g[0], ln_ffn_b[0])

    w_in = mlstm_w_in[0]
    w_main = w_in[:, :MLSTM_MAIN_COLS].astype(BF16)
    w_gate = jnp.zeros((D_MODEL, LANES), F32).at[:, :2 * MLSTM_HEADS].set(w_in[:, MLSTM_MAIN_COLS:]).astype(BF16)
    bg_pad = jnp.zeros((1, LANES), F32).at[0, :2 * MLSTM_HEADS].set(mlstm_b_gates[0])
    z, gates = _inproj(x2, w_main, w_gate)
    y = _mlstm_cell(z, gates, bg_pad, mlstm_conv_w[0], mlstm_conv_b[0], mlstm_norm_g[0], batch)
    x2 = _mix_out(y, x2, mlstm_w_out[0].astype(BF16), ln_mix_g[1], ln_mix_b[1])
    x2 = _moe_layer(x2, wr_pad, br_pad, wg_bf[1], wu_bf[1], wd_bf[1], ln_ffn_g[1], ln_ffn_b[1])
    return x2.reshape(batch, seq, d)
```

```python
import functools

import numpy as np
import jax
import jax.numpy as jnp
from jax import lax
from jax.experimental import pallas as pl
from jax.experimental.pallas import tpu as pltpu

F32 = jnp.float32
BF16 = jnp.bfloat16
I32 = jnp.int32
HIGHEST = lax.Precision.HIGHEST

D_MODEL = 2048
SEQ = 2048
DEPTH = 2
DILATIONS = (1, 4, 16)
N_BACK = 128
ATTN_HEADS = 8
HEAD_DIM = 128
ROT_DIM = HEAD_DIM // 4
ROPE_THETA = 500000.0
ATTN_BLOCK = 128
ATTN_COLS = ATTN_HEADS * HEAD_DIM

MLSTM_HEADS = 8
QK_DIM = 128
V_DIM = D_MODEL // MLSTM_HEADS
CONV_K = 4
MLSTM_QK_COLS = 2 * MLSTM_HEADS * QK_DIM
MLSTM_V_COLS = MLSTM_HEADS * V_DIM
MLSTM_MAIN_COLS = MLSTM_QK_COLS + 2 * MLSTM_V_COLS

N_EXPERTS = 16
N_EXPERT_GROUPS = 4
EXPERTS_PER_GROUP = 4
D_EXPERT = 768
PAIRS = ((0, 1), (0, 2), (0, 3), (1, 2), (1, 3), (2, 3))
N_CLASSES = N_EXPERT_GROUPS * len(PAIRS)

ALPHA = (2 * DEPTH) ** 0.25
LN_EPS = 1e-5
NEG = -1e30

LANES = 128
V7X_VMEM_BYTES = 64 * 1024 * 1024
MIB = 1024 * 1024

ATTN_STEP_ROWS = 512
ATTN_HEAD_SET = 8
ROPE_PAIR_SHIFT = 64
ROW_TILE = 512
INPROJ_TM = 1024
INPROJ_TN = 1024
MLSTM_CHUNK = 128
MLSTM_HEADS_PER_STEP = 4
MOE_BLOCK = 256
GATHER_TM = 256
ROW_GATHER_DMA_PRIORITY = 1


def _dot(a, b, **kw):
    return jnp.dot(a, b, preferred_element_type=F32, **kw)


def _dot_nt(a, b):
    return lax.dot_general(a, b, (((1,), (1,)), ((), ())), preferred_element_type=F32)


def _layer_norm_rows(y, g, b):
    mu = jnp.mean(y, axis=-1, keepdims=True)
    yc = y - mu
    var = jnp.mean(yc * yc, axis=-1, keepdims=True)
    return yc * lax.rsqrt(var + LN_EPS) * g + b


def _attn_group_kernel(x_ref, w_ref, tab_ref, o_ref, lse_ref, q_s, k_s, v_s, *, n_cls, lc, carry):
    step = pl.program_id(1)
    rows_total = n_cls * lc
    blocks_per_class = lc // ATTN_BLOCK

    if n_cls == 1:
        xs = x_ref[0]
    else:
        xs = jnp.concatenate([x_ref[0, :, c * D_MODEL:(c + 1) * D_MODEL] for c in range(n_cls)], axis=0)
    qkv = _dot(xs.astype(BF16), w_ref[...])

    if carry:
        @pl.when(step == 0)
        def _():
            k_s[0:ATTN_BLOCK, :] = jnp.zeros((ATTN_BLOCK, ATTN_COLS), BF16)
            v_s[0:ATTN_BLOCK, :] = jnp.zeros((ATTN_BLOCK, ATTN_COLS), BF16)

        @pl.when(step > 0)
        def _():
            k_s[0:ATTN_BLOCK, :] = k_s[rows_total:rows_total + ATTN_BLOCK, :]
            v_s[0:ATTN_BLOCK, :] = v_s[rows_total:rows_total + ATTN_BLOCK, :]

    cosf = tab_ref[:, 0:LANES]
    sinr = tab_ref[:, LANES:2 * LANES]

    def rope(t):
        return t * cosf + pltpu.roll(t, ROPE_PAIR_SHIFT, 1) * sinr

    for h in range(ATTN_HEADS):
        cs = slice(h * HEAD_DIM, (h + 1) * HEAD_DIM)
        q_s[:, cs] = rope(qkv[:, h * HEAD_DIM:(h + 1) * HEAD_DIM]).astype(BF16)
        k_s[ATTN_BLOCK:, cs] = rope(qkv[:, ATTN_COLS + h * HEAD_DIM:ATTN_COLS + (h + 1) * HEAD_DIM]).astype(BF16)
    v_s[ATTN_BLOCK:, :] = qkv[:, 2 * ATTN_COLS:3 * ATTN_COLS].astype(BF16)

    row = lax.broadcasted_iota(I32, (ATTN_BLOCK, ATTN_BLOCK), 0)
    col = lax.broadcasted_iota(I32, (ATTN_BLOCK, ATTN_BLOCK), 1)
    row2 = lax.broadcasted_iota(I32, (ATTN_BLOCK, 2 * ATTN_BLOCK), 0)
    col2 = lax.broadcasted_iota(I32, (ATTN_BLOCK, 2 * ATTN_BLOCK), 1)
    lane = lax.broadcasted_iota(I32, (ATTN_BLOCK, LANES), 1)
    mask_cur = col <= row
    mask_both = jnp.logical_and(col2 >= row2, col2 <= row2 + N_BACK)
    if carry:
        mask_first = jnp.logical_and(mask_both, jnp.logical_or(col2 >= ATTN_BLOCK, step > 0))
    scale = HEAD_DIM ** -0.5

    for c in range(n_cls):
        for bi in range(blocks_per_class):
            j = c * blocks_per_class + bi
            ors = slice(bi * ATTN_BLOCK, (bi + 1) * ATTN_BLOCK)
            with_prev = bi > 0 or carry
            if with_prev:
                krows = slice(j * ATTN_BLOCK, (j + 2) * ATTN_BLOCK)
                mask = mask_first if bi == 0 else mask_both
            else:
                krows = slice((j + 1) * ATTN_BLOCK, (j + 2) * ATTN_BLOCK)
                mask = mask_cur
            lse_tile = jnp.zeros((ATTN_BLOCK, LANES), F32)
            for h0 in range(0, ATTN_HEADS, ATTN_HEAD_SET):
                heads = range(h0, h0 + ATTN_HEAD_SET)
                cols = [slice(h * HEAD_DIM, (h + 1) * HEAD_DIM) for h in heads]
                scores = [jnp.where(mask, _dot_nt(q_s[j * ATTN_BLOCK:(j + 1) * ATTN_BLOCK, cs], k_s[krows, cs])
                                    * scale, NEG) for cs in cols]
                maxes = [jnp.max(s, axis=1, keepdims=True) for s in scores]
                probs = [jnp.exp(s - m) for s, m in zip(scores, maxes)]
                dens = [jnp.sum(p, axis=1, keepdims=True) for p in probs]
                accs = [_dot(p.astype(BF16), v_s[krows, cs]) for p, cs in zip(probs, cols)]
                for h, acc, den, m in zip(heads, accs, dens, maxes):
                    o_ref[0, ors, c * ATTN_COLS + h * HEAD_DIM:c * ATTN_COLS + (h + 1) * HEAD_DIM] = (
                        acc / den).astype(o_ref.dtype)
                    lse_tile = jnp.where(lane == h, m + jnp.log(den), lse_tile)
            lse_ref[0, ors, c * LANES:(c + 1) * LANES] = lse_tile


def _class_major_perm(dil):
    width = ATTN_STEP_ROWS // dil
    perm = np.zeros((ATTN_STEP_ROWS, ATTN_STEP_ROWS), np.float32)
    for r in range(dil):
        for m in range(width):
            perm[r * width + m, m * dil + r] = 1.0
    return perm


def _class_major_kernel(x_ref, *refs):
    n = len(DILATIONS) - 1
    xb = x_ref[0].astype(BF16)
    for p_ref, out_ref, dil in zip(refs[:n], refs[n:], DILATIONS[1:]):
        width = ATTN_STEP_ROWS // dil
        rows = _dot(p_ref[...], xb).astype(BF16)
        for r in range(dil):
            out_ref[0, :, r * D_MODEL:(r + 1) * D_MODEL] = rows[r * width:(r + 1) * width, :]


def _class_major_views(x3):
    batch = x3.shape[0]
    perms = [jnp.asarray(_class_major_perm(d), BF16) for d in DILATIONS[1:]]
    return pl.pallas_call(
        _class_major_kernel,
        grid=(batch, SEQ // ATTN_STEP_ROWS),
        in_specs=[pl.BlockSpec((1, ATTN_STEP_ROWS, D_MODEL), lambda b, s: (b, s, 0))] + [
            pl.BlockSpec((ATTN_STEP_ROWS, ATTN_STEP_ROWS), lambda b, s: (0, 0)) for _ in perms],
        out_specs=[pl.BlockSpec((1, ATTN_STEP_ROWS // d, d * D_MODEL), lambda b, s: (b, s, 0)) for d in DILATIONS[1:]],
        out_shape=[jax.ShapeDtypeStruct((batch, SEQ // d, d * D_MODEL), BF16) for d in DILATIONS[1:]],
        compiler_params=pltpu.CompilerParams(dimension_semantics=("parallel", "parallel"), vmem_limit_bytes=40 * MIB),
        name="class_major_views",
    )(x3, *perms)


def _attn_group(xg, w_qkv_bf16, tab, group, dil):
    batch = xg.shape[0]
    per_class = SEQ // dil
    lc = min(ATTN_STEP_ROWS, per_class)
    n_cls = ATTN_STEP_ROWS // lc
    steps = SEQ // ATTN_STEP_ROWS
    carry = dil == 1
    if dil == 1:
        imap = lambda b, s: (b, s, 0)
    else:
        imap = lambda b, s: (b, 0, s)
    kern = functools.partial(_attn_group_kernel, n_cls=n_cls, lc=lc, carry=carry)
    o, lse = pl.pallas_call(
        kern,
        grid=(batch, steps),
        in_specs=[
            pl.BlockSpec((1, lc, n_cls * D_MODEL), imap),
            pl.BlockSpec((D_MODEL, 3 * ATTN_COLS), lambda b, s: (0, group), pipeline_mode=pl.Buffered(1)),
            pl.BlockSpec((ATTN_STEP_ROWS, 2 * LANES), lambda b, s: (s, 0)),
        ],
        out_specs=[
            pl.BlockSpec((1, lc, n_cls * ATTN_COLS), imap),
            pl.BlockSpec((1, lc, n_cls * LANES), imap),
        ],
        out_shape=[
            jax.ShapeDtypeStruct((batch, per_class, dil * ATTN_COLS), BF16),
            jax.ShapeDtypeStruct((batch, per_class, dil * LANES), F32),
        ],
        scratch_shapes=[
            pltpu.VMEM((ATTN_STEP_ROWS, ATTN_COLS), BF16),
            pltpu.VMEM((ATTN_BLOCK + ATTN_STEP_ROWS, ATTN_COLS), BF16),
            pltpu.VMEM((ATTN_BLOCK + ATTN_STEP_ROWS, ATTN_COLS), BF16),
        ],
        compiler_params=pltpu.CompilerParams(
            dimension_semantics=("parallel", "arbitrary"), vmem_limit_bytes=56 * MIB),
        name=f"attn_group{group}",
    )(xg, w_qkv_bf16, tab)
    return o, lse


def _rope_permute_qk(w_qkv):
    half = ROT_DIM // 2
    lo, hi = ROPE_PAIR_SHIFT, ROPE_PAIR_SHIFT + half
    w = w_qkv.reshape(D_MODEL, len(DILATIONS), 3, ATTN_HEADS, HEAD_DIM)
    qk = w[:, :, 0:2]
    qk = jnp.concatenate([qk[..., :half], qk[..., lo:hi], qk[..., ROT_DIM:lo], qk[..., half:ROT_DIM], qk[..., hi:]], -1)
    return jnp.concatenate([qk, w[:, :, 2:3]], axis=2).reshape(w_qkv.shape)


def _rope_table(dil):
    inv_freq = ROPE_THETA ** (-np.arange(0, ROT_DIM, 2, dtype=np.float64) / ROT_DIM)
    ang = np.arange(SEQ, dtype=np.float64)[:, None] * inv_freq[None, :]
    ang = np.concatenate([ang, ang], -1)
    cos, sin = np.cos(ang), np.sin(ang)
    half = ROT_DIM // 2
    gap = ROPE_PAIR_SHIFT - half
    tail = LANES - ROPE_PAIR_SHIFT - half
    cosf = np.concatenate([cos[:, :half], np.ones((SEQ, gap)), cos[:, half:], np.ones((SEQ, tail))], 1)
    sinr = np.concatenate([-sin[:, :half], np.zeros((SEQ, gap)), sin[:, half:], np.zeros((SEQ, tail))], 1)
    tab = np.concatenate([cosf, sinr], 1)
    tab = tab.reshape(SEQ // dil, dil, 2 * LANES).transpose(1, 0, 2).reshape(SEQ, 2 * LANES)
    return jnp.asarray(tab.astype(np.float32))


def _attn_out_kernel(o0_ref, o1_ref, o2_ref, l0_ref, l1_ref, l2_ref, pt1_ref, pt2_ref, x_ref, w_ref, g_ref, b_ref,
                     out_ref, l1_s, l2_s):
    o_nat = [None]
    for o_ref, l_ref, pt_ref, l_s, dil in ((o1_ref, l1_ref, pt1_ref, l1_s, DILATIONS[1]),
                                          (o2_ref, l2_ref, pt2_ref, l2_s, DILATIONS[2])):
        width = ATTN_STEP_ROWS // dil
        o_cm = jnp.concatenate([o_ref[0, :, r * ATTN_COLS:(r + 1) * ATTN_COLS] for r in range(dil)], axis=0)
        o_nat.append(_dot(pt_ref[...], o_cm))
        for r in range(dil):
            l_s[pl.ds(r, width, stride=dil), :] = l_ref[0, :, r * LANES:(r + 1) * LANES]
    ls = [l0_ref[0], l1_s[...], l2_s[...]]
    mx = jnp.maximum(jnp.maximum(ls[0], ls[1]), ls[2])
    es = [jnp.exp(l - mx) for l in ls]
    den = es[0] + es[1] + es[2]
    ws = [e / den for e in es]
    parts = []
    for h in range(ATTN_HEADS):
        cs = slice(h * HEAD_DIM, (h + 1) * HEAD_DIM)
        acc = ws[0][:, h:h + 1] * o0_ref[0, :, cs].astype(F32)
        acc = acc + ws[1][:, h:h + 1] * o_nat[1][:, cs]
        acc = acc + ws[2][:, h:h + 1] * o_nat[2][:, cs]
        parts.append(acc)
    mixed_in = jnp.concatenate(parts, axis=1).astype(BF16)
    y = ALPHA * x_ref[...] + _dot(mixed_in, w_ref[...])
    out_ref[...] = _layer_norm_rows(y, g_ref[...], b_ref[...])


def _attn_out(outs, lses, x2, w_bf16, g, b):
    t = x2.shape[0]
    steps = SEQ // ATTN_STEP_ROWS
    view_map = lambda i: (i // steps, i % steps, 0)
    view_spec = lambda dil, width: pl.BlockSpec((1, ATTN_STEP_ROWS // dil, dil * width), view_map)
    const_spec = lambda shape: pl.BlockSpec(shape, lambda i: (0, 0))
    row_spec = pl.BlockSpec((ATTN_STEP_ROWS, D_MODEL), lambda i: (i, 0))
    perms_t = [jnp.asarray(_class_major_perm(d).T, BF16) for d in DILATIONS[1:]]
    return pl.pallas_call(
        _attn_out_kernel,
        grid=(t // ATTN_STEP_ROWS,),
        in_specs=[view_spec(d, ATTN_COLS) for d in DILATIONS] + [view_spec(d, LANES) for d in DILATIONS] + [
            const_spec((ATTN_STEP_ROWS, ATTN_STEP_ROWS)) for _ in perms_t] + [
            row_spec, pl.BlockSpec((ATTN_COLS, D_MODEL), lambda i: (0, 0), pipeline_mode=pl.Buffered(1)),
            const_spec((1, D_MODEL)), const_spec((1, D_MODEL))],
        out_specs=row_spec,
        out_shape=jax.ShapeDtypeStruct((t, D_MODEL), F32),
        scratch_shapes=[pltpu.VMEM((ATTN_STEP_ROWS, LANES), F32), pltpu.VMEM((ATTN_STEP_ROWS, LANES), F32)],
        compiler_params=pltpu.CompilerParams(dimension_semantics=("parallel",), vmem_limit_bytes=48 * MIB),
        name="attn_out",
    )(*outs, *lses, *perms_t, x2, w_bf16, g.reshape(1, D_MODEL), b.reshape(1, D_MODEL))


def _mix_out_kernel(y_ref, x_ref, w_ref, g_ref, b_ref, out_ref):
    y = ALPHA * x_ref[...] + _dot(y_ref[...], w_ref[...])
    out_ref[...] = _layer_norm_rows(y, g_ref[...], b_ref[...])


def _mix_out(mixer_y, x2, w_bf16, g, b):
    t = x2.shape[0]
    k = w_bf16.shape[0]
    row_spec = lambda width: pl.BlockSpec((ROW_TILE, width), lambda i: (i, 0))
    const_spec = lambda shape: pl.BlockSpec(shape, lambda i: (0, 0))
    return pl.pallas_call(
        _mix_out_kernel,
        grid=(t // ROW_TILE,),
        in_specs=[row_spec(k), row_spec(D_MODEL),
                  pl.BlockSpec((k, D_MODEL), lambda i: (0, 0), pipeline_mode=pl.Buffered(1)),
                  const_spec((1, D_MODEL)), const_spec((1, D_MODEL))],
        out_specs=row_spec(D_MODEL),
        out_shape=jax.ShapeDtypeStruct((t, D_MODEL), F32),
        compiler_params=pltpu.CompilerParams(dimension_semantics=("parallel",), vmem_limit_bytes=48 * MIB),
        name="mix_out",
    )(mixer_y, x2, w_bf16, g.reshape(1, D_MODEL), b.reshape(1, D_MODEL))


def _router_kernel(x_ref, wr_ref, br_ref, mi_ref, mf_ref, cnt_ref, carry_s, *, tm):
    i = pl.program_id(0)

    @pl.when(i == 0)
    def _():
        carry_s[...] = jnp.zeros_like(carry_s)

    x = x_ref[...]
    x_hi = x.astype(BF16)
    x_lo = (x - x_hi.astype(F32)).astype(BF16)
    hi_terms = _dot(x_hi, wr_ref[...])
    logits = (hi_terms[:, :LANES] + _dot(x_lo, wr_ref[:, :LANES])) + hi_terms[:, LANES:] + br_ref[...]
    lt = logits.T
    l = [lt[e:e + 1, :] for e in range(N_EXPERTS)]
    mx = l[0]
    for e in range(1, N_EXPERTS):
        mx = jnp.maximum(mx, l[e])
    ex = [jnp.exp(v - mx) for v in l]
    tot = ex[0]
    for e in range(1, N_EXPERTS):
        tot = tot + ex[e]
    p = [v / tot for v in ex]

    def first_index_of(vals, target):
        idx = jnp.full_like(target, len(vals) - 1).astype(I32)
        for k in range(len(vals) - 2, -1, -1):
            idx = jnp.where(vals[k] == target, k, idx)
        return idx

    best = None
    for g in range(N_EXPERT_GROUPS):
        pg = p[g * EXPERTS_PER_GROUP:(g + 1) * EXPERTS_PER_GROUP]
        top1 = jnp.maximum(jnp.maximum(pg[0], pg[1]), jnp.maximum(pg[2], pg[3]))
        i1 = first_index_of(pg, top1)
        rest = [jnp.where(i1 == k, -1.0, pg[k]) for k in range(EXPERTS_PER_GROUP)]
        top2 = jnp.maximum(jnp.maximum(rest[0], rest[1]), jnp.maximum(rest[2], rest[3]))
        i2 = first_index_of(rest, top2)
        score = top1 + top2
        if best is None:
            best = (score, jnp.zeros_like(i1), top1, top2, i1, i2)
        else:
            better = score > best[0]
            cand = (score, jnp.full_like(i1, g), top1, top2, i1, i2)
            best = tuple(jnp.where(better, cv, bv) for cv, bv in zip(cand, best))
    _, g_sel, p1, p2, i1, i2 = best
    psum = p1 + p2
    gate1, gate2 = p1 / psum, p2 / psum
    first_low = i1 < i2
    lo = jnp.where(first_low, i1, i2)
    hi = jnp.where(first_low, i2, i1)
    gate_lo = jnp.where(first_low, gate1, gate2)
    gate_hi = jnp.where(first_low, gate2, gate1)
    pair = jnp.where(lo == 0, 0, jnp.where(lo == 1, 3, 5)) + hi - lo - 1
    cls = g_sel * len(PAIRS) + pair

    n_rows = carry_s.shape[0]
    sub = lax.broadcasted_iota(I32, (n_rows, tm), 0)
    onehot = sub == cls
    oh = jnp.where(onehot, 1.0, 0.0)
    upper = (lax.broadcasted_iota(I32, (tm, tm), 0) <= lax.broadcasted_iota(I32, (tm, tm), 1))
    cum = _dot(oh.astype(BF16), jnp.where(upper, 1.0, 0.0).astype(BF16))
    carry = carry_s[:, 0:1]
    rank = jnp.sum(jnp.where(onehot, cum - 1.0 + carry, 0.0), axis=0, keepdims=True)
    carry_new = carry + jnp.sum(oh, axis=1, keepdims=True)
    carry_s[...] = jnp.broadcast_to(carry_new, carry_s.shape)
    cnt_ref[...] = jnp.broadcast_to(carry_new, cnt_ref.shape)

    sub8 = lax.broadcasted_iota(I32, (8, tm), 0)
    mi_ref[...] = jnp.where(sub8 == 0, cls, jnp.where(sub8 == 1, rank.astype(I32), 0))
    mf_ref[...] = jnp.where(sub8 == 0, gate_lo, jnp.where(sub8 == 1, gate_hi, 0.0))


def _router(x2, wr_pad, br_pad):
    t = x2.shape[0]
    tm = ROW_TILE
    return pl.pallas_call(
        functools.partial(_router_kernel, tm=tm),
        grid=(t // tm,),
        in_specs=[
            pl.BlockSpec((tm, D_MODEL), lambda i: (i, 0)),
            pl.BlockSpec((D_MODEL, 2 * LANES), lambda i: (0, 0)),
            pl.BlockSpec((1, LANES), lambda i: (0, 0)),
        ],
        out_specs=[
            pl.BlockSpec((8, tm), lambda i: (0, i)),
            pl.BlockSpec((8, tm), lambda i: (0, i)),
            pl.BlockSpec((32, LANES), lambda i: (0, 0)),
        ],
        out_shape=[
            jax.ShapeDtypeStruct((8, t), I32),
            jax.ShapeDtypeStruct((8, t), F32),
            jax.ShapeDtypeStruct((32, LANES), F32),
        ],
        scratch_shapes=[pltpu.VMEM((32, LANES), F32)],
        compiler_params=pltpu.CompilerParams(dimension_semantics=("arbitrary",), vmem_limit_bytes=32 * MIB),
        name="moe_router",
    )(x2, wr_pad, br_pad)


def _expert_kernel(elo_ref, ehi_ref, valid_ref, src_first_ref, src_next_ref, x_hbm, gs_ref, wg0, wu0, wd0, wg1, wu1, wd1,
                   out_ref, rows_s, sems):
    del elo_ref, ehi_ref
    i = pl.program_id(0)
    slot = i % 2

    def start_row(idx_ref, t, to_slot):
        pltpu.make_async_copy(x_hbm.at[pl.ds(idx_ref[0, 0, t], 1)], rows_s.at[to_slot, pl.ds(t, 1)],
                              sems.at[to_slot]).start(priority=ROW_GATHER_DMA_PRIORITY)

    def wait_rows(of_slot):
        pltpu.make_async_copy(x_hbm.at[pl.ds(0, MOE_BLOCK)], rows_s.at[of_slot], sems.at[of_slot]).wait()

    @pl.when(i == 0)
    def _():
        def issue(t, c):
            start_row(src_first_ref, t, slot)
            return c

        lax.fori_loop(0, MOE_BLOCK, issue, 0, unroll=8)

    wait_rows(slot)
    for t in range(MOE_BLOCK):
        start_row(src_next_ref, t, 1 - slot)

    @pl.when(valid_ref[i] > 0)
    def _():
        xb = rows_s[slot].astype(BF16)

        def ffn(wg, wu, wd):
            hg = _dot(xb, wg[0])
            hu = _dot(xb, wu[0])
            hidden = (hg * jax.nn.sigmoid(hg)) * hu
            return _dot(hidden.astype(BF16), wd[0])

        gs = gs_ref[...]
        out_ref[...] = gs[:, 0:1] * ffn(wg0, wu0, wd0) + gs[:, 1:2] * ffn(wg1, wu1, wd1)

    @pl.when(valid_ref[i] == 0)
    def _():
        out_ref[...] = jnp.zeros_like(out_ref)

    @pl.when(i == pl.num_programs(0) - 1)
    def _():
        wait_rows(1 - slot)


def _experts(x2, src_rows, gates_sorted, blk_elo, blk_ehi, blk_valid, wg, wu, wd):
    nb = src_rows.shape[0] - 1
    n_rows = nb * MOE_BLOCK
    lo_map = lambda b, elo, ehi, valid: (elo[b], 0, 0)
    hi_map = lambda b, elo, ehi, valid: (ehi[b], 0, 0)
    row_map = lambda b, elo, ehi, valid: (b, 0)
    up_shape = (1, D_MODEL, D_EXPERT)
    down_shape = (1, D_EXPERT, D_MODEL)
    grid_spec = pltpu.PrefetchScalarGridSpec(
        num_scalar_prefetch=3,
        grid=(nb,),
        in_specs=[
            pl.BlockSpec((1, 1, MOE_BLOCK), lambda b, elo, ehi, valid: (0, 0, 0), memory_space=pltpu.SMEM),
            pl.BlockSpec((1, 1, MOE_BLOCK), lambda b, elo, ehi, valid: (b + 1, 0, 0), memory_space=pltpu.SMEM),
            pl.BlockSpec(memory_space=pl.ANY),
            pl.BlockSpec((MOE_BLOCK, LANES), row_map),
            pl.BlockSpec(up_shape, lo_map), pl.BlockSpec(up_shape, lo_map), pl.BlockSpec(down_shape, lo_map),
            pl.BlockSpec(up_shape, hi_map), pl.BlockSpec(up_shape, hi_map), pl.BlockSpec(down_shape, hi_map),
        ],
        out_specs=pl.BlockSpec((MOE_BLOCK, D_MODEL), row_map),
        scratch_shapes=[pltpu.VMEM((2, MOE_BLOCK, D_MODEL), F32), pltpu.SemaphoreType.DMA((2,))],
    )
    return pl.pallas_call(
        _expert_kernel,
        grid_spec=grid_spec,
        out_shape=jax.ShapeDtypeStruct((n_rows, D_MODEL), F32),
        compiler_params=pltpu.CompilerParams(dimension_semantics=("arbitrary",), vmem_limit_bytes=58 * MIB),
        name="moe_experts",
    )(blk_elo, blk_ehi, blk_valid, src_rows, src_rows, x2, gates_sorted, wg, wu, wd, wg, wu, wd)


def _gather_ln_kernel(dest_ref, dest_next_ref, x_ref, src_hbm, g_ref, b_ref, y_ref, rows_s, sems, *, tm):
    i = pl.program_id(0)
    slot = i % 2

    def issue_rows(idx_ref, to_slot):
        def issue(t, c):
            pltpu.make_async_copy(src_hbm.at[pl.ds(idx_ref[0, 0, t], 1)], rows_s.at[to_slot, pl.ds(t, 1)],
                                  sems.at[to_slot]).start()
            return c

        lax.fori_loop(0, tm, issue, 0, unroll=8)

    @pl.when(i == 0)
    def _():
        issue_rows(dest_ref, slot)

    @pl.when(i + 1 < pl.num_programs(0))
    def _():
        issue_rows(dest_next_ref, 1 - slot)

    pltpu.make_async_copy(src_hbm.at[pl.ds(0, tm)], rows_s.at[slot], sems.at[slot]).wait()
    y = ALPHA * x_ref[...] + rows_s[slot]
    y_ref[...] = _layer_norm_rows(y, g_ref[...], b_ref[...])


def _gather_ln(x2, expert_out, dest, g, b):
    t = x2.shape[0]
    tm = GATHER_TM
    n_tiles = t // tm
    dest3 = dest.reshape(n_tiles, 1, tm)
    return pl.pallas_call(
        functools.partial(_gather_ln_kernel, tm=tm),
        grid=(n_tiles,),
        in_specs=[
            pl.BlockSpec((1, 1, tm), lambda i: (i, 0, 0), memory_space=pltpu.SMEM),
            pl.BlockSpec((1, 1, tm), lambda i: (jnp.minimum(i + 1, n_tiles - 1), 0, 0), memory_space=pltpu.SMEM),
            pl.BlockSpec((tm, D_MODEL), lambda i: (i, 0)),
            pl.BlockSpec(memory_space=pl.ANY),
            pl.BlockSpec((1, D_MODEL), lambda i: (0, 0)),
            pl.BlockSpec((1, D_MODEL), lambda i: (0, 0)),
        ],
        out_specs=pl.BlockSpec((tm, D_MODEL), lambda i: (i, 0)),
        out_shape=jax.ShapeDtypeStruct((t, D_MODEL), F32),
        scratch_shapes=[pltpu.VMEM((2, tm, D_MODEL), F32), pltpu.SemaphoreType.DMA((2,))],
        compiler_params=pltpu.CompilerParams(dimension_semantics=("arbitrary",), vmem_limit_bytes=32 * MIB),
        name="moe_gather_ln",
    )(dest3, dest3, x2, expert_out, g.reshape(1, D_MODEL), b.reshape(1, D_MODEL))


_CLASS_LO = np.array([4 * (c // 6) + PAIRS[c % 6][0] for c in range(N_CLASSES)], np.int32)
_CLASS_HI = np.array([4 * (c // 6) + PAIRS[c % 6][1] for c in range(N_CLASSES)], np.int32)


def _moe_layer(x2, wr_pad, br_pad, wg, wu, wd, ln_g, ln_b):
    t = x2.shape[0]
    nb = -(-(t + N_CLASSES * (MOE_BLOCK - 1)) // MOE_BLOCK)
    n_rows = nb * MOE_BLOCK
    meta_i, meta_f, cnt = _router(x2, wr_pad, br_pad)
    cls, rank = meta_i[0], meta_i[1]
    counts = cnt[:N_CLASSES, 0].astype(I32)
    padded = (counts + MOE_BLOCK - 1) // MOE_BLOCK * MOE_BLOCK
    ends = jnp.cumsum(padded)
    starts = ends - padded
    dest = starts[cls] + rank
    blk_start = jnp.arange(nb, dtype=I32) * MOE_BLOCK
    blk_valid = (blk_start < ends[-1]).astype(I32)
    n_valid = jnp.sum(blk_valid)
    blk_cls = jnp.minimum(jnp.sum((ends[None, :] <= blk_start[:, None]).astype(I32), axis=1), N_CLASSES - 1)
    blk_cls = blk_cls[jnp.minimum(jnp.arange(nb), n_valid - 1)]
    blk_elo = jnp.asarray(_CLASS_LO)[blk_cls]
    blk_ehi = jnp.asarray(_CLASS_HI)[blk_cls]
    tok_meta = jnp.concatenate([meta_f[:2].T, jnp.arange(t, dtype=F32)[:, None], jnp.zeros((t, LANES - 3), F32)], 1)
    sorted_meta = jnp.zeros((n_rows + MOE_BLOCK, LANES), F32).at[dest].set(tok_meta)
    src_rows = sorted_meta[:, 2].astype(I32).reshape(nb + 1, 1, MOE_BLOCK)
    out = _experts(x2, src_rows, sorted_meta, blk_elo, blk_ehi, blk_valid, wg, wu, wd)
    return _gather_ln(x2, out, dest, ln_g, ln_b)


def _inproj_kernel(x_ref, w_ref, wgate_ref, z_ref, gates_ref, xb_s):
    @pl.when(pl.program_id(1) == 0)
    def _():
        xb_s[...] = x_ref[...].astype(BF16)
        gates_ref[...] = _dot(xb_s[...], wgate_ref[...])

    z_ref[...] = _dot(xb_s[...], w_ref[...]).astype(z_ref.dtype)


def _inproj(x2, w_main_bf16, w_gate_bf16):
    t = x2.shape[0]
    tm, tn = INPROJ_TM, INPROJ_TN
    n_main = w_main_bf16.shape[1]
    return pl.pallas_call(
        _inproj_kernel,
        grid=(t // tm, n_main // tn),
        in_specs=[
            pl.BlockSpec((tm, D_MODEL), lambda m, n: (m, 0)),
            pl.BlockSpec((D_MODEL, tn), lambda m, n: (0, n)),
            pl.BlockSpec((D_MODEL, LANES), lambda m, n: (0, 0)),
        ],
        out_specs=[
            pl.BlockSpec((tm, tn), lambda m, n: (m, n)),
            pl.BlockSpec((tm, LANES), lambda m, n: (m, 0)),
        ],
        out_shape=[
            jax.ShapeDtypeStruct((t, n_main), BF16),
            jax.ShapeDtypeStruct((t, LANES), F32),
        ],
        scratch_shapes=[pltpu.VMEM((tm, D_MODEL), BF16)],
        compiler_params=pltpu.CompilerParams(
            dimension_semantics=("parallel", "arbitrary"), vmem_limit_bytes=48 * MIB),
        name="mlstm_inproj",
    )(x2, w_main_bf16, w_gate_bf16)


def _mlstm_cell_kernel(zq_ref, zk_ref, v_ref, op_ref, gt_ref, bg_ref, cwq_ref, cbq_ref, cwk_ref, cbk_ref,
                       ng_ref, y_ref, q_s, k_s, gb_s, bc_s, gbt_s, bct_s, *, chunk, n_heads):
    hgroup = pl.program_id(1)
    seq = zq_ref.shape[0]

    def conv_silu(z_ref, cw_ref, cb_ref):
        z = z_ref[...].astype(F32)
        rowi = lax.broadcasted_iota(I32, z.shape, 0)
        out = cb_ref[...] + cw_ref[0:1, :] * jnp.where(rowi >= CONV_K - 1, pltpu.roll(z, CONV_K - 1, 0), 0.0)
        for j in range(1, CONV_K - 1):
            shift = CONV_K - 1 - j
            out = out + cw_ref[j:j + 1, :] * jnp.where(rowi >= shift, pltpu.roll(z, shift, 0), 0.0)
        out = out + cw_ref[CONV_K - 1:CONV_K, :] * z
        return out * jax.nn.sigmoid(out)

    q_s[...] = conv_silu(zq_ref, cwq_ref, cbq_ref).astype(BF16)
    k_s[...] = conv_silu(zk_ref, cwk_ref, cbk_ref) * (QK_DIM ** -0.5)

    lane = lax.broadcasted_iota(I32, (chunk, LANES), 1)
    sub = lax.broadcasted_iota(I32, (MLSTM_HEADS, chunk), 0)
    ti = lax.broadcasted_iota(I32, (chunk, chunk), 0)
    si = lax.broadcasted_iota(I32, (chunk, chunk), 1)
    causal = ti >= si
    ones_cols = jnp.ones((chunk, LANES), BF16)

    @pl.when(hgroup == 0)
    def _():
        tri = jnp.where(causal, 1.0, 0.0)
        bias = bg_ref[...]
        for c in range(seq // chunk):
            rs = slice(c * chunk, (c + 1) * chunk)
            gb = gt_ref[rs, :] + bias
            log_f = -(jnp.maximum(-gb, 0.0) + jnp.log1p(jnp.exp(-jnp.abs(gb))))
            bcum = _dot(tri, log_f, precision=HIGHEST)
            gb_s[rs, :] = gb
            bc_s[rs, :] = bcum
            gbt_s[:, rs] = gb.T
            bct_s[:, rs] = bcum.T

    def pick_col(a, idx):
        return jnp.sum(jnp.where(lane == idx, a, 0.0), axis=1, keepdims=True)

    def pick_row(a, idx):
        return jnp.sum(jnp.where(sub == idx, a, 0.0), axis=0, keepdims=True)

    heads = [hgroup * n_heads + i for i in range(n_heads)]
    qcols = [slice(i * QK_DIM, (i + 1) * QK_DIM) for i in range(n_heads)]
    vcols = [slice(i * V_DIM, (i + 1) * V_DIM) for i in range(n_heads)]
    hr = range(n_heads)
    c_state = [jnp.zeros((QK_DIM, V_DIM + LANES), F32) for _ in hr]
    m_state = [jnp.full((1, 1), NEG, F32) for _ in hr]
    for c in range(seq // chunk):
        rs = slice(c * chunk, (c + 1) * chunk)
        gb_c, bc_c = gb_s[rs, :], bc_s[rs, :]
        gbt_c, bct_c = gbt_s[0:MLSTM_HEADS, rs], bct_s[MLSTM_HEADS:2 * MLSTM_HEADS, rs]
        li_col = [pick_col(gb_c, hd) for hd in heads]
        bc_col = [pick_col(bc_c, hd + MLSTM_HEADS) for hd in heads]
        li_row = [pick_row(gbt_c, hd) for hd in heads]
        bc_row = [pick_row(bct_c, hd) for hd in heads]
        dmat = [jnp.where(causal, bc_col[i] + (li_row[i] - bc_row[i]), NEG) for i in hr]
        inter = [bc_col[i] + m_state[i] for i in hr]
        m_t = [jnp.maximum(inter[i], jnp.max(dmat[i], axis=1, keepdims=True)) for i in hr]
        qc = [q_s[rs, qcols[i]] for i in hr]
        kf = [k_s[rs, qcols[i]] for i in hr]
        vc = [v_ref[rs, vcols[i]] for i in hr]
        a = [_dot_nt(qc[i], kf[i].astype(BF16)) * jnp.exp(dmat[i] - m_t[i]) for i in hr]
        w_inter = [jnp.exp(inter[i] - m_t[i]) for i in hr]
        q_state = [_dot(qc[i], c_state[i].astype(BF16)) for i in hr]
        num = [_dot(a[i].astype(BF16), vc[i]) + w_inter[i] * q_state[i][:, :V_DIM] for i in hr]
        den = [jnp.sum(a[i], axis=1, keepdims=True) + w_inter[i] * q_state[i][:, V_DIM:V_DIM + 1] for i in hr]
        h_out = [num[i] / jnp.maximum(jnp.abs(den[i]), jnp.exp(-m_t[i])) for i in hr]
        mu = [jnp.mean(h_out[i], axis=1, keepdims=True) for i in hr]
        hc = [h_out[i] - mu[i] for i in hr]
        var = [jnp.mean(hc[i] * hc[i], axis=1, keepdims=True) for i in hr]
        for i in hr:
            hn = hc[i] * lax.rsqrt(var[i] + LN_EPS) * ng_ref[:, vcols[i]]
            y_ref[rs, vcols[i]] = (hn * jax.nn.sigmoid(op_ref[rs, vcols[i]].astype(F32))).astype(y_ref.dtype)
        b_last = [bc_col[i][chunk - 1:chunk, :] for i in hr]
        g = [b_last[i] - bc_col[i] + li_col[i] for i in hr]
        m_new = [jnp.maximum(b_last[i] + m_state[i], jnp.max(g[i], axis=0, keepdims=True)) for i in hr]
        wk = [jnp.exp(g[i] - m_new[i]) for i in hr]
        decay = [jnp.exp(b_last[i] + m_state[i] - m_new[i]) for i in hr]
        kw_t = [(wk[i] * kf[i]).T.astype(BF16) for i in hr]
        upd = [_dot(kw_t[i], jnp.concatenate([vc[i], ones_cols], axis=1)) for i in hr]
        c_state = [decay[i] * c_state[i] + upd[i] for i in hr]
        m_state = m_new


def _mlstm_cell(z, gates, b_gates_pad, conv_w, conv_b, norm_g, batch):
    t = z.shape[0]
    nh = MLSTM_HEADS_PER_STEP
    groups = MLSTM_HEADS // nh
    qw, vw = nh * QK_DIM, nh * V_DIM
    k_blk0 = (MLSTM_QK_COLS // 2) // qw
    v_blk0 = MLSTM_QK_COLS // vw
    o_blk0 = (MLSTM_QK_COLS + MLSTM_V_COLS) // vw
    return pl.pallas_call(
        functools.partial(_mlstm_cell_kernel, chunk=MLSTM_CHUNK, n_heads=nh),
        grid=(batch, groups),
        in_specs=[
            pl.BlockSpec((SEQ, qw), lambda b, h: (b, h)),
            pl.BlockSpec((SEQ, qw), lambda b, h: (b, k_blk0 + h)),
            pl.BlockSpec((SEQ, vw), lambda b, h: (b, v_blk0 + h)),
            pl.BlockSpec((SEQ, vw), lambda b, h: (b, o_blk0 + h)),
            pl.BlockSpec((SEQ, LANES), lambda b, h: (b, 0)),
            pl.BlockSpec((1, LANES), lambda b, h: (0, 0)),
            pl.BlockSpec((CONV_K, qw), lambda b, h: (0, h)),
            pl.BlockSpec((1, qw), lambda b, h: (0, h)),
            pl.BlockSpec((CONV_K, qw), lambda b, h: (0, k_blk0 + h)),
            pl.BlockSpec((1, qw), lambda b, h: (0, k_blk0 + h)),
            pl.BlockSpec((1, vw), lambda b, h: (0, h)),
        ],
        out_specs=pl.BlockSpec((SEQ, vw), lambda b, h: (b, h)),
        out_shape=jax.ShapeDtypeStruct((t, MLSTM_V_COLS), BF16),
        scratch_shapes=[pltpu.VMEM((SEQ, qw), BF16), pltpu.VMEM((SEQ, qw), F32),
                        pltpu.VMEM((SEQ, LANES), F32), pltpu.VMEM((SEQ, LANES), F32),
                        pltpu.VMEM((LANES, SEQ), F32), pltpu.VMEM((LANES, SEQ), F32)],
        compiler_params=pltpu.CompilerParams(
            dimension_semantics=("parallel", "arbitrary"), vmem_limit_bytes=56 * MIB),
        name="mlstm_cell",
    )(z, z, z, z, gates, b_gates_pad, conv_w, conv_b.reshape(1, -1), conv_w, conv_b.reshape(1, -1),
      norm_g.reshape(1, -1))


def kernel(x, attn_w_qkv, attn_w_o, mlstm_w_in, mlstm_b_gates, mlstm_conv_w, mlstm_conv_b, mlstm_norm_g,
           mlstm_w_out, ln_mix_g, ln_mix_b, ln_ffn_g, ln_ffn_b, router_w, router_b, moe_w_gate, moe_w_up,
           moe_w_down):
    batch, seq, d = x.shape
    assert (seq, d) == (SEQ, D_MODEL)
    t = batch * seq
    x2 = x.reshape(t, d)

    wr_f32 = jnp.zeros((D_MODEL, LANES), F32).at[:, :N_EXPERTS].set(router_w)
    wr_hi = lax.bitcast_convert_type(lax.bitcast_convert_type(wr_f32, jnp.uint32) & jnp.uint32(0xFFFF0000), F32)
    wr_pad = jnp.concatenate([wr_hi.astype(BF16), (wr_f32 - wr_hi).astype(BF16)], axis=1)
    br_pad = jnp.zeros((1, LANES), F32).at[0, :N_EXPERTS].set(router_b)
    wg_bf, wu_bf, wd_bf = moe_w_gate.astype(BF16), moe_w_up.astype(BF16), moe_w_down.astype(BF16)

    w_qkv = _rope_permute_qk(attn_w_qkv[0]).astype(BF16)
    views = [x] + list(_class_major_views(x))
    outs, lses = [], []
    for group, dil in enumerate(DILATIONS):
        o, lse = _attn_group(views[group], w_qkv, _rope_table(dil), group, dil)
        outs.append(o)
        lses.append(lse)
    x2 = _attn_out(outs, lses, x2, attn_w_o[0].astype(BF16), ln_mix_g[0], ln_mix_b[0])
    x2 = _moe_layer(x2, wr_pad, br_pad, wg_bf[0], wu_bf[0], wd_bf[0], ln_ffn_g[0], ln_ffn_b[0])

    w_in = mlstm_w_in[0]
    w_main = w_in[:, :MLSTM_MAIN_COLS].astype(BF16)
    w_gate = jnp.zeros((D_MODEL, LANES), F32).at[:, :2 * MLSTM_HEADS].set(w_in[:, MLSTM_MAIN_COLS:]).astype(BF16)
    bg_pad = jnp.zeros((1, LANES), F32).at[0, :2 * MLSTM_HEADS].set(mlstm_b_gates[0])
    z, gates = _inproj(x2, w_main, w_gate)
    y = _mlstm_cell(z, gates, bg_pad, mlstm_conv_w[0], mlstm_conv_b[0], mlstm_norm_g[0], batch)
    x2 = _mix_out(y, x2, mlstm_w_out[0].astype(BF16), ln_mix_g[1], ln_mix_b[1])
    x2 = _moe_layer(x2, wr_pad, br_pad, wg_bf[1], wu_bf[1], wd_bf[1], ln_ffn_g[1], ln_ffn_b[1])
    return x2.reshape(batch, seq, d)
```

```python
import functools

import numpy as np
import jax
import jax.numpy as jnp
from jax import lax
from jax.experimental import pallas as pl
from jax.experimental.pallas import tpu as pltpu

F32 = jnp.float32
BF16 = jnp.bfloat16
I32 = jnp.int32
HIGHEST = lax.Precision.HIGHEST

D_MODEL = 2048
SEQ = 2048
DEPTH = 2
DILATIONS = (1, 4, 16)
N_BACK = 128
ATTN_HEADS = 8
HEAD_DIM = 128
ROT_DIM = HEAD_DIM // 4
ROPE_THETA = 500000.0
ATTN_BLOCK = 128
ATTN_COLS = ATTN_HEADS * HEAD_DIM

MLSTM_HEADS = 8
QK_DIM = 128
V_DIM = D_MODEL // MLSTM_HEADS
CONV_K = 4
MLSTM_QK_COLS = 2 * MLSTM_HEADS * QK_DIM
MLSTM_V_COLS = MLSTM_HEADS * V_DIM
MLSTM_MAIN_COLS = MLSTM_QK_COLS + 2 * MLSTM_V_COLS

N_EXPERTS = 16
N_EXPERT_GROUPS = 4
EXPERTS_PER_GROUP = 4
D_EXPERT = 768
PAIRS = ((0, 1), (0, 2), (0, 3), (1, 2), (1, 3), (2, 3))
N_CLASSES = N_EXPERT_GROUPS * len(PAIRS)

ALPHA = (2 * DEPTH) ** 0.25
LN_EPS = 1e-5
NEG = -1e30

LANES = 128
V7X_VMEM_BYTES = 64 * 1024 * 1024
N_DMA_QUEUES = 2
MIB = 1024 * 1024

ATTN_STEP_ROWS = 512
ATTN_HEAD_SET = 8
ROPE_PAIR_SHIFT = 64
ROW_TILE = 512
INPROJ_TM = 1024
INPROJ_TN = 1024
MLSTM_CHUNK = 128
MLSTM_HEADS_PER_STEP = 4
MOE_BLOCK = 256
GATHER_TM = 256


def _dot(a, b, **kw):
    return jnp.dot(a, b, preferred_element_type=F32, **kw)


def _dot_nt(a, b):
    return lax.dot_general(a, b, (((1,), (1,)), ((), ())), preferred_element_type=F32)


def _layer_norm_rows(y, g, b):
    mu = jnp.mean(y, axis=-1, keepdims=True)
    yc = y - mu
    var = jnp.mean(yc * yc, axis=-1, keepdims=True)
    return yc * lax.rsqrt(var + LN_EPS) * g + b


def _attn_group_kernel(x_ref, w_ref, tab_ref, o_ref, lse_ref, q_s, k_s, v_s, *, n_cls, lc, carry):
    step = pl.program_id(1)
    rows_total = n_cls * lc
    blocks_per_class = lc // ATTN_BLOCK

    if n_cls == 1:
        xs = x_ref[0]
    else:
        xs = jnp.concatenate([x_ref[0, :, c * D_MODEL:(c + 1) * D_MODEL] for c in range(n_cls)], axis=0)
    qkv = _dot(xs.astype(BF16), w_ref[...])

    if carry:
        @pl.when(step == 0)
        def _():
            k_s[0:ATTN_BLOCK, :] = jnp.zeros((ATTN_BLOCK, ATTN_COLS), BF16)
            v_s[0:ATTN_BLOCK, :] = jnp.zeros((ATTN_BLOCK, ATTN_COLS), BF16)

        @pl.when(step > 0)
        def _():
            k_s[0:ATTN_BLOCK, :] = k_s[rows_total:rows_total + ATTN_BLOCK, :]
            v_s[0:ATTN_BLOCK, :] = v_s[rows_total:rows_total + ATTN_BLOCK, :]

    cosf = tab_ref[:, 0:LANES]
    sinr = tab_ref[:, LANES:2 * LANES]

    def rope(t):
        return t * cosf + pltpu.roll(t, ROPE_PAIR_SHIFT, 1) * sinr

    for h in range(ATTN_HEADS):
        cs = slice(h * HEAD_DIM, (h + 1) * HEAD_DIM)
        q_s[:, cs] = rope(qkv[:, h * HEAD_DIM:(h + 1) * HEAD_DIM]).astype(BF16)
        k_s[ATTN_BLOCK:, cs] = rope(qkv[:, ATTN_COLS + h * HEAD_DIM:ATTN_COLS + (h + 1) * HEAD_DIM]).astype(BF16)
    v_s[ATTN_BLOCK:, :] = qkv[:, 2 * ATTN_COLS:3 * ATTN_COLS].astype(BF16)

    row = lax.broadcasted_iota(I32, (ATTN_BLOCK, ATTN_BLOCK), 0)
    col = lax.broadcasted_iota(I32, (ATTN_BLOCK, ATTN_BLOCK), 1)
    row2 = lax.broadcasted_iota(I32, (ATTN_BLOCK, 2 * ATTN_BLOCK), 0)
    col2 = lax.broadcasted_iota(I32, (ATTN_BLOCK, 2 * ATTN_BLOCK), 1)
    lane = lax.broadcasted_iota(I32, (ATTN_BLOCK, LANES), 1)
    mask_cur = col <= row
    mask_both = jnp.logical_and(col2 >= row2, col2 <= row2 + N_BACK)
    if carry:
        mask_first = jnp.logical_and(mask_both, jnp.logical_or(col2 >= ATTN_BLOCK, step > 0))
    scale = HEAD_DIM ** -0.5

    for c in range(n_cls):
        for bi in range(blocks_per_class):
            j = c * blocks_per_class + bi
            ors = slice(bi * ATTN_BLOCK, (bi + 1) * ATTN_BLOCK)
            with_prev = bi > 0 or carry
            if with_prev:
                krows = slice(j * ATTN_BLOCK, (j + 2) * ATTN_BLOCK)
                mask = mask_first if bi == 0 else mask_both
            else:
                krows = slice((j + 1) * ATTN_BLOCK, (j + 2) * ATTN_BLOCK)
                mask = mask_cur
            lse_tile = jnp.zeros((ATTN_BLOCK, LANES), F32)
            for h0 in range(0, ATTN_HEADS, ATTN_HEAD_SET):
                heads = range(h0, h0 + ATTN_HEAD_SET)
                cols = [slice(h * HEAD_DIM, (h + 1) * HEAD_DIM) for h in heads]
                scores = [jnp.where(mask, _dot_nt(q_s[j * ATTN_BLOCK:(j + 1) * ATTN_BLOCK, cs], k_s[krows, cs])
                                    * scale, NEG) for cs in cols]
                maxes = [jnp.max(s, axis=1, keepdims=True) for s in scores]
                probs = [jnp.exp(s - m) for s, m in zip(scores, maxes)]
                dens = [jnp.sum(p, axis=1, keepdims=True) for p in probs]
                accs = [_dot(p.astype(BF16), v_s[krows, cs]) for p, cs in zip(probs, cols)]
                for h, acc, den, m in zip(heads, accs, dens, maxes):
                    o_ref[0, ors, c * ATTN_COLS + h * HEAD_DIM:c * ATTN_COLS + (h + 1) * HEAD_DIM] = (
                        acc / den).astype(o_ref.dtype)
                    lse_tile = jnp.where(lane == h, m + jnp.log(den), lse_tile)
            lse_ref[0, ors, c * LANES:(c + 1) * LANES] = lse_tile


def _class_major_perm(dil):
    width = ATTN_STEP_ROWS // dil
    perm = np.zeros((ATTN_STEP_ROWS, ATTN_STEP_ROWS), np.float32)
    for r in range(dil):
        for m in range(width):
            perm[r * width + m, m * dil + r] = 1.0
    return perm


def _class_major_kernel(x_ref, *refs):
    n = len(DILATIONS) - 1
    xb = x_ref[0].astype(BF16)
    for p_ref, out_ref, dil in zip(refs[:n], refs[n:], DILATIONS[1:]):
        width = ATTN_STEP_ROWS // dil
        rows = _dot(p_ref[...], xb).astype(BF16)
        for r in range(dil):
            out_ref[0, :, r * D_MODEL:(r + 1) * D_MODEL] = rows[r * width:(r + 1) * width, :]


def _class_major_views(x3):
    batch = x3.shape[0]
    perms = [jnp.asarray(_class_major_perm(d), BF16) for d in DILATIONS[1:]]
    return pl.pallas_call(
        _class_major_kernel,
        grid=(batch, SEQ // ATTN_STEP_ROWS),
        in_specs=[pl.BlockSpec((1, ATTN_STEP_ROWS, D_MODEL), lambda b, s: (b, s, 0))] + [
            pl.BlockSpec((ATTN_STEP_ROWS, ATTN_STEP_ROWS), lambda b, s: (0, 0)) for _ in perms],
        out_specs=[pl.BlockSpec((1, ATTN_STEP_ROWS // d, d * D_MODEL), lambda b, s: (b, s, 0)) for d in DILATIONS[1:]],
        out_shape=[jax.ShapeDtypeStruct((batch, SEQ // d, d * D_MODEL), BF16) for d in DILATIONS[1:]],
        compiler_params=pltpu.CompilerParams(dimension_semantics=("parallel", "parallel"), vmem_limit_bytes=40 * MIB),
        name="class_major_views",
    )(x3, *perms)


def _attn_group(xg, w_qkv_bf16, tab, group, dil):
    batch = xg.shape[0]
    per_class = SEQ // dil
    lc = min(ATTN_STEP_ROWS, per_class)
    n_cls = ATTN_STEP_ROWS // lc
    steps = SEQ // ATTN_STEP_ROWS
    carry = dil == 1
    if dil == 1:
        imap = lambda b, s: (b, s, 0)
    else:
        imap = lambda b, s: (b, 0, s)
    kern = functools.partial(_attn_group_kernel, n_cls=n_cls, lc=lc, carry=carry)
    o, lse = pl.pallas_call(
        kern,
        grid=(batch, steps),
        in_specs=[
            pl.BlockSpec((1, lc, n_cls * D_MODEL), imap),
            pl.BlockSpec((D_MODEL, 3 * ATTN_COLS), lambda b, s: (0, group), pipeline_mode=pl.Buffered(1)),
            pl.BlockSpec((ATTN_STEP_ROWS, 2 * LANES), lambda b, s: (s, 0)),
        ],
        out_specs=[
            pl.BlockSpec((1, lc, n_cls * ATTN_COLS), imap),
            pl.BlockSpec((1, lc, n_cls * LANES), imap),
        ],
        out_shape=[
            jax.ShapeDtypeStruct((batch, per_class, dil * ATTN_COLS), BF16),
            jax.ShapeDtypeStruct((batch, per_class, dil * LANES), F32),
        ],
        scratch_shapes=[
            pltpu.VMEM((ATTN_STEP_ROWS, ATTN_COLS), BF16),
            pltpu.VMEM((ATTN_BLOCK + ATTN_STEP_ROWS, ATTN_COLS), BF16),
            pltpu.VMEM((ATTN_BLOCK + ATTN_STEP_ROWS, ATTN_COLS), BF16),
        ],
        compiler_params=pltpu.CompilerParams(
            dimension_semantics=("parallel", "arbitrary"), vmem_limit_bytes=56 * MIB),
        name=f"attn_group{group}",
    )(xg, w_qkv_bf16, tab)
    return o, lse


def _rope_permute_qk(w_qkv):
    half = ROT_DIM // 2
    lo, hi = ROPE_PAIR_SHIFT, ROPE_PAIR_SHIFT + half
    w = w_qkv.reshape(D_MODEL, len(DILATIONS), 3, ATTN_HEADS, HEAD_DIM)
    qk = w[:, :, 0:2]
    qk = jnp.concatenate([qk[..., :half], qk[..., lo:hi], qk[..., ROT_DIM:lo], qk[..., half:ROT_DIM], qk[..., hi:]], -1)
    return jnp.concatenate([qk, w[:, :, 2:3]], axis=2).reshape(w_qkv.shape)


def _rope_table(dil):
    inv_freq = ROPE_THETA ** (-np.arange(0, ROT_DIM, 2, dtype=np.float64) / ROT_DIM)
    ang = np.arange(SEQ, dtype=np.float64)[:, None] * inv_freq[None, :]
    ang = np.concatenate([ang, ang], -1)
    cos, sin = np.cos(ang), np.sin(ang)
    half = ROT_DIM // 2
    gap = ROPE_PAIR_SHIFT - half
    tail = LANES - ROPE_PAIR_SHIFT - half
    cosf = np.concatenate([cos[:, :half], np.ones((SEQ, gap)), cos[:, half:], np.ones((SEQ, tail))], 1)
    sinr = np.concatenate([-sin[:, :half], np.zeros((SEQ, gap)), sin[:, half:], np.zeros((SEQ, tail))], 1)
    tab = np.concatenate([cosf, sinr], 1)
    tab = tab.reshape(SEQ // dil, dil, 2 * LANES).transpose(1, 0, 2).reshape(SEQ, 2 * LANES)
    return jnp.asarray(tab.astype(np.float32))


def _attn_out_kernel(o0_ref, o1_ref, o2_ref, l0_ref, l1_ref, l2_ref, pt1_ref, pt2_ref, x_ref, w_ref, g_ref, b_ref,
                     out_ref, l1_s, l2_s):
    o_nat = [None]
    for o_ref, l_ref, pt_ref, l_s, dil in ((o1_ref, l1_ref, pt1_ref, l1_s, DILATIONS[1]),
                                          (o2_ref, l2_ref, pt2_ref, l2_s, DILATIONS[2])):
        width = ATTN_STEP_ROWS // dil
        o_cm = jnp.concatenate([o_ref[0, :, r * ATTN_COLS:(r + 1) * ATTN_COLS] for r in range(dil)], axis=0)
        o_nat.append(_dot(pt_ref[...], o_cm))
        for r in range(dil):
            l_s[pl.ds(r, width, stride=dil), :] = l_ref[0, :, r * LANES:(r + 1) * LANES]
    ls = [l0_ref[0], l1_s[...], l2_s[...]]
    mx = jnp.maximum(jnp.maximum(ls[0], ls[1]), ls[2])
    es = [jnp.exp(l - mx) for l in ls]
    den = es[0] + es[1] + es[2]
    ws = [e / den for e in es]
    parts = []
    for h in range(ATTN_HEADS):
        cs = slice(h * HEAD_DIM, (h + 1) * HEAD_DIM)
        acc = ws[0][:, h:h + 1] * o0_ref[0, :, cs].astype(F32)
        acc = acc + ws[1][:, h:h + 1] * o_nat[1][:, cs]
        acc = acc + ws[2][:, h:h + 1] * o_nat[2][:, cs]
        parts.append(acc)
    mixed_in = jnp.concatenate(parts, axis=1).astype(BF16)
    y = ALPHA * x_ref[...] + _dot(mixed_in, w_ref[...])
    out_ref[...] = _layer_norm_rows(y, g_ref[...], b_ref[...])


def _attn_out(outs, lses, x2, w_bf16, g, b):
    t = x2.shape[0]
    steps = SEQ // ATTN_STEP_ROWS
    view_map = lambda i: (i // steps, i % steps, 0)
    view_spec = lambda dil, width: pl.BlockSpec((1, ATTN_STEP_ROWS // dil, dil * width), view_map)
    const_spec = lambda shape: pl.BlockSpec(shape, lambda i: (0, 0))
    row_spec = pl.BlockSpec((ATTN_STEP_ROWS, D_MODEL), lambda i: (i, 0))
    perms_t = [jnp.asarray(_class_major_perm(d).T, BF16) for d in DILATIONS[1:]]
    return pl.pallas_call(
        _attn_out_kernel,
        grid=(t // ATTN_STEP_ROWS,),
        in_specs=[view_spec(d, ATTN_COLS) for d in DILATIONS] + [view_spec(d, LANES) for d in DILATIONS] + [
            const_spec((ATTN_STEP_ROWS, ATTN_STEP_ROWS)) for _ in perms_t] + [
            row_spec, pl.BlockSpec((ATTN_COLS, D_MODEL), lambda i: (0, 0), pipeline_mode=pl.Buffered(1)),
            const_spec((1, D_MODEL)), const_spec((1, D_MODEL))],
        out_specs=row_spec,
        out_shape=jax.ShapeDtypeStruct((t, D_MODEL), F32),
        scratch_shapes=[pltpu.VMEM((ATTN_STEP_ROWS, LANES), F32), pltpu.VMEM((ATTN_STEP_ROWS, LANES), F32)],
        compiler_params=pltpu.CompilerParams(dimension_semantics=("parallel",), vmem_limit_bytes=48 * MIB),
        name="attn_out",
    )(*outs, *lses, *perms_t, x2, w_bf16, g.reshape(1, D_MODEL), b.reshape(1, D_MODEL))


def _mix_out_kernel(y_ref, x_ref, w_ref, g_ref, b_ref, out_ref):
    y = ALPHA * x_ref[...] + _dot(y_ref[...], w_ref[...])
    out_ref[...] = _layer_norm_rows(y, g_ref[...], b_ref[...])


def _mix_out(mixer_y, x2, w_bf16, g, b):
    t = x2.shape[0]
    k = w_bf16.shape[0]
    row_spec = lambda width: pl.BlockSpec((ROW_TILE, width), lambda i: (i, 0))
    const_spec = lambda shape: pl.BlockSpec(shape, lambda i: (0, 0))
    return pl.pallas_call(
        _mix_out_kernel,
        grid=(t // ROW_TILE,),
        in_specs=[row_spec(k), row_spec(D_MODEL),
                  pl.BlockSpec((k, D_MODEL), lambda i: (0, 0), pipeline_mode=pl.Buffered(1)),
                  const_spec((1, D_MODEL)), const_spec((1, D_MODEL))],
        out_specs=row_spec(D_MODEL),
        out_shape=jax.ShapeDtypeStruct((t, D_MODEL), F32),
        compiler_params=pltpu.CompilerParams(dimension_semantics=("parallel",), vmem_limit_bytes=48 * MIB),
        name="mix_out",
    )(mixer_y, x2, w_bf16, g.reshape(1, D_MODEL), b.reshape(1, D_MODEL))


def _router_kernel(x_ref, wr_ref, br_ref, mi_ref, mf_ref, cnt_ref, carry_s, *, tm):
    i = pl.program_id(0)

    @pl.when(i == 0)
    def _():
        carry_s[...] = jnp.zeros_like(carry_s)

    x = x_ref[...]
    x_hi = x.astype(BF16)
    x_lo = (x - x_hi.astype(F32)).astype(BF16)
    hi_terms = _dot(x_hi, wr_ref[...])
    logits = (hi_terms[:, :LANES] + _dot(x_lo, wr_ref[:, :LANES])) + hi_terms[:, LANES:] + br_ref[...]
    lt = logits.T
    l = [lt[e:e + 1, :] for e in range(N_EXPERTS)]
    mx = l[0]
    for e in range(1, N_EXPERTS):
        mx = jnp.maximum(mx, l[e])
    ex = [jnp.exp(v - mx) for v in l]
    tot = ex[0]
    for e in range(1, N_EXPERTS):
        tot = tot + ex[e]
    p = [v / tot for v in ex]

    def first_index_of(vals, target):
        idx = jnp.full_like(target, len(vals) - 1).astype(I32)
        for k in range(len(vals) - 2, -1, -1):
            idx = jnp.where(vals[k] == target, k, idx)
        return idx

    best = None
    for g in range(N_EXPERT_GROUPS):
        pg = p[g * EXPERTS_PER_GROUP:(g + 1) * EXPERTS_PER_GROUP]
        top1 = jnp.maximum(jnp.maximum(pg[0], pg[1]), jnp.maximum(pg[2], pg[3]))
        i1 = first_index_of(pg, top1)
        rest = [jnp.where(i1 == k, -1.0, pg[k]) for k in range(EXPERTS_PER_GROUP)]
        top2 = jnp.maximum(jnp.maximum(rest[0], rest[1]), jnp.maximum(rest[2], rest[3]))
        i2 = first_index_of(rest, top2)
        score = top1 + top2
        if best is None:
            best = (score, jnp.zeros_like(i1), top1, top2, i1, i2)
        else:
            better = score > best[0]
            cand = (score, jnp.full_like(i1, g), top1, top2, i1, i2)
            best = tuple(jnp.where(better, cv, bv) for cv, bv in zip(cand, best))
    _, g_sel, p1, p2, i1, i2 = best
    psum = p1 + p2
    gate1, gate2 = p1 / psum, p2 / psum
    first_low = i1 < i2
    lo = jnp.where(first_low, i1, i2)
    hi = jnp.where(first_low, i2, i1)
    gate_lo = jnp.where(first_low, gate1, gate2)
    gate_hi = jnp.where(first_low, gate2, gate1)
    pair = jnp.where(lo == 0, 0, jnp.where(lo == 1, 3, 5)) + hi - lo - 1
    cls = g_sel * len(PAIRS) + pair

    n_rows = carry_s.shape[0]
    sub = lax.broadcasted_iota(I32, (n_rows, tm), 0)
    onehot = sub == cls
    oh = jnp.where(onehot, 1.0, 0.0)
    upper = (lax.broadcasted_iota(I32, (tm, tm), 0) <= lax.broadcasted_iota(I32, (tm, tm), 1))
    cum = _dot(oh.astype(BF16), jnp.where(upper, 1.0, 0.0).astype(BF16))
    carry = carry_s[:, 0:1]
    rank = jnp.sum(jnp.where(onehot, cum - 1.0 + carry, 0.0), axis=0, keepdims=True)
    carry_new = carry + jnp.sum(oh, axis=1, keepdims=True)
    carry_s[...] = jnp.broadcast_to(carry_new, carry_s.shape)
    cnt_ref[...] = jnp.broadcast_to(carry_new, cnt_ref.shape)

    sub8 = lax.broadcasted_iota(I32, (8, tm), 0)
    mi_ref[...] = jnp.where(sub8 == 0, cls, jnp.where(sub8 == 1, rank.astype(I32), 0))
    mf_ref[...] = jnp.where(sub8 == 0, gate_lo, jnp.where(sub8 == 1, gate_hi, 0.0))


def _router(x2, wr_pad, br_pad):
    t = x2.shape[0]
    tm = ROW_TILE
    return pl.pallas_call(
        functools.partial(_router_kernel, tm=tm),
        grid=(t // tm,),
        in_specs=[
            pl.BlockSpec((tm, D_MODEL), lambda i: (i, 0)),
            pl.BlockSpec((D_MODEL, 2 * LANES), lambda i: (0, 0)),
            pl.BlockSpec((1, LANES), lambda i: (0, 0)),
        ],
        out_specs=[
            pl.BlockSpec((8, tm), lambda i: (0, i)),
            pl.BlockSpec((8, tm), lambda i: (0, i)),
            pl.BlockSpec((32, LANES), lambda i: (0, 0)),
        ],
        out_shape=[
            jax.ShapeDtypeStruct((8, t), I32),
            jax.ShapeDtypeStruct((8, t), F32),
            jax.ShapeDtypeStruct((32, LANES), F32),
        ],
        scratch_shapes=[pltpu.VMEM((32, LANES), F32)],
        compiler_params=pltpu.CompilerParams(dimension_semantics=("arbitrary",), vmem_limit_bytes=32 * MIB),
        name="moe_router",
    )(x2, wr_pad, br_pad)


def _expert_kernel(elo_ref, ehi_ref, valid_ref, src_first_ref, src_next_ref, x_hbm, gs_ref, wg0, wu0, wd0, wg1, wu1, wd1,
                   out_ref, rows_s, sems):
    del elo_ref, ehi_ref
    i = pl.program_id(0)
    slot = i % 2

    def start_row(idx_ref, t, to_slot, priority):
        pltpu.make_async_copy(x_hbm.at[pl.ds(idx_ref[0, 0, t], 1)], rows_s.at[to_slot, pl.ds(t, 1)],
                              sems.at[to_slot]).start(priority=priority)

    def wait_rows(of_slot):
        pltpu.make_async_copy(x_hbm.at[pl.ds(0, MOE_BLOCK)], rows_s.at[of_slot], sems.at[of_slot]).wait()

    @pl.when(i == 0)
    def _():
        def issue(t, c):
            start_row(src_first_ref, t, slot, 0)
            return c

        lax.fori_loop(0, MOE_BLOCK, issue, 0, unroll=8)

    @pl.when(valid_ref[i] > 0)
    def _():
        wait_rows(slot)
        for t in range(MOE_BLOCK):
            start_row(src_next_ref, t, 1 - slot, priority=t % N_DMA_QUEUES)
        xb = rows_s[slot].astype(BF16)

        def ffn(wg, wu, wd):
            hg = _dot(xb, wg[0])
            hu = _dot(xb, wu[0])
            hidden = (hg * jax.nn.sigmoid(hg)) * hu
            return _dot(hidden.astype(BF16), wd[0])

        gs = gs_ref[...]
        out_ref[...] = gs[:, 0:1] * ffn(wg0, wu0, wd0) + gs[:, 1:2] * ffn(wg1, wu1, wd1)

        @pl.when(valid_ref[i + 1] == 0)
        def _():
            wait_rows(1 - slot)

    @pl.when(valid_ref[i] == 0)
    def _():
        out_ref[...] = jnp.zeros_like(out_ref)


def _experts(x2, src_rows, gates_sorted, blk_elo, blk_ehi, blk_valid, wg, wu, wd):
    nb = src_rows.shape[0] - 1
    n_rows = nb * MOE_BLOCK
    lo_map = lambda b, elo, ehi, valid: (elo[b], 0, 0)
    hi_map = lambda b, elo, ehi, valid: (ehi[b], 0, 0)
    row_map = lambda b, elo, ehi, valid: (b, 0)
    up_shape = (1, D_MODEL, D_EXPERT)
    down_shape = (1, D_EXPERT, D_MODEL)
    grid_spec = pltpu.PrefetchScalarGridSpec(
        num_scalar_prefetch=3,
        grid=(nb,),
        in_specs=[
            pl.BlockSpec((1, 1, MOE_BLOCK), lambda b, elo, ehi, valid: (0, 0, 0), memory_space=pltpu.SMEM),
            pl.BlockSpec((1, 1, MOE_BLOCK), lambda b, elo, ehi, valid: (b + 1, 0, 0), memory_space=pltpu.SMEM),
            pl.BlockSpec(memory_space=pl.ANY),
            pl.BlockSpec((MOE_BLOCK, LANES), row_map),
            pl.BlockSpec(up_shape, lo_map), pl.BlockSpec(up_shape, lo_map), pl.BlockSpec(down_shape, lo_map),
            pl.BlockSpec(up_shape, hi_map), pl.BlockSpec(up_shape, hi_map), pl.BlockSpec(down_shape, hi_map),
        ],
        out_specs=pl.BlockSpec((MOE_BLOCK, D_MODEL), row_map),
        scratch_shapes=[pltpu.VMEM((2, MOE_BLOCK, D_MODEL), F32), pltpu.SemaphoreType.DMA((2,))],
    )
    return pl.pallas_call(
        _expert_kernel,
        grid_spec=grid_spec,
        out_shape=jax.ShapeDtypeStruct((n_rows, D_MODEL), F32),
        compiler_params=pltpu.CompilerParams(dimension_semantics=("arbitrary",), vmem_limit_bytes=58 * MIB),
        name="moe_experts",
    )(blk_elo, blk_ehi, blk_valid, src_rows, src_rows, x2, gates_sorted, wg, wu, wd, wg, wu, wd)


def _gather_ln_kernel(dest_ref, dest_next_ref, x_ref, src_hbm, g_ref, b_ref, y_ref, rows_s, sems, *, tm):
    i = pl.program_id(0)
    slot = i % 2

    def issue_rows(idx_ref, to_slot):
        for t in range(tm):
            pltpu.make_async_copy(src_hbm.at[pl.ds(idx_ref[0, 0, t], 1)], rows_s.at[to_slot, pl.ds(t, 1)],
                                  sems.at[to_slot]).start(priority=t % N_DMA_QUEUES)

    @pl.when(i == 0)
    def _():
        issue_rows(dest_ref, slot)

    @pl.when(i + 1 < pl.num_programs(0))
    def _():
        issue_rows(dest_next_ref, 1 - slot)

    pltpu.make_async_copy(src_hbm.at[pl.ds(0, tm)], rows_s.at[slot], sems.at[slot]).wait()
    y = ALPHA * x_ref[...] + rows_s[slot]
    y_ref[...] = _layer_norm_rows(y, g_ref[...], b_ref[...])


def _gather_ln(x2, expert_out, dest, g, b):
    t = x2.shape[0]
    tm = GATHER_TM
    n_tiles = t // tm
    dest3 = dest.reshape(n_tiles, 1, tm)
    return pl.pallas_call(
        functools.partial(_gather_ln_kernel, tm=tm),
        grid=(n_tiles,),
        in_specs=[
            pl.BlockSpec((1, 1, tm), lambda i: (i, 0, 0), memory_space=pltpu.SMEM),
            pl.BlockSpec((1, 1, tm), lambda i: (jnp.minimum(i + 1, n_tiles - 1), 0, 0), memory_space=pltpu.SMEM),
            pl.BlockSpec((tm, D_MODEL), lambda i: (i, 0)),
            pl.BlockSpec(memory_space=pl.ANY),
            pl.BlockSpec((1, D_MODEL), lambda i: (0, 0)),
            pl.BlockSpec((1, D_MODEL), lambda i: (0, 0)),
        ],
        out_specs=pl.BlockSpec((tm, D_MODEL), lambda i: (i, 0)),
        out_shape=jax.ShapeDtypeStruct((t, D_MODEL), F32),
        scratch_shapes=[pltpu.VMEM((2, tm, D_MODEL), F32), pltpu.SemaphoreType.DMA((2,))],
        compiler_params=pltpu.CompilerParams(dimension_semantics=("arbitrary",), vmem_limit_bytes=32 * MIB),
        name="moe_gather_ln",
    )(dest3, dest3, x2, expert_out, g.reshape(1, D_MODEL), b.reshape(1, D_MODEL))


_CLASS_LO = np.array([4 * (c // 6) + PAIRS[c % 6][0] for c in range(N_CLASSES)], np.int32)
_CLASS_HI = np.array([4 * (c // 6) + PAIRS[c % 6][1] for c in range(N_CLASSES)], np.int32)


def _moe_layer(x2, wr_pad, br_pad, wg, wu, wd, ln_g, ln_b):
    t = x2.shape[0]
    nb = -(-(t + N_CLASSES * (MOE_BLOCK - 1)) // MOE_BLOCK)
    n_rows = nb * MOE_BLOCK
    meta_i, meta_f, cnt = _router(x2, wr_pad, br_pad)
    cls, rank = meta_i[0], meta_i[1]
    counts = cnt[:N_CLASSES, 0].astype(I32)
    padded = (counts + MOE_BLOCK - 1) // MOE_BLOCK * MOE_BLOCK
    ends = jnp.cumsum(padded)
    starts = ends - padded
    dest = starts[cls] + rank
    blk_start = jnp.arange(nb, dtype=I32) * MOE_BLOCK
    blk_valid = (jnp.arange(nb + 1, dtype=I32) * MOE_BLOCK < ends[-1]).astype(I32)
    n_valid = jnp.sum(blk_valid)
    blk_cls = jnp.minimum(jnp.sum((ends[None, :] <= blk_start[:, None]).astype(I32), axis=1), N_CLASSES - 1)
    blk_cls = blk_cls[jnp.minimum(jnp.arange(nb), n_valid - 1)]
    blk_elo = jnp.asarray(_CLASS_LO)[blk_cls]
    blk_ehi = jnp.asarray(_CLASS_HI)[blk_cls]
    tok_meta = jnp.concatenate([meta_f[:2].T, jnp.arange(t, dtype=F32)[:, None], jnp.zeros((t, LANES - 3), F32)], 1)
    sorted_meta = jnp.zeros((n_rows + MOE_BLOCK, LANES), F32).at[dest].set(tok_meta)
    src_rows = sorted_meta[:, 2].astype(I32).reshape(nb + 1, 1, MOE_BLOCK)
    out = _experts(x2, src_rows, sorted_meta, blk_elo, blk_ehi, blk_valid, wg, wu, wd)
    return _gather_ln(x2, out, dest, ln_g, ln_b)


def _inproj_kernel(x_ref, w_ref, wgate_ref, z_ref, gates_ref, xb_s):
    @pl.when(pl.program_id(1) == 0)
    def _():
        xb_s[...] = x_ref[...].astype(BF16)
        gates_ref[...] = _dot(xb_s[...], wgate_ref[...])

    z_ref[...] = _dot(xb_s[...], w_ref[...]).astype(z_ref.dtype)


def _inproj(x2, w_main_bf16, w_gate_bf16):
    t = x2.shape[0]
    tm, tn = INPROJ_TM, INPROJ_TN
    n_main = w_main_bf16.shape[1]
    return pl.pallas_call(
        _inproj_kernel,
        grid=(t // tm, n_main // tn),
        in_specs=[
            pl.BlockSpec((tm, D_MODEL), lambda m, n: (m, 0)),
            pl.BlockSpec((D_MODEL, tn), lambda m, n: (0, n)),
            pl.BlockSpec((D_MODEL, LANES), lambda m, n: (0, 0)),
        ],
        out_specs=[
            pl.BlockSpec((tm, tn), lambda m, n: (m, n)),
            pl.BlockSpec((tm, LANES), lambda m, n: (m, 0)),
        ],
        out_shape=[
            jax.ShapeDtypeStruct((t, n_main), BF16),
            jax.ShapeDtypeStruct((t, LANES), F32),
        ],
        scratch_shapes=[pltpu.VMEM((tm, D_MODEL), BF16)],
        compiler_params=pltpu.CompilerParams(
            dimension_semantics=("parallel", "arbitrary"), vmem_limit_bytes=48 * MIB),
        name="mlstm_inproj",
    )(x2, w_main_bf16, w_gate_bf16)


def _mlstm_cell_kernel(zq_ref, zk_ref, v_ref, op_ref, gt_ref, bg_ref, cwq_ref, cbq_ref, cwk_ref, cbk_ref,
                       ng_ref, y_ref, q_s, k_s, gb_s, bc_s, gbt_s, bct_s, *, chunk, n_heads):
    hgroup = pl.program_id(1)
    seq = zq_ref.shape[0]

    def conv_silu(z_ref, cw_ref, cb_ref):
        z = z_ref[...].astype(F32)
        rowi = lax.broadcasted_iota(I32, z.shape, 0)
        out = cb_ref[...] + cw_ref[0:1, :] * jnp.where(rowi >= CONV_K - 1, pltpu.roll(z, CONV_K - 1, 0), 0.0)
        for j in range(1, CONV_K - 1):
            shift = CONV_K - 1 - j
            out = out + cw_ref[j:j + 1, :] * jnp.where(rowi >= shift, pltpu.roll(z, shift, 0), 0.0)
        out = out + cw_ref[CONV_K - 1:CONV_K, :] * z
        return out * jax.nn.sigmoid(out)

    q_s[...] = conv_silu(zq_ref, cwq_ref, cbq_ref).astype(BF16)
    k_s[...] = conv_silu(zk_ref, cwk_ref, cbk_ref) * (QK_DIM ** -0.5)

    lane = lax.broadcasted_iota(I32, (chunk, LANES), 1)
    sub = lax.broadcasted_iota(I32, (MLSTM_HEADS, chunk), 0)
    ti = lax.broadcasted_iota(I32, (chunk, chunk), 0)
    si = lax.broadcasted_iota(I32, (chunk, chunk), 1)
    causal = ti >= si
    ones_cols = jnp.ones((chunk, LANES), BF16)

    @pl.when(hgroup == 0)
    def _():
        tri = jnp.where(causal, 1.0, 0.0)
        bias = bg_ref[...]
        for c in range(seq // chunk):
            rs = slice(c * chunk, (c + 1) * chunk)
            gb = gt_ref[rs, :] + bias
            log_f = -(jnp.maximum(-gb, 0.0) + jnp.log1p(jnp.exp(-jnp.abs(gb))))
            bcum = _dot(tri, log_f, precision=HIGHEST)
            gb_s[rs, :] = gb
            bc_s[rs, :] = bcum
            gbt_s[:, rs] = gb.T
            bct_s[:, rs] = bcum.T

    def pick_col(a, idx):
        return jnp.sum(jnp.where(lane == idx, a, 0.0), axis=1, keepdims=True)

    def pick_row(a, idx):
        return jnp.sum(jnp.where(sub == idx, a, 0.0), axis=0, keepdims=True)

    heads = [hgroup * n_heads + i for i in range(n_heads)]
    qcols = [slice(i * QK_DIM, (i + 1) * QK_DIM) for i in range(n_heads)]
    vcols = [slice(i * V_DIM, (i + 1) * V_DIM) for i in range(n_heads)]
    hr = range(n_heads)
    c_state = [jnp.zeros((QK_DIM, V_DIM + LANES), F32) for _ in hr]
    m_state = [jnp.full((1, 1), NEG, F32) for _ in hr]
    for c in range(seq // chunk):
        rs = slice(c * chunk, (c + 1) * chunk)
        gb_c, bc_c = gb_s[rs, :], bc_s[rs, :]
        gbt_c, bct_c = gbt_s[0:MLSTM_HEADS, rs], bct_s[MLSTM_HEADS:2 * MLSTM_HEADS, rs]
        li_col = [pick_col(gb_c, hd) for hd in heads]
        bc_col = [pick_col(bc_c, hd + MLSTM_HEADS) for hd in heads]
        li_row = [pick_row(gbt_c, hd) for hd in heads]
        bc_row = [pick_row(bct_c, hd) for hd in heads]
        dmat = [jnp.where(causal, bc_col[i] + (li_row[i] - bc_row[i]), NEG) for i in hr]
        inter = [bc_col[i] + m_state[i] for i in hr]
        m_t = [jnp.maximum(inter[i], jnp.max(dmat[i], axis=1, keepdims=True)) for i in hr]
        qc = [q_s[rs, qcols[i]] for i in hr]
        kf = [k_s[rs, qcols[i]] for i in hr]
        vc = [v_ref[rs, vcols[i]] for i in hr]
        a = [_dot_nt(qc[i], kf[i].astype(BF16)) * jnp.exp(dmat[i] - m_t[i]) for i in hr]
        w_inter = [jnp.exp(inter[i] - m_t[i]) for i in hr]
        q_state = [_dot(qc[i], c_state[i].astype(BF16)) for i in hr]
        num = [_dot(a[i].astype(BF16), vc[i]) + w_inter[i] * q_state[i][:, :V_DIM] for i in hr]
        den = [jnp.sum(a[i], axis=1, keepdims=True) + w_inter[i] * q_state[i][:, V_DIM:V_DIM + 1] for i in hr]
        h_out = [num[i] / jnp.maximum(jnp.abs(den[i]), jnp.exp(-m_t[i])) for i in hr]
        mu = [jnp.mean(h_out[i], axis=1, keepdims=True) for i in hr]
        hc = [h_out[i] - mu[i] for i in hr]
        var = [jnp.mean(hc[i] * hc[i], axis=1, keepdims=True) for i in hr]
        for i in hr:
            hn = hc[i] * lax.rsqrt(var[i] + LN_EPS) * ng_ref[:, vcols[i]]
            y_ref[rs, vcols[i]] = (hn * jax.nn.sigmoid(op_ref[rs, vcols[i]].astype(F32))).astype(y_ref.dtype)
        b_last = [bc_col[i][chunk - 1:chunk, :] for i in hr]
        g = [b_last[i] - bc_col[i] + li_col[i] for i in hr]
        m_new = [jnp.maximum(b_last[i] + m_state[i], jnp.max(g[i], axis=0, keepdims=True)) for i in hr]
        wk = [jnp.exp(g[i] - m_new[i]) for i in hr]
        decay = [jnp.exp(b_last[i] + m_state[i] - m_new[i]) for i in hr]
        kw_t = [(wk[i] * kf[i]).T.astype(BF16) for i in hr]
        upd = [_dot(kw_t[i], jnp.concatenate([vc[i], ones_cols], axis=1)) for i in hr]
        c_state = [decay[i] * c_state[i] + upd[i] for i in hr]
        m_state = m_new


def _mlstm_cell(z, gates, b_gates_pad, conv_w, conv_b, norm_g, batch):
    t = z.shape[0]
    nh = MLSTM_HEADS_PER_STEP
    groups = MLSTM_HEADS // nh
    qw, vw = nh * QK_DIM, nh * V_DIM
    k_blk0 = (MLSTM_QK_COLS // 2) // qw
    v_blk0 = MLSTM_QK_COLS // vw
    o_blk0 = (MLSTM_QK_COLS + MLSTM_V_COLS) // vw
    return pl.pallas_call(
        functools.partial(_mlstm_cell_kernel, chunk=MLSTM_CHUNK, n_heads=nh),
        grid=(batch, groups),
        in_specs=[
            pl.BlockSpec((SEQ, qw), lambda b, h: (b, h)),
            pl.BlockSpec((SEQ, qw), lambda b, h: (b, k_blk0 + h)),
            pl.BlockSpec((SEQ, vw), lambda b, h: (b, v_blk0 + h)),
            pl.BlockSpec((SEQ, vw), lambda b, h: (b, o_blk0 + h)),
            pl.BlockSpec((SEQ, LANES), lambda b, h: (b, 0)),
            pl.BlockSpec((1, LANES), lambda b, h: (0, 0)),
            pl.BlockSpec((CONV_K, qw), lambda b, h: (0, h)),
            pl.BlockSpec((1, qw), lambda b, h: (0, h)),
            pl.BlockSpec((CONV_K, qw), lambda b, h: (0, k_blk0 + h)),
            pl.BlockSpec((1, qw), lambda b, h: (0, k_blk0 + h)),
            pl.BlockSpec((1, vw), lambda b, h: (0, h)),
        ],
        out_specs=pl.BlockSpec((SEQ, vw), lambda b, h: (b, h)),
        out_shape=jax.ShapeDtypeStruct((t, MLSTM_V_COLS), BF16),
        scratch_shapes=[pltpu.VMEM((SEQ, qw), BF16), pltpu.VMEM((SEQ, qw), F32),
                        pltpu.VMEM((SEQ, LANES), F32), pltpu.VMEM((SEQ, LANES), F32),
                        pltpu.VMEM((LANES, SEQ), F32), pltpu.VMEM((LANES, SEQ), F32)],
        compiler_params=pltpu.CompilerParams(
            dimension_semantics=("parallel", "arbitrary"), vmem_limit_bytes=56 * MIB),
        name="mlstm_cell",
    )(z, z, z, z, gates, b_gates_pad, conv_w, conv_b.reshape(1, -1), conv_w, conv_b.reshape(1, -1),
      norm_g.reshape(1, -1))


def kernel(x, attn_w_qkv, attn_w_o, mlstm_w_in, mlstm_b_gates, mlstm_conv_w, mlstm_conv_b, mlstm_norm_g,
           mlstm_w_out, ln_mix_g, ln_mix_b, ln_ffn_g, ln_ffn_b, router_w, router_b, moe_w_gate, moe_w_up,
           moe_w_down):
    batch, seq, d = x.shape
    assert (seq, d) == (SEQ, D_MODEL)
    t = batch * seq
    x2 = x.reshape(t, d)

    wr_f32 = jnp.zeros((D_MODEL, LANES), F32).at[:, :N_EXPERTS].set(router_w)
    wr_hi = lax.bitcast_convert_type(lax.bitcast_convert_type(wr_f32, jnp.uint32) & jnp.uint32(0xFFFF0000), F32)
    wr_pad = jnp.concatenate([wr_hi.astype(BF16), (wr_f32 - wr_hi).astype(BF16)], axis=1)
    br_pad = jnp.zeros((1, LANES), F32).at[0, :N_EXPERTS].set(router_b)
    wg_bf, wu_bf, wd_bf = moe_w_gate.astype(BF16), moe_w_up.astype(BF16), moe_w_down.astype(BF16)

    w_qkv = _rope_permute_qk(attn_w_qkv[0]).astype(BF16)
    views = [x] + list(_class_major_views(x))
    outs, lses = [], []
    for group, dil in enumerate(DILATIONS):
        o, lse = _attn_group(views[group], w_qkv, _rope_table(dil), group, dil)
        outs.append(o)
        lses.append(lse)
    x2 = _attn_out(outs, lses, x2, attn_w_o[0].astype(BF16), ln_mix_g[0], ln_mix_b[0])
    x2 = _moe_layer(x2, wr_pad, br_pad, wg_bf[0], wu_bf[0], wd_bf[0], ln_ffn_g[0], ln_ffn_b[0])

    w_in = mlstm_w_in[0]
    w_main = w_in[:, :MLSTM_MAIN_COLS].astype(BF16)
    w_gate = jnp.zeros((D_MODEL, LANES), F32).at[:, :2 * MLSTM_HEADS].set(w_in[:, MLSTM_MAIN_COLS:]).astype(BF16)
    bg_pad = jnp.zeros((1, LANES), F32).at[0, :2 * MLSTM_HEADS].set(mlstm_b_gates[0])
    z, gates = _inproj(x2, w_main, w_gate)
    y = _mlstm_cell(z, gates, bg_pad, mlstm_conv_w[0], mlstm_conv_b[0], mlstm_norm_g[0], batch)
    x2 = _mix_out(y, x2, mlstm_w_out[0].astype(BF16), ln_mix_g[1], ln_mix_b[1])
    x2 = _moe_layer(x2, wr_pad, br_pad, wg_bf[1], wu_bf[1], wd_bf[1], ln_ffn_g[1], ln_ffn_b[1])
    return x2.reshape(batch, seq, d)
```

```python
import functools

import numpy as np
import jax
import jax.numpy as jnp
from jax import lax
from jax.experimental import pallas as pl
from jax.experimental.pallas import tpu as pltpu

F32 = jnp.float32
BF16 = jnp.bfloat16
I32 = jnp.int32
HIGHEST = lax.Precision.HIGHEST

D_MODEL = 2048
SEQ = 2048
DEPTH = 2
DILATIONS = (1, 4, 16)
N_BACK = 128
ATTN_HEADS = 8
HEAD_DIM = 128
ROT_DIM = HEAD_DIM // 4
ROPE_THETA = 500000.0
ATTN_BLOCK = 128
ATTN_COLS = ATTN_HEADS * HEAD_DIM

MLSTM_HEADS = 8
QK_DIM = 128
V_DIM = D_MODEL // MLSTM_HEADS
CONV_K = 4
MLSTM_QK_COLS = 2 * MLSTM_HEADS * QK_DIM
MLSTM_V_COLS = MLSTM_HEADS * V_DIM
MLSTM_MAIN_COLS = MLSTM_QK_COLS + 2 * MLSTM_V_COLS

N_EXPERTS = 16
N_EXPERT_GROUPS = 4
EXPERTS_PER_GROUP = 4
D_EXPERT = 768
PAIRS = ((0, 1), (0, 2), (0, 3), (1, 2), (1, 3), (2, 3))
N_CLASSES = N_EXPERT_GROUPS * len(PAIRS)

ALPHA = (2 * DEPTH) ** 0.25
LN_EPS = 1e-5
NEG = -1e30

LANES = 128
V7X_VMEM_BYTES = 64 * 1024 * 1024
N_DMA_QUEUES = 2
MIB = 1024 * 1024

ATTN_STEP_ROWS = 512
ATTN_HEAD_SET = 8
ROPE_PAIR_SHIFT = 64
ROW_TILE = 512
INPROJ_TM = 1024
INPROJ_TN = 1024
MLSTM_CHUNK = 128
MLSTM_HEADS_PER_STEP = 4
MOE_BLOCK = 256
GATHER_TM = 256


def _dot(a, b, **kw):
    return jnp.dot(a, b, preferred_element_type=F32, **kw)


def _dot_nt(a, b):
    return lax.dot_general(a, b, (((1,), (1,)), ((), ())), preferred_element_type=F32)


def _layer_norm_rows(y, g, b):
    mu = jnp.mean(y, axis=-1, keepdims=True)
    yc = y - mu
    var = jnp.mean(yc * yc, axis=-1, keepdims=True)
    return yc * lax.rsqrt(var + LN_EPS) * g + b


def _attn_group_kernel(x_ref, w_ref, tab_ref, o_ref, lse_ref, q_s, k_s, v_s, *, n_cls, lc, carry):
    step = pl.program_id(1)
    rows_total = n_cls * lc
    blocks_per_class = lc // ATTN_BLOCK

    if n_cls == 1:
        xs = x_ref[0]
    else:
        xs = jnp.concatenate([x_ref[0, :, c * D_MODEL:(c + 1) * D_MODEL] for c in range(n_cls)], axis=0)
    qkv = _dot(xs.astype(BF16), w_ref[...])

    if carry:
        @pl.when(step == 0)
        def _():
            k_s[0:ATTN_BLOCK, :] = jnp.zeros((ATTN_BLOCK, ATTN_COLS), BF16)
            v_s[0:ATTN_BLOCK, :] = jnp.zeros((ATTN_BLOCK, ATTN_COLS), BF16)

        @pl.when(step > 0)
        def _():
            k_s[0:ATTN_BLOCK, :] = k_s[rows_total:rows_total + ATTN_BLOCK, :]
            v_s[0:ATTN_BLOCK, :] = v_s[rows_total:rows_total + ATTN_BLOCK, :]

    cosf = tab_ref[:, 0:LANES]
    sinr = tab_ref[:, LANES:2 * LANES]

    def rope(t):
        return t * cosf + pltpu.roll(t, ROPE_PAIR_SHIFT, 1) * sinr

    for h in range(ATTN_HEADS):
        cs = slice(h * HEAD_DIM, (h + 1) * HEAD_DIM)
        q_s[:, cs] = rope(qkv[:, h * HEAD_DIM:(h + 1) * HEAD_DIM]).astype(BF16)
        k_s[ATTN_BLOCK:, cs] = rope(qkv[:, ATTN_COLS + h * HEAD_DIM:ATTN_COLS + (h + 1) * HEAD_DIM]).astype(BF16)
    v_s[ATTN_BLOCK:, :] = qkv[:, 2 * ATTN_COLS:3 * ATTN_COLS].astype(BF16)

    row = lax.broadcasted_iota(I32, (ATTN_BLOCK, ATTN_BLOCK), 0)
    col = lax.broadcasted_iota(I32, (ATTN_BLOCK, ATTN_BLOCK), 1)
    row2 = lax.broadcasted_iota(I32, (ATTN_BLOCK, 2 * ATTN_BLOCK), 0)
    col2 = lax.broadcasted_iota(I32, (ATTN_BLOCK, 2 * ATTN_BLOCK), 1)
    lane = lax.broadcasted_iota(I32, (ATTN_BLOCK, LANES), 1)
    mask_cur = col <= row
    mask_both = jnp.logical_and(col2 >= row2, col2 <= row2 + N_BACK)
    if carry:
        mask_first = jnp.logical_and(mask_both, jnp.logical_or(col2 >= ATTN_BLOCK, step > 0))
    scale = HEAD_DIM ** -0.5

    for c in range(n_cls):
        for bi in range(blocks_per_class):
            j = c * blocks_per_class + bi
            ors = slice(bi * ATTN_BLOCK, (bi + 1) * ATTN_BLOCK)
            with_prev = bi > 0 or carry
            if with_prev:
                krows = slice(j * ATTN_BLOCK, (j + 2) * ATTN_BLOCK)
                mask = mask_first if bi == 0 else mask_both
            else:
                krows = slice((j + 1) * ATTN_BLOCK, (j + 2) * ATTN_BLOCK)
                mask = mask_cur
            lse_tile = jnp.zeros((ATTN_BLOCK, LANES), F32)
            for h0 in range(0, ATTN_HEADS, ATTN_HEAD_SET):
                heads = range(h0, h0 + ATTN_HEAD_SET)
                cols = [slice(h * HEAD_DIM, (h + 1) * HEAD_DIM) for h in heads]
                scores = [jnp.where(mask, _dot_nt(q_s[j * ATTN_BLOCK:(j + 1) * ATTN_BLOCK, cs], k_s[krows, cs])
                                    * scale, NEG) for cs in cols]
                maxes = [jnp.max(s, axis=1, keepdims=True) for s in scores]
                probs = [jnp.exp(s - m) for s, m in zip(scores, maxes)]
                dens = [jnp.sum(p, axis=1, keepdims=True) for p in probs]
                accs = [_dot(p.astype(BF16), v_s[krows, cs]) for p, cs in zip(probs, cols)]
                for h, acc, den, m in zip(heads, accs, dens, maxes):
                    o_ref[0, ors, c * ATTN_COLS + h * HEAD_DIM:c * ATTN_COLS + (h + 1) * HEAD_DIM] = (
                        acc / den).astype(o_ref.dtype)
                    lse_tile = jnp.where(lane == h, m + jnp.log(den), lse_tile)
            lse_ref[0, ors, c * LANES:(c + 1) * LANES] = lse_tile


def _class_major_perm(dil):
    width = ATTN_STEP_ROWS // dil
    perm = np.zeros((ATTN_STEP_ROWS, ATTN_STEP_ROWS), np.float32)
    for r in range(dil):
        for m in range(width):
            perm[r * width + m, m * dil + r] = 1.0
    return perm


def _class_major_kernel(x_ref, *refs):
    n = len(DILATIONS) - 1
    xb = x_ref[0].astype(BF16)
    for p_ref, out_ref, dil in zip(refs[:n], refs[n:], DILATIONS[1:]):
        width = ATTN_STEP_ROWS // dil
        rows = _dot(p_ref[...], xb).astype(BF16)
        for r in range(dil):
            out_ref[0, :, r * D_MODEL:(r + 1) * D_MODEL] = rows[r * width:(r + 1) * width, :]


def _class_major_views(x3):
    batch = x3.shape[0]
    perms = [jnp.asarray(_class_major_perm(d), BF16) for d in DILATIONS[1:]]
    return pl.pallas_call(
        _class_major_kernel,
        grid=(batch, SEQ // ATTN_STEP_ROWS),
        in_specs=[pl.BlockSpec((1, ATTN_STEP_ROWS, D_MODEL), lambda b, s: (b, s, 0))] + [
            pl.BlockSpec((ATTN_STEP_ROWS, ATTN_STEP_ROWS), lambda b, s: (0, 0)) for _ in perms],
        out_specs=[pl.BlockSpec((1, ATTN_STEP_ROWS // d, d * D_MODEL), lambda b, s: (b, s, 0)) for d in DILATIONS[1:]],
        out_shape=[jax.ShapeDtypeStruct((batch, SEQ // d, d * D_MODEL), BF16) for d in DILATIONS[1:]],
        compiler_params=pltpu.CompilerParams(dimension_semantics=("parallel", "parallel"), vmem_limit_bytes=40 * MIB),
        name="class_major_views",
    )(x3, *perms)


def _attn_group(xg, w_qkv_bf16, tab, group, dil):
    batch = xg.shape[0]
    per_class = SEQ // dil
    lc = min(ATTN_STEP_ROWS, per_class)
    n_cls = ATTN_STEP_ROWS // lc
    steps = SEQ // ATTN_STEP_ROWS
    carry = dil == 1
    if dil == 1:
        imap = lambda b, s: (b, s, 0)
    else:
        imap = lambda b, s: (b, 0, s)
    kern = functools.partial(_attn_group_kernel, n_cls=n_cls, lc=lc, carry=carry)
    o, lse = pl.pallas_call(
        kern,
        grid=(batch, steps),
        in_specs=[
            pl.BlockSpec((1, lc, n_cls * D_MODEL), imap),
            pl.BlockSpec((D_MODEL, 3 * ATTN_COLS), lambda b, s: (0, group), pipeline_mode=pl.Buffered(1)),
            pl.BlockSpec((ATTN_STEP_ROWS, 2 * LANES), lambda b, s: (s, 0)),
        ],
        out_specs=[
            pl.BlockSpec((1, lc, n_cls * ATTN_COLS), imap),
            pl.BlockSpec((1, lc, n_cls * LANES), imap),
        ],
        out_shape=[
            jax.ShapeDtypeStruct((batch, per_class, dil * ATTN_COLS), BF16),
            jax.ShapeDtypeStruct((batch, per_class, dil * LANES), F32),
        ],
        scratch_shapes=[
            pltpu.VMEM((ATTN_STEP_ROWS, ATTN_COLS), BF16),
            pltpu.VMEM((ATTN_BLOCK + ATTN_STEP_ROWS, ATTN_COLS), BF16),
            pltpu.VMEM((ATTN_BLOCK + ATTN_STEP_ROWS, ATTN_COLS), BF16),
        ],
        compiler_params=pltpu.CompilerParams(
            dimension_semantics=("parallel", "arbitrary"), vmem_limit_bytes=56 * MIB),
        name=f"attn_group{group}",
    )(xg, w_qkv_bf16, tab)
    return o, lse


def _rope_permute_qk(w_qkv):
    half = ROT_DIM // 2
    lo, hi = ROPE_PAIR_SHIFT, ROPE_PAIR_SHIFT + half
    w = w_qkv.reshape(D_MODEL, len(DILATIONS), 3, ATTN_HEADS, HEAD_DIM)
    qk = w[:, :, 0:2]
    qk = jnp.concatenate([qk[..., :half], qk[..., lo:hi], qk[..., ROT_DIM:lo], qk[..., half:ROT_DIM], qk[..., hi:]], -1)
    return jnp.concatenate([qk, w[:, :, 2:3]], axis=2).reshape(w_qkv.shape)


def _rope_table(dil):
    inv_freq = ROPE_THETA ** (-np.arange(0, ROT_DIM, 2, dtype=np.float64) / ROT_DIM)
    ang = np.arange(SEQ, dtype=np.float64)[:, None] * inv_freq[None, :]
    ang = np.concatenate([ang, ang], -1)
    cos, sin = np.cos(ang), np.sin(ang)
    half = ROT_DIM // 2
    gap = ROPE_PAIR_SHIFT - half
    tail = LANES - ROPE_PAIR_SHIFT - half
    cosf = np.concatenate([cos[:, :half], np.ones((SEQ, gap)), cos[:, half:], np.ones((SEQ, tail))], 1)
    sinr = np.concatenate([-sin[:, :half], np.zeros((SEQ, gap)), sin[:, half:], np.zeros((SEQ, tail))], 1)
    tab = np.concatenate([cosf, sinr], 1)
    tab = tab.reshape(SEQ // dil, dil, 2 * LANES).transpose(1, 0, 2).reshape(SEQ, 2 * LANES)
    return jnp.asarray(tab.astype(np.float32))


def _attn_out_kernel(o0_ref, o1_ref, o2_ref, l0_ref, l1_ref, l2_ref, pt1_ref, pt2_ref, x_ref, w_ref, g_ref, b_ref,
                     wr_ref, br_ref, out_ref, mi_ref, mf_ref, cnt_ref, l1_s, l2_s, carry_s):
    o_nat = [None]
    for o_ref, l_ref, pt_ref, l_s, dil in ((o1_ref, l1_ref, pt1_ref, l1_s, DILATIONS[1]),
                                          (o2_ref, l2_ref, pt2_ref, l2_s, DILATIONS[2])):
        width = ATTN_STEP_ROWS // dil
        o_cm = jnp.concatenate([o_ref[0, :, r * ATTN_COLS:(r + 1) * ATTN_COLS] for r in range(dil)], axis=0)
        o_nat.append(_dot(pt_ref[...], o_cm))
        for r in range(dil):
            l_s[pl.ds(r, width, stride=dil), :] = l_ref[0, :, r * LANES:(r + 1) * LANES]
    ls = [l0_ref[0], l1_s[...], l2_s[...]]
    mx = jnp.maximum(jnp.maximum(ls[0], ls[1]), ls[2])
    es = [jnp.exp(l - mx) for l in ls]
    den = es[0] + es[1] + es[2]
    ws = [e / den for e in es]
    parts = []
    for h in range(ATTN_HEADS):
        cs = slice(h * HEAD_DIM, (h + 1) * HEAD_DIM)
        acc = ws[0][:, h:h + 1] * o0_ref[0, :, cs].astype(F32)
        acc = acc + ws[1][:, h:h + 1] * o_nat[1][:, cs]
        acc = acc + ws[2][:, h:h + 1] * o_nat[2][:, cs]
        parts.append(acc)
    mixed_in = jnp.concatenate(parts, axis=1).astype(BF16)
    y = ALPHA * x_ref[...] + _dot(mixed_in, w_ref[...])
    out = _layer_norm_rows(y, g_ref[...], b_ref[...])
    out_ref[...] = out
    _route_tile(out, wr_ref, br_ref, mi_ref, mf_ref, cnt_ref, carry_s)


def _attn_out(outs, lses, x2, w_bf16, g, b, wr_pad, br_pad):
    t = x2.shape[0]
    steps = SEQ // ATTN_STEP_ROWS
    r_in, r_out, r_shape, r_scratch = _route_specs(t, ATTN_STEP_ROWS)
    view_map = lambda i: (i // steps, i % steps, 0)
    view_spec = lambda dil, width: pl.BlockSpec((1, ATTN_STEP_ROWS // dil, dil * width), view_map)
    const_spec = lambda shape: pl.BlockSpec(shape, lambda i: (0, 0))
    row_spec = pl.BlockSpec((ATTN_STEP_ROWS, D_MODEL), lambda i: (i, 0))
    perms_t = [jnp.asarray(_class_major_perm(d).T, BF16) for d in DILATIONS[1:]]
    return pl.pallas_call(
        _attn_out_kernel,
        grid=(t // ATTN_STEP_ROWS,),
        in_specs=[view_spec(d, ATTN_COLS) for d in DILATIONS] + [view_spec(d, LANES) for d in DILATIONS] + [
            const_spec((ATTN_STEP_ROWS, ATTN_STEP_ROWS)) for _ in perms_t] + [
            row_spec, pl.BlockSpec((ATTN_COLS, D_MODEL), lambda i: (0, 0), pipeline_mode=pl.Buffered(1)),
            const_spec((1, D_MODEL)), const_spec((1, D_MODEL))] + r_in,
        out_specs=[row_spec] + r_out,
        out_shape=[jax.ShapeDtypeStruct((t, D_MODEL), F32)] + r_shape,
        scratch_shapes=[pltpu.VMEM((ATTN_STEP_ROWS, LANES), F32), pltpu.VMEM((ATTN_STEP_ROWS, LANES), F32)] + r_scratch,
        compiler_params=pltpu.CompilerParams(dimension_semantics=("arbitrary",), vmem_limit_bytes=48 * MIB),
        name="attn_out",
    )(*outs, *lses, *perms_t, x2, w_bf16, g.reshape(1, D_MODEL), b.reshape(1, D_MODEL), wr_pad, br_pad)


def _mix_out_kernel(y_ref, x_ref, w_ref, g_ref, b_ref, wr_ref, br_ref, out_ref, mi_ref, mf_ref, cnt_ref, carry_s):
    y = ALPHA * x_ref[...] + _dot(y_ref[...], w_ref[...])
    out = _layer_norm_rows(y, g_ref[...], b_ref[...])
    out_ref[...] = out
    _route_tile(out, wr_ref, br_ref, mi_ref, mf_ref, cnt_ref, carry_s)


def _mix_out(mixer_y, x2, w_bf16, g, b, wr_pad, br_pad):
    t = x2.shape[0]
    k = w_bf16.shape[0]
    row_spec = lambda width: pl.BlockSpec((ROW_TILE, width), lambda i: (i, 0))
    const_spec = lambda shape: pl.BlockSpec(shape, lambda i: (0, 0))
    r_in, r_out, r_shape, r_scratch = _route_specs(t, ROW_TILE)
    return pl.pallas_call(
        _mix_out_kernel,
        grid=(t // ROW_TILE,),
        in_specs=[row_spec(k), row_spec(D_MODEL),
                  pl.BlockSpec((k, D_MODEL), lambda i: (0, 0), pipeline_mode=pl.Buffered(1)),
                  const_spec((1, D_MODEL)), const_spec((1, D_MODEL))] + r_in,
        out_specs=[row_spec(D_MODEL)] + r_out,
        out_shape=[jax.ShapeDtypeStruct((t, D_MODEL), F32)] + r_shape,
        scratch_shapes=r_scratch,
        compiler_params=pltpu.CompilerParams(dimension_semantics=("arbitrary",), vmem_limit_bytes=48 * MIB),
        name="mix_out",
    )(mixer_y, x2, w_bf16, g.reshape(1, D_MODEL), b.reshape(1, D_MODEL), wr_pad, br_pad)


def _route_tile(x, wr_ref, br_ref, mi_ref, mf_ref, cnt_ref, carry_s):
    tm = x.shape[0]

    @pl.when(pl.program_id(0) == 0)
    def _():
        carry_s[...] = jnp.zeros_like(carry_s)

    x_hi = x.astype(BF16)
    x_lo = (x - x_hi.astype(F32)).astype(BF16)
    hi_terms = _dot(x_hi, wr_ref[...])
    logits = (hi_terms[:, :LANES] + _dot(x_lo, wr_ref[:, :LANES])) + hi_terms[:, LANES:] + br_ref[...]
    lt = logits.T
    l = [lt[e:e + 1, :] for e in range(N_EXPERTS)]
    mx = l[0]
    for e in range(1, N_EXPERTS):
        mx = jnp.maximum(mx, l[e])
    ex = [jnp.exp(v - mx) for v in l]
    tot = ex[0]
    for e in range(1, N_EXPERTS):
        tot = tot + ex[e]
    p = [v / tot for v in ex]

    def first_index_of(vals, target):
        idx = jnp.full_like(target, len(vals) - 1).astype(I32)
        for k in range(len(vals) - 2, -1, -1):
            idx = jnp.where(vals[k] == target, k, idx)
        return idx

    best = None
    for g in range(N_EXPERT_GROUPS):
        pg = p[g * EXPERTS_PER_GROUP:(g + 1) * EXPERTS_PER_GROUP]
        top1 = jnp.maximum(jnp.maximum(pg[0], pg[1]), jnp.maximum(pg[2], pg[3]))
        i1 = first_index_of(pg, top1)
        rest = [jnp.where(i1 == k, -1.0, pg[k]) for k in range(EXPERTS_PER_GROUP)]
        top2 = jnp.maximum(jnp.maximum(rest[0], rest[1]), jnp.maximum(rest[2], rest[3]))
        i2 = first_index_of(rest, top2)
        score = top1 + top2
        if best is None:
            best = (score, jnp.zeros_like(i1), top1, top2, i1, i2)
        else:
            better = score > best[0]
            cand = (score, jnp.full_like(i1, g), top1, top2, i1, i2)
            best = tuple(jnp.where(better, cv, bv) for cv, bv in zip(cand, best))
    _, g_sel, p1, p2, i1, i2 = best
    psum = p1 + p2
    gate1, gate2 = p1 / psum, p2 / psum
    first_low = i1 < i2
    lo = jnp.where(first_low, i1, i2)
    hi = jnp.where(first_low, i2, i1)
    gate_lo = jnp.where(first_low, gate1, gate2)
    gate_hi = jnp.where(first_low, gate2, gate1)
    pair = jnp.where(lo == 0, 0, jnp.where(lo == 1, 3, 5)) + hi - lo - 1
    cls = g_sel * len(PAIRS) + pair

    n_rows = carry_s.shape[0]
    sub = lax.broadcasted_iota(I32, (n_rows, tm), 0)
    onehot = sub == cls
    oh = jnp.where(onehot, 1.0, 0.0)
    upper = (lax.broadcasted_iota(I32, (tm, tm), 0) <= lax.broadcasted_iota(I32, (tm, tm), 1))
    cum = _dot(oh.astype(BF16), jnp.where(upper, 1.0, 0.0).astype(BF16))
    carry = carry_s[:, 0:1]
    rank = jnp.sum(jnp.where(onehot, cum - 1.0 + carry, 0.0), axis=0, keepdims=True)
    carry_new = carry + jnp.sum(oh, axis=1, keepdims=True)
    carry_s[...] = jnp.broadcast_to(carry_new, carry_s.shape)
    cnt_ref[...] = jnp.broadcast_to(carry_new, cnt_ref.shape)

    sub8 = lax.broadcasted_iota(I32, (8, tm), 0)
    mi_ref[...] = jnp.where(sub8 == 0, cls, jnp.where(sub8 == 1, rank.astype(I32), 0))
    mf_ref[...] = jnp.where(sub8 == 0, gate_lo, jnp.where(sub8 == 1, gate_hi, 0.0))


def _route_specs(t, tm):
    in_specs = [pl.BlockSpec((D_MODEL, 2 * LANES), lambda i: (0, 0)), pl.BlockSpec((1, LANES), lambda i: (0, 0))]
    out_specs = [pl.BlockSpec((8, tm), lambda i: (0, i)), pl.BlockSpec((8, tm), lambda i: (0, i)),
                 pl.BlockSpec((32, LANES), lambda i: (0, 0))]
    out_shape = [jax.ShapeDtypeStruct((8, t), I32), jax.ShapeDtypeStruct((8, t), F32),
                 jax.ShapeDtypeStruct((32, LANES), F32)]
    return in_specs, out_specs, out_shape, [pltpu.VMEM((32, LANES), F32)]


def _expert_kernel(elo_ref, ehi_ref, valid_ref, src_first_ref, src_next_ref, x_hbm, gs_ref, wg0, wu0, wd0, wg1, wu1, wd1,
                   out_ref, rows_s, sems):
    del elo_ref, ehi_ref
    i = pl.program_id(0)
    slot = i % 2

    def start_row(idx_ref, t, to_slot, priority):
        pltpu.make_async_copy(x_hbm.at[pl.ds(idx_ref[0, 0, t], 1)], rows_s.at[to_slot, pl.ds(t, 1)],
                              sems.at[to_slot]).start(priority=priority)

    def wait_rows(of_slot):
        pltpu.make_async_copy(x_hbm.at[pl.ds(0, MOE_BLOCK)], rows_s.at[of_slot], sems.at[of_slot]).wait()

    @pl.when(i == 0)
    def _():
        def issue(t, c):
            start_row(src_first_ref, t, 0, 0)
            return c

        lax.fori_loop(0, MOE_BLOCK, issue, 0, unroll=8)

    def used_block(cur):
        wait_rows(cur)
        for t in range(MOE_BLOCK):
            start_row(src_next_ref, t, 1 - cur, priority=t % N_DMA_QUEUES)
        xb = rows_s[cur].astype(BF16)

        def ffn(wg, wu, wd):
            hg = _dot(xb, wg[0])
            hu = _dot(xb, wu[0])
            hidden = (hg * jax.nn.sigmoid(hg)) * hu
            return _dot(hidden.astype(BF16), wd[0])

        gs = gs_ref[...]
        out_ref[...] = gs[:, 0:1] * ffn(wg0, wu0, wd0) + gs[:, 1:2] * ffn(wg1, wu1, wd1)

        @pl.when(valid_ref[i + 1] == 0)
        def _():
            wait_rows(1 - cur)

    for cur in range(2):
        pl.when(jnp.logical_and(valid_ref[i] > 0, slot == cur))(functools.partial(used_block, cur))

    @pl.when(valid_ref[i] == 0)
    def _():
        out_ref[...] = jnp.zeros_like(out_ref)


def _experts(x2, src_rows, gates_sorted, blk_elo, blk_ehi, blk_valid, wg, wu, wd):
    nb = src_rows.shape[0] - 1
    n_rows = nb * MOE_BLOCK
    lo_map = lambda b, elo, ehi, valid: (elo[b], 0, 0)
    hi_map = lambda b, elo, ehi, valid: (ehi[b], 0, 0)
    row_map = lambda b, elo, ehi, valid: (b, 0)
    up_shape = (1, D_MODEL, D_EXPERT)
    down_shape = (1, D_EXPERT, D_MODEL)
    grid_spec = pltpu.PrefetchScalarGridSpec(
        num_scalar_prefetch=3,
        grid=(nb,),
        in_specs=[
            pl.BlockSpec((1, 1, MOE_BLOCK), lambda b, elo, ehi, valid: (0, 0, 0), memory_space=pltpu.SMEM),
            pl.BlockSpec((1, 1, MOE_BLOCK), lambda b, elo, ehi, valid: (b + 1, 0, 0), memory_space=pltpu.SMEM),
            pl.BlockSpec(memory_space=pl.ANY),
            pl.BlockSpec((MOE_BLOCK, LANES), row_map),
            pl.BlockSpec(up_shape, lo_map), pl.BlockSpec(up_shape, lo_map), pl.BlockSpec(down_shape, lo_map),
            pl.BlockSpec(up_shape, hi_map), pl.BlockSpec(up_shape, hi_map), pl.BlockSpec(down_shape, hi_map),
        ],
        out_specs=pl.BlockSpec((MOE_BLOCK, D_MODEL), row_map),
        scratch_shapes=[pltpu.VMEM((2, MOE_BLOCK, D_MODEL), F32), pltpu.SemaphoreType.DMA((2,))],
    )
    return pl.pallas_call(
        _expert_kernel,
        grid_spec=grid_spec,
        out_shape=jax.ShapeDtypeStruct((n_rows, D_MODEL), F32),
        compiler_params=pltpu.CompilerParams(dimension_semantics=("arbitrary",), vmem_limit_bytes=58 * MIB),
        name="moe_experts",
    )(blk_elo, blk_ehi, blk_valid, src_rows, src_rows, x2, gates_sorted, wg, wu, wd, wg, wu, wd)


def _gather_ln_kernel(dest_ref, dest_next_ref, x_ref, src_hbm, g_ref, b_ref, y_ref, rows_s, sems, *, tm):
    i = pl.program_id(0)
    slot = i % 2

    def issue_rows(idx_ref, to_slot):
        for t in range(tm):
            pltpu.make_async_copy(src_hbm.at[pl.ds(idx_ref[0, 0, t], 1)], rows_s.at[to_slot, pl.ds(t, 1)],
                                  sems.at[to_slot]).start(priority=t % N_DMA_QUEUES)

    @pl.when(i == 0)
    def _():
        issue_rows(dest_ref, slot)

    @pl.when(i + 1 < pl.num_programs(0))
    def _():
        issue_rows(dest_next_ref, 1 - slot)

    pltpu.make_async_copy(src_hbm.at[pl.ds(0, tm)], rows_s.at[slot], sems.at[slot]).wait()
    y = ALPHA * x_ref[...] + rows_s[slot]
    y_ref[...] = _layer_norm_rows(y, g_ref[...], b_ref[...])


def _gather_ln(x2, expert_out, dest, g, b):
    t = x2.shape[0]
    tm = GATHER_TM
    n_tiles = t // tm
    dest3 = dest.reshape(n_tiles, 1, tm)
    return pl.pallas_call(
        functools.partial(_gather_ln_kernel, tm=tm),
        grid=(n_tiles,),
        in_specs=[
            pl.BlockSpec((1, 1, tm), lambda i: (i, 0, 0), memory_space=pltpu.SMEM),
            pl.BlockSpec((1, 1, tm), lambda i: (jnp.minimum(i + 1, n_tiles - 1), 0, 0), memory_space=pltpu.SMEM),
            pl.BlockSpec((tm, D_MODEL), lambda i: (i, 0)),
            pl.BlockSpec(memory_space=pl.ANY),
            pl.BlockSpec((1, D_MODEL), lambda i: (0, 0)),
            pl.BlockSpec((1, D_MODEL), lambda i: (0, 0)),
        ],
        out_specs=pl.BlockSpec((tm, D_MODEL), lambda i: (i, 0)),
        out_shape=jax.ShapeDtypeStruct((t, D_MODEL), F32),
        scratch_shapes=[pltpu.VMEM((2, tm, D_MODEL), F32), pltpu.SemaphoreType.DMA((2,))],
        compiler_params=pltpu.CompilerParams(dimension_semantics=("arbitrary",), vmem_limit_bytes=32 * MIB),
        name="moe_gather_ln",
    )(dest3, dest3, x2, expert_out, g.reshape(1, D_MODEL), b.reshape(1, D_MODEL))


_CLASS_LO = np.array([4 * (c // 6) + PAIRS[c % 6][0] for c in range(N_CLASSES)], np.int32)
_CLASS_HI = np.array([4 * (c // 6) + PAIRS[c % 6][1] for c in range(N_CLASSES)], np.int32)


def _moe_layer(x2, routing, wg, wu, wd, ln_g, ln_b):
    t = x2.shape[0]
    nb = -(-(t + N_CLASSES * (MOE_BLOCK - 1)) // MOE_BLOCK)
    n_rows = nb * MOE_BLOCK
    meta_i, meta_f, cnt = routing
    cls, rank = meta_i[0], meta_i[1]
    counts = cnt[:N_CLASSES, 0].astype(I32)
    padded = (counts + MOE_BLOCK - 1) // MOE_BLOCK * MOE_BLOCK
    ends = jnp.cumsum(padded)
    starts = ends - padded
    dest = starts[cls] + rank
    blk_start = jnp.arange(nb, dtype=I32) * MOE_BLOCK
    blk_valid = (jnp.arange(nb + 1, dtype=I32) * MOE_BLOCK < ends[-1]).astype(I32)
    n_valid = jnp.sum(blk_valid)
    blk_cls = jnp.minimum(jnp.sum((ends[None, :] <= blk_start[:, None]).astype(I32), axis=1), N_CLASSES - 1)
    blk_cls = blk_cls[jnp.minimum(jnp.arange(nb), n_valid - 1)]
    blk_elo = jnp.asarray(_CLASS_LO)[blk_cls]
    blk_ehi = jnp.asarray(_CLASS_HI)[blk_cls]
    tok_meta = jnp.concatenate([meta_f[:2].T, jnp.arange(t, dtype=F32)[:, None], jnp.zeros((t, LANES - 3), F32)], 1)
    sorted_meta = jnp.zeros((n_rows + MOE_BLOCK, LANES), F32).at[dest].set(tok_meta)
    src_rows = sorted_meta[:, 2].astype(I32).reshape(nb + 1, 1, MOE_BLOCK)
    out = _experts(x2, src_rows, sorted_meta, blk_elo, blk_ehi, blk_valid, wg, wu, wd)
    return _gather_ln(x2, out, dest, ln_g, ln_b)


def _inproj_kernel(x_ref, w_ref, wgate_ref, z_ref, gates_ref, xb_s):
    @pl.when(pl.program_id(1) == 0)
    def _():
        xb_s[...] = x_ref[...].astype(BF16)
        gates_ref[...] = _dot(xb_s[...], wgate_ref[...])

    z_ref[...] = _dot(xb_s[...], w_ref[...]).astype(z_ref.dtype)


def _inproj(x2, w_in_bf16, w_gate_bf16):
    t = x2.shape[0]
    tm, tn = INPROJ_TM, INPROJ_TN
    n_main = MLSTM_MAIN_COLS
    return pl.pallas_call(
        _inproj_kernel,
        grid=(t // tm, n_main // tn),
        in_specs=[
            pl.BlockSpec((tm, D_MODEL), lambda m, n: (m, 0)),
            pl.BlockSpec((D_MODEL, tn), lambda m, n: (0, n)),
            pl.BlockSpec((D_MODEL, LANES), lambda m, n: (0, 0)),
        ],
        out_specs=[
            pl.BlockSpec((tm, tn), lambda m, n: (m, n)),
            pl.BlockSpec((tm, LANES), lambda m, n: (m, 0)),
        ],
        out_shape=[
            jax.ShapeDtypeStruct((t, n_main), BF16),
            jax.ShapeDtypeStruct((t, LANES), F32),
        ],
        scratch_shapes=[pltpu.VMEM((tm, D_MODEL), BF16)],
        compiler_params=pltpu.CompilerParams(
            dimension_semantics=("parallel", "arbitrary"), vmem_limit_bytes=48 * MIB),
        name="mlstm_inproj",
    )(x2, w_in_bf16, w_gate_bf16)


def _mlstm_cell_kernel(zq_ref, zk_ref, v_ref, op_ref, gt_ref, bg_ref, cwq_ref, cbq_ref, cwk_ref, cbk_ref,
                       ng_ref, y_ref, q_s, k_s, gb_s, bc_s, gbt_s, bct_s, *, chunk, n_heads):
    hgroup = pl.program_id(1)
    seq = zq_ref.shape[0]

    def conv_silu(z_ref, cw_ref, cb_ref):
        z = z_ref[...].astype(F32)
        rowi = lax.broadcasted_iota(I32, z.shape, 0)
        out = cb_ref[...] + cw_ref[0:1, :] * jnp.where(rowi >= CONV_K - 1, pltpu.roll(z, CONV_K - 1, 0), 0.0)
        for j in range(1, CONV_K - 1):
            shift = CONV_K - 1 - j
            out = out + cw_ref[j:j + 1, :] * jnp.where(rowi >= shift, pltpu.roll(z, shift, 0), 0.0)
        out = out + cw_ref[CONV_K - 1:CONV_K, :] * z
        return out * jax.nn.sigmoid(out)

    q_s[...] = conv_silu(zq_ref, cwq_ref, cbq_ref).astype(BF16)
    k_s[...] = conv_silu(zk_ref, cwk_ref, cbk_ref) * (QK_DIM ** -0.5)

    lane = lax.broadcasted_iota(I32, (chunk, LANES), 1)
    sub = lax.broadcasted_iota(I32, (MLSTM_HEADS, chunk), 0)
    ti = lax.broadcasted_iota(I32, (chunk, chunk), 0)
    si = lax.broadcasted_iota(I32, (chunk, chunk), 1)
    causal = ti >= si
    ones_cols = jnp.ones((chunk, LANES), BF16)

    @pl.when(hgroup == 0)
    def _():
        tri = jnp.where(causal, 1.0, 0.0)
        bias = bg_ref[...]
        for c in range(seq // chunk):
            rs = slice(c * chunk, (c + 1) * chunk)
            gb = gt_ref[rs, :] + bias
            log_f = -(jnp.maximum(-gb, 0.0) + jnp.log1p(jnp.exp(-jnp.abs(gb))))
            bcum = _dot(tri, log_f, precision=HIGHEST)
            gb_s[rs, :] = gb
            bc_s[rs, :] = bcum
            gbt_s[:, rs] = gb.T
            bct_s[:, rs] = bcum.T

    def pick_col(a, idx):
        return jnp.sum(jnp.where(lane == idx, a, 0.0), axis=1, keepdims=True)

    def pick_row(a, idx):
        return jnp.sum(jnp.where(sub == idx, a, 0.0), axis=0, keepdims=True)

    heads = [hgroup * n_heads + i for i in range(n_heads)]
    qcols = [slice(i * QK_DIM, (i + 1) * QK_DIM) for i in range(n_heads)]
    vcols = [slice(i * V_DIM, (i + 1) * V_DIM) for i in range(n_heads)]
    hr = range(n_heads)
    c_state = [jnp.zeros((QK_DIM, V_DIM + LANES), F32) for _ in hr]
    m_state = [jnp.full((1, 1), NEG, F32) for _ in hr]
    for c in range(seq // chunk):
        rs = slice(c * chunk, (c + 1) * chunk)
        gb_c, bc_c = gb_s[rs, :], bc_s[rs, :]
        gbt_c, bct_c = gbt_s[0:MLSTM_HEADS, rs], bct_s[MLSTM_HEADS:2 * MLSTM_HEADS, rs]
        li_col = [pick_col(gb_c, hd) for hd in heads]
        bc_col = [pick_col(bc_c, hd + MLSTM_HEADS) for hd in heads]
        li_row = [pick_row(gbt_c, hd) for hd in heads]
        bc_row = [pick_row(bct_c, hd) for hd in heads]
        dmat = [jnp.where(causal, bc_col[i] + (li_row[i] - bc_row[i]), NEG) for i in hr]
        inter = [bc_col[i] + m_state[i] for i in hr]
        m_t = [jnp.maximum(inter[i], jnp.max(dmat[i], axis=1, keepdims=True)) for i in hr]
        qc = [q_s[rs, qcols[i]] for i in hr]
        kf = [k_s[rs, qcols[i]] for i in hr]
        vc = [v_ref[rs, vcols[i]] for i in hr]
        a = [_dot_nt(qc[i], kf[i].astype(BF16)) * jnp.exp(dmat[i] - m_t[i]) for i in hr]
        w_inter = [jnp.exp(inter[i] - m_t[i]) for i in hr]
        q_state = [_dot(qc[i], c_state[i].astype(BF16)) for i in hr]
        num = [_dot(a[i].astype(BF16), vc[i]) + w_inter[i] * q_state[i][:, :V_DIM] for i in hr]
        den = [jnp.sum(a[i], axis=1, keepdims=True) + w_inter[i] * q_state[i][:, V_DIM:V_DIM + 1] for i in hr]
        h_out = [num[i] / jnp.maximum(jnp.abs(den[i]), jnp.exp(-m_t[i])) for i in hr]
        mu = [jnp.mean(h_out[i], axis=1, keepdims=True) for i in hr]
        hc = [h_out[i] - mu[i] for i in hr]
        var = [jnp.mean(hc[i] * hc[i], axis=1, keepdims=True) for i in hr]
        for i in hr:
            hn = hc[i] * lax.rsqrt(var[i] + LN_EPS) * ng_ref[:, vcols[i]]
            y_ref[rs, vcols[i]] = (hn * jax.nn.sigmoid(op_ref[rs, vcols[i]].astype(F32))).astype(y_ref.dtype)
        b_last = [bc_col[i][chunk - 1:chunk, :] for i in hr]
        g = [b_last[i] - bc_col[i] + li_col[i] for i in hr]
        m_new = [jnp.maximum(b_last[i] + m_state[i], jnp.max(g[i], axis=0, keepdims=True)) for i in hr]
        wk = [jnp.exp(g[i] - m_new[i]) for i in hr]
        decay = [jnp.exp(b_last[i] + m_state[i] - m_new[i]) for i in hr]
        kw_t = [(wk[i] * kf[i]).T.astype(BF16) for i in hr]
        upd = [_dot(kw_t[i], jnp.concatenate([vc[i], ones_cols], axis=1)) for i in hr]
        c_state = [decay[i] * c_state[i] + upd[i] for i in hr]
        m_state = m_new


def _mlstm_cell(z, gates, b_gates_pad, conv_w, conv_b, norm_g, batch):
    t = z.shape[0]
    nh = MLSTM_HEADS_PER_STEP
    groups = MLSTM_HEADS // nh
    qw, vw = nh * QK_DIM, nh * V_DIM
    k_blk0 = (MLSTM_QK_COLS // 2) // qw
    v_blk0 = MLSTM_QK_COLS // vw
    o_blk0 = (MLSTM_QK_COLS + MLSTM_V_COLS) // vw
    return pl.pallas_call(
        functools.partial(_mlstm_cell_kernel, chunk=MLSTM_CHUNK, n_heads=nh),
        grid=(batch, groups),
        in_specs=[
            pl.BlockSpec((SEQ, qw), lambda b, h: (b, h)),
            pl.BlockSpec((SEQ, qw), lambda b, h: (b, k_blk0 + h)),
            pl.BlockSpec((SEQ, vw), lambda b, h: (b, v_blk0 + h)),
            pl.BlockSpec((SEQ, vw), lambda b, h: (b, o_blk0 + h)),
            pl.BlockSpec((SEQ, LANES), lambda b, h: (b, 0)),
            pl.BlockSpec((1, LANES), lambda b, h: (0, 0)),
            pl.BlockSpec((CONV_K, qw), lambda b, h: (0, h)),
            pl.BlockSpec((1, qw), lambda b, h: (0, h)),
            pl.BlockSpec((CONV_K, qw), lambda b, h: (0, k_blk0 + h)),
            pl.BlockSpec((1, qw), lambda b, h: (0, k_blk0 + h)),
            pl.BlockSpec((1, vw), lambda b, h: (0, h)),
        ],
        out_specs=pl.BlockSpec((SEQ, vw), lambda b, h: (b, h)),
        out_shape=jax.ShapeDtypeStruct((t, MLSTM_V_COLS), BF16),
        scratch_shapes=[pltpu.VMEM((SEQ, qw), BF16), pltpu.VMEM((SEQ, qw), F32),
                        pltpu.VMEM((SEQ, LANES), F32), pltpu.VMEM((SEQ, LANES), F32),
                        pltpu.VMEM((LANES, SEQ), F32), pltpu.VMEM((LANES, SEQ), F32)],
        compiler_params=pltpu.CompilerParams(
            dimension_semantics=("parallel", "arbitrary"), vmem_limit_bytes=56 * MIB),
        name="mlstm_cell",
    )(z, z, z, z, gates, b_gates_pad, conv_w, conv_b.reshape(1, -1), conv_w, conv_b.reshape(1, -1),
      norm_g.reshape(1, -1))


def kernel(x, attn_w_qkv, attn_w_o, mlstm_w_in, mlstm_b_gates, mlstm_conv_w, mlstm_conv_b, mlstm_norm_g,
           mlstm_w_out, ln_mix_g, ln_mix_b, ln_ffn_g, ln_ffn_b, router_w, router_b, moe_w_gate, moe_w_up,
           moe_w_down):
    batch, seq, d = x.shape
    assert (seq, d) == (SEQ, D_MODEL)
    t = batch * seq
    x2 = x.reshape(t, d)

    wr_f32 = jnp.zeros((D_MODEL, LANES), F32).at[:, :N_EXPERTS].set(router_w)
    wr_hi = lax.bitcast_convert_type(lax.bitcast_convert_type(wr_f32, jnp.uint32) & jnp.uint32(0xFFFF0000), F32)
    wr_pad = jnp.concatenate([wr_hi.astype(BF16), (wr_f32 - wr_hi).astype(BF16)], axis=1)
    br_pad = jnp.zeros((1, LANES), F32).at[0, :N_EXPERTS].set(router_b)
    wg_bf, wu_bf, wd_bf = moe_w_gate.astype(BF16), moe_w_up.astype(BF16), moe_w_down.astype(BF16)

    w_qkv = _rope_permute_qk(attn_w_qkv[0]).astype(BF16)
    views = [x] + list(_class_major_views(x))
    outs, lses = [], []
    for group, dil in enumerate(DILATIONS):
        o, lse = _attn_group(views[group], w_qkv, _rope_table(dil), group, dil)
        outs.append(o)
        lses.append(lse)
    x2, *routing = _attn_out(outs, lses, x2, attn_w_o[0].astype(BF16), ln_mix_g[0], ln_mix_b[0], wr_pad, br_pad)
    x2 = _moe_layer(x2, routing, wg_bf[0], wu_bf[0], wd_bf[0], ln_ffn_g[0], ln_ffn_b[0])

    w_in = mlstm_w_in[0]
    w_main = w_in.astype(BF16)
    w_gate = jnp.zeros((D_MODEL, LANES), F32).at[:, :2 * MLSTM_HEADS].set(w_in[:, MLSTM_MAIN_COLS:]).astype(BF16)
    bg_pad = jnp.zeros((1, LANES), F32).at[0, :2 * MLSTM_HEADS].set(mlstm_b_gates[0])
    z, gates = _inproj(x2, w_main, w_gate)
    y = _mlstm_cell(z, gates, bg_pad, mlstm_conv_w[0], mlstm_conv_b[0], mlstm_norm_g[0], batch)
    x2, *routing = _mix_out(y, x2, mlstm_w_out[0].astype(BF16), ln_mix_g[1], ln_mix_b[1], wr_pad, br_pad)
    x2 = _moe_layer(x2, routing, wg_bf[1], wu_bf[1], wd_bf[1], ln_ffn_g[1], ln_ffn_b[1])
    return x2.reshape(batch, seq, d)
```

```python
import functools

import numpy as np
import jax
import jax.numpy as jnp
from jax import lax
from jax.experimental import pallas as pl
from jax.experimental.pallas import tpu as pltpu

F32 = jnp.float32
BF16 = jnp.bfloat16
I32 = jnp.int32
HIGHEST = lax.Precision.HIGHEST

D_MODEL = 2048
SEQ = 2048
DEPTH = 2
DILATIONS = (1, 4, 16)
N_BACK = 128
ATTN_HEADS = 8
HEAD_DIM = 128
ROT_DIM = HEAD_DIM // 4
ROPE_THETA = 500000.0
ATTN_BLOCK = 128
ATTN_COLS = ATTN_HEADS * HEAD_DIM

MLSTM_HEADS = 8
QK_DIM = 128
V_DIM = D_MODEL // MLSTM_HEADS
CONV_K = 4
MLSTM_QK_COLS = 2 * MLSTM_HEADS * QK_DIM
MLSTM_V_COLS = MLSTM_HEADS * V_DIM
MLSTM_MAIN_COLS = MLSTM_QK_COLS + 2 * MLSTM_V_COLS

N_EXPERTS = 16
N_EXPERT_GROUPS = 4
EXPERTS_PER_GROUP = 4
D_EXPERT = 768
PAIRS = ((0, 1), (0, 2), (0, 3), (1, 2), (1, 3), (2, 3))
N_CLASSES = N_EXPERT_GROUPS * len(PAIRS)

ALPHA = (2 * DEPTH) ** 0.25
LN_EPS = 1e-5
NEG = -1e30

LANES = 128
V7X_VMEM_BYTES = 64 * 1024 * 1024
N_DMA_QUEUES = 2
MIB = 1024 * 1024

ATTN_STEP_ROWS = 512
ATTN_HEAD_SET = 8
ROPE_PAIR_SHIFT = 64
ROW_TILE = 512
INPROJ_TM = 1024
INPROJ_TN = 1024
MLSTM_CHUNK = 128
MLSTM_HEADS_PER_STEP = 4
MOE_BLOCK = 256
GATHER_TM = 256


def _dot(a, b, **kw):
    return jnp.dot(a, b, preferred_element_type=F32, **kw)


def _dot_nt(a, b):
    return lax.dot_general(a, b, (((1,), (1,)), ((), ())), preferred_element_type=F32)


def _layer_norm_rows(y, g, b):
    mu = jnp.mean(y, axis=-1, keepdims=True)
    yc = y - mu
    var = jnp.mean(yc * yc, axis=-1, keepdims=True)
    return yc * lax.rsqrt(var + LN_EPS) * g + b


def _attn_group_kernel(x_ref, w_ref, tab_ref, o_ref, lse_ref, q_s, k_s, v_s, *, n_cls, lc, carry):
    step = pl.program_id(1)
    rows_total = n_cls * lc
    blocks_per_class = lc // ATTN_BLOCK

    if n_cls == 1:
        xs = x_ref[0]
    else:
        xs = jnp.concatenate([x_ref[0, :, c * D_MODEL:(c + 1) * D_MODEL] for c in range(n_cls)], axis=0)
    qkv = _dot(xs.astype(BF16), w_ref[...])

    if carry:
        @pl.when(step == 0)
        def _():
            k_s[0:ATTN_BLOCK, :] = jnp.zeros((ATTN_BLOCK, ATTN_COLS), BF16)
            v_s[0:ATTN_BLOCK, :] = jnp.zeros((ATTN_BLOCK, ATTN_COLS), BF16)

        @pl.when(step > 0)
        def _():
            k_s[0:ATTN_BLOCK, :] = k_s[rows_total:rows_total + ATTN_BLOCK, :]
            v_s[0:ATTN_BLOCK, :] = v_s[rows_total:rows_total + ATTN_BLOCK, :]

    cosf = tab_ref[:, 0:LANES]
    sinr = tab_ref[:, LANES:2 * LANES]

    def rope(t):
        return t * cosf + pltpu.roll(t, ROPE_PAIR_SHIFT, 1) * sinr

    for h in range(ATTN_HEADS):
        cs = slice(h * HEAD_DIM, (h + 1) * HEAD_DIM)
        q_s[:, cs] = rope(qkv[:, h * HEAD_DIM:(h + 1) * HEAD_DIM]).astype(BF16)
        k_s[ATTN_BLOCK:, cs] = rope(qkv[:, ATTN_COLS + h * HEAD_DIM:ATTN_COLS + (h + 1) * HEAD_DIM]).astype(BF16)
    v_s[ATTN_BLOCK:, :] = qkv[:, 2 * ATTN_COLS:3 * ATTN_COLS].astype(BF16)

    row = lax.broadcasted_iota(I32, (ATTN_BLOCK, ATTN_BLOCK), 0)
    col = lax.broadcasted_iota(I32, (ATTN_BLOCK, ATTN_BLOCK), 1)
    row2 = lax.broadcasted_iota(I32, (ATTN_BLOCK, 2 * ATTN_BLOCK), 0)
    col2 = lax.broadcasted_iota(I32, (ATTN_BLOCK, 2 * ATTN_BLOCK), 1)
    lane = lax.broadcasted_iota(I32, (ATTN_BLOCK, LANES), 1)
    mask_cur = col <= row
    mask_both = jnp.logical_and(col2 >= row2, col2 <= row2 + N_BACK)
    if carry:
        mask_first = jnp.logical_and(mask_both, jnp.logical_or(col2 >= ATTN_BLOCK, step > 0))
    scale = HEAD_DIM ** -0.5

    for c in range(n_cls):
        for bi in range(blocks_per_class):
            j = c * blocks_per_class + bi
            ors = slice(bi * ATTN_BLOCK, (bi + 1) * ATTN_BLOCK)
            with_prev = bi > 0 or carry
            if with_prev:
                krows = slice(j * ATTN_BLOCK, (j + 2) * ATTN_BLOCK)
                mask = mask_first if bi == 0 else mask_both
            else:
                krows = slice((j + 1) * ATTN_BLOCK, (j + 2) * ATTN_BLOCK)
                mask = mask_cur
            lse_tile = jnp.zeros((ATTN_BLOCK, LANES), F32)
            for h0 in range(0, ATTN_HEADS, ATTN_HEAD_SET):
                heads = range(h0, h0 + ATTN_HEAD_SET)
                cols = [slice(h * HEAD_DIM, (h + 1) * HEAD_DIM) for h in heads]
                scores = [jnp.where(mask, _dot_nt(q_s[j * ATTN_BLOCK:(j + 1) * ATTN_BLOCK, cs], k_s[krows, cs])
                                    * scale, NEG) for cs in cols]
                maxes = [jnp.max(s, axis=1, keepdims=True) for s in scores]
                probs = [jnp.exp(s - m) for s, m in zip(scores, maxes)]
                dens = [jnp.sum(p, axis=1, keepdims=True) for p in probs]
                accs = [_dot(p.astype(BF16), v_s[krows, cs]) for p, cs in zip(probs, cols)]
                for h, acc, den, m in zip(heads, accs, dens, maxes):
                    o_ref[0, ors, c * ATTN_COLS + h * HEAD_DIM:c * ATTN_COLS + (h + 1) * HEAD_DIM] = (
                        acc / den).astype(o_ref.dtype)
                    lse_tile = jnp.where(lane == h, m + jnp.log(den), lse_tile)
            lse_ref[0, ors, c * LANES:(c + 1) * LANES] = lse_tile


def _class_major_perm(dil):
    width = ATTN_STEP_ROWS // dil
    perm = np.zeros((ATTN_STEP_ROWS, ATTN_STEP_ROWS), np.float32)
    for r in range(dil):
        for m in range(width):
            perm[r * width + m, m * dil + r] = 1.0
    return perm


def _class_major_kernel(x_ref, *refs):
    n = len(DILATIONS) - 1
    xb = x_ref[0].astype(BF16)
    for p_ref, out_ref, dil in zip(refs[:n], refs[n:], DILATIONS[1:]):
        width = ATTN_STEP_ROWS // dil
        rows = _dot(p_ref[...], xb).astype(BF16)
        for r in range(dil):
            out_ref[0, :, r * D_MODEL:(r + 1) * D_MODEL] = rows[r * width:(r + 1) * width, :]


def _class_major_views(x3):
    batch = x3.shape[0]
    perms = [jnp.asarray(_class_major_perm(d), BF16) for d in DILATIONS[1:]]
    return pl.pallas_call(
        _class_major_kernel,
        grid=(batch, SEQ // ATTN_STEP_ROWS),
        in_specs=[pl.BlockSpec((1, ATTN_STEP_ROWS, D_MODEL), lambda b, s: (b, s, 0))] + [
            pl.BlockSpec((ATTN_STEP_ROWS, ATTN_STEP_ROWS), lambda b, s: (0, 0)) for _ in perms],
        out_specs=[pl.BlockSpec((1, ATTN_STEP_ROWS // d, d * D_MODEL), lambda b, s: (b, s, 0)) for d in DILATIONS[1:]],
        out_shape=[jax.ShapeDtypeStruct((batch, SEQ // d, d * D_MODEL), BF16) for d in DILATIONS[1:]],
        compiler_params=pltpu.CompilerParams(dimension_semantics=("parallel", "parallel"), vmem_limit_bytes=40 * MIB),
        name="class_major_views",
    )(x3, *perms)


def _attn_group(xg, w_qkv_bf16, tab, group, dil):
    batch = xg.shape[0]
    per_class = SEQ // dil
    lc = min(ATTN_STEP_ROWS, per_class)
    n_cls = ATTN_STEP_ROWS // lc
    steps = SEQ // ATTN_STEP_ROWS
    carry = dil == 1
    if dil == 1:
        imap = lambda b, s: (b, s, 0)
    else:
        imap = lambda b, s: (b, 0, s)
    kern = functools.partial(_attn_group_kernel, n_cls=n_cls, lc=lc, carry=carry)
    o, lse = pl.pallas_call(
        kern,
        grid=(batch, steps),
        in_specs=[
            pl.BlockSpec((1, lc, n_cls * D_MODEL), imap),
            pl.BlockSpec((D_MODEL, 3 * ATTN_COLS), lambda b, s: (0, group), pipeline_mode=pl.Buffered(1)),
            pl.BlockSpec((ATTN_STEP_ROWS, 2 * LANES), lambda b, s: (s, 0)),
        ],
        out_specs=[
            pl.BlockSpec((1, lc, n_cls * ATTN_COLS), imap),
            pl.BlockSpec((1, lc, n_cls * LANES), imap),
        ],
        out_shape=[
            jax.ShapeDtypeStruct((batch, per_class, dil * ATTN_COLS), BF16),
            jax.ShapeDtypeStruct((batch, per_class, dil * LANES), F32),
        ],
        scratch_shapes=[
            pltpu.VMEM((ATTN_STEP_ROWS, ATTN_COLS), BF16),
            pltpu.VMEM((ATTN_BLOCK + ATTN_STEP_ROWS, ATTN_COLS), BF16),
            pltpu.VMEM((ATTN_BLOCK + ATTN_STEP_ROWS, ATTN_COLS), BF16),
        ],
        compiler_params=pltpu.CompilerParams(
            dimension_semantics=("parallel", "arbitrary"), vmem_limit_bytes=56 * MIB),
        name=f"attn_group{group}",
    )(xg, w_qkv_bf16, tab)
    return o, lse


def _rope_permute_qk(w_qkv):
    half = ROT_DIM // 2
    src = np.arange(HEAD_DIM)
    src[half:ROT_DIM] = np.arange(ROPE_PAIR_SHIFT, ROPE_PAIR_SHIFT + half)
    src[ROPE_PAIR_SHIFT:ROPE_PAIR_SHIFT + half] = np.arange(half, ROT_DIM)
    swap = np.zeros((HEAD_DIM, HEAD_DIM), np.float32)
    swap[src, np.arange(HEAD_DIM)] = 1.0
    per_part = jnp.asarray(np.stack([swap, swap, np.eye(HEAD_DIM, dtype=np.float32)]), w_qkv.dtype)
    w = w_qkv.reshape(D_MODEL, len(DILATIONS), 3, ATTN_HEADS, HEAD_DIM)
    return jnp.einsum('dgthi,tij->dgthj', w, per_part).reshape(w_qkv.shape)


def _rope_table(dil):
    inv_freq = ROPE_THETA ** (-np.arange(0, ROT_DIM, 2, dtype=np.float64) / ROT_DIM)
    ang = np.arange(SEQ, dtype=np.float64)[:, None] * inv_freq[None, :]
    ang = np.concatenate([ang, ang], -1)
    cos, sin = np.cos(ang), np.sin(ang)
    half = ROT_DIM // 2
    gap = ROPE_PAIR_SHIFT - half
    tail = LANES - ROPE_PAIR_SHIFT - half
    cosf = np.concatenate([cos[:, :half], np.ones((SEQ, gap)), cos[:, half:], np.ones((SEQ, tail))], 1)
    sinr = np.concatenate([-sin[:, :half], np.zeros((SEQ, gap)), sin[:, half:], np.zeros((SEQ, tail))], 1)
    tab = np.concatenate([cosf, sinr], 1)
    tab = tab.reshape(SEQ // dil, dil, 2 * LANES).transpose(1, 0, 2).reshape(SEQ, 2 * LANES)
    return jnp.asarray(tab.astype(np.float32))


def _attn_out_kernel(o0_ref, o1_ref, o2_ref, l0_ref, l1_ref, l2_ref, pt1_ref, pt2_ref, x_ref, w_ref, g_ref, b_ref,
                     wr_ref, br_ref, out_ref, mi_ref, mf_ref, cnt_ref, l1_s, l2_s, carry_s):
    o_nat = [None]
    for o_ref, l_ref, pt_ref, l_s, dil in ((o1_ref, l1_ref, pt1_ref, l1_s, DILATIONS[1]),
                                          (o2_ref, l2_ref, pt2_ref, l2_s, DILATIONS[2])):
        width = ATTN_STEP_ROWS // dil
        o_cm = jnp.concatenate([o_ref[0, :, r * ATTN_COLS:(r + 1) * ATTN_COLS] for r in range(dil)], axis=0)
        o_nat.append(_dot(pt_ref[...], o_cm))
        for r in range(dil):
            l_s[pl.ds(r, width, stride=dil), :] = l_ref[0, :, r * LANES:(r + 1) * LANES]
    ls = [l0_ref[0], l1_s[...], l2_s[...]]
    mx = jnp.maximum(jnp.maximum(ls[0], ls[1]), ls[2])
    es = [jnp.exp(l - mx) for l in ls]
    den = es[0] + es[1] + es[2]
    ws = [e / den for e in es]
    parts = []
    for h in range(ATTN_HEADS):
        cs = slice(h * HEAD_DIM, (h + 1) * HEAD_DIM)
        acc = ws[0][:, h:h + 1] * o0_ref[0, :, cs].astype(F32)
        acc = acc + ws[1][:, h:h + 1] * o_nat[1][:, cs]
        acc = acc + ws[2][:, h:h + 1] * o_nat[2][:, cs]
        parts.append(acc)
    mixed_in = jnp.concatenate(parts, axis=1).astype(BF16)
    y = ALPHA * x_ref[...] + _dot(mixed_in, w_ref[...])
    out = _layer_norm_rows(y, g_ref[...], b_ref[...])
    out_ref[...] = out
    _route_tile(out, wr_ref, br_ref, mi_ref, mf_ref, cnt_ref, carry_s)


def _attn_out(outs, lses, x2, w_bf16, g, b, wr_pad, br_pad):
    t = x2.shape[0]
    steps = SEQ // ATTN_STEP_ROWS
    r_in, r_out, r_shape, r_scratch = _route_specs(t, ATTN_STEP_ROWS)
    view_map = lambda i: (i // steps, i % steps, 0)
    view_spec = lambda dil, width: pl.BlockSpec((1, ATTN_STEP_ROWS // dil, dil * width), view_map)
    const_spec = lambda shape: pl.BlockSpec(shape, lambda i: (0, 0))
    row_spec = pl.BlockSpec((ATTN_STEP_ROWS, D_MODEL), lambda i: (i, 0))
    perms_t = [jnp.asarray(_class_major_perm(d).T, BF16) for d in DILATIONS[1:]]
    return pl.pallas_call(
        _attn_out_kernel,
        grid=(t // ATTN_STEP_ROWS,),
        in_specs=[view_spec(d, ATTN_COLS) for d in DILATIONS] + [view_spec(d, LANES) for d in DILATIONS] + [
            const_spec((ATTN_STEP_ROWS, ATTN_STEP_ROWS)) for _ in perms_t] + [
            row_spec, pl.BlockSpec((ATTN_COLS, D_MODEL), lambda i: (0, 0), pipeline_mode=pl.Buffered(1)),
            const_spec((1, D_MODEL)), const_spec((1, D_MODEL))] + r_in,
        out_specs=[row_spec] + r_out,
        out_shape=[jax.ShapeDtypeStruct((t, D_MODEL), F32)] + r_shape,
        scratch_shapes=[pltpu.VMEM((ATTN_STEP_ROWS, LANES), F32), pltpu.VMEM((ATTN_STEP_ROWS, LANES), F32)] + r_scratch,
        compiler_params=pltpu.CompilerParams(dimension_semantics=("arbitrary",), vmem_limit_bytes=48 * MIB),
        name="attn_out",
    )(*outs, *lses, *perms_t, x2, w_bf16, g.reshape(1, D_MODEL), b.reshape(1, D_MODEL), wr_pad, br_pad)


def _mix_out_kernel(y_ref, x_ref, w_ref, g_ref, b_ref, wr_ref, br_ref, out_ref, mi_ref, mf_ref, cnt_ref, carry_s):
    y = ALPHA * x_ref[...] + _dot(y_ref[...], w_ref[...])
    out = _layer_norm_rows(y, g_ref[...], b_ref[...])
    out_ref[...] = out
    _route_tile(out, wr_ref, br_ref, mi_ref, mf_ref, cnt_ref, carry_s)


def _mix_out(mixer_y, x2, w_bf16, g, b, wr_pad, br_pad):
    t = x2.shape[0]
    k = w_bf16.shape[0]
    row_spec = lambda width: pl.BlockSpec((ROW_TILE, width), lambda i: (i, 0))
    const_spec = lambda shape: pl.BlockSpec(shape, lambda i: (0, 0))
    r_in, r_out, r_shape, r_scratch = _route_specs(t, ROW_TILE)
    return pl.pallas_call(
        _mix_out_kernel,
        grid=(t // ROW_TILE,),
        in_specs=[row_spec(k), row_spec(D_MODEL),
                  pl.BlockSpec((k, D_MODEL), lambda i: (0, 0), pipeline_mode=pl.Buffered(1)),
                  const_spec((1, D_MODEL)), const_spec((1, D_MODEL))] + r_in,
        out_specs=[row_spec(D_MODEL)] + r_out,
        out_shape=[jax.ShapeDtypeStruct((t, D_MODEL), F32)] + r_shape,
        scratch_shapes=r_scratch,
        compiler_params=pltpu.CompilerParams(dimension_semantics=("arbitrary",), vmem_limit_bytes=48 * MIB),
        name="mix_out",
    )(mixer_y, x2, w_bf16, g.reshape(1, D_MODEL), b.reshape(1, D_MODEL), wr_pad, br_pad)


def _route_tile(x, wr_ref, br_ref, mi_ref, mf_ref, cnt_ref, carry_s):
    tm = x.shape[0]

    @pl.when(pl.program_id(0) == 0)
    def _():
        carry_s[...] = jnp.zeros_like(carry_s)

    x_hi = x.astype(BF16)
    x_lo = (x - x_hi.astype(F32)).astype(BF16)
    hi_terms = _dot(x_hi, wr_ref[...])
    logits = (hi_terms[:, :LANES] + _dot(x_lo, wr_ref[:, :LANES])) + hi_terms[:, LANES:] + br_ref[...]
    lt = logits.T
    l = [lt[e:e + 1, :] for e in range(N_EXPERTS)]
    mx = l[0]
    for e in range(1, N_EXPERTS):
        mx = jnp.maximum(mx, l[e])
    ex = [jnp.exp(v - mx) for v in l]
    tot = ex[0]
    for e in range(1, N_EXPERTS):
        tot = tot + ex[e]
    p = [v / tot for v in ex]

    def first_index_of(vals, target):
        idx = jnp.full_like(target, len(vals) - 1).astype(I32)
        for k in range(len(vals) - 2, -1, -1):
            idx = jnp.where(vals[k] == target, k, idx)
        return idx

    best = None
    for g in range(N_EXPERT_GROUPS):
        pg = p[g * EXPERTS_PER_GROUP:(g + 1) * EXPERTS_PER_GROUP]
        top1 = jnp.maximum(jnp.maximum(pg[0], pg[1]), jnp.maximum(pg[2], pg[3]))
        i1 = first_index_of(pg, top1)
        rest = [jnp.where(i1 == k, -1.0, pg[k]) for k in range(EXPERTS_PER_GROUP)]
        top2 = jnp.maximum(jnp.maximum(rest[0], rest[1]), jnp.maximum(rest[2], rest[3]))
        i2 = first_index_of(rest, top2)
        score = top1 + top2
        if best is None:
            best = (score, jnp.zeros_like(i1), top1, top2, i1, i2)
        else:
            better = score > best[0]
            cand = (score, jnp.full_like(i1, g), top1, top2, i1, i2)
            best = tuple(jnp.where(better, cv, bv) for cv, bv in zip(cand, best))
    _, g_sel, p1, p2, i1, i2 = best
    psum = p1 + p2
    gate1, gate2 = p1 / psum, p2 / psum
    first_low = i1 < i2
    lo = jnp.where(first_low, i1, i2)
    hi = jnp.where(first_low, i2, i1)
    gate_lo = jnp.where(first_low, gate1, gate2)
    gate_hi = jnp.where(first_low, gate2, gate1)
    pair = jnp.where(lo == 0, 0, jnp.where(lo == 1, 3, 5)) + hi - lo - 1
    cls = g_sel * len(PAIRS) + pair

    n_rows = carry_s.shape[0]
    sub = lax.broadcasted_iota(I32, (n_rows, tm), 0)
    onehot = sub == cls
    oh = jnp.where(onehot, 1.0, 0.0)
    upper = (lax.broadcasted_iota(I32, (tm, tm), 0) <= lax.broadcasted_iota(I32, (tm, tm), 1))
    cum = _dot(oh.astype(BF16), jnp.where(upper, 1.0, 0.0).astype(BF16))
    carry = carry_s[:, 0:1]
    rank = jnp.sum(jnp.where(onehot, cum - 1.0 + carry, 0.0), axis=0, keepdims=True)
    carry_new = carry + jnp.sum(oh, axis=1, keepdims=True)
    carry_s[...] = jnp.broadcast_to(carry_new, carry_s.shape)
    cnt_ref[...] = jnp.broadcast_to(carry_new, cnt_ref.shape)

    sub8 = lax.broadcasted_iota(I32, (8, tm), 0)
    mi_ref[...] = jnp.where(sub8 == 0, cls, jnp.where(sub8 == 1, rank.astype(I32), 0))
    mf_ref[...] = jnp.where(sub8 == 0, gate_lo, jnp.where(sub8 == 1, gate_hi, 0.0))


def _route_specs(t, tm):
    in_specs = [pl.BlockSpec((D_MODEL, 2 * LANES), lambda i: (0, 0)), pl.BlockSpec((1, LANES), lambda i: (0, 0))]
    out_specs = [pl.BlockSpec((8, tm), lambda i: (0, i)), pl.BlockSpec((8, tm), lambda i: (0, i)),
                 pl.BlockSpec((32, LANES), lambda i: (0, 0))]
    out_shape = [jax.ShapeDtypeStruct((8, t), I32), jax.ShapeDtypeStruct((8, t), F32),
                 jax.ShapeDtypeStruct((32, LANES), F32)]
    return in_specs, out_specs, out_shape, [pltpu.VMEM((32, LANES), F32)]


def _expert_kernel(elo_ref, ehi_ref, valid_ref, src_first_ref, src_next_ref, x_hbm, gs_ref, wg0, wu0, wd0, wg1, wu1, wd1,
                   out_ref, rows_s, sems):
    del elo_ref, ehi_ref
    i = pl.program_id(0)
    slot = i % 2

    def start_row(idx_ref, t, to_slot, priority):
        pltpu.make_async_copy(x_hbm.at[pl.ds(idx_ref[0, 0, t], 1)], rows_s.at[to_slot, pl.ds(t, 1)],
                              sems.at[to_slot]).start(priority=priority)

    def wait_rows(of_slot):
        pltpu.make_async_copy(x_hbm.at[pl.ds(0, MOE_BLOCK)], rows_s.at[of_slot], sems.at[of_slot]).wait()

    @pl.when(i == 0)
    def _():
        def issue(t, c):
            start_row(src_first_ref, t, 0, 0)
            return c

        lax.fori_loop(0, MOE_BLOCK, issue, 0, unroll=8)

    def used_block(cur):
        wait_rows(cur)
        for t in range(MOE_BLOCK):
            start_row(src_next_ref, t, 1 - cur, priority=t % N_DMA_QUEUES)
        xb = rows_s[cur].astype(BF16)

        def ffn(wg, wu, wd):
            hg = _dot(xb, wg[0, 0])
            hu = _dot(xb, wu[0, 0])
            hidden = (hg * jax.nn.sigmoid(hg)) * hu
            return _dot(hidden.astype(BF16), wd[0, 0])

        gs = gs_ref[...]
        out_ref[...] = gs[:, 0:1] * ffn(wg0, wu0, wd0) + gs[:, 1:2] * ffn(wg1, wu1, wd1)

        @pl.when(valid_ref[i + 1] == 0)
        def _():
            wait_rows(1 - cur)

    for cur in range(2):
        pl.when(jnp.logical_and(valid_ref[i] > 0, slot == cur))(functools.partial(used_block, cur))

    @pl.when(valid_ref[i] == 0)
    def _():
        out_ref[...] = jnp.zeros_like(out_ref)


def _experts(x2, src_rows, gates_sorted, blk_elo, blk_ehi, blk_valid, layer, wg, wu, wd):
    nb = src_rows.shape[0] - 1
    n_rows = nb * MOE_BLOCK
    lo_map = lambda b, elo, ehi, valid: (layer, elo[b], 0, 0)
    hi_map = lambda b, elo, ehi, valid: (layer, ehi[b], 0, 0)
    row_map = lambda b, elo, ehi, valid: (b, 0)
    up_shape = (1, 1, D_MODEL, D_EXPERT)
    down_shape = (1, 1, D_EXPERT, D_MODEL)
    grid_spec = pltpu.PrefetchScalarGridSpec(
        num_scalar_prefetch=3,
        grid=(nb,),
        in_specs=[
            pl.BlockSpec((1, 1, MOE_BLOCK), lambda b, elo, ehi, valid: (0, 0, 0), memory_space=pltpu.SMEM),
            pl.BlockSpec((1, 1, MOE_BLOCK), lambda b, elo, ehi, valid: (b + 1, 0, 0), memory_space=pltpu.SMEM),
            pl.BlockSpec(memory_space=pl.ANY),
            pl.BlockSpec((MOE_BLOCK, LANES), row_map),
            pl.BlockSpec(up_shape, lo_map), pl.BlockSpec(up_shape, lo_map), pl.BlockSpec(down_shape, lo_map),
            pl.BlockSpec(up_shape, hi_map), pl.BlockSpec(up_shape, hi_map), pl.BlockSpec(down_shape, hi_map),
        ],
        out_specs=pl.BlockSpec((MOE_BLOCK, D_MODEL), row_map),
        scratch_shapes=[pltpu.VMEM((2, MOE_BLOCK, D_MODEL), F32), pltpu.SemaphoreType.DMA((2,))],
    )
    return pl.pallas_call(
        _expert_kernel,
        grid_spec=grid_spec,
        out_shape=jax.ShapeDtypeStruct((n_rows, D_MODEL), F32),
        compiler_params=pltpu.CompilerParams(dimension_semantics=("arbitrary",), vmem_limit_bytes=58 * MIB),
        name="moe_experts",
    )(blk_elo, blk_ehi, blk_valid, src_rows, src_rows, x2, gates_sorted, wg, wu, wd, wg, wu, wd)


def _gather_ln_kernel(dest_ref, dest_next_ref, x_ref, src_hbm, g_ref, b_ref, y_ref, rows_s, sems, *, tm):
    i = pl.program_id(0)
    slot = i % 2

    def issue_rows(idx_ref, to_slot):
        for t in range(tm):
            pltpu.make_async_copy(src_hbm.at[pl.ds(idx_ref[0, 0, t], 1)], rows_s.at[to_slot, pl.ds(t, 1)],
                                  sems.at[to_slot]).start(priority=t % N_DMA_QUEUES)

    @pl.when(i == 0)
    def _():
        issue_rows(dest_ref, slot)

    @pl.when(i + 1 < pl.num_programs(0))
    def _():
        issue_rows(dest_next_ref, 1 - slot)

    pltpu.make_async_copy(src_hbm.at[pl.ds(0, tm)], rows_s.at[slot], sems.at[slot]).wait()
    y = ALPHA * x_ref[...] + rows_s[slot]
    y_ref[...] = _layer_norm_rows(y, g_ref[...], b_ref[...])


def _gather_ln(x2, expert_out, dest, g, b):
    t = x2.shape[0]
    tm = GATHER_TM
    n_tiles = t // tm
    dest3 = dest.reshape(n_tiles, 1, tm)
    return pl.pallas_call(
        functools.partial(_gather_ln_kernel, tm=tm),
        grid=(n_tiles,),
        in_specs=[
            pl.BlockSpec((1, 1, tm), lambda i: (i, 0, 0), memory_space=pltpu.SMEM),
            pl.BlockSpec((1, 1, tm), lambda i: (jnp.minimum(i + 1, n_tiles - 1), 0, 0), memory_space=pltpu.SMEM),
            pl.BlockSpec((tm, D_MODEL), lambda i: (i, 0)),
            pl.BlockSpec(memory_space=pl.ANY),
            pl.BlockSpec((1, D_MODEL), lambda i: (0, 0)),
            pl.BlockSpec((1, D_MODEL), lambda i: (0, 0)),
        ],
        out_specs=pl.BlockSpec((tm, D_MODEL), lambda i: (i, 0)),
        out_shape=jax.ShapeDtypeStruct((t, D_MODEL), F32),
        scratch_shapes=[pltpu.VMEM((2, tm, D_MODEL), F32), pltpu.SemaphoreType.DMA((2,))],
        compiler_params=pltpu.CompilerParams(dimension_semantics=("arbitrary",), vmem_limit_bytes=32 * MIB),
        name="moe_gather_ln",
    )(dest3, dest3, x2, expert_out, g.reshape(1, D_MODEL), b.reshape(1, D_MODEL))


_CLASS_LO = np.array([4 * (c // 6) + PAIRS[c % 6][0] for c in range(N_CLASSES)], np.int32)
_CLASS_HI = np.array([4 * (c // 6) + PAIRS[c % 6][1] for c in range(N_CLASSES)], np.int32)


def _moe_layer(x2, routing, layer, wg, wu, wd, ln_g, ln_b):
    t = x2.shape[0]
    nb = -(-(t + N_CLASSES * (MOE_BLOCK - 1)) // MOE_BLOCK)
    n_rows = nb * MOE_BLOCK
    meta_i, meta_f, cnt = routing
    cls, rank = meta_i[0], meta_i[1]
    counts = cnt[:N_CLASSES, 0].astype(I32)
    padded = (counts + MOE_BLOCK - 1) // MOE_BLOCK * MOE_BLOCK
    ends = jnp.cumsum(padded)
    starts = ends - padded
    dest = starts[cls] + rank
    blk_start = jnp.arange(nb, dtype=I32) * MOE_BLOCK
    blk_valid = (jnp.arange(nb + 1, dtype=I32) * MOE_BLOCK < ends[-1]).astype(I32)
    n_valid = jnp.sum(blk_valid)
    blk_cls = jnp.minimum(jnp.sum((ends[None, :] <= blk_start[:, None]).astype(I32), axis=1), N_CLASSES - 1)
    blk_cls = blk_cls[jnp.minimum(jnp.arange(nb), n_valid - 1)]
    blk_elo = jnp.asarray(_CLASS_LO)[blk_cls]
    blk_ehi = jnp.asarray(_CLASS_HI)[blk_cls]
    tok_meta = jnp.concatenate([meta_f[:2].T, jnp.arange(t, dtype=F32)[:, None], jnp.zeros((t, LANES - 3), F32)], 1)
    sorted_meta = jnp.zeros((n_rows + MOE_BLOCK, LANES), F32).at[dest].set(tok_meta)
    src_rows = sorted_meta[:, 2].astype(I32).reshape(nb + 1, 1, MOE_BLOCK)
    out = _experts(x2, src_rows, sorted_meta, blk_elo, blk_ehi, blk_valid, layer, wg, wu, wd)
    return _gather_ln(x2, out, dest, ln_g, ln_b)


def _inproj_kernel(x_ref, w_ref, wgate_ref, z_ref, gates_ref, xb_s):
    @pl.when(pl.program_id(1) == 0)
    def _():
        xb_s[...] = x_ref[...].astype(BF16)
        gates_ref[...] = _dot(xb_s[...], wgate_ref[...])

    z_ref[...] = _dot(xb_s[...], w_ref[...]).astype(z_ref.dtype)


def _inproj(x2, w_in_bf16, w_gate_bf16):
    t = x2.shape[0]
    tm, tn = INPROJ_TM, INPROJ_TN
    n_main = MLSTM_MAIN_COLS
    return pl.pallas_call(
        _inproj_kernel,
        grid=(t // tm, n_main // tn),
        in_specs=[
            pl.BlockSpec((tm, D_MODEL), lambda m, n: (m, 0)),
            pl.BlockSpec((D_MODEL, tn), lambda m, n: (0, n)),
            pl.BlockSpec((D_MODEL, LANES), lambda m, n: (0, 0)),
        ],
        out_specs=[
            pl.BlockSpec((tm, tn), lambda m, n: (m, n)),
            pl.BlockSpec((tm, LANES), lambda m, n: (m, 0)),
        ],
        out_shape=[
            jax.ShapeDtypeStruct((t, n_main), BF16),
            jax.ShapeDtypeStruct((t, LANES), F32),
        ],
        scratch_shapes=[pltpu.VMEM((tm, D_MODEL), BF16)],
        compiler_params=pltpu.CompilerParams(
            dimension_semantics=("parallel", "arbitrary"), vmem_limit_bytes=48 * MIB),
        name="mlstm_inproj",
    )(x2, w_in_bf16, w_gate_bf16)


def _mlstm_cell_kernel(zq_ref, zk_ref, v_ref, op_ref, gt_ref, bg_ref, cwq_ref, cbq_ref, cwk_ref, cbk_ref,
                       ng_ref, y_ref, q_s, k_s, gb_s, bc_s, gbt_s, bct_s, *, chunk, n_heads):
    hgroup = pl.program_id(1)
    seq = zq_ref.shape[0]

    def conv_silu(z_ref, cw_ref, cb_ref):
        z = z_ref[...].astype(F32)
        rowi = lax.broadcasted_iota(I32, z.shape, 0)
        out = cb_ref[...] + cw_ref[0:1, :] * jnp.where(rowi >= CONV_K - 1, pltpu.roll(z, CONV_K - 1, 0), 0.0)
        for j in range(1, CONV_K - 1):
            shift = CONV_K - 1 - j
            out = out + cw_ref[j:j + 1, :] * jnp.where(rowi >= shift, pltpu.roll(z, shift, 0), 0.0)
        out = out + cw_ref[CONV_K - 1:CONV_K, :] * z
        return out * jax.nn.sigmoid(out)

    q_s[...] = conv_silu(zq_ref, cwq_ref, cbq_ref).astype(BF16)
    k_s[...] = conv_silu(zk_ref, cwk_ref, cbk_ref) * (QK_DIM ** -0.5)

    lane = lax.broadcasted_iota(I32, (chunk, LANES), 1)
    sub = lax.broadcasted_iota(I32, (MLSTM_HEADS, chunk), 0)
    ti = lax.broadcasted_iota(I32, (chunk, chunk), 0)
    si = lax.broadcasted_iota(I32, (chunk, chunk), 1)
    causal = ti >= si
    ones_cols = jnp.ones((chunk, LANES), BF16)

    @pl.when(hgroup == 0)
    def _():
        tri = jnp.where(causal, 1.0, 0.0)
        bias = bg_ref[...]
        for c in range(seq // chunk):
            rs = slice(c * chunk, (c + 1) * chunk)
            gb = gt_ref[rs, :] + bias
            log_f = -(jnp.maximum(-gb, 0.0) + jnp.log1p(jnp.exp(-jnp.abs(gb))))
            bcum = _dot(tri, log_f, precision=HIGHEST)
            gb_s[rs, :] = gb
            bc_s[rs, :] = bcum
            gbt_s[:, rs] = gb.T
            bct_s[:, rs] = bcum.T

    def pick_col(a, idx):
        return jnp.sum(jnp.where(lane == idx, a, 0.0), axis=1, keepdims=True)

    def pick_row(a, idx):
        return jnp.sum(jnp.where(sub == idx, a, 0.0), axis=0, keepdims=True)

    heads = [hgroup * n_heads + i for i in range(n_heads)]
    qcols = [slice(i * QK_DIM, (i + 1) * QK_DIM) for i in range(n_heads)]
    vcols = [slice(i * V_DIM, (i + 1) * V_DIM) for i in range(n_heads)]
    hr = range(n_heads)
    c_state = [jnp.zeros((QK_DIM, V_DIM + LANES), F32) for _ in hr]
    m_state = [jnp.full((1, 1), NEG, F32) for _ in hr]
    for c in range(seq // chunk):
        rs = slice(c * chunk, (c + 1) * chunk)
        gb_c, bc_c = gb_s[rs, :], bc_s[rs, :]
        gbt_c, bct_c = gbt_s[0:MLSTM_HEADS, rs], bct_s[MLSTM_HEADS:2 * MLSTM_HEADS, rs]
        li_col = [pick_col(gb_c, hd) for hd in heads]
        bc_col = [pick_col(bc_c, hd + MLSTM_HEADS) for hd in heads]
        li_row = [pick_row(gbt_c, hd) for hd in heads]
        bc_row = [pick_row(bct_c, hd) for hd in heads]
        dmat = [jnp.where(causal, bc_col[i] + (li_row[i] - bc_row[i]), NEG) for i in hr]
        inter = [bc_col[i] + m_state[i] for i in hr]
        m_t = [jnp.maximum(inter[i], jnp.max(dmat[i], axis=1, keepdims=True)) for i in hr]
        qc = [q_s[rs, qcols[i]] for i in hr]
        kf = [k_s[rs, qcols[i]] for i in hr]
        vc = [v_ref[rs, vcols[i]] for i in hr]
        a = [_dot_nt(qc[i], kf[i].astype(BF16)) * jnp.exp(dmat[i] - m_t[i]) for i in hr]
        w_inter = [jnp.exp(inter[i] - m_t[i]) for i in hr]
        q_state = [_dot(qc[i], c_state[i].astype(BF16)) for i in hr]
        num = [_dot(a[i].astype(BF16), vc[i]) + w_inter[i] * q_state[i][:, :V_DIM] for i in hr]
        den = [jnp.sum(a[i], axis=1, keepdims=True) + w_inter[i] * q_state[i][:, V_DIM:V_DIM + 1] for i in hr]
        h_out = [num[i] / jnp.maximum(jnp.abs(den[i]), jnp.exp(-m_t[i])) for i in hr]
        mu = [jnp.mean(h_out[i], axis=1, keepdims=True) for i in hr]
        hc = [h_out[i] - mu[i] for i in hr]
        var = [jnp.mean(hc[i] * hc[i], axis=1, keepdims=True) for i in hr]
        for i in hr:
            hn = hc[i] * lax.rsqrt(var[i] + LN_EPS) * ng_ref[:, vcols[i]]
            y_ref[rs, vcols[i]] = (hn * jax.nn.sigmoid(op_ref[rs, vcols[i]].astype(F32))).astype(y_ref.dtype)
        b_last = [bc_col[i][chunk - 1:chunk, :] for i in hr]
        g = [b_last[i] - bc_col[i] + li_col[i] for i in hr]
        m_new = [jnp.maximum(b_last[i] + m_state[i], jnp.max(g[i], axis=0, keepdims=True)) for i in hr]
        wk = [jnp.exp(g[i] - m_new[i]) for i in hr]
        decay = [jnp.exp(b_last[i] + m_state[i] - m_new[i]) for i in hr]
        kw_t = [(wk[i] * kf[i]).T.astype(BF16) for i in hr]
        upd = [_dot(kw_t[i], jnp.concatenate([vc[i], ones_cols], axis=1)) for i in hr]
        c_state = [decay[i] * c_state[i] + upd[i] for i in hr]
        m_state = m_new


def _mlstm_cell(z, gates, b_gates_pad, conv_w, conv_b, norm_g, batch):
    t = z.shape[0]
    nh = MLSTM_HEADS_PER_STEP
    groups = MLSTM_HEADS // nh
    qw, vw = nh * QK_DIM, nh * V_DIM
    k_blk0 = (MLSTM_QK_COLS // 2) // qw
    v_blk0 = MLSTM_QK_COLS // vw
    o_blk0 = (MLSTM_QK_COLS + MLSTM_V_COLS) // vw
    return pl.pallas_call(
        functools.partial(_mlstm_cell_kernel, chunk=MLSTM_CHUNK, n_heads=nh),
        grid=(batch, groups),
        in_specs=[
            pl.BlockSpec((SEQ, qw), lambda b, h: (b, h)),
            pl.BlockSpec((SEQ, qw), lambda b, h: (b, k_blk0 + h)),
            pl.BlockSpec((SEQ, vw), lambda b, h: (b, v_blk0 + h)),
            pl.BlockSpec((SEQ, vw), lambda b, h: (b, o_blk0 + h)),
            pl.BlockSpec((SEQ, LANES), lambda b, h: (b, 0)),
            pl.BlockSpec((1, LANES), lambda b, h: (0, 0)),
            pl.BlockSpec((CONV_K, qw), lambda b, h: (0, h)),
            pl.BlockSpec((1, qw), lambda b, h: (0, h)),
            pl.BlockSpec((CONV_K, qw), lambda b, h: (0, k_blk0 + h)),
            pl.BlockSpec((1, qw), lambda b, h: (0, k_blk0 + h)),
            pl.BlockSpec((1, vw), lambda b, h: (0, h)),
        ],
        out_specs=pl.BlockSpec((SEQ, vw), lambda b, h: (b, h)),
        out_shape=jax.ShapeDtypeStruct((t, MLSTM_V_COLS), BF16),
        scratch_shapes=[pltpu.VMEM((SEQ, qw), BF16), pltpu.VMEM((SEQ, qw), F32),
                        pltpu.VMEM((SEQ, LANES), F32), pltpu.VMEM((SEQ, LANES), F32),
                        pltpu.VMEM((LANES, SEQ), F32), pltpu.VMEM((LANES, SEQ), F32)],
        compiler_params=pltpu.CompilerParams(
            dimension_semantics=("parallel", "arbitrary"), vmem_limit_bytes=56 * MIB),
        name="mlstm_cell",
    )(z, z, z, z, gates, b_gates_pad, conv_w, conv_b.reshape(1, -1), conv_w, conv_b.reshape(1, -1),
      norm_g.reshape(1, -1))


def kernel(x, attn_w_qkv, attn_w_o, mlstm_w_in, mlstm_b_gates, mlstm_conv_w, mlstm_conv_b, mlstm_norm_g,
           mlstm_w_out, ln_mix_g, ln_mix_b, ln_ffn_g, ln_ffn_b, router_w, router_b, moe_w_gate, moe_w_up,
           moe_w_down):
    batch, seq, d = x.shape
    assert (seq, d) == (SEQ, D_MODEL)
    t = batch * seq
    x2 = x.reshape(t, d)

    wr_f32 = jnp.zeros((D_MODEL, LANES), F32).at[:, :N_EXPERTS].set(router_w)
    wr_hi = lax.bitcast_convert_type(lax.bitcast_convert_type(wr_f32, jnp.uint32) & jnp.uint32(0xFFFF0000), F32)
    wr_pad = jnp.concatenate([wr_hi.astype(BF16), (wr_f32 - wr_hi).astype(BF16)], axis=1)
    br_pad = jnp.zeros((1, LANES), F32).at[0, :N_EXPERTS].set(router_b)
    wg_bf, wu_bf, wd_bf = moe_w_gate.astype(BF16), moe_w_up.astype(BF16), moe_w_down.astype(BF16)

    w_qkv = _rope_permute_qk(attn_w_qkv[0].astype(BF16))
    views = [x] + list(_class_major_views(x))
    outs, lses = [], []
    for group, dil in enumerate(DILATIONS):
        o, lse = _attn_group(views[group], w_qkv, _rope_table(dil), group, dil)
        outs.append(o)
        lses.append(lse)
    x2, *routing = _attn_out(outs, lses, x2, attn_w_o[0].astype(BF16), ln_mix_g[0], ln_mix_b[0], wr_pad, br_pad)
    x2 = _moe_layer(x2, routing, 0, wg_bf, wu_bf, wd_bf, ln_ffn_g[0], ln_ffn_b[0])

    w_in = mlstm_w_in[0]
    w_main = w_in.astype(BF16)
    w_gate = jnp.zeros((D_MODEL, LANES), F32).at[:, :2 * MLSTM_HEADS].set(w_in[:, MLSTM_MAIN_COLS:]).astype(BF16)
    bg_pad = jnp.zeros((1, LANES), F32).at[0, :2 * MLSTM_HEADS].set(mlstm_b_gates[0])
    z, gates = _inproj(x2, w_main, w_gate)
    y = _mlstm_cell(z, gates, bg_pad, mlstm_conv_w[0], mlstm_conv_b[0], mlstm_norm_g[0], batch)
    x2, *routing = _mix_out(y, x2, mlstm_w_out[0].astype(BF16), ln_mix_g[1], ln_mix_b[1], wr_pad, br_pad)
    x2 = _moe_layer(x2, routing, 1, wg_bf, wu_bf, wd_bf, ln_ffn_g[1], ln_ffn_b[1])
    return x2.reshape(batch, seq, d)
```

```python
import functools

import numpy as np
import jax
import jax.numpy as jnp
from jax import lax
from jax.experimental import pallas as pl
from jax.experimental.pallas import tpu as pltpu

F32 = jnp.float32
BF16 = jnp.bfloat16
I32 = jnp.int32
HIGHEST = lax.Precision.HIGHEST

D_MODEL = 2048
SEQ = 2048
DEPTH = 2
DILATIONS = (1, 4, 16)
N_BACK = 128
ATTN_HEADS = 8
HEAD_DIM = 128
ROT_DIM = HEAD_DIM // 4
ROPE_THETA = 500000.0
ATTN_BLOCK = 128
ATTN_COLS = ATTN_HEADS * HEAD_DIM

MLSTM_HEADS = 8
QK_DIM = 128
V_DIM = D_MODEL // MLSTM_HEADS
CONV_K = 4
MLSTM_QK_COLS = 2 * MLSTM_HEADS * QK_DIM
MLSTM_V_COLS = MLSTM_HEADS * V_DIM
MLSTM_MAIN_COLS = MLSTM_QK_COLS + 2 * MLSTM_V_COLS

N_EXPERTS = 16
N_EXPERT_GROUPS = 4
EXPERTS_PER_GROUP = 4
D_EXPERT = 768
PAIRS = ((0, 1), (0, 2), (0, 3), (1, 2), (1, 3), (2, 3))
N_CLASSES = N_EXPERT_GROUPS * len(PAIRS)

ALPHA = (2 * DEPTH) ** 0.25
LN_EPS = 1e-5
NEG = -1e30

LANES = 128
V7X_VMEM_BYTES = 64 * 1024 * 1024
N_DMA_QUEUES = 2
MIB = 1024 * 1024

ATTN_STEP_ROWS = 512
ATTN_HEAD_SET = 8
ROPE_PAIR_SHIFT = 64
ROW_TILE = 512
INPROJ_TM = 1024
INPROJ_TN = 1024
MLSTM_CHUNK = 128
MLSTM_HEADS_PER_STEP = 4
MOE_BLOCK = 256
MOE_GATHER_AHEAD = 2
GATHER_TM = 256


def _dot(a, b, **kw):
    return jnp.dot(a, b, preferred_element_type=F32, **kw)


def _dot_nt(a, b):
    return lax.dot_general(a, b, (((1,), (1,)), ((), ())), preferred_element_type=F32)


def _layer_norm_rows(y, g, b):
    mu = jnp.mean(y, axis=-1, keepdims=True)
    yc = y - mu
    var = jnp.mean(yc * yc, axis=-1, keepdims=True)
    return yc * lax.rsqrt(var + LN_EPS) * g + b


def _attn_group_kernel(x_ref, w_ref, tab_ref, o_ref, lse_ref, q_s, k_s, v_s, *, n_cls, lc, carry):
    step = pl.program_id(1)
    rows_total = n_cls * lc
    blocks_per_class = lc // ATTN_BLOCK

    if n_cls == 1:
        xs = x_ref[0]
    else:
        xs = jnp.concatenate([x_ref[0, :, c * D_MODEL:(c + 1) * D_MODEL] for c in range(n_cls)], axis=0)
    qkv = _dot(xs.astype(BF16), w_ref[...])

    if carry:
        @pl.when(step == 0)
        def _():
            k_s[0:ATTN_BLOCK, :] = jnp.zeros((ATTN_BLOCK, ATTN_COLS), BF16)
            v_s[0:ATTN_BLOCK, :] = jnp.zeros((ATTN_BLOCK, ATTN_COLS), BF16)

        @pl.when(step > 0)
        def _():
            k_s[0:ATTN_BLOCK, :] = k_s[rows_total:rows_total + ATTN_BLOCK, :]
            v_s[0:ATTN_BLOCK, :] = v_s[rows_total:rows_total + ATTN_BLOCK, :]

    cosf = tab_ref[:, 0:LANES]
    sinr = tab_ref[:, LANES:2 * LANES]

    def rope(t):
        return t * cosf + pltpu.roll(t, ROPE_PAIR_SHIFT, 1) * sinr

    for h in range(ATTN_HEADS):
        cs = slice(h * HEAD_DIM, (h + 1) * HEAD_DIM)
        q_s[:, cs] = rope(qkv[:, h * HEAD_DIM:(h + 1) * HEAD_DIM]).astype(BF16)
        k_s[ATTN_BLOCK:, cs] = rope(qkv[:, ATTN_COLS + h * HEAD_DIM:ATTN_COLS + (h + 1) * HEAD_DIM]).astype(BF16)
    v_s[ATTN_BLOCK:, :] = qkv[:, 2 * ATTN_COLS:3 * ATTN_COLS].astype(BF16)

    row = lax.broadcasted_iota(I32, (ATTN_BLOCK, ATTN_BLOCK), 0)
    col = lax.broadcasted_iota(I32, (ATTN_BLOCK, ATTN_BLOCK), 1)
    row2 = lax.broadcasted_iota(I32, (ATTN_BLOCK, 2 * ATTN_BLOCK), 0)
    col2 = lax.broadcasted_iota(I32, (ATTN_BLOCK, 2 * ATTN_BLOCK), 1)
    lane = lax.broadcasted_iota(I32, (ATTN_BLOCK, LANES), 1)
    mask_cur = col <= row
    mask_both = jnp.logical_and(col2 >= row2, col2 <= row2 + N_BACK)
    if carry:
        mask_first = jnp.logical_and(mask_both, jnp.logical_or(col2 >= ATTN_BLOCK, step > 0))
    scale = HEAD_DIM ** -0.5

    for c in range(n_cls):
        for bi in range(blocks_per_class):
            j = c * blocks_per_class + bi
            ors = slice(bi * ATTN_BLOCK, (bi + 1) * ATTN_BLOCK)
            with_prev = bi > 0 or carry
            if with_prev:
                krows = slice(j * ATTN_BLOCK, (j + 2) * ATTN_BLOCK)
                mask = mask_first if bi == 0 else mask_both
            else:
                krows = slice((j + 1) * ATTN_BLOCK, (j + 2) * ATTN_BLOCK)
                mask = mask_cur
            lse_tile = jnp.zeros((ATTN_BLOCK, LANES), F32)
            for h0 in range(0, ATTN_HEADS, ATTN_HEAD_SET):
                heads = range(h0, h0 + ATTN_HEAD_SET)
                cols = [slice(h * HEAD_DIM, (h + 1) * HEAD_DIM) for h in heads]
                scores = [jnp.where(mask, _dot_nt(q_s[j * ATTN_BLOCK:(j + 1) * ATTN_BLOCK, cs], k_s[krows, cs])
                                    * scale, NEG) for cs in cols]
                maxes = [jnp.max(s, axis=1, keepdims=True) for s in scores]
                probs = [jnp.exp(s - m) for s, m in zip(scores, maxes)]
                dens = [jnp.sum(p, axis=1, keepdims=True) for p in probs]
                accs = [_dot(p.astype(BF16), v_s[krows, cs]) for p, cs in zip(probs, cols)]
                for h, acc, den, m in zip(heads, accs, dens, maxes):
                    o_ref[0, ors, c * ATTN_COLS + h * HEAD_DIM:c * ATTN_COLS + (h + 1) * HEAD_DIM] = (
                        acc / den).astype(o_ref.dtype)
                    lse_tile = jnp.where(lane == h, m + jnp.log(den), lse_tile)
            lse_ref[0, ors, c * LANES:(c + 1) * LANES] = lse_tile


def _class_major_perm(dil):
    width = ATTN_STEP_ROWS // dil
    perm = np.zeros((ATTN_STEP_ROWS, ATTN_STEP_ROWS), np.float32)
    for r in range(dil):
        for m in range(width):
            perm[r * width + m, m * dil + r] = 1.0
    return perm


def _class_major_kernel(x_ref, *refs):
    n = len(DILATIONS) - 1
    xb = x_ref[0].astype(BF16)
    for p_ref, out_ref, dil in zip(refs[:n], refs[n:], DILATIONS[1:]):
        width = ATTN_STEP_ROWS // dil
        rows = _dot(p_ref[...], xb).astype(BF16)
        for r in range(dil):
            out_ref[0, :, r * D_MODEL:(r + 1) * D_MODEL] = rows[r * width:(r + 1) * width, :]


def _class_major_views(x3):
    batch = x3.shape[0]
    perms = [jnp.asarray(_class_major_perm(d), BF16) for d in DILATIONS[1:]]
    return pl.pallas_call(
        _class_major_kernel,
        grid=(batch, SEQ // ATTN_STEP_ROWS),
        in_specs=[pl.BlockSpec((1, ATTN_STEP_ROWS, D_MODEL), lambda b, s: (b, s, 0))] + [
            pl.BlockSpec((ATTN_STEP_ROWS, ATTN_STEP_ROWS), lambda b, s: (0, 0)) for _ in perms],
        out_specs=[pl.BlockSpec((1, ATTN_STEP_ROWS // d, d * D_MODEL), lambda b, s: (b, s, 0)) for d in DILATIONS[1:]],
        out_shape=[jax.ShapeDtypeStruct((batch, SEQ // d, d * D_MODEL), BF16) for d in DILATIONS[1:]],
        compiler_params=pltpu.CompilerParams(dimension_semantics=("parallel", "parallel"), vmem_limit_bytes=40 * MIB),
        name="class_major_views",
    )(x3, *perms)


def _attn_group(xg, w_qkv_bf16, tab, group, dil):
    batch = xg.shape[0]
    per_class = SEQ // dil
    lc = min(ATTN_STEP_ROWS, per_class)
    n_cls = ATTN_STEP_ROWS // lc
    steps = SEQ // ATTN_STEP_ROWS
    carry = dil == 1
    if dil == 1:
        imap = lambda b, s: (b, s, 0)
    else:
        imap = lambda b, s: (b, 0, s)
    kern = functools.partial(_attn_group_kernel, n_cls=n_cls, lc=lc, carry=carry)
    o, lse = pl.pallas_call(
        kern,
        grid=(batch, steps),
        in_specs=[
            pl.BlockSpec((1, lc, n_cls * D_MODEL), imap),
            pl.BlockSpec((D_MODEL, 3 * ATTN_COLS), lambda b, s: (0, group), pipeline_mode=pl.Buffered(1)),
            pl.BlockSpec((ATTN_STEP_ROWS, 2 * LANES), lambda b, s: (s, 0)),
        ],
        out_specs=[
            pl.BlockSpec((1, lc, n_cls * ATTN_COLS), imap),
            pl.BlockSpec((1, lc, n_cls * LANES), imap),
        ],
        out_shape=[
            jax.ShapeDtypeStruct((batch, per_class, dil * ATTN_COLS), BF16),
            jax.ShapeDtypeStruct((batch, per_class, dil * LANES), F32),
        ],
        scratch_shapes=[
            pltpu.VMEM((ATTN_STEP_ROWS, ATTN_COLS), BF16),
            pltpu.VMEM((ATTN_BLOCK + ATTN_STEP_ROWS, ATTN_COLS), BF16),
            pltpu.VMEM((ATTN_BLOCK + ATTN_STEP_ROWS, ATTN_COLS), BF16),
        ],
        compiler_params=pltpu.CompilerParams(
            dimension_semantics=("parallel", "arbitrary"), vmem_limit_bytes=56 * MIB),
        name=f"attn_group{group}",
    )(xg, w_qkv_bf16, tab)
    return o, lse


def _rope_permute_qk(w_qkv):
    half = ROT_DIM // 2
    src = np.arange(HEAD_DIM)
    src[half:ROT_DIM] = np.arange(ROPE_PAIR_SHIFT, ROPE_PAIR_SHIFT + half)
    src[ROPE_PAIR_SHIFT:ROPE_PAIR_SHIFT + half] = np.arange(half, ROT_DIM)
    swap = np.zeros((HEAD_DIM, HEAD_DIM), np.float32)
    swap[src, np.arange(HEAD_DIM)] = 1.0
    per_part = jnp.asarray(np.stack([swap, swap, np.eye(HEAD_DIM, dtype=np.float32)]), w_qkv.dtype)
    w = w_qkv.reshape(D_MODEL, len(DILATIONS), 3, ATTN_HEADS, HEAD_DIM)
    return jnp.einsum('dgthi,tij->dgthj', w, per_part).reshape(w_qkv.shape)


def _rope_table(dil):
    inv_freq = ROPE_THETA ** (-np.arange(0, ROT_DIM, 2, dtype=np.float64) / ROT_DIM)
    ang = np.arange(SEQ, dtype=np.float64)[:, None] * inv_freq[None, :]
    ang = np.concatenate([ang, ang], -1)
    cos, sin = np.cos(ang), np.sin(ang)
    half = ROT_DIM // 2
    gap = ROPE_PAIR_SHIFT - half
    tail = LANES - ROPE_PAIR_SHIFT - half
    cosf = np.concatenate([cos[:, :half], np.ones((SEQ, gap)), cos[:, half:], np.ones((SEQ, tail))], 1)
    sinr = np.concatenate([-sin[:, :half], np.zeros((SEQ, gap)), sin[:, half:], np.zeros((SEQ, tail))], 1)
    tab = np.concatenate([cosf, sinr], 1)
    tab = tab.reshape(SEQ // dil, dil, 2 * LANES).transpose(1, 0, 2).reshape(SEQ, 2 * LANES)
    return jnp.asarray(tab.astype(np.float32))


def _attn_out_kernel(o0_ref, o1_ref, o2_ref, l0_ref, l1_ref, l2_ref, pt1_ref, pt2_ref, x_ref, w_ref, g_ref, b_ref,
                     wr_ref, br_ref, out_ref, mi_ref, mf_ref, cnt_ref, l1_s, l2_s, carry_s):
    o_nat = [None]
    for o_ref, l_ref, pt_ref, l_s, dil in ((o1_ref, l1_ref, pt1_ref, l1_s, DILATIONS[1]),
                                          (o2_ref, l2_ref, pt2_ref, l2_s, DILATIONS[2])):
        width = ATTN_STEP_ROWS // dil
        o_cm = jnp.concatenate([o_ref[0, :, r * ATTN_COLS:(r + 1) * ATTN_COLS] for r in range(dil)], axis=0)
        o_nat.append(_dot(pt_ref[...], o_cm))
        for r in range(dil):
            l_s[pl.ds(r, width, stride=dil), :] = l_ref[0, :, r * LANES:(r + 1) * LANES]
    ls = [l0_ref[0], l1_s[...], l2_s[...]]
    mx = jnp.maximum(jnp.maximum(ls[0], ls[1]), ls[2])
    es = [jnp.exp(l - mx) for l in ls]
    den = es[0] + es[1] + es[2]
    ws = [e / den for e in es]
    parts = []
    for h in range(ATTN_HEADS):
        cs = slice(h * HEAD_DIM, (h + 1) * HEAD_DIM)
        acc = ws[0][:, h:h + 1] * o0_ref[0, :, cs].astype(F32)
        acc = acc + ws[1][:, h:h + 1] * o_nat[1][:, cs]
        acc = acc + ws[2][:, h:h + 1] * o_nat[2][:, cs]
        parts.append(acc)
    mixed_in = jnp.concatenate(parts, axis=1).astype(BF16)
    y = ALPHA * x_ref[...] + _dot(mixed_in, w_ref[...])
    out = _layer_norm_rows(y, g_ref[...], b_ref[...])
    out_ref[...] = out
    _route_tile(out, wr_ref, br_ref, mi_ref, mf_ref, cnt_ref, carry_s)


def _attn_out(outs, lses, x2, w_bf16, g, b, wr_pad, br_pad):
    t = x2.shape[0]
    steps = SEQ // ATTN_STEP_ROWS
    r_in, r_out, r_shape, r_scratch = _route_specs(t, ATTN_STEP_ROWS)
    view_map = lambda i: (i // steps, i % steps, 0)
    view_spec = lambda dil, width: pl.BlockSpec((1, ATTN_STEP_ROWS // dil, dil * width), view_map)
    const_spec = lambda shape: pl.BlockSpec(shape, lambda i: (0, 0))
    row_spec = pl.BlockSpec((ATTN_STEP_ROWS, D_MODEL), lambda i: (i, 0))
    perms_t = [jnp.asarray(_class_major_perm(d).T, BF16) for d in DILATIONS[1:]]
    return pl.pallas_call(
        _attn_out_kernel,
        grid=(t // ATTN_STEP_ROWS,),
        in_specs=[view_spec(d, ATTN_COLS) for d in DILATIONS] + [view_spec(d, LANES) for d in DILATIONS] + [
            const_spec((ATTN_STEP_ROWS, ATTN_STEP_ROWS)) for _ in perms_t] + [
            row_spec, pl.BlockSpec((ATTN_COLS, D_MODEL), lambda i: (0, 0), pipeline_mode=pl.Buffered(1)),
            const_spec((1, D_MODEL)), const_spec((1, D_MODEL))] + r_in,
        out_specs=[row_spec] + r_out,
        out_shape=[jax.ShapeDtypeStruct((t, D_MODEL), F32)] + r_shape,
        scratch_shapes=[pltpu.VMEM((ATTN_STEP_ROWS, LANES), F32), pltpu.VMEM((ATTN_STEP_ROWS, LANES), F32)] + r_scratch,
        compiler_params=pltpu.CompilerParams(dimension_semantics=("arbitrary",), vmem_limit_bytes=48 * MIB),
        name="attn_out",
    )(*outs, *lses, *perms_t, x2, w_bf16, g.reshape(1, D_MODEL), b.reshape(1, D_MODEL), wr_pad, br_pad)


def _mix_out_kernel(y_ref, x_ref, w_ref, g_ref, b_ref, wr_ref, br_ref, out_ref, mi_ref, mf_ref, cnt_ref, carry_s):
    y = ALPHA * x_ref[...] + _dot(y_ref[...], w_ref[...])
    out = _layer_norm_rows(y, g_ref[...], b_ref[...])
    out_ref[...] = out
    _route_tile(out, wr_ref, br_ref, mi_ref, mf_ref, cnt_ref, carry_s)


def _mix_out(mixer_y, x2, w_bf16, g, b, wr_pad, br_pad):
    t = x2.shape[0]
    k = w_bf16.shape[0]
    row_spec = lambda width: pl.BlockSpec((ROW_TILE, width), lambda i: (i, 0))
    const_spec = lambda shape: pl.BlockSpec(shape, lambda i: (0, 0))
    r_in, r_out, r_shape, r_scratch = _route_specs(t, ROW_TILE)
    return pl.pallas_call(
        _mix_out_kernel,
        grid=(t // ROW_TILE,),
        in_specs=[row_spec(k), row_spec(D_MODEL),
                  pl.BlockSpec((k, D_MODEL), lambda i: (0, 0), pipeline_mode=pl.Buffered(1)),
                  const_spec((1, D_MODEL)), const_spec((1, D_MODEL))] + r_in,
        out_specs=[row_spec(D_MODEL)] + r_out,
        out_shape=[jax.ShapeDtypeStruct((t, D_MODEL), F32)] + r_shape,
        scratch_shapes=r_scratch,
        compiler_params=pltpu.CompilerParams(dimension_semantics=("arbitrary",), vmem_limit_bytes=48 * MIB),
        name="mix_out",
    )(mixer_y, x2, w_bf16, g.reshape(1, D_MODEL), b.reshape(1, D_MODEL), wr_pad, br_pad)


def _route_tile(x, wr_ref, br_ref, mi_ref, mf_ref, cnt_ref, carry_s):
    tm = x.shape[0]

    @pl.when(pl.program_id(0) == 0)
    def _():
        carry_s[...] = jnp.zeros_like(carry_s)

    x_hi = x.astype(BF16)
    x_lo = (x - x_hi.astype(F32)).astype(BF16)
    hi_terms = _dot(x_hi, wr_ref[...])
    logits = (hi_terms[:, :LANES] + _dot(x_lo, wr_ref[:, :LANES])) + hi_terms[:, LANES:] + br_ref[...]
    lt = logits.T
    l = [lt[e:e + 1, :] for e in range(N_EXPERTS)]
    mx = l[0]
    for e in range(1, N_EXPERTS):
        mx = jnp.maximum(mx, l[e])
    ex = [jnp.exp(v - mx) for v in l]
    tot = ex[0]
    for e in range(1, N_EXPERTS):
        tot = tot + ex[e]
    p = [v / tot for v in ex]

    def first_index_of(vals, target):
        idx = jnp.full_like(target, len(vals) - 1).astype(I32)
        for k in range(len(vals) - 2, -1, -1):
            idx = jnp.where(vals[k] == target, k, idx)
        return idx

    best = None
    for g in range(N_EXPERT_GROUPS):
        pg = p[g * EXPERTS_PER_GROUP:(g + 1) * EXPERTS_PER_GROUP]
        top1 = jnp.maximum(jnp.maximum(pg[0], pg[1]), jnp.maximum(pg[2], pg[3]))
        i1 = first_index_of(pg, top1)
        rest = [jnp.where(i1 == k, -1.0, pg[k]) for k in range(EXPERTS_PER_GROUP)]
        top2 = jnp.maximum(jnp.maximum(rest[0], rest[1]), jnp.maximum(rest[2], rest[3]))
        i2 = first_index_of(rest, top2)
        score = top1 + top2
        if best is None:
            best = (score, jnp.zeros_like(i1), top1, top2, i1, i2)
        else:
            better = score > best[0]
            cand = (score, jnp.full_like(i1, g), top1, top2, i1, i2)
            best = tuple(jnp.where(better, cv, bv) for cv, bv in zip(cand, best))
    _, g_sel, p1, p2, i1, i2 = best
    psum = p1 + p2
    gate1, gate2 = p1 / psum, p2 / psum
    first_low = i1 < i2
    lo = jnp.where(first_low, i1, i2)
    hi = jnp.where(first_low, i2, i1)
    gate_lo = jnp.where(first_low, gate1, gate2)
    gate_hi = jnp.where(first_low, gate2, gate1)
    pair = jnp.where(lo == 0, 0, jnp.where(lo == 1, 3, 5)) + hi - lo - 1
    cls = g_sel * len(PAIRS) + pair

    n_rows = carry_s.shape[0]
    sub = lax.broadcasted_iota(I32, (n_rows, tm), 0)
    onehot = sub == cls
    oh = jnp.where(onehot, 1.0, 0.0)
    upper = (lax.broadcasted_iota(I32, (tm, tm), 0) <= lax.broadcasted_iota(I32, (tm, tm), 1))
    cum = _dot(oh.astype(BF16), jnp.where(upper, 1.0, 0.0).astype(BF16))
    carry = carry_s[:, 0:1]
    rank = jnp.sum(jnp.where(onehot, cum - 1.0 + carry, 0.0), axis=0, keepdims=True)
    carry_new = carry + jnp.sum(oh, axis=1, keepdims=True)
    carry_s[...] = jnp.broadcast_to(carry_new, carry_s.shape)
    cnt_ref[...] = jnp.broadcast_to(carry_new, cnt_ref.shape)

    sub8 = lax.broadcasted_iota(I32, (8, tm), 0)
    mi_ref[...] = jnp.where(sub8 == 0, cls, jnp.where(sub8 == 1, rank.astype(I32), 0))
    mf_ref[...] = jnp.where(sub8 == 0, gate_lo, jnp.where(sub8 == 1, gate_hi, 0.0))


def _route_specs(t, tm):
    in_specs = [pl.BlockSpec((D_MODEL, 2 * LANES), lambda i: (0, 0)), pl.BlockSpec((1, LANES), lambda i: (0, 0))]
    out_specs = [pl.BlockSpec((8, tm), lambda i: (0, i)), pl.BlockSpec((8, tm), lambda i: (0, i)),
                 pl.BlockSpec((32, LANES), lambda i: (0, 0))]
    out_shape = [jax.ShapeDtypeStruct((8, t), I32), jax.ShapeDtypeStruct((8, t), F32),
                 jax.ShapeDtypeStruct((32, LANES), F32)]
    return in_specs, out_specs, out_shape, [pltpu.VMEM((32, LANES), F32)]


def _expert_kernel(elo_ref, ehi_ref, valid_ref, src0_ref, src1_ref, src_ahead_ref, x_hbm, gs_ref,
                   wg0, wu0, wd0, wg1, wu1, wd1, out_ref, rows_s, sems):
    del elo_ref, ehi_ref
    i = pl.program_id(0)
    n_slots = MOE_GATHER_AHEAD + 1
    slot = i % n_slots

    def start_row(idx_ref, t, to_slot, priority):
        pltpu.make_async_copy(x_hbm.at[pl.ds(idx_ref[0, 0, t], 1)], rows_s.at[to_slot, pl.ds(t, 1)],
                              sems.at[to_slot]).start(priority=priority)

    def wait_rows(of_slot):
        pltpu.make_async_copy(x_hbm.at[pl.ds(0, MOE_BLOCK)], rows_s.at[of_slot], sems.at[of_slot]).wait()

    def prime(idx_ref, to_slot):
        def issue(t, c):
            start_row(idx_ref, t, to_slot, 0)
            return c

        lax.fori_loop(0, MOE_BLOCK, issue, 0, unroll=8)

    @pl.when(i == 0)
    def _():
        prime(src0_ref, 0)

    @pl.when(jnp.logical_and(i == 0, valid_ref[1] > 0))
    def _():
        prime(src1_ref, 1)

    def used_block(cur):
        wait_rows(cur)
        xb = rows_s[cur].astype(BF16)

        def ffn(wg, wu, wd):
            hg = _dot(xb, wg[0, 0])
            hu = _dot(xb, wu[0, 0])
            hidden = (hg * jax.nn.sigmoid(hg)) * hu
            return _dot(hidden.astype(BF16), wd[0, 0])

        gs = gs_ref[...]
        out_ref[...] = gs[:, 0:1] * ffn(wg0, wu0, wd0) + gs[:, 1:2] * ffn(wg1, wu1, wd1)

        @pl.when(valid_ref[i + MOE_GATHER_AHEAD] > 0)
        def _():
            for t in range(MOE_BLOCK):
                start_row(src_ahead_ref, t, (cur + MOE_GATHER_AHEAD) % n_slots, priority=t % N_DMA_QUEUES)

    for cur in range(n_slots):
        pl.when(jnp.logical_and(valid_ref[i] > 0, slot == cur))(functools.partial(used_block, cur))

    @pl.when(valid_ref[i] == 0)
    def _():
        out_ref[...] = jnp.zeros_like(out_ref)


def _experts(x2, src_rows, gates_sorted, blk_elo, blk_ehi, blk_valid, layer, wg, wu, wd):
    assert MOE_GATHER_AHEAD == 2
    nb = src_rows.shape[0] - MOE_GATHER_AHEAD
    n_rows = nb * MOE_BLOCK
    lo_map = lambda b, elo, ehi, valid: (layer, elo[b], 0, 0)
    hi_map = lambda b, elo, ehi, valid: (layer, ehi[b], 0, 0)
    row_map = lambda b, elo, ehi, valid: (b, 0)
    up_shape = (1, 1, D_MODEL, D_EXPERT)
    down_shape = (1, 1, D_EXPERT, D_MODEL)
    grid_spec = pltpu.PrefetchScalarGridSpec(
        num_scalar_prefetch=3,
        grid=(nb,),
        in_specs=[
            pl.BlockSpec((1, 1, MOE_BLOCK), lambda b, elo, ehi, valid: (0, 0, 0), memory_space=pltpu.SMEM),
            pl.BlockSpec((1, 1, MOE_BLOCK), lambda b, elo, ehi, valid: (1, 0, 0), memory_space=pltpu.SMEM),
            pl.BlockSpec((1, 1, MOE_BLOCK), lambda b, elo, ehi, valid: (b + MOE_GATHER_AHEAD, 0, 0),
                         memory_space=pltpu.SMEM),
            pl.BlockSpec(memory_space=pl.ANY),
            pl.BlockSpec((MOE_BLOCK, LANES), row_map),
            pl.BlockSpec(up_shape, lo_map), pl.BlockSpec(up_shape, lo_map), pl.BlockSpec(down_shape, lo_map),
            pl.BlockSpec(up_shape, hi_map), pl.BlockSpec(up_shape, hi_map), pl.BlockSpec(down_shape, hi_map),
        ],
        out_specs=pl.BlockSpec((MOE_BLOCK, D_MODEL), row_map),
        scratch_shapes=[pltpu.VMEM((MOE_GATHER_AHEAD + 1, MOE_BLOCK, D_MODEL), F32),
                        pltpu.SemaphoreType.DMA((MOE_GATHER_AHEAD + 1,))],
    )
    return pl.pallas_call(
        _expert_kernel,
        grid_spec=grid_spec,
        out_shape=jax.ShapeDtypeStruct((n_rows, D_MODEL), F32),
        compiler_params=pltpu.CompilerParams(dimension_semantics=("arbitrary",), vmem_limit_bytes=58 * MIB),
        name="moe_experts",
    )(blk_elo, blk_ehi, blk_valid, src_rows, src_rows, src_rows, x2, gates_sorted, wg, wu, wd, wg, wu, wd)


def _gather_ln_kernel(dest_ref, dest_next_ref, x_ref, src_hbm, g_ref, b_ref, y_ref, rows_s, sems, *, tm):
    i = pl.program_id(0)
    slot = i % 2

    def issue_rows(idx_ref, to_slot):
        for t in range(tm):
            pltpu.make_async_copy(src_hbm.at[pl.ds(idx_ref[0, 0, t], 1)], rows_s.at[to_slot, pl.ds(t, 1)],
                                  sems.at[to_slot]).start(priority=t % N_DMA_QUEUES)

    @pl.when(i == 0)
    def _():
        issue_rows(dest_ref, slot)

    @pl.when(i + 1 < pl.num_programs(0))
    def _():
        issue_rows(dest_next_ref, 1 - slot)

    pltpu.make_async_copy(src_hbm.at[pl.ds(0, tm)], rows_s.at[slot], sems.at[slot]).wait()
    y = ALPHA * x_ref[...] + rows_s[slot]
    y_ref[...] = _layer_norm_rows(y, g_ref[...], b_ref[...])


def _gather_ln(x2, expert_out, dest, g, b):
    t = x2.shape[0]
    tm = GATHER_TM
    n_tiles = t // tm
    dest3 = dest.reshape(n_tiles, 1, tm)
    return pl.pallas_call(
        functools.partial(_gather_ln_kernel, tm=tm),
        grid=(n_tiles,),
        in_specs=[
            pl.BlockSpec((1, 1, tm), lambda i: (i, 0, 0), memory_space=pltpu.SMEM),
            pl.BlockSpec((1, 1, tm), lambda i: (jnp.minimum(i + 1, n_tiles - 1), 0, 0), memory_space=pltpu.SMEM),
            pl.BlockSpec((tm, D_MODEL), lambda i: (i, 0)),
            pl.BlockSpec(memory_space=pl.ANY),
            pl.BlockSpec((1, D_MODEL), lambda i: (0, 0)),
            pl.BlockSpec((1, D_MODEL), lambda i: (0, 0)),
        ],
        out_specs=pl.BlockSpec((tm, D_MODEL), lambda i: (i, 0)),
        out_shape=jax.ShapeDtypeStruct((t, D_MODEL), F32),
        scratch_shapes=[pltpu.VMEM((2, tm, D_MODEL), F32), pltpu.SemaphoreType.DMA((2,))],
        compiler_params=pltpu.CompilerParams(dimension_semantics=("arbitrary",), vmem_limit_bytes=32 * MIB),
        name="moe_gather_ln",
    )(dest3, dest3, x2, expert_out, g.reshape(1, D_MODEL), b.reshape(1, D_MODEL))


_CLASS_LO = np.array([4 * (c // 6) + PAIRS[c % 6][0] for c in range(N_CLASSES)], np.int32)
_CLASS_HI = np.array([4 * (c // 6) + PAIRS[c % 6][1] for c in range(N_CLASSES)], np.int32)


def _moe_layer(x2, routing, layer, wg, wu, wd, ln_g, ln_b):
    t = x2.shape[0]
    nb = -(-(t + N_CLASSES * (MOE_BLOCK - 1)) // MOE_BLOCK)
    n_rows = nb * MOE_BLOCK
    meta_i, meta_f, cnt = routing
    cls, rank = meta_i[0], meta_i[1]
    counts = cnt[:N_CLASSES, 0].astype(I32)
    padded = (counts + MOE_BLOCK - 1) // MOE_BLOCK * MOE_BLOCK
    ends = jnp.cumsum(padded)
    starts = ends - padded
    dest = starts[cls] + rank
    blk_start = jnp.arange(nb, dtype=I32) * MOE_BLOCK
    nb_ext = nb + MOE_GATHER_AHEAD
    blk_valid = (jnp.arange(nb_ext, dtype=I32) * MOE_BLOCK < ends[-1]).astype(I32)
    n_valid = jnp.sum(blk_valid)
    blk_cls = jnp.minimum(jnp.sum((ends[None, :] <= blk_start[:, None]).astype(I32), axis=1), N_CLASSES - 1)
    blk_cls = blk_cls[jnp.minimum(jnp.arange(nb), n_valid - 1)]
    blk_elo = jnp.asarray(_CLASS_LO)[blk_cls]
    blk_ehi = jnp.asarray(_CLASS_HI)[blk_cls]
    tok_meta = jnp.concatenate([meta_f[:2].T, jnp.arange(t, dtype=F32)[:, None], jnp.zeros((t, LANES - 3), F32)], 1)
    sorted_meta = jnp.zeros((nb_ext * MOE_BLOCK, LANES), F32).at[dest].set(tok_meta)
    src_rows = sorted_meta[:, 2].astype(I32).reshape(nb_ext, 1, MOE_BLOCK)
    out = _experts(x2, src_rows, sorted_meta, blk_elo, blk_ehi, blk_valid, layer, wg, wu, wd)
    return _gather_ln(x2, out, dest, ln_g, ln_b)


def _inproj_kernel(x_ref, w_ref, wgate_ref, z_ref, gates_ref, xb_s):
    @pl.when(pl.program_id(1) == 0)
    def _():
        xb_s[...] = x_ref[...].astype(BF16)
        gates_ref[...] = _dot(xb_s[...], wgate_ref[...])

    z_ref[...] = _dot(xb_s[...], w_ref[...]).astype(z_ref.dtype)


def _inproj(x2, w_in_bf16, w_gate_bf16):
    t = x2.shape[0]
    tm, tn = INPROJ_TM, INPROJ_TN
    n_main = MLSTM_MAIN_COLS
    return pl.pallas_call(
        _inproj_kernel,
        grid=(t // tm, n_main // tn),
        in_specs=[
            pl.BlockSpec((tm, D_MODEL), lambda m, n: (m, 0)),
            pl.BlockSpec((D_MODEL, tn), lambda m, n: (0, n)),
            pl.BlockSpec((D_MODEL, LANES), lambda m, n: (0, 0)),
        ],
        out_specs=[
            pl.BlockSpec((tm, tn), lambda m, n: (m, n)),
            pl.BlockSpec((tm, LANES), lambda m, n: (m, 0)),
        ],
        out_shape=[
            jax.ShapeDtypeStruct((t, n_main), BF16),
            jax.ShapeDtypeStruct((t, LANES), F32),
        ],
        scratch_shapes=[pltpu.VMEM((tm, D_MODEL), BF16)],
        compiler_params=pltpu.CompilerParams(
            dimension_semantics=("parallel", "arbitrary"), vmem_limit_bytes=48 * MIB),
        name="mlstm_inproj",
    )(x2, w_in_bf16, w_gate_bf16)


def _mlstm_cell_kernel(zq_ref, zk_ref, v_ref, op_ref, gt_ref, bg_ref, cwq_ref, cbq_ref, cwk_ref, cbk_ref,
                       ng_ref, y_ref, q_s, k_s, gb_s, bc_s, gbt_s, bct_s, *, chunk, n_heads):
    hgroup = pl.program_id(1)
    seq = zq_ref.shape[0]

    def conv_silu(z_ref, cw_ref, cb_ref):
        z = z_ref[...].astype(F32)
        rowi = lax.broadcasted_iota(I32, z.shape, 0)
        out = cb_ref[...] + cw_ref[0:1, :] * jnp.where(rowi >= CONV_K - 1, pltpu.roll(z, CONV_K - 1, 0), 0.0)
        for j in range(1, CONV_K - 1):
            shift = CONV_K - 1 - j
            out = out + cw_ref[j:j + 1, :] * jnp.where(rowi >= shift, pltpu.roll(z, shift, 0), 0.0)
        out = out + cw_ref[CONV_K - 1:CONV_K, :] * z
        return out * jax.nn.sigmoid(out)

    q_s[...] = conv_silu(zq_ref, cwq_ref, cbq_ref).astype(BF16)
    k_s[...] = conv_silu(zk_ref, cwk_ref, cbk_ref) * (QK_DIM ** -0.5)

    lane = lax.broadcasted_iota(I32, (chunk, LANES), 1)
    sub = lax.broadcasted_iota(I32, (MLSTM_HEADS, chunk), 0)
    ti = lax.broadcasted_iota(I32, (chunk, chunk), 0)
    si = lax.broadcasted_iota(I32, (chunk, chunk), 1)
    causal = ti >= si
    ones_cols = jnp.ones((chunk, LANES), BF16)

    @pl.when(hgroup == 0)
    def _():
        tri = jnp.where(causal, 1.0, 0.0)
        bias = bg_ref[...]
        for c in range(seq // chunk):
            rs = slice(c * chunk, (c + 1) * chunk)
            gb = gt_ref[rs, :] + bias
            log_f = -(jnp.maximum(-gb, 0.0) + jnp.log1p(jnp.exp(-jnp.abs(gb))))
            bcum = _dot(tri, log_f, precision=HIGHEST)
            gb_s[rs, :] = gb
            bc_s[rs, :] = bcum
            gbt_s[:, rs] = gb.T
            bct_s[:, rs] = bcum.T

    def pick_col(a, idx):
        return jnp.sum(jnp.where(lane == idx, a, 0.0), axis=1, keepdims=True)

    def pick_row(a, idx):
        return jnp.sum(jnp.where(sub == idx, a, 0.0), axis=0, keepdims=True)

    heads = [hgroup * n_heads + i for i in range(n_heads)]
    qcols = [slice(i * QK_DIM, (i + 1) * QK_DIM) for i in range(n_heads)]
    vcols = [slice(i * V_DIM, (i + 1) * V_DIM) for i in range(n_heads)]
    hr = range(n_heads)
    c_state = [jnp.zeros((QK_DIM, V_DIM + LANES), F32) for _ in hr]
    m_state = [jnp.full((1, 1), NEG, F32) for _ in hr]
    for c in range(seq // chunk):
        rs = slice(c * chunk, (c + 1) * chunk)
        gb_c, bc_c = gb_s[rs, :], bc_s[rs, :]
        gbt_c, bct_c = gbt_s[0:MLSTM_HEADS, rs], bct_s[MLSTM_HEADS:2 * MLSTM_HEADS, rs]
        li_col = [pick_col(gb_c, hd) for hd in heads]
        bc_col = [pick_col(bc_c, hd + MLSTM_HEADS) for hd in heads]
        li_row = [pick_row(gbt_c, hd) for hd in heads]
        bc_row = [pick_row(bct_c, hd) for hd in heads]
        dmat = [jnp.where(causal, bc_col[i] + (li_row[i] - bc_row[i]), NEG) for i in hr]
        inter = [bc_col[i] + m_state[i] for i in hr]
        m_t = [jnp.maximum(inter[i], jnp.max(dmat[i], axis=1, keepdims=True)) for i in hr]
        qc = [q_s[rs, qcols[i]] for i in hr]
        kf = [k_s[rs, qcols[i]] for i in hr]
        vc = [v_ref[rs, vcols[i]] for i in hr]
        a = [_dot_nt(qc[i], kf[i].astype(BF16)) * jnp.exp(dmat[i] - m_t[i]) for i in hr]
        w_inter = [jnp.exp(inter[i] - m_t[i]) for i in hr]
        q_state = [_dot(qc[i], c_state[i].astype(BF16)) for i in hr]
        num = [_dot(a[i].astype(BF16), vc[i]) + w_inter[i] * q_state[i][:, :V_DIM] for i in hr]
        den = [jnp.sum(a[i], axis=1, keepdims=True) + w_inter[i] * q_state[i][:, V_DIM:V_DIM + 1] for i in hr]
        h_out = [num[i] / jnp.maximum(jnp.abs(den[i]), jnp.exp(-m_t[i])) for i in hr]
        mu = [jnp.mean(h_out[i], axis=1, keepdims=True) for i in hr]
        hc = [h_out[i] - mu[i] for i in hr]
        var = [jnp.mean(hc[i] * hc[i], axis=1, keepdims=True) for i in hr]
        for i in hr:
            hn = hc[i] * lax.rsqrt(var[i] + LN_EPS) * ng_ref[:, vcols[i]]
            y_ref[rs, vcols[i]] = (hn * jax.nn.sigmoid(op_ref[rs, vcols[i]].astype(F32))).astype(y_ref.dtype)
        b_last = [bc_col[i][chunk - 1:chunk, :] for i in hr]
        g = [b_last[i] - bc_col[i] + li_col[i] for i in hr]
        m_new = [jnp.maximum(b_last[i] + m_state[i], jnp.max(g[i], axis=0, keepdims=True)) for i in hr]
        wk = [jnp.exp(g[i] - m_new[i]) for i in hr]
        decay = [jnp.exp(b_last[i] + m_state[i] - m_new[i]) for i in hr]
        kw_t = [(wk[i] * kf[i]).T.astype(BF16) for i in hr]
        upd = [_dot(kw_t[i], jnp.concatenate([vc[i], ones_cols], axis=1)) for i in hr]
        c_state = [decay[i] * c_state[i] + upd[i] for i in hr]
        m_state = m_new


def _mlstm_cell(z, gates, b_gates_pad, conv_w, conv_b, norm_g, batch):
    t = z.shape[0]
    nh = MLSTM_HEADS_PER_STEP
    groups = MLSTM_HEADS // nh
    qw, vw = nh * QK_DIM, nh * V_DIM
    k_blk0 = (MLSTM_QK_COLS // 2) // qw
    v_blk0 = MLSTM_QK_COLS // vw
    o_blk0 = (MLSTM_QK_COLS + MLSTM_V_COLS) // vw
    return pl.pallas_call(
        functools.partial(_mlstm_cell_kernel, chunk=MLSTM_CHUNK, n_heads=nh),
        grid=(batch, groups),
        in_specs=[
            pl.BlockSpec((SEQ, qw), lambda b, h: (b, h)),
            pl.BlockSpec((SEQ, qw), lambda b, h: (b, k_blk0 + h)),
            pl.BlockSpec((SEQ, vw), lambda b, h: (b, v_blk0 + h)),
            pl.BlockSpec((SEQ, vw), lambda b, h: (b, o_blk0 + h)),
            pl.BlockSpec((SEQ, LANES), lambda b, h: (b, 0)),
            pl.BlockSpec((1, LANES), lambda b, h: (0, 0)),
            pl.BlockSpec((CONV_K, qw), lambda b, h: (0, h)),
            pl.BlockSpec((1, qw), lambda b, h: (0, h)),
            pl.BlockSpec((CONV_K, qw), lambda b, h: (0, k_blk0 + h)),
            pl.BlockSpec((1, qw), lambda b, h: (0, k_blk0 + h)),
            pl.BlockSpec((1, vw), lambda b, h: (0, h)),
        ],
        out_specs=pl.BlockSpec((SEQ, vw), lambda b, h: (b, h)),
        out_shape=jax.ShapeDtypeStruct((t, MLSTM_V_COLS), BF16),
        scratch_shapes=[pltpu.VMEM((SEQ, qw), BF16), pltpu.VMEM((SEQ, qw), F32),
                        pltpu.VMEM((SEQ, LANES), F32), pltpu.VMEM((SEQ, LANES), F32),
                        pltpu.VMEM((LANES, SEQ), F32), pltpu.VMEM((LANES, SEQ), F32)],
        compiler_params=pltpu.CompilerParams(
            dimension_semantics=("parallel", "arbitrary"), vmem_limit_bytes=56 * MIB),
        name="mlstm_cell",
    )(z, z, z, z, gates, b_gates_pad, conv_w, conv_b.reshape(1, -1), conv_w, conv_b.reshape(1, -1),
      norm_g.reshape(1, -1))


def kernel(x, attn_w_qkv, attn_w_o, mlstm_w_in, mlstm_b_gates, mlstm_conv_w, mlstm_conv_b, mlstm_norm_g,
           mlstm_w_out, ln_mix_g, ln_mix_b, ln_ffn_g, ln_ffn_b, router_w, router_b, moe_w_gate, moe_w_up,
           moe_w_down):
    batch, seq, d = x.shape
    assert (seq, d) == (SEQ, D_MODEL)
    t = batch * seq
    x2 = x.reshape(t, d)

    wr_f32 = jnp.zeros((D_MODEL, LANES), F32).at[:, :N_EXPERTS].set(router_w)
    wr_hi = lax.bitcast_convert_type(lax.bitcast_convert_type(wr_f32, jnp.uint32) & jnp.uint32(0xFFFF0000), F32)
    wr_pad = jnp.concatenate([wr_hi.astype(BF16), (wr_f32 - wr_hi).astype(BF16)], axis=1)
    br_pad = jnp.zeros((1, LANES), F32).at[0, :N_EXPERTS].set(router_b)
    wg_bf, wu_bf, wd_bf = moe_w_gate.astype(BF16), moe_w_up.astype(BF16), moe_w_down.astype(BF16)

    w_qkv = _rope_permute_qk(attn_w_qkv[0].astype(BF16))
    views = [x] + list(_class_major_views(x))
    outs, lses = [], []
    for group, dil in enumerate(DILATIONS):
        o, lse = _attn_group(views[group], w_qkv, _rope_table(dil), group, dil)
        outs.append(o)
        lses.append(lse)
    x2, *routing = _attn_out(outs, lses, x2, attn_w_o[0].astype(BF16), ln_mix_g[0], ln_mix_b[0], wr_pad, br_pad)
    x2 = _moe_layer(x2, routing, 0, wg_bf, wu_bf, wd_bf, ln_ffn_g[0], ln_ffn_b[0])

    w_in = mlstm_w_in[0]
    w_main = w_in.astype(BF16)
    w_gate = jnp.zeros((D_MODEL, LANES), F32).at[:, :2 * MLSTM_HEADS].set(w_in[:, MLSTM_MAIN_COLS:]).astype(BF16)
    bg_pad = jnp.zeros((1, LANES), F32).at[0, :2 * MLSTM_HEADS].set(mlstm_b_gates[0])
    z, gates = _inproj(x2, w_main, w_gate)
    y = _mlstm_cell(z, gates, bg_pad, mlstm_conv_w[0], mlstm_conv_b[0], mlstm_norm_g[0], batch)
    x2, *routing = _mix_out(y, x2, mlstm_w_out[0].astype(BF16), ln_mix_g[1], ln_mix_b[1], wr_pad, br_pad)
    x2 = _moe_layer(x2, routing, 1, wg_bf, wu_bf, wd_bf, ln_ffn_g[1], ln_ffn_b[1])
    return x2.reshape(batch, seq, d)
```

```python
import functools

import numpy as np
import jax
import jax.numpy as jnp
from jax import lax
from jax.experimental import pallas as pl
from jax.experimental.pallas import tpu as pltpu

F32 = jnp.float32
BF16 = jnp.bfloat16
I32 = jnp.int32
HIGHEST = lax.Precision.HIGHEST

D_MODEL = 2048
SEQ = 2048
DEPTH = 2
DILATIONS = (1, 4, 16)
N_BACK = 128
ATTN_HEADS = 8
HEAD_DIM = 128
ROT_DIM = HEAD_DIM // 4
ROPE_THETA = 500000.0
ATTN_BLOCK = 128
ATTN_COLS = ATTN_HEADS * HEAD_DIM

MLSTM_HEADS = 8
QK_DIM = 128
V_DIM = D_MODEL // MLSTM_HEADS
CONV_K = 4
MLSTM_QK_COLS = 2 * MLSTM_HEADS * QK_DIM
MLSTM_V_COLS = MLSTM_HEADS * V_DIM
MLSTM_MAIN_COLS = MLSTM_QK_COLS + 2 * MLSTM_V_COLS

N_EXPERTS = 16
N_EXPERT_GROUPS = 4
EXPERTS_PER_GROUP = 4
D_EXPERT = 768
PAIRS = ((0, 1), (0, 2), (0, 3), (1, 2), (1, 3), (2, 3))
N_CLASSES = N_EXPERT_GROUPS * len(PAIRS)

ALPHA = (2 * DEPTH) ** 0.25
LN_EPS = 1e-5
NEG = -1e30

LANES = 128
N_DMA_QUEUES = 2
MIB = 1024 * 1024

ATTN_STEP_ROWS = 512
ATTN_HEAD_SET = 8
ROPE_PAIR_SHIFT = 64
ROW_TILE = 512
INPROJ_TM = 1024
INPROJ_TN = 1024
MLSTM_CHUNK = 128
MLSTM_HEADS_PER_STEP = 4
MOE_BLOCK = 256
MOE_GATHER_AHEAD = 2
GATHER_TM = 256


def _dot(a, b, **kw):
    return jnp.dot(a, b, preferred_element_type=F32, **kw)


def _dot_nt(a, b):
    return lax.dot_general(a, b, (((1,), (1,)), ((), ())), preferred_element_type=F32)


def _layer_norm_rows(y, g, b):
    mu = jnp.mean(y, axis=-1, keepdims=True)
    yc = y - mu
    var = jnp.mean(yc * yc, axis=-1, keepdims=True)
    return yc * lax.rsqrt(var + LN_EPS) * g + b


def _attn_group_kernel(x_ref, w_ref, tab_ref, cast_in_ref, o_ref, lse_ref, cast_out_ref, q_s, k_s, v_s,
                       *, n_cls, lc, carry):
    cast_out_ref[...] = cast_in_ref[...].astype(BF16)
    step = pl.program_id(1)
    rows_total = n_cls * lc
    blocks_per_class = lc // ATTN_BLOCK

    if n_cls == 1:
        xs = x_ref[0]
    else:
        xs = jnp.concatenate([x_ref[0, :, c * D_MODEL:(c + 1) * D_MODEL] for c in range(n_cls)], axis=0)
    qkv = _dot(xs.astype(BF16), w_ref[...])

    if carry:
        @pl.when(step == 0)
        def _():
            k_s[0:ATTN_BLOCK, :] = jnp.zeros((ATTN_BLOCK, ATTN_COLS), BF16)
            v_s[0:ATTN_BLOCK, :] = jnp.zeros((ATTN_BLOCK, ATTN_COLS), BF16)

        @pl.when(step > 0)
        def _():
            k_s[0:ATTN_BLOCK, :] = k_s[rows_total:rows_total + ATTN_BLOCK, :]
            v_s[0:ATTN_BLOCK, :] = v_s[rows_total:rows_total + ATTN_BLOCK, :]

    cosf = tab_ref[:, 0:LANES]
    sinr = tab_ref[:, LANES:2 * LANES]

    def rope(t):
        return t * cosf + pltpu.roll(t, ROPE_PAIR_SHIFT, 1) * sinr

    for h in range(ATTN_HEADS):
        cs = slice(h * HEAD_DIM, (h + 1) * HEAD_DIM)
        q_s[:, cs] = rope(qkv[:, h * HEAD_DIM:(h + 1) * HEAD_DIM]).astype(BF16)
        k_s[ATTN_BLOCK:, cs] = rope(qkv[:, ATTN_COLS + h * HEAD_DIM:ATTN_COLS + (h + 1) * HEAD_DIM]).astype(BF16)
    v_s[ATTN_BLOCK:, :] = qkv[:, 2 * ATTN_COLS:3 * ATTN_COLS].astype(BF16)

    row = lax.broadcasted_iota(I32, (ATTN_BLOCK, ATTN_BLOCK), 0)
    col = lax.broadcasted_iota(I32, (ATTN_BLOCK, ATTN_BLOCK), 1)
    row2 = lax.broadcasted_iota(I32, (ATTN_BLOCK, 2 * ATTN_BLOCK), 0)
    col2 = lax.broadcasted_iota(I32, (ATTN_BLOCK, 2 * ATTN_BLOCK), 1)
    lane = lax.broadcasted_iota(I32, (ATTN_BLOCK, LANES), 1)
    mask_cur = col <= row
    mask_both = jnp.logical_and(col2 >= row2, col2 <= row2 + N_BACK)
    if carry:
        mask_first = jnp.logical_and(mask_both, jnp.logical_or(col2 >= ATTN_BLOCK, step > 0))
    scale = HEAD_DIM ** -0.5

    for c in range(n_cls):
        for bi in range(blocks_per_class):
            j = c * blocks_per_class + bi
            ors = slice(bi * ATTN_BLOCK, (bi + 1) * ATTN_BLOCK)
            with_prev = bi > 0 or carry
            if with_prev:
                krows = slice(j * ATTN_BLOCK, (j + 2) * ATTN_BLOCK)
                mask = mask_first if bi == 0 else mask_both
            else:
                krows = slice((j + 1) * ATTN_BLOCK, (j + 2) * ATTN_BLOCK)
                mask = mask_cur
            lse_tile = jnp.zeros((ATTN_BLOCK, LANES), F32)
            for h0 in range(0, ATTN_HEADS, ATTN_HEAD_SET):
                heads = range(h0, h0 + ATTN_HEAD_SET)
                cols = [slice(h * HEAD_DIM, (h + 1) * HEAD_DIM) for h in heads]
                scores = [jnp.where(mask, _dot_nt(q_s[j * ATTN_BLOCK:(j + 1) * ATTN_BLOCK, cs], k_s[krows, cs])
                                    * scale, NEG) for cs in cols]
                maxes = [jnp.max(s, axis=1, keepdims=True) for s in scores]
                probs = [jnp.exp(s - m) for s, m in zip(scores, maxes)]
                dens = [jnp.sum(p, axis=1, keepdims=True) for p in probs]
                accs = [_dot(p.astype(BF16), v_s[krows, cs]) for p, cs in zip(probs, cols)]
                for h, acc, den, m in zip(heads, accs, dens, maxes):
                    o_ref[0, ors, c * ATTN_COLS + h * HEAD_DIM:c * ATTN_COLS + (h + 1) * HEAD_DIM] = (
                        acc / den).astype(o_ref.dtype)
                    lse_tile = jnp.where(lane == h, m + jnp.log(den), lse_tile)
            lse_ref[0, ors, c * LANES:(c + 1) * LANES] = lse_tile


def _class_major_perm(dil):
    width = ATTN_STEP_ROWS // dil
    perm = np.zeros((ATTN_STEP_ROWS, ATTN_STEP_ROWS), np.float32)
    for r in range(dil):
        for m in range(width):
            perm[r * width + m, m * dil + r] = 1.0
    return perm


def _class_major_kernel(x_ref, *refs):
    n = len(DILATIONS) - 1
    xb = x_ref[0].astype(BF16)
    for p_ref, out_ref, dil in zip(refs[:n], refs[n:], DILATIONS[1:]):
        width = ATTN_STEP_ROWS // dil
        rows = _dot(p_ref[...], xb).astype(BF16)
        for r in range(dil):
            out_ref[0, :, r * D_MODEL:(r + 1) * D_MODEL] = rows[r * width:(r + 1) * width, :]


def _class_major_views(x3):
    batch = x3.shape[0]
    perms = [jnp.asarray(_class_major_perm(d), BF16) for d in DILATIONS[1:]]
    return pl.pallas_call(
        _class_major_kernel,
        grid=(batch, SEQ // ATTN_STEP_ROWS),
        in_specs=[pl.BlockSpec((1, ATTN_STEP_ROWS, D_MODEL), lambda b, s: (b, s, 0))] + [
            pl.BlockSpec((ATTN_STEP_ROWS, ATTN_STEP_ROWS), lambda b, s: (0, 0)) for _ in perms],
        out_specs=[pl.BlockSpec((1, ATTN_STEP_ROWS // d, d * D_MODEL), lambda b, s: (b, s, 0)) for d in DILATIONS[1:]],
        out_shape=[jax.ShapeDtypeStruct((batch, SEQ // d, d * D_MODEL), BF16) for d in DILATIONS[1:]],
        compiler_params=pltpu.CompilerParams(dimension_semantics=("parallel", "parallel"), vmem_limit_bytes=40 * MIB),
        name="class_major_views",
    )(x3, *perms)


def _attn_group(xg, w_qkv_bf16, tab, group, dil, moe_w):
    batch = xg.shape[0]
    cast_cols = moe_w.shape[-1]
    moe_w2 = moe_w.reshape(-1, cast_cols)
    cast_rows = moe_w2.shape[0] // (batch * (SEQ // ATTN_STEP_ROWS))
    cast_spec = pl.BlockSpec((cast_rows, cast_cols), lambda b, s: (b * (SEQ // ATTN_STEP_ROWS) + s, 0))
    per_class = SEQ // dil
    lc = min(ATTN_STEP_ROWS, per_class)
    n_cls = ATTN_STEP_ROWS // lc
    steps = SEQ // ATTN_STEP_ROWS
    carry = dil == 1
    if dil == 1:
        imap = lambda b, s: (b, s, 0)
    else:
        imap = lambda b, s: (b, 0, s)
    kern = functools.partial(_attn_group_kernel, n_cls=n_cls, lc=lc, carry=carry)
    o, lse, moe_w_bf16 = pl.pallas_call(
        kern,
        grid=(batch, steps),
        in_specs=[
            pl.BlockSpec((1, lc, n_cls * D_MODEL), imap),
            pl.BlockSpec((D_MODEL, 3 * ATTN_COLS), lambda b, s: (0, group), pipeline_mode=pl.Buffered(1)),
            pl.BlockSpec((ATTN_STEP_ROWS, 2 * LANES), lambda b, s: (s, 0)),
            cast_spec,
        ],
        out_specs=[
            pl.BlockSpec((1, lc, n_cls * ATTN_COLS), imap),
            pl.BlockSpec((1, lc, n_cls * LANES), imap),
            cast_spec,
        ],
        out_shape=[
            jax.ShapeDtypeStruct((batch, per_class, dil * ATTN_COLS), BF16),
            jax.ShapeDtypeStruct((batch, per_class, dil * LANES), F32),
            jax.ShapeDtypeStruct(moe_w2.shape, BF16),
        ],
        scratch_shapes=[
            pltpu.VMEM((ATTN_STEP_ROWS, ATTN_COLS), BF16),
            pltpu.VMEM((ATTN_BLOCK + ATTN_STEP_ROWS, ATTN_COLS), BF16),
            pltpu.VMEM((ATTN_BLOCK + ATTN_STEP_ROWS, ATTN_COLS), BF16),
        ],
        compiler_params=pltpu.CompilerParams(
            dimension_semantics=("parallel", "arbitrary"), vmem_limit_bytes=56 * MIB),
        name=f"attn_group{group}",
    )(xg, w_qkv_bf16, tab, moe_w2)
    return o, lse, moe_w_bf16.reshape(moe_w.shape)


def _rope_permute_qk(w_qkv):
    half = ROT_DIM // 2
    src = np.arange(HEAD_DIM)
    src[half:ROT_DIM] = np.arange(ROPE_PAIR_SHIFT, ROPE_PAIR_SHIFT + half)
    src[ROPE_PAIR_SHIFT:ROPE_PAIR_SHIFT + half] = np.arange(half, ROT_DIM)
    swap = np.zeros((HEAD_DIM, HEAD_DIM), np.float32)
    swap[src, np.arange(HEAD_DIM)] = 1.0
    per_part = jnp.asarray(np.stack([swap, swap, np.eye(HEAD_DIM, dtype=np.float32)]), w_qkv.dtype)
    w = w_qkv.reshape(D_MODEL, len(DILATIONS), 3, ATTN_HEADS, HEAD_DIM)
    return jnp.einsum('dgthi,tij->dgthj', w, per_part).reshape(w_qkv.shape)


def _rope_table(dil):
    inv_freq = ROPE_THETA ** (-np.arange(0, ROT_DIM, 2, dtype=np.float64) / ROT_DIM)
    ang = np.arange(SEQ, dtype=np.float64)[:, None] * inv_freq[None, :]
    ang = np.concatenate([ang, ang], -1)
    cos, sin = np.cos(ang), np.sin(ang)
    half = ROT_DIM // 2
    gap = ROPE_PAIR_SHIFT - half
    tail = LANES - ROPE_PAIR_SHIFT - half
    cosf = np.concatenate([cos[:, :half], np.ones((SEQ, gap)), cos[:, half:], np.ones((SEQ, tail))], 1)
    sinr = np.concatenate([-sin[:, :half], np.zeros((SEQ, gap)), sin[:, half:], np.zeros((SEQ, tail))], 1)
    tab = np.concatenate([cosf, sinr], 1)
    tab = tab.reshape(SEQ // dil, dil, 2 * LANES).transpose(1, 0, 2).reshape(SEQ, 2 * LANES)
    return jnp.asarray(tab.astype(np.float32))


def _attn_out_kernel(o0_ref, o1_ref, o2_ref, l0_ref, l1_ref, l2_ref, pt1_ref, pt2_ref, x_ref, w_ref, g_ref, b_ref,
                     wr_ref, br_ref, out_ref, mi_ref, mf_ref, cnt_ref, l1_s, l2_s, carry_s):
    o_nat = [None]
    for o_ref, l_ref, pt_ref, l_s, dil in ((o1_ref, l1_ref, pt1_ref, l1_s, DILATIONS[1]),
                                          (o2_ref, l2_ref, pt2_ref, l2_s, DILATIONS[2])):
        width = ATTN_STEP_ROWS // dil
        o_cm = jnp.concatenate([o_ref[0, :, r * ATTN_COLS:(r + 1) * ATTN_COLS] for r in range(dil)], axis=0)
        o_nat.append(_dot(pt_ref[...], o_cm))
        for r in range(dil):
            l_s[pl.ds(r, width, stride=dil), :] = l_ref[0, :, r * LANES:(r + 1) * LANES]
    ls = [l0_ref[0], l1_s[...], l2_s[...]]
    mx = jnp.maximum(jnp.maximum(ls[0], ls[1]), ls[2])
    es = [jnp.exp(l - mx) for l in ls]
    den = es[0] + es[1] + es[2]
    ws = [e / den for e in es]
    parts = []
    for h in range(ATTN_HEADS):
        cs = slice(h * HEAD_DIM, (h + 1) * HEAD_DIM)
        acc = ws[0][:, h:h + 1] * o0_ref[0, :, cs].astype(F32)
        acc = acc + ws[1][:, h:h + 1] * o_nat[1][:, cs]
        acc = acc + ws[2][:, h:h + 1] * o_nat[2][:, cs]
        parts.append(acc)
    mixed_in = jnp.concatenate(parts, axis=1).astype(BF16)
    y = ALPHA * x_ref[...] + _dot(mixed_in, w_ref[...])
    out = _layer_norm_rows(y, g_ref[...], b_ref[...])
    out_ref[...] = out
    _route_tile(out, wr_ref, br_ref, mi_ref, mf_ref, cnt_ref, carry_s)


def _attn_out(outs, lses, x2, w_bf16, g, b, wr_pad, br_pad):
    t = x2.shape[0]
    steps = SEQ // ATTN_STEP_ROWS
    r_in, r_out, r_shape, r_scratch = _route_specs(t, ATTN_STEP_ROWS)
    view_map = lambda i: (i // steps, i % steps, 0)
    view_spec = lambda dil, width: pl.BlockSpec((1, ATTN_STEP_ROWS // dil, dil * width), view_map)
    const_spec = lambda shape: pl.BlockSpec(shape, lambda i: (0, 0))
    row_spec = pl.BlockSpec((ATTN_STEP_ROWS, D_MODEL), lambda i: (i, 0))
    perms_t = [jnp.asarray(_class_major_perm(d).T, BF16) for d in DILATIONS[1:]]
    return pl.pallas_call(
        _attn_out_kernel,
        grid=(t // ATTN_STEP_ROWS,),
        in_specs=[view_spec(d, ATTN_COLS) for d in DILATIONS] + [view_spec(d, LANES) for d in DILATIONS] + [
            const_spec((ATTN_STEP_ROWS, ATTN_STEP_ROWS)) for _ in perms_t] + [
            row_spec, pl.BlockSpec((ATTN_COLS, D_MODEL), lambda i: (0, 0), pipeline_mode=pl.Buffered(1)),
            const_spec((1, D_MODEL)), const_spec((1, D_MODEL))] + r_in,
        out_specs=[row_spec] + r_out,
        out_shape=[jax.ShapeDtypeStruct((t, D_MODEL), F32)] + r_shape,
        scratch_shapes=[pltpu.VMEM((ATTN_STEP_ROWS, LANES), F32), pltpu.VMEM((ATTN_STEP_ROWS, LANES), F32)] + r_scratch,
        compiler_params=pltpu.CompilerParams(dimension_semantics=("arbitrary",), vmem_limit_bytes=48 * MIB),
        name="attn_out",
    )(*outs, *lses, *perms_t, x2, w_bf16, g.reshape(1, D_MODEL), b.reshape(1, D_MODEL), wr_pad, br_pad)


def _mix_out_kernel(y_ref, x_ref, w_ref, g_ref, b_ref, wr_ref, br_ref, out_ref, mi_ref, mf_ref, cnt_ref, carry_s):
    y = ALPHA * x_ref[...] + _dot(y_ref[...], w_ref[...])
    out = _layer_norm_rows(y, g_ref[...], b_ref[...])
    out_ref[...] = out
    _route_tile(out, wr_ref, br_ref, mi_ref, mf_ref, cnt_ref, carry_s)


def _mix_out(mixer_y, x2, w_bf16, g, b, wr_pad, br_pad):
    t = x2.shape[0]
    k = w_bf16.shape[0]
    row_spec = lambda width: pl.BlockSpec((ROW_TILE, width), lambda i: (i, 0))
    const_spec = lambda shape: pl.BlockSpec(shape, lambda i: (0, 0))
    r_in, r_out, r_shape, r_scratch = _route_specs(t, ROW_TILE)
    return pl.pallas_call(
        _mix_out_kernel,
        grid=(t // ROW_TILE,),
        in_specs=[row_spec(k), row_spec(D_MODEL),
                  pl.BlockSpec((k, D_MODEL), lambda i: (0, 0), pipeline_mode=pl.Buffered(1)),
                  const_spec((1, D_MODEL)), const_spec((1, D_MODEL))] + r_in,
        out_specs=[row_spec(D_MODEL)] + r_out,
        out_shape=[jax.ShapeDtypeStruct((t, D_MODEL), F32)] + r_shape,
        scratch_shapes=r_scratch,
        compiler_params=pltpu.CompilerParams(dimension_semantics=("arbitrary",), vmem_limit_bytes=48 * MIB),
        name="mix_out",
    )(mixer_y, x2, w_bf16, g.reshape(1, D_MODEL), b.reshape(1, D_MODEL), wr_pad, br_pad)


def _route_tile(x, wr_ref, br_ref, mi_ref, mf_ref, cnt_ref, carry_s):
    tm = x.shape[0]

    @pl.when(pl.program_id(0) == 0)
    def _():
        carry_s[...] = jnp.zeros_like(carry_s)

    x_hi = x.astype(BF16)
    x_lo = (x - x_hi.astype(F32)).astype(BF16)
    hi_terms = _dot(x_hi, wr_ref[...])
    logits = (hi_terms[:, :LANES] + _dot(x_lo, wr_ref[:, :LANES])) + hi_terms[:, LANES:] + br_ref[...]
    lt = logits.T
    l = [lt[e:e + 1, :] for e in range(N_EXPERTS)]
    mx = l[0]
    for e in range(1, N_EXPERTS):
        mx = jnp.maximum(mx, l[e])
    ex = [jnp.exp(v - mx) for v in l]
    tot = ex[0]
    for e in range(1, N_EXPERTS):
        tot = tot + ex[e]
    p = [v / tot for v in ex]

    def first_index_of(vals, target):
        idx = jnp.full_like(target, len(vals) - 1).astype(I32)
        for k in range(len(vals) - 2, -1, -1):
            idx = jnp.where(vals[k] == target, k, idx)
        return idx

    best = None
    for g in range(N_EXPERT_GROUPS):
        pg = p[g * EXPERTS_PER_GROUP:(g + 1) * EXPERTS_PER_GROUP]
        top1 = jnp.maximum(jnp.maximum(pg[0], pg[1]), jnp.maximum(pg[2], pg[3]))
        i1 = first_index_of(pg, top1)
        rest = [jnp.where(i1 == k, -1.0, pg[k]) for k in range(EXPERTS_PER_GROUP)]
        top2 = jnp.maximum(jnp.maximum(rest[0], rest[1]), jnp.maximum(rest[2], rest[3]))
        i2 = first_index_of(rest, top2)
        score = top1 + top2
        if best is None:
            best = (score, jnp.zeros_like(i1), top1, top2, i1, i2)
        else:
            better = score > best[0]
            cand = (score, jnp.full_like(i1, g), top1, top2, i1, i2)
            best = tuple(jnp.where(better, cv, bv) for cv, bv in zip(cand, best))
    _, g_sel, p1, p2, i1, i2 = best
    psum = p1 + p2
    gate1, gate2 = p1 / psum, p2 / psum
    first_low = i1 < i2
    lo = jnp.where(first_low, i1, i2)
    hi = jnp.where(first_low, i2, i1)
    gate_lo = jnp.where(first_low, gate1, gate2)
    gate_hi = jnp.where(first_low, gate2, gate1)
    pair = jnp.where(lo == 0, 0, jnp.where(lo == 1, 3, 5)) + hi - lo - 1
    cls = g_sel * len(PAIRS) + pair

    n_rows = carry_s.shape[0]
    sub = lax.broadcasted_iota(I32, (n_rows, tm), 0)
    onehot = sub == cls
    oh = jnp.where(onehot, 1.0, 0.0)
    upper = (lax.broadcasted_iota(I32, (tm, tm), 0) <= lax.broadcasted_iota(I32, (tm, tm), 1))
    cum = _dot(oh.astype(BF16), jnp.where(upper, 1.0, 0.0).astype(BF16))
    carry = carry_s[:, 0:1]
    rank = jnp.sum(jnp.where(onehot, cum - 1.0 + carry, 0.0), axis=0, keepdims=True)
    carry_new = carry + jnp.sum(oh, axis=1, keepdims=True)
    carry_s[...] = jnp.broadcast_to(carry_new, carry_s.shape)
    cnt_ref[...] = jnp.broadcast_to(carry_new, cnt_ref.shape)

    sub8 = lax.broadcasted_iota(I32, (8, tm), 0)
    mi_ref[...] = jnp.where(sub8 == 0, cls, jnp.where(sub8 == 1, rank.astype(I32), 0))
    mf_ref[...] = jnp.where(sub8 == 0, gate_lo, jnp.where(sub8 == 1, gate_hi, 0.0))


def _route_specs(t, tm):
    in_specs = [pl.BlockSpec((D_MODEL, 2 * LANES), lambda i: (0, 0)), pl.BlockSpec((1, LANES), lambda i: (0, 0))]
    out_specs = [pl.BlockSpec((8, tm), lambda i: (0, i)), pl.BlockSpec((8, tm), lambda i: (0, i)),
                 pl.BlockSpec((32, LANES), lambda i: (0, 0))]
    out_shape = [jax.ShapeDtypeStruct((8, t), I32), jax.ShapeDtypeStruct((8, t), F32),
                 jax.ShapeDtypeStruct((32, LANES), F32)]
    return in_specs, out_specs, out_shape, [pltpu.VMEM((32, LANES), F32)]


def _expert_kernel(elo_ref, ehi_ref, valid_ref, src0_ref, src1_ref, src_ahead_ref, x_hbm, gs_ref,
                   wg0, wu0, wd0, wg1, wu1, wd1, out_ref, rows_s, sems):
    del elo_ref, ehi_ref
    i = pl.program_id(0)
    n_slots = MOE_GATHER_AHEAD + 1
    slot = i % n_slots

    def start_row(idx_ref, t, to_slot, priority):
        pltpu.make_async_copy(x_hbm.at[pl.ds(idx_ref[0, 0, t], 1)], rows_s.at[to_slot, pl.ds(t, 1)],
                              sems.at[to_slot]).start(priority=priority)

    def wait_rows(of_slot):
        pltpu.make_async_copy(x_hbm.at[pl.ds(0, MOE_BLOCK)], rows_s.at[of_slot], sems.at[of_slot]).wait()

    def prime(idx_ref, to_slot):
        def issue(t, c):
            start_row(idx_ref, t, to_slot, 0)
            return c

        lax.fori_loop(0, MOE_BLOCK, issue, 0, unroll=8)

    @pl.when(i == 0)
    def _():
        prime(src0_ref, 0)

    @pl.when(jnp.logical_and(i == 0, valid_ref[1] > 0))
    def _():
        prime(src1_ref, 1)

    def used_block(cur):
        wait_rows(cur)
        xb = rows_s[cur].astype(BF16)

        def ffn(wg, wu, wd):
            hg = _dot(xb, wg[0, 0])
            hu = _dot(xb, wu[0, 0])
            hidden = (hg * jax.nn.sigmoid(hg)) * hu
            return _dot(hidden.astype(BF16), wd[0, 0])

        gs = gs_ref[...]
        out_ref[...] = gs[:, 0:1] * ffn(wg0, wu0, wd0) + gs[:, 1:2] * ffn(wg1, wu1, wd1)

        @pl.when(valid_ref[i + MOE_GATHER_AHEAD] > 0)
        def _():
            for t in range(MOE_BLOCK):
                start_row(src_ahead_ref, t, (cur + MOE_GATHER_AHEAD) % n_slots, priority=t % N_DMA_QUEUES)

    for cur in range(n_slots):
        pl.when(jnp.logical_and(valid_ref[i] > 0, slot == cur))(functools.partial(used_block, cur))

    @pl.when(valid_ref[i] == 0)
    def _():
        out_ref[...] = jnp.zeros_like(out_ref)


def _experts(x2, src_rows, gates_sorted, blk_elo, blk_ehi, blk_valid, layer, wg, wu, wd):
    assert MOE_GATHER_AHEAD == 2
    nb = src_rows.shape[0] - MOE_GATHER_AHEAD
    n_rows = nb * MOE_BLOCK
    lo_map = lambda b, elo, ehi, valid: (layer, elo[b], 0, 0)
    hi_map = lambda b, elo, ehi, valid: (layer, ehi[b], 0, 0)
    row_map = lambda b, elo, ehi, valid: (b, 0)
    up_shape = (1, 1, D_MODEL, D_EXPERT)
    down_shape = (1, 1, D_EXPERT, D_MODEL)
    grid_spec = pltpu.PrefetchScalarGridSpec(
        num_scalar_prefetch=3,
        grid=(nb,),
        in_specs=[
            pl.BlockSpec((1, 1, MOE_BLOCK), lambda b, elo, ehi, valid: (0, 0, 0), memory_space=pltpu.SMEM),
            pl.BlockSpec((1, 1, MOE_BLOCK), lambda b, elo, ehi, valid: (1, 0, 0), memory_space=pltpu.SMEM),
            pl.BlockSpec((1, 1, MOE_BLOCK), lambda b, elo, ehi, valid: (b + MOE_GATHER_AHEAD, 0, 0),
                         memory_space=pltpu.SMEM),
            pl.BlockSpec(memory_space=pl.ANY),
            pl.BlockSpec((MOE_BLOCK, LANES), row_map),
            pl.BlockSpec(up_shape, lo_map), pl.BlockSpec(up_shape, lo_map), pl.BlockSpec(down_shape, lo_map),
            pl.BlockSpec(up_shape, hi_map), pl.BlockSpec(up_shape, hi_map), pl.BlockSpec(down_shape, hi_map),
        ],
        out_specs=pl.BlockSpec((MOE_BLOCK, D_MODEL), row_map),
        scratch_shapes=[pltpu.VMEM((MOE_GATHER_AHEAD + 1, MOE_BLOCK, D_MODEL), F32),
                        pltpu.SemaphoreType.DMA((MOE_GATHER_AHEAD + 1,))],
    )
    return pl.pallas_call(
        _expert_kernel,
        grid_spec=grid_spec,
        out_shape=jax.ShapeDtypeStruct((n_rows, D_MODEL), F32),
        compiler_params=pltpu.CompilerParams(dimension_semantics=("arbitrary",), vmem_limit_bytes=58 * MIB),
        name="moe_experts",
    )(blk_elo, blk_ehi, blk_valid, src_rows, src_rows, src_rows, x2, gates_sorted, wg, wu, wd, wg, wu, wd)


def _gather_ln_kernel(dest_ref, dest_next_ref, x_ref, src_hbm, g_ref, b_ref, y_ref, rows_s, sems, *, tm):
    i = pl.program_id(0)
    slot = i % 2

    def issue_rows(idx_ref, to_slot):
        for t in range(tm):
            pltpu.make_async_copy(src_hbm.at[pl.ds(idx_ref[0, 0, t], 1)], rows_s.at[to_slot, pl.ds(t, 1)],
                                  sems.at[to_slot]).start(priority=t % N_DMA_QUEUES)

    @pl.when(i == 0)
    def _():
        issue_rows(dest_ref, slot)

    @pl.when(i + 1 < pl.num_programs(0))
    def _():
        issue_rows(dest_next_ref, 1 - slot)

    pltpu.make_async_copy(src_hbm.at[pl.ds(0, tm)], rows_s.at[slot], sems.at[slot]).wait()
    y = ALPHA * x_ref[...] + rows_s[slot]
    y_ref[...] = _layer_norm_rows(y, g_ref[...], b_ref[...])


def _gather_ln(x2, expert_out, dest, g, b):
    t = x2.shape[0]
    tm = GATHER_TM
    n_tiles = t // tm
    dest3 = dest.reshape(n_tiles, 1, tm)
    return pl.pallas_call(
        functools.partial(_gather_ln_kernel, tm=tm),
        grid=(n_tiles,),
        in_specs=[
            pl.BlockSpec((1, 1, tm), lambda i: (i, 0, 0), memory_space=pltpu.SMEM),
            pl.BlockSpec((1, 1, tm), lambda i: (jnp.minimum(i + 1, n_tiles - 1), 0, 0), memory_space=pltpu.SMEM),
            pl.BlockSpec((tm, D_MODEL), lambda i: (i, 0)),
            pl.BlockSpec(memory_space=pl.ANY),
            pl.BlockSpec((1, D_MODEL), lambda i: (0, 0)),
            pl.BlockSpec((1, D_MODEL), lambda i: (0, 0)),
        ],
        out_specs=pl.BlockSpec((tm, D_MODEL), lambda i: (i, 0)),
        out_shape=jax.ShapeDtypeStruct((t, D_MODEL), F32),
        scratch_shapes=[pltpu.VMEM((2, tm, D_MODEL), F32), pltpu.SemaphoreType.DMA((2,))],
        compiler_params=pltpu.CompilerParams(dimension_semantics=("arbitrary",), vmem_limit_bytes=32 * MIB),
        name="moe_gather_ln",
    )(dest3, dest3, x2, expert_out, g.reshape(1, D_MODEL), b.reshape(1, D_MODEL))


_CLASS_LO = np.array([4 * (c // 6) + PAIRS[c % 6][0] for c in range(N_CLASSES)], np.int32)
_CLASS_HI = np.array([4 * (c // 6) + PAIRS[c % 6][1] for c in range(N_CLASSES)], np.int32)


def _moe_layer(x2, routing, layer, wg, wu, wd, ln_g, ln_b):
    t = x2.shape[0]
    nb = -(-(t + N_CLASSES * (MOE_BLOCK - 1)) // MOE_BLOCK)
    n_rows = nb * MOE_BLOCK
    meta_i, meta_f, cnt = routing
    cls, rank = meta_i[0], meta_i[1]
    counts = cnt[:N_CLASSES, 0].astype(I32)
    padded = (counts + MOE_BLOCK - 1) // MOE_BLOCK * MOE_BLOCK
    ends = jnp.cumsum(padded)
    starts = ends - padded
    dest = starts[cls] + rank
    blk_start = jnp.arange(nb, dtype=I32) * MOE_BLOCK
    nb_ext = nb + MOE_GATHER_AHEAD
    blk_valid = (jnp.arange(nb_ext, dtype=I32) * MOE_BLOCK < ends[-1]).astype(I32)
    n_valid = jnp.sum(blk_valid)
    blk_cls = jnp.minimum(jnp.sum((ends[None, :] <= blk_start[:, None]).astype(I32), axis=1), N_CLASSES - 1)
    blk_cls = blk_cls[jnp.minimum(jnp.arange(nb), n_valid - 1)]
    blk_elo = jnp.asarray(_CLASS_LO)[blk_cls]
    blk_ehi = jnp.asarray(_CLASS_HI)[blk_cls]
    tok_meta = jnp.concatenate([meta_f[:2].T, jnp.arange(t, dtype=F32)[:, None], jnp.zeros((t, LANES - 3), F32)], 1)
    sorted_meta = jnp.zeros((nb_ext * MOE_BLOCK, LANES), F32).at[dest].set(tok_meta)
    src_rows = sorted_meta[:, 2].astype(I32).reshape(nb_ext, 1, MOE_BLOCK)
    out = _experts(x2, src_rows, sorted_meta, blk_elo, blk_ehi, blk_valid, layer, wg, wu, wd)
    return _gather_ln(x2, out, dest, ln_g, ln_b)


def _inproj_kernel(x_ref, w_ref, wgate_ref, z_ref, gates_ref, xb_s):
    @pl.when(pl.program_id(1) == 0)
    def _():
        xb_s[...] = x_ref[...].astype(BF16)
        gates_ref[...] = _dot(xb_s[...], wgate_ref[...])

    z_ref[...] = _dot(xb_s[...], w_ref[...]).astype(z_ref.dtype)


def _inproj(x2, w_in_bf16, w_gate_bf16):
    t = x2.shape[0]
    tm, tn = INPROJ_TM, INPROJ_TN
    n_main = MLSTM_MAIN_COLS
    return pl.pallas_call(
        _inproj_kernel,
        grid=(t // tm, n_main // tn),
        in_specs=[
            pl.BlockSpec((tm, D_MODEL), lambda m, n: (m, 0)),
            pl.BlockSpec((D_MODEL, tn), lambda m, n: (0, n)),
            pl.BlockSpec((D_MODEL, LANES), lambda m, n: (0, 0)),
        ],
        out_specs=[
            pl.BlockSpec((tm, tn), lambda m, n: (m, n)),
            pl.BlockSpec((tm, LANES), lambda m, n: (m, 0)),
        ],
        out_shape=[
            jax.ShapeDtypeStruct((t, n_main), BF16),
            jax.ShapeDtypeStruct((t, LANES), F32),
        ],
        scratch_shapes=[pltpu.VMEM((tm, D_MODEL), BF16)],
        compiler_params=pltpu.CompilerParams(
            dimension_semantics=("parallel", "arbitrary"), vmem_limit_bytes=48 * MIB),
        name="mlstm_inproj",
    )(x2, w_in_bf16, w_gate_bf16)


def _mlstm_cell_kernel(zq_ref, zk_ref, v_ref, op_ref, gt_ref, bg_ref, cwq_ref, cbq_ref, cwk_ref, cbk_ref,
                       ng_ref, y_ref, q_s, k_s, gb_s, bc_s, gbt_s, bct_s, *, chunk, n_heads):
    hgroup = pl.program_id(1)
    seq = zq_ref.shape[0]

    def conv_silu(z_ref, cw_ref, cb_ref):
        z = z_ref[...].astype(F32)
        rowi = lax.broadcasted_iota(I32, z.shape, 0)
        out = cb_ref[...] + cw_ref[0:1, :] * jnp.where(rowi >= CONV_K - 1, pltpu.roll(z, CONV_K - 1, 0), 0.0)
        for j in range(1, CONV_K - 1):
            shift = CONV_K - 1 - j
            out = out + cw_ref[j:j + 1, :] * jnp.where(rowi >= shift, pltpu.roll(z, shift, 0), 0.0)
        out = out + cw_ref[CONV_K - 1:CONV_K, :] * z
        return out * jax.nn.sigmoid(out)

    q_s[...] = conv_silu(zq_ref, cwq_ref, cbq_ref).astype(BF16)
    k_s[...] = conv_silu(zk_ref, cwk_ref, cbk_ref) * (QK_DIM ** -0.5)

    lane = lax.broadcasted_iota(I32, (chunk, LANES), 1)
    sub = lax.broadcasted_iota(I32, (MLSTM_HEADS, chunk), 0)
    ti = lax.broadcasted_iota(I32, (chunk, chunk), 0)
    si = lax.broadcasted_iota(I32, (chunk, chunk), 1)
    causal = ti >= si
    ones_cols = jnp.ones((chunk, LANES), BF16)

    @pl.when(hgroup == 0)
    def _():
        tri = jnp.where(causal, 1.0, 0.0)
        bias = bg_ref[...]
        for c in range(seq // chunk):
            rs = slice(c * chunk, (c + 1) * chunk)
            gb = gt_ref[rs, :] + bias
            log_f = -(jnp.maximum(-gb, 0.0) + jnp.log1p(jnp.exp(-jnp.abs(gb))))
            bcum = _dot(tri, log_f, precision=HIGHEST)
            gb_s[rs, :] = gb
            bc_s[rs, :] = bcum
            gbt_s[:, rs] = gb.T
            bct_s[:, rs] = bcum.T

    def pick_col(a, idx):
        return jnp.sum(jnp.where(lane == idx, a, 0.0), axis=1, keepdims=True)

    def pick_row(a, idx):
        return jnp.sum(jnp.where(sub == idx, a, 0.0), axis=0, keepdims=True)

    heads = [hgroup * n_heads + i for i in range(n_heads)]
    qcols = [slice(i * QK_DIM, (i + 1) * QK_DIM) for i in range(n_heads)]
    vcols = [slice(i * V_DIM, (i + 1) * V_DIM) for i in range(n_heads)]
    hr = range(n_heads)
    c_state = [jnp.zeros((QK_DIM, V_DIM + LANES), F32) for _ in hr]
    m_state = [jnp.full((1, 1), NEG, F32) for _ in hr]
    for c in range(seq // chunk):
        rs = slice(c * chunk, (c + 1) * chunk)
        gb_c, bc_c = gb_s[rs, :], bc_s[rs, :]
        gbt_c, bct_c = gbt_s[0:MLSTM_HEADS, rs], bct_s[MLSTM_HEADS:2 * MLSTM_HEADS, rs]
        li_col = [pick_col(gb_c, hd) for hd in heads]
        bc_col = [pick_col(bc_c, hd + MLSTM_HEADS) for hd in heads]
        li_row = [pick_row(gbt_c, hd) for hd in heads]
        bc_row = [pick_row(bct_c, hd) for hd in heads]
        dmat = [jnp.where(causal, bc_col[i] + (li_row[i] - bc_row[i]), NEG) for i in hr]
        inter = [bc_col[i] + m_state[i] for i in hr]
        m_t = [jnp.maximum(inter[i], jnp.max(dmat[i], axis=1, keepdims=True)) for i in hr]
        qc = [q_s[rs, qcols[i]] for i in hr]
        kf = [k_s[rs, qcols[i]] for i in hr]
        vc = [v_ref[rs, vcols[i]] for i in hr]
        a = [_dot_nt(qc[i], kf[i].astype(BF16)) * jnp.exp(dmat[i] - m_t[i]) for i in hr]
        w_inter = [jnp.exp(inter[i] - m_t[i]) for i in hr]
        q_state = [_dot(qc[i], c_state[i].astype(BF16)) for i in hr]
        num = [_dot(a[i].astype(BF16), vc[i]) + w_inter[i] * q_state[i][:, :V_DIM] for i in hr]
        den = [jnp.sum(a[i], axis=1, keepdims=True) + w_inter[i] * q_state[i][:, V_DIM:V_DIM + 1] for i in hr]
        h_out = [num[i] / jnp.maximum(jnp.abs(den[i]), jnp.exp(-m_t[i])) for i in hr]
        mu = [jnp.mean(h_out[i], axis=1, keepdims=True) for i in hr]
        hc = [h_out[i] - mu[i] for i in hr]
        var = [jnp.mean(hc[i] * hc[i], axis=1, keepdims=True) for i in hr]
        for i in hr:
            hn = hc[i] * lax.rsqrt(var[i] + LN_EPS) * ng_ref[:, vcols[i]]
            y_ref[rs, vcols[i]] = (hn * jax.nn.sigmoid(op_ref[rs, vcols[i]].astype(F32))).astype(y_ref.dtype)
        b_last = [bc_col[i][chunk - 1:chunk, :] for i in hr]
        g = [b_last[i] - bc_col[i] + li_col[i] for i in hr]
        m_new = [jnp.maximum(b_last[i] + m_state[i], jnp.max(g[i], axis=0, keepdims=True)) for i in hr]
        wk = [jnp.exp(g[i] - m_new[i]) for i in hr]
        decay = [jnp.exp(b_last[i] + m_state[i] - m_new[i]) for i in hr]
        kw_t = [(wk[i] * kf[i]).T.astype(BF16) for i in hr]
        upd = [_dot(kw_t[i], jnp.concatenate([vc[i], ones_cols], axis=1)) for i in hr]
        c_state = [decay[i] * c_state[i] + upd[i] for i in hr]
        m_state = m_new


def _mlstm_cell(z, gates, b_gates_pad, conv_w, conv_b, norm_g, batch):
    t = z.shape[0]
    nh = MLSTM_HEADS_PER_STEP
    groups = MLSTM_HEADS // nh
    qw, vw = nh * QK_DIM, nh * V_DIM
    k_blk0 = (MLSTM_QK_COLS // 2) // qw
    v_blk0 = MLSTM_QK_COLS // vw
    o_blk0 = (MLSTM_QK_COLS + MLSTM_V_COLS) // vw
    return pl.pallas_call(
        functools.partial(_mlstm_cell_kernel, chunk=MLSTM_CHUNK, n_heads=nh),
        grid=(batch, groups),
        in_specs=[
            pl.BlockSpec((SEQ, qw), lambda b, h: (b, h)),
            pl.BlockSpec((SEQ, qw), lambda b, h: (b, k_blk0 + h)),
            pl.BlockSpec((SEQ, vw), lambda b, h: (b, v_blk0 + h)),
            pl.BlockSpec((SEQ, vw), lambda b, h: (b, o_blk0 + h)),
            pl.BlockSpec((SEQ, LANES), lambda b, h: (b, 0)),
            pl.BlockSpec((1, LANES), lambda b, h: (0, 0)),
            pl.BlockSpec((CONV_K, qw), lambda b, h: (0, h)),
            pl.BlockSpec((1, qw), lambda b, h: (0, h)),
            pl.BlockSpec((CONV_K, qw), lambda b, h: (0, k_blk0 + h)),
            pl.BlockSpec((1, qw), lambda b, h: (0, k_blk0 + h)),
            pl.BlockSpec((1, vw), lambda b, h: (0, h)),
        ],
        out_specs=pl.BlockSpec((SEQ, vw), lambda b, h: (b, h)),
        out_shape=jax.ShapeDtypeStruct((t, MLSTM_V_COLS), BF16),
        scratch_shapes=[pltpu.VMEM((SEQ, qw), BF16), pltpu.VMEM((SEQ, qw), F32),
                        pltpu.VMEM((SEQ, LANES), F32), pltpu.VMEM((SEQ, LANES), F32),
                        pltpu.VMEM((LANES, SEQ), F32), pltpu.VMEM((LANES, SEQ), F32)],
        compiler_params=pltpu.CompilerParams(
            dimension_semantics=("parallel", "arbitrary"), vmem_limit_bytes=56 * MIB),
        name="mlstm_cell",
    )(z, z, z, z, gates, b_gates_pad, conv_w, conv_b.reshape(1, -1), conv_w, conv_b.reshape(1, -1),
      norm_g.reshape(1, -1))


def kernel(x, attn_w_qkv, attn_w_o, mlstm_w_in, mlstm_b_gates, mlstm_conv_w, mlstm_conv_b, mlstm_norm_g,
           mlstm_w_out, ln_mix_g, ln_mix_b, ln_ffn_g, ln_ffn_b, router_w, router_b, moe_w_gate, moe_w_up,
           moe_w_down):
    batch, seq, d = x.shape
    assert (seq, d) == (SEQ, D_MODEL)
    t = batch * seq
    x2 = x.reshape(t, d)

    wr_f32 = jnp.zeros((D_MODEL, LANES), F32).at[:, :N_EXPERTS].set(router_w)
    wr_hi = lax.bitcast_convert_type(lax.bitcast_convert_type(wr_f32, jnp.uint32) & jnp.uint32(0xFFFF0000), F32)
    wr_pad = jnp.concatenate([wr_hi.astype(BF16), (wr_f32 - wr_hi).astype(BF16)], axis=1)
    br_pad = jnp.zeros((1, LANES), F32).at[0, :N_EXPERTS].set(router_b)

    w_qkv = _rope_permute_qk(attn_w_qkv[0].astype(BF16))
    views = [x] + list(_class_major_views(x))
    outs, lses, moe_bf = [], [], []
    for group, (dil, moe_w) in enumerate(zip(DILATIONS, (moe_w_gate, moe_w_up, moe_w_down))):
        o, lse, w_bf = _attn_group(views[group], w_qkv, _rope_table(dil), group, dil, moe_w)
        outs.append(o)
        lses.append(lse)
        moe_bf.append(w_bf)
    wg_bf, wu_bf, wd_bf = moe_bf
    x2, *routing = _attn_out(outs, lses, x2, attn_w_o[0].astype(BF16), ln_mix_g[0], ln_mix_b[0], wr_pad, br_pad)
    x2 = _moe_layer(x2, routing, 0, wg_bf, wu_bf, wd_bf, ln_ffn_g[0], ln_ffn_b[0])

    w_in = mlstm_w_in[0]
    w_main = w_in.astype(BF16)
    w_gate = jnp.zeros((D_MODEL, LANES), F32).at[:, :2 * MLSTM_HEADS].set(w_in[:, MLSTM_MAIN_COLS:]).astype(BF16)
    bg_pad = jnp.zeros((1, LANES), F32).at[0, :2 * MLSTM_HEADS].set(mlstm_b_gates[0])
    z, gates = _inproj(x2, w_main, w_gate)
    y = _mlstm_cell(z, gates, bg_pad, mlstm_conv_w[0], mlstm_conv_b[0], mlstm_norm_g[0], batch)
    x2, *routing = _mix_out(y, x2, mlstm_w_out[0].astype(BF16), ln_mix_g[1], ln_mix_b[1], wr_pad, br_pad)
    x2 = _moe_layer(x2, routing, 1, wg_bf, wu_bf, wd_bf, ln_ffn_g[1], ln_ffn_b[1])
    return x2.reshape(batch, seq, d)
```

```python
import functools

import numpy as np
import jax
import jax.numpy as jnp
from jax import lax
from jax.experimental import pallas as pl
from jax.experimental.pallas import tpu as pltpu

F32 = jnp.float32
BF16 = jnp.bfloat16
I32 = jnp.int32
HIGHEST = lax.Precision.HIGHEST

D_MODEL = 2048
SEQ = 2048
DEPTH = 2
DILATIONS = (1, 4, 16)
N_BACK = 128
ATTN_HEADS = 8
HEAD_DIM = 128
ROT_DIM = HEAD_DIM // 4
ROPE_THETA = 500000.0
ATTN_BLOCK = 128
ATTN_COLS = ATTN_HEADS * HEAD_DIM

MLSTM_HEADS = 8
QK_DIM = 128
V_DIM = D_MODEL // MLSTM_HEADS
CONV_K = 4
MLSTM_QK_COLS = 2 * MLSTM_HEADS * QK_DIM
MLSTM_V_COLS = MLSTM_HEADS * V_DIM
MLSTM_MAIN_COLS = MLSTM_QK_COLS + 2 * MLSTM_V_COLS

N_EXPERTS = 16
N_EXPERT_GROUPS = 4
EXPERTS_PER_GROUP = 4
D_EXPERT = 768
PAIRS = ((0, 1), (0, 2), (0, 3), (1, 2), (1, 3), (2, 3))
N_CLASSES = N_EXPERT_GROUPS * len(PAIRS)

ALPHA = (2 * DEPTH) ** 0.25
LN_EPS = 1e-5
NEG = -1e30

LANES = 128
N_DMA_QUEUES = 2
MIB = 1024 * 1024

ATTN_STEP_ROWS = 512
ATTN_HEAD_SET = 8
ROPE_PAIR_SHIFT = 64
ROW_TILE = 512
INPROJ_TM = 1024
INPROJ_TN = 1024
MLSTM_CHUNK = 128
MLSTM_HEADS_PER_STEP = 4
MOE_BLOCK = 256
MOE_GATHER_AHEAD = 2
GATHER_TM = 256


def _dot(a, b, **kw):
    return jnp.dot(a, b, preferred_element_type=F32, **kw)


def _dot_nt(a, b):
    return lax.dot_general(a, b, (((1,), (1,)), ((), ())), preferred_element_type=F32)


def _layer_norm_rows(y, g, b):
    mu = jnp.mean(y, axis=-1, keepdims=True)
    yc = y - mu
    var = jnp.mean(yc * yc, axis=-1, keepdims=True)
    return yc * lax.rsqrt(var + LN_EPS) * g + b


def _attn_group_kernel(x_ref, w_ref, tab_ref, cast_in_ref, o_ref, lse_ref, cast_out_ref, q_s, k_s, v_s,
                       *, n_cls, lc, carry):
    cast_out_ref[...] = cast_in_ref[...].astype(BF16)
    step = pl.program_id(1)
    rows_total = n_cls * lc
    blocks_per_class = lc // ATTN_BLOCK

    if n_cls == 1:
        xs = x_ref[0]
    else:
        xs = jnp.concatenate([x_ref[0, :, c * D_MODEL:(c + 1) * D_MODEL] for c in range(n_cls)], axis=0)
    qkv = _dot(xs.astype(BF16), w_ref[...])

    if carry:
        @pl.when(step == 0)
        def _():
            k_s[0:ATTN_BLOCK, :] = jnp.zeros((ATTN_BLOCK, ATTN_COLS), BF16)
            v_s[0:ATTN_BLOCK, :] = jnp.zeros((ATTN_BLOCK, ATTN_COLS), BF16)

        @pl.when(step > 0)
        def _():
            k_s[0:ATTN_BLOCK, :] = k_s[rows_total:rows_total + ATTN_BLOCK, :]
            v_s[0:ATTN_BLOCK, :] = v_s[rows_total:rows_total + ATTN_BLOCK, :]

    cosf = tab_ref[:, 0:LANES]
    sinr = tab_ref[:, LANES:2 * LANES]

    def rope(t):
        return t * cosf + pltpu.roll(t, ROPE_PAIR_SHIFT, 1) * sinr

    for h in range(ATTN_HEADS):
        cs = slice(h * HEAD_DIM, (h + 1) * HEAD_DIM)
        q_s[:, cs] = rope(qkv[:, h * HEAD_DIM:(h + 1) * HEAD_DIM]).astype(BF16)
        k_s[ATTN_BLOCK:, cs] = rope(qkv[:, ATTN_COLS + h * HEAD_DIM:ATTN_COLS + (h + 1) * HEAD_DIM]).astype(BF16)
    v_s[ATTN_BLOCK:, :] = qkv[:, 2 * ATTN_COLS:3 * ATTN_COLS].astype(BF16)

    row = lax.broadcasted_iota(I32, (ATTN_BLOCK, ATTN_BLOCK), 0)
    col = lax.broadcasted_iota(I32, (ATTN_BLOCK, ATTN_BLOCK), 1)
    row2 = lax.broadcasted_iota(I32, (ATTN_BLOCK, 2 * ATTN_BLOCK), 0)
    col2 = lax.broadcasted_iota(I32, (ATTN_BLOCK, 2 * ATTN_BLOCK), 1)
    lane = lax.broadcasted_iota(I32, (ATTN_BLOCK, LANES), 1)
    mask_cur = col <= row
    mask_both = jnp.logical_and(col2 >= row2, col2 <= row2 + N_BACK)
    if carry:
        mask_first = jnp.logical_and(mask_both, jnp.logical_or(col2 >= ATTN_BLOCK, step > 0))
    scale = HEAD_DIM ** -0.5

    for c in range(n_cls):
        for bi in range(blocks_per_class):
            j = c * blocks_per_class + bi
            ors = slice(bi * ATTN_BLOCK, (bi + 1) * ATTN_BLOCK)
            with_prev = bi > 0 or carry
            if with_prev:
                krows = slice(j * ATTN_BLOCK, (j + 2) * ATTN_BLOCK)
                mask = mask_first if bi == 0 else mask_both
            else:
                krows = slice((j + 1) * ATTN_BLOCK, (j + 2) * ATTN_BLOCK)
                mask = mask_cur
            lse_tile = jnp.zeros((ATTN_BLOCK, LANES), F32)
            for h0 in range(0, ATTN_HEADS, ATTN_HEAD_SET):
                heads = range(h0, h0 + ATTN_HEAD_SET)
                cols = [slice(h * HEAD_DIM, (h + 1) * HEAD_DIM) for h in heads]
                scores = [jnp.where(mask, _dot_nt(q_s[j * ATTN_BLOCK:(j + 1) * ATTN_BLOCK, cs], k_s[krows, cs])
                                    * scale, NEG) for cs in cols]
                maxes = [jnp.max(s, axis=1, keepdims=True) for s in scores]
                probs = [jnp.exp(s - m) for s, m in zip(scores, maxes)]
                dens = [jnp.sum(p, axis=1, keepdims=True) for p in probs]
                accs = [_dot(p.astype(BF16), v_s[krows, cs]) for p, cs in zip(probs, cols)]
                for h, acc, den, m in zip(heads, accs, dens, maxes):
                    o_ref[0, ors, c * ATTN_COLS + h * HEAD_DIM:c * ATTN_COLS + (h + 1) * HEAD_DIM] = (
                        acc / den).astype(o_ref.dtype)
                    lse_tile = jnp.where(lane == h, m + jnp.log(den), lse_tile)
            lse_ref[0, ors, c * LANES:(c + 1) * LANES] = lse_tile


def _class_major_perm(dil):
    width = ATTN_STEP_ROWS // dil
    perm = np.zeros((ATTN_STEP_ROWS, ATTN_STEP_ROWS), np.float32)
    for r in range(dil):
        for m in range(width):
            perm[r * width + m, m * dil + r] = 1.0
    return perm


def _class_major_kernel(x_ref, p1_ref, p2_ref, swap_ref, wqkv_ref, *refs):
    n_plain = (len(refs) - len(DILATIONS)) // 2
    plain_in = refs[:n_plain]
    view_out = refs[n_plain:n_plain + len(DILATIONS) - 1]
    wqkv_out = refs[n_plain + len(DILATIONS) - 1]
    plain_out = refs[n_plain + len(DILATIONS):]

    xb = x_ref[0].astype(BF16)
    for p_ref, out_ref, dil in zip((p1_ref, p2_ref), view_out, DILATIONS[1:]):
        width = ATTN_STEP_ROWS // dil
        rows = _dot(p_ref[...], xb).astype(BF16)
        for r in range(dil):
            out_ref[0, :, r * D_MODEL:(r + 1) * D_MODEL] = rows[r * width:(r + 1) * width, :]

    for src_ref, dst_ref in zip(plain_in, plain_out):
        dst_ref[...] = src_ref[...].astype(BF16)
    for head in range(len(DILATIONS) * 3 * ATTN_HEADS):
        cs = slice(head * HEAD_DIM, (head + 1) * HEAD_DIM)
        piece = wqkv_ref[:, cs].astype(BF16)
        if (head // ATTN_HEADS) % 3 < 2:
            piece = _dot(piece, swap_ref[...]).astype(BF16)
        wqkv_out[:, cs] = piece


def _class_major_views(x3, w_qkv, plain_weights):
    batch = x3.shape[0]
    n_steps = batch * (SEQ // ATTN_STEP_ROWS)
    step_map = lambda b, s: (b * (SEQ // ATTN_STEP_ROWS) + s, 0)
    const_map = lambda b, s: (0, 0)
    slab = lambda w: pl.BlockSpec((w.shape[0] // n_steps, w.shape[1]), step_map)
    perms = [jnp.asarray(_class_major_perm(d), BF16) for d in DILATIONS[1:]]
    swap = jnp.asarray(_rope_swap_matrix(), BF16)
    outs = pl.pallas_call(
        _class_major_kernel,
        grid=(batch, SEQ // ATTN_STEP_ROWS),
        in_specs=[pl.BlockSpec((1, ATTN_STEP_ROWS, D_MODEL), lambda b, s: (b, s, 0))] + [
            pl.BlockSpec((ATTN_STEP_ROWS, ATTN_STEP_ROWS), const_map) for _ in perms] + [
            pl.BlockSpec((HEAD_DIM, HEAD_DIM), const_map), slab(w_qkv)] + [slab(w) for w in plain_weights],
        out_specs=[pl.BlockSpec((1, ATTN_STEP_ROWS // d, d * D_MODEL), lambda b, s: (b, s, 0)) for d in DILATIONS[1:]]
        + [slab(w_qkv)] + [slab(w) for w in plain_weights],
        out_shape=[jax.ShapeDtypeStruct((batch, SEQ // d, d * D_MODEL), BF16) for d in DILATIONS[1:]]
        + [jax.ShapeDtypeStruct(w.shape, BF16) for w in [w_qkv] + list(plain_weights)],
        compiler_params=pltpu.CompilerParams(dimension_semantics=("parallel", "parallel"), vmem_limit_bytes=40 * MIB),
        name="class_major_views",
    )(x3, *perms, swap, w_qkv, *plain_weights)
    n_views = len(DILATIONS) - 1
    return outs[:n_views], outs[n_views], outs[n_views + 1:]


def _attn_group(xg, w_qkv_bf16, tab, group, dil, moe_w):
    batch = xg.shape[0]
    cast_cols = moe_w.shape[-1]
    moe_w2 = moe_w.reshape(-1, cast_cols)
    cast_rows = moe_w2.shape[0] // (batch * (SEQ // ATTN_STEP_ROWS))
    cast_spec = pl.BlockSpec((cast_rows, cast_cols), lambda b, s: (b * (SEQ // ATTN_STEP_ROWS) + s, 0))
    per_class = SEQ // dil
    lc = min(ATTN_STEP_ROWS, per_class)
    n_cls = ATTN_STEP_ROWS // lc
    steps = SEQ // ATTN_STEP_ROWS
    carry = dil == 1
    if dil == 1:
        imap = lambda b, s: (b, s, 0)
    else:
        imap = lambda b, s: (b, 0, s)
    kern = functools.partial(_attn_group_kernel, n_cls=n_cls, lc=lc, carry=carry)
    o, lse, moe_w_bf16 = pl.pallas_call(
        kern,
        grid=(batch, steps),
        in_specs=[
            pl.BlockSpec((1, lc, n_cls * D_MODEL), imap),
            pl.BlockSpec((D_MODEL, 3 * ATTN_COLS), lambda b, s: (0, group), pipeline_mode=pl.Buffered(1)),
            pl.BlockSpec((ATTN_STEP_ROWS, 2 * LANES), lambda b, s: (s, 0)),
            cast_spec,
        ],
        out_specs=[
            pl.BlockSpec((1, lc, n_cls * ATTN_COLS), imap),
            pl.BlockSpec((1, lc, n_cls * LANES), imap),
            cast_spec,
        ],
        out_shape=[
            jax.ShapeDtypeStruct((batch, per_class, dil * ATTN_COLS), BF16),
            jax.ShapeDtypeStruct((batch, per_class, dil * LANES), F32),
            jax.ShapeDtypeStruct(moe_w2.shape, BF16),
        ],
        scratch_shapes=[
            pltpu.VMEM((ATTN_STEP_ROWS, ATTN_COLS), BF16),
            pltpu.VMEM((ATTN_BLOCK + ATTN_STEP_ROWS, ATTN_COLS), BF16),
            pltpu.VMEM((ATTN_BLOCK + ATTN_STEP_ROWS, ATTN_COLS), BF16),
        ],
        compiler_params=pltpu.CompilerParams(
            dimension_semantics=("parallel", "arbitrary"), vmem_limit_bytes=56 * MIB),
        name=f"attn_group{group}",
    )(xg, w_qkv_bf16, tab, moe_w2)
    return o, lse, moe_w_bf16.reshape(moe_w.shape)


def _rope_swap_matrix():
    half = ROT_DIM // 2
    src = np.arange(HEAD_DIM)
    src[half:ROT_DIM] = np.arange(ROPE_PAIR_SHIFT, ROPE_PAIR_SHIFT + half)
    src[ROPE_PAIR_SHIFT:ROPE_PAIR_SHIFT + half] = np.arange(half, ROT_DIM)
    swap = np.zeros((HEAD_DIM, HEAD_DIM), np.float32)
    swap[src, np.arange(HEAD_DIM)] = 1.0
    return swap


def _rope_table(dil):
    inv_freq = ROPE_THETA ** (-np.arange(0, ROT_DIM, 2, dtype=np.float64) / ROT_DIM)
    ang = np.arange(SEQ, dtype=np.float64)[:, None] * inv_freq[None, :]
    ang = np.concatenate([ang, ang], -1)
    cos, sin = np.cos(ang), np.sin(ang)
    half = ROT_DIM // 2
    gap = ROPE_PAIR_SHIFT - half
    tail = LANES - ROPE_PAIR_SHIFT - half
    cosf = np.concatenate([cos[:, :half], np.ones((SEQ, gap)), cos[:, half:], np.ones((SEQ, tail))], 1)
    sinr = np.concatenate([-sin[:, :half], np.zeros((SEQ, gap)), sin[:, half:], np.zeros((SEQ, tail))], 1)
    tab = np.concatenate([cosf, sinr], 1)
    tab = tab.reshape(SEQ // dil, dil, 2 * LANES).transpose(1, 0, 2).reshape(SEQ, 2 * LANES)
    return jnp.asarray(tab.astype(np.float32))


def _attn_out_kernel(o0_ref, o1_ref, o2_ref, l0_ref, l1_ref, l2_ref, pt1_ref, pt2_ref, x_ref, w_ref, g_ref, b_ref,
                     wr_ref, br_ref, out_ref, mi_ref, mf_ref, cnt_ref, l1_s, l2_s, carry_s):
    o_nat = [None]
    for o_ref, l_ref, pt_ref, l_s, dil in ((o1_ref, l1_ref, pt1_ref, l1_s, DILATIONS[1]),
                                          (o2_ref, l2_ref, pt2_ref, l2_s, DILATIONS[2])):
        width = ATTN_STEP_ROWS // dil
        o_cm = jnp.concatenate([o_ref[0, :, r * ATTN_COLS:(r + 1) * ATTN_COLS] for r in range(dil)], axis=0)
        o_nat.append(_dot(pt_ref[...], o_cm))
        for r in range(dil):
            l_s[pl.ds(r, width, stride=dil), :] = l_ref[0, :, r * LANES:(r + 1) * LANES]
    ls = [l0_ref[0], l1_s[...], l2_s[...]]
    mx = jnp.maximum(jnp.maximum(ls[0], ls[1]), ls[2])
    es = [jnp.exp(l - mx) for l in ls]
    den = es[0] + es[1] + es[2]
    ws = [e / den for e in es]
    parts = []
    for h in range(ATTN_HEADS):
        cs = slice(h * HEAD_DIM, (h + 1) * HEAD_DIM)
        acc = ws[0][:, h:h + 1] * o0_ref[0, :, cs].astype(F32)
        acc = acc + ws[1][:, h:h + 1] * o_nat[1][:, cs]
        acc = acc + ws[2][:, h:h + 1] * o_nat[2][:, cs]
        parts.append(acc)
    mixed_in = jnp.concatenate(parts, axis=1).astype(BF16)
    y = ALPHA * x_ref[...] + _dot(mixed_in, w_ref[...])
    out = _layer_norm_rows(y, g_ref[...], b_ref[...])
    out_ref[...] = out
    _route_tile(out, wr_ref, br_ref, mi_ref, mf_ref, cnt_ref, carry_s)


def _attn_out(outs, lses, x2, w_bf16, g, b, wr_pad, br_pad):
    t = x2.shape[0]
    steps = SEQ // ATTN_STEP_ROWS
    r_in, r_out, r_shape, r_scratch = _route_specs(t, ATTN_STEP_ROWS)
    view_map = lambda i: (i // steps, i % steps, 0)
    view_spec = lambda dil, width: pl.BlockSpec((1, ATTN_STEP_ROWS // dil, dil * width), view_map)
    const_spec = lambda shape: pl.BlockSpec(shape, lambda i: (0, 0))
    row_spec = pl.BlockSpec((ATTN_STEP_ROWS, D_MODEL), lambda i: (i, 0))
    perms_t = [jnp.asarray(_class_major_perm(d).T, BF16) for d in DILATIONS[1:]]
    return pl.pallas_call(
        _attn_out_kernel,
        grid=(t // ATTN_STEP_ROWS,),
        in_specs=[view_spec(d, ATTN_COLS) for d in DILATIONS] + [view_spec(d, LANES) for d in DILATIONS] + [
            const_spec((ATTN_STEP_ROWS, ATTN_STEP_ROWS)) for _ in perms_t] + [
            row_spec, pl.BlockSpec((ATTN_COLS, D_MODEL), lambda i: (0, 0), pipeline_mode=pl.Buffered(1)),
            const_spec((1, D_MODEL)), const_spec((1, D_MODEL))] + r_in,
        out_specs=[row_spec] + r_out,
        out_shape=[jax.ShapeDtypeStruct((t, D_MODEL), F32)] + r_shape,
        scratch_shapes=[pltpu.VMEM((ATTN_STEP_ROWS, LANES), F32), pltpu.VMEM((ATTN_STEP_ROWS, LANES), F32)] + r_scratch,
        compiler_params=pltpu.CompilerParams(dimension_semantics=("arbitrary",), vmem_limit_bytes=48 * MIB),
        name="attn_out",
    )(*outs, *lses, *perms_t, x2, w_bf16, g.reshape(1, D_MODEL), b.reshape(1, D_MODEL), wr_pad, br_pad)


def _mix_out_kernel(y_ref, x_ref, w_ref, g_ref, b_ref, wr_ref, br_ref, out_ref, mi_ref, mf_ref, cnt_ref, carry_s):
    y = ALPHA * x_ref[...] + _dot(y_ref[...], w_ref[...])
    out = _layer_norm_rows(y, g_ref[...], b_ref[...])
    out_ref[...] = out
    _route_tile(out, wr_ref, br_ref, mi_ref, mf_ref, cnt_ref, carry_s)


def _mix_out(mixer_y, x2, w_bf16, g, b, wr_pad, br_pad):
    t = x2.shape[0]
    k = w_bf16.shape[0]
    row_spec = lambda width: pl.BlockSpec((ROW_TILE, width), lambda i: (i, 0))
    const_spec = lambda shape: pl.BlockSpec(shape, lambda i: (0, 0))
    r_in, r_out, r_shape, r_scratch = _route_specs(t, ROW_TILE)
    return pl.pallas_call(
        _mix_out_kernel,
        grid=(t // ROW_TILE,),
        in_specs=[row_spec(k), row_spec(D_MODEL),
                  pl.BlockSpec((k, D_MODEL), lambda i: (0, 0), pipeline_mode=pl.Buffered(1)),
                  const_spec((1, D_MODEL)), const_spec((1, D_MODEL))] + r_in,
        out_specs=[row_spec(D_MODEL)] + r_out,
        out_shape=[jax.ShapeDtypeStruct((t, D_MODEL), F32)] + r_shape,
        scratch_shapes=r_scratch,
        compiler_params=pltpu.CompilerParams(dimension_semantics=("arbitrary",), vmem_limit_bytes=48 * MIB),
        name="mix_out",
    )(mixer_y, x2, w_bf16, g.reshape(1, D_MODEL), b.reshape(1, D_MODEL), wr_pad, br_pad)


def _route_tile(x, wr_ref, br_ref, mi_ref, mf_ref, cnt_ref, carry_s):
    tm = x.shape[0]

    @pl.when(pl.program_id(0) == 0)
    def _():
        carry_s[...] = jnp.zeros_like(carry_s)

    x_hi = x.astype(BF16)
    x_lo = (x - x_hi.astype(F32)).astype(BF16)
    hi_terms = _dot(x_hi, wr_ref[...])
    logits = (hi_terms[:, :LANES] + _dot(x_lo, wr_ref[:, :LANES])) + hi_terms[:, LANES:] + br_ref[...]
    lt = logits.T
    l = [lt[e:e + 1, :] for e in range(N_EXPERTS)]
    mx = l[0]
    for e in range(1, N_EXPERTS):
        mx = jnp.maximum(mx, l[e])
    ex = [jnp.exp(v - mx) for v in l]
    tot = ex[0]
    for e in range(1, N_EXPERTS):
        tot = tot + ex[e]
    p = [v / tot for v in ex]

    def first_index_of(vals, target):
        idx = jnp.full_like(target, len(vals) - 1).astype(I32)
        for k in range(len(vals) - 2, -1, -1):
            idx = jnp.where(vals[k] == target, k, idx)
        return idx

    best = None
    for g in range(N_EXPERT_GROUPS):
        pg = p[g * EXPERTS_PER_GROUP:(g + 1) * EXPERTS_PER_GROUP]
        top1 = jnp.maximum(jnp.maximum(pg[0], pg[1]), jnp.maximum(pg[2], pg[3]))
        i1 = first_index_of(pg, top1)
        rest = [jnp.where(i1 == k, -1.0, pg[k]) for k in range(EXPERTS_PER_GROUP)]
        top2 = jnp.maximum(jnp.maximum(rest[0], rest[1]), jnp.maximum(rest[2], rest[3]))
        i2 = first_index_of(rest, top2)
        score = top1 + top2
        if best is None:
            best = (score, jnp.zeros_like(i1), top1, top2, i1, i2)
        else:
            better = score > best[0]
            cand = (score, jnp.full_like(i1, g), top1, top2, i1, i2)
            best = tuple(jnp.where(better, cv, bv) for cv, bv in zip(cand, best))
    _, g_sel, p1, p2, i1, i2 = best
    psum = p1 + p2
    gate1, gate2 = p1 / psum, p2 / psum
    first_low = i1 < i2
    lo = jnp.where(first_low, i1, i2)
    hi = jnp.where(first_low, i2, i1)
    gate_lo = jnp.where(first_low, gate1, gate2)
    gate_hi = jnp.where(first_low, gate2, gate1)
    pair = jnp.where(lo == 0, 0, jnp.where(lo == 1, 3, 5)) + hi - lo - 1
    cls = g_sel * len(PAIRS) + pair

    n_rows = carry_s.shape[0]
    sub = lax.broadcasted_iota(I32, (n_rows, tm), 0)
    onehot = sub == cls
    oh = jnp.where(onehot, 1.0, 0.0)
    upper = (lax.broadcasted_iota(I32, (tm, tm), 0) <= lax.broadcasted_iota(I32, (tm, tm), 1))
    cum = _dot(oh.astype(BF16), jnp.where(upper, 1.0, 0.0).astype(BF16))
    carry = carry_s[:, 0:1]
    rank = jnp.sum(jnp.where(onehot, cum - 1.0 + carry, 0.0), axis=0, keepdims=True)
    carry_new = carry + jnp.sum(oh, axis=1, keepdims=True)
    carry_s[...] = jnp.broadcast_to(carry_new, carry_s.shape)
    cnt_ref[...] = jnp.broadcast_to(carry_new, cnt_ref.shape)

    sub8 = lax.broadcasted_iota(I32, (8, tm), 0)
    mi_ref[...] = jnp.where(sub8 == 0, cls, jnp.where(sub8 == 1, rank.astype(I32), 0))
    mf_ref[...] = jnp.where(sub8 == 0, gate_lo, jnp.where(sub8 == 1, gate_hi, 0.0))


def _route_specs(t, tm):
    in_specs = [pl.BlockSpec((D_MODEL, 2 * LANES), lambda i: (0, 0)), pl.BlockSpec((1, LANES), lambda i: (0, 0))]
    out_specs = [pl.BlockSpec((8, tm), lambda i: (0, i)), pl.BlockSpec((8, tm), lambda i: (0, i)),
                 pl.BlockSpec((32, LANES), lambda i: (0, 0))]
    out_shape = [jax.ShapeDtypeStruct((8, t), I32), jax.ShapeDtypeStruct((8, t), F32),
                 jax.ShapeDtypeStruct((32, LANES), F32)]
    return in_specs, out_specs, out_shape, [pltpu.VMEM((32, LANES), F32)]


def _expert_kernel(elo_ref, ehi_ref, valid_ref, src0_ref, src1_ref, src_ahead_ref, x_hbm, gs_ref,
                   wg0, wu0, wd0, wg1, wu1, wd1, out_ref, rows_s, sems):
    del elo_ref, ehi_ref
    i = pl.program_id(0)
    n_slots = MOE_GATHER_AHEAD + 1
    slot = i % n_slots

    def start_row(idx_ref, t, to_slot, priority):
        pltpu.make_async_copy(x_hbm.at[pl.ds(idx_ref[0, 0, t], 1)], rows_s.at[to_slot, pl.ds(t, 1)],
                              sems.at[to_slot]).start(priority=priority)

    def wait_rows(of_slot):
        pltpu.make_async_copy(x_hbm.at[pl.ds(0, MOE_BLOCK)], rows_s.at[of_slot], sems.at[of_slot]).wait()

    def prime(idx_ref, to_slot):
        def issue(t, c):
            start_row(idx_ref, t, to_slot, 0)
            return c

        lax.fori_loop(0, MOE_BLOCK, issue, 0, unroll=8)

    @pl.when(i == 0)
    def _():
        prime(src0_ref, 0)

    @pl.when(jnp.logical_and(i == 0, valid_ref[1] > 0))
    def _():
        prime(src1_ref, 1)

    def used_block(cur):
        wait_rows(cur)
        xb = rows_s[cur].astype(BF16)

        def ffn(wg, wu, wd):
            hg = _dot(xb, wg[0, 0])
            hu = _dot(xb, wu[0, 0])
            hidden = (hg * jax.nn.sigmoid(hg)) * hu
            return _dot(hidden.astype(BF16), wd[0, 0])

        gs = gs_ref[...]
        out_ref[...] = gs[:, 0:1] * ffn(wg0, wu0, wd0) + gs[:, 1:2] * ffn(wg1, wu1, wd1)

        @pl.when(valid_ref[i + MOE_GATHER_AHEAD] > 0)
        def _():
            for t in range(MOE_BLOCK):
                start_row(src_ahead_ref, t, (cur + MOE_GATHER_AHEAD) % n_slots, priority=t % N_DMA_QUEUES)

    for cur in range(n_slots):
        pl.when(jnp.logical_and(valid_ref[i] > 0, slot == cur))(functools.partial(used_block, cur))

    @pl.when(valid_ref[i] == 0)
    def _():
        out_ref[...] = jnp.zeros_like(out_ref)


def _experts(x2, src_rows, gates_sorted, blk_elo, blk_ehi, blk_valid, layer, wg, wu, wd):
    assert MOE_GATHER_AHEAD == 2
    nb = src_rows.shape[0] - MOE_GATHER_AHEAD
    n_rows = nb * MOE_BLOCK
    lo_map = lambda b, elo, ehi, valid: (layer, elo[b], 0, 0)
    hi_map = lambda b, elo, ehi, valid: (layer, ehi[b], 0, 0)
    row_map = lambda b, elo, ehi, valid: (b, 0)
    up_shape = (1, 1, D_MODEL, D_EXPERT)
    down_shape = (1, 1, D_EXPERT, D_MODEL)
    grid_spec = pltpu.PrefetchScalarGridSpec(
        num_scalar_prefetch=3,
        grid=(nb,),
        in_specs=[
            pl.BlockSpec((1, 1, MOE_BLOCK), lambda b, elo, ehi, valid: (0, 0, 0), memory_space=pltpu.SMEM),
            pl.BlockSpec((1, 1, MOE_BLOCK), lambda b, elo, ehi, valid: (1, 0, 0), memory_space=pltpu.SMEM),
            pl.BlockSpec((1, 1, MOE_BLOCK), lambda b, elo, ehi, valid: (b + MOE_GATHER_AHEAD, 0, 0),
                         memory_space=pltpu.SMEM),
            pl.BlockSpec(memory_space=pl.ANY),
            pl.BlockSpec((MOE_BLOCK, LANES), row_map),
            pl.BlockSpec(up_shape, lo_map), pl.BlockSpec(up_shape, lo_map), pl.BlockSpec(down_shape, lo_map),
            pl.BlockSpec(up_shape, hi_map), pl.BlockSpec(up_shape, hi_map), pl.BlockSpec(down_shape, hi_map),
        ],
        out_specs=pl.BlockSpec((MOE_BLOCK, D_MODEL), row_map),
        scratch_shapes=[pltpu.VMEM((MOE_GATHER_AHEAD + 1, MOE_BLOCK, D_MODEL), F32),
                        pltpu.SemaphoreType.DMA((MOE_GATHER_AHEAD + 1,))],
    )
    return pl.pallas_call(
        _expert_kernel,
        grid_spec=grid_spec,
        out_shape=jax.ShapeDtypeStruct((n_rows, D_MODEL), F32),
        compiler_params=pltpu.CompilerParams(dimension_semantics=("arbitrary",), vmem_limit_bytes=58 * MIB),
        name="moe_experts",
    )(blk_elo, blk_ehi, blk_valid, src_rows, src_rows, src_rows, x2, gates_sorted, wg, wu, wd, wg, wu, wd)


def _gather_ln_kernel(dest_ref, dest_next_ref, x_ref, src_hbm, g_ref, b_ref, y_ref, rows_s, sems, *, tm):
    i = pl.program_id(0)
    slot = i % 2

    def issue_rows(idx_ref, to_slot):
        for t in range(tm):
            pltpu.make_async_copy(src_hbm.at[pl.ds(idx_ref[0, 0, t], 1)], rows_s.at[to_slot, pl.ds(t, 1)],
                                  sems.at[to_slot]).start(priority=t % N_DMA_QUEUES)

    @pl.when(i == 0)
    def _():
        issue_rows(dest_ref, slot)

    @pl.when(i + 1 < pl.num_programs(0))
    def _():
        issue_rows(dest_next_ref, 1 - slot)

    pltpu.make_async_copy(src_hbm.at[pl.ds(0, tm)], rows_s.at[slot], sems.at[slot]).wait()
    y = ALPHA * x_ref[...] + rows_s[slot]
    y_ref[...] = _layer_norm_rows(y, g_ref[...], b_ref[...])


def _gather_ln(x2, expert_out, dest, g, b):
    t = x2.shape[0]
    tm = GATHER_TM
    n_tiles = t // tm
    dest3 = dest.reshape(n_tiles, 1, tm)
    return pl.pallas_call(
        functools.partial(_gather_ln_kernel, tm=tm),
        grid=(n_tiles,),
        in_specs=[
            pl.BlockSpec((1, 1, tm), lambda i: (i, 0, 0), memory_space=pltpu.SMEM),
            pl.BlockSpec((1, 1, tm), lambda i: (jnp.minimum(i + 1, n_tiles - 1), 0, 0), memory_space=pltpu.SMEM),
            pl.BlockSpec((tm, D_MODEL), lambda i: (i, 0)),
            pl.BlockSpec(memory_space=pl.ANY),
            pl.BlockSpec((1, D_MODEL), lambda i: (0, 0)),
            pl.BlockSpec((1, D_MODEL), lambda i: (0, 0)),
        ],
        out_specs=pl.BlockSpec((tm, D_MODEL), lambda i: (i, 0)),
        out_shape=jax.ShapeDtypeStruct((t, D_MODEL), F32),
        scratch_shapes=[pltpu.VMEM((2, tm, D_MODEL), F32), pltpu.SemaphoreType.DMA((2,))],
        compiler_params=pltpu.CompilerParams(dimension_semantics=("arbitrary",), vmem_limit_bytes=32 * MIB),
        name="moe_gather_ln",
    )(dest3, dest3, x2, expert_out, g.reshape(1, D_MODEL), b.reshape(1, D_MODEL))


_CLASS_LO = np.array([4 * (c // 6) + PAIRS[c % 6][0] for c in range(N_CLASSES)], np.int32)
_CLASS_HI = np.array([4 * (c // 6) + PAIRS[c % 6][1] for c in range(N_CLASSES)], np.int32)


def _moe_layer(x2, routing, layer, wg, wu, wd, ln_g, ln_b):
    t = x2.shape[0]
    nb = -(-(t + N_CLASSES * (MOE_BLOCK - 1)) // MOE_BLOCK)
    n_rows = nb * MOE_BLOCK
    meta_i, meta_f, cnt = routing
    cls, rank = meta_i[0], meta_i[1]
    counts = cnt[:N_CLASSES, 0].astype(I32)
    padded = (counts + MOE_BLOCK - 1) // MOE_BLOCK * MOE_BLOCK
    ends = jnp.cumsum(padded)
    starts = ends - padded
    dest = starts[cls] + rank
    blk_start = jnp.arange(nb, dtype=I32) * MOE_BLOCK
    nb_ext = nb + MOE_GATHER_AHEAD
    blk_valid = (jnp.arange(nb_ext, dtype=I32) * MOE_BLOCK < ends[-1]).astype(I32)
    n_valid = jnp.sum(blk_valid)
    blk_cls = jnp.minimum(jnp.sum((ends[None, :] <= blk_start[:, None]).astype(I32), axis=1), N_CLASSES - 1)
    blk_cls = blk_cls[jnp.minimum(jnp.arange(nb), n_valid - 1)]
    blk_elo = jnp.asarray(_CLASS_LO)[blk_cls]
    blk_ehi = jnp.asarray(_CLASS_HI)[blk_cls]
    tok_meta = jnp.concatenate([meta_f[:2].T, jnp.arange(t, dtype=F32)[:, None], jnp.zeros((t, LANES - 3), F32)], 1)
    sorted_meta = jnp.zeros((nb_ext * MOE_BLOCK, LANES), F32).at[dest].set(tok_meta)
    src_rows = sorted_meta[:, 2].astype(I32).reshape(nb_ext, 1, MOE_BLOCK)
    out = _experts(x2, src_rows, sorted_meta, blk_elo, blk_ehi, blk_valid, layer, wg, wu, wd)
    return _gather_ln(x2, out, dest, ln_g, ln_b)


def _inproj_kernel(x_ref, w_ref, wgate_ref, z_ref, gates_ref, xb_s):
    @pl.when(pl.program_id(1) == 0)
    def _():
        xb_s[...] = x_ref[...].astype(BF16)
        gates_ref[...] = _dot(xb_s[...], wgate_ref[...])

    z_ref[...] = _dot(xb_s[...], w_ref[...]).astype(z_ref.dtype)


def _inproj(x2, w_in_bf16, w_gate_bf16):
    t = x2.shape[0]
    tm, tn = INPROJ_TM, INPROJ_TN
    n_main = MLSTM_MAIN_COLS
    return pl.pallas_call(
        _inproj_kernel,
        grid=(t // tm, n_main // tn),
        in_specs=[
            pl.BlockSpec((tm, D_MODEL), lambda m, n: (m, 0)),
            pl.BlockSpec((D_MODEL, tn), lambda m, n: (0, n)),
            pl.BlockSpec((D_MODEL, LANES), lambda m, n: (0, 0)),
        ],
        out_specs=[
            pl.BlockSpec((tm, tn), lambda m, n: (m, n)),
            pl.BlockSpec((tm, LANES), lambda m, n: (m, 0)),
        ],
        out_shape=[
            jax.ShapeDtypeStruct((t, n_main), BF16),
            jax.ShapeDtypeStruct((t, LANES), F32),
        ],
        scratch_shapes=[pltpu.VMEM((tm, D_MODEL), BF16)],
        compiler_params=pltpu.CompilerParams(
            dimension_semantics=("parallel", "arbitrary"), vmem_limit_bytes=48 * MIB),
        name="mlstm_inproj",
    )(x2, w_in_bf16, w_gate_bf16)


def _mlstm_cell_kernel(zq_ref, zk_ref, v_ref, op_ref, gt_ref, bg_ref, cwq_ref, cbq_ref, cwk_ref, cbk_ref,
                       ng_ref, y_ref, q_s, k_s, gb_s, bc_s, gbt_s, bct_s, *, chunk, n_heads):
    hgroup = pl.program_id(1)
    seq = zq_ref.shape[0]

    def conv_silu(z_ref, cw_ref, cb_ref):
        z = z_ref[...].astype(F32)
        rowi = lax.broadcasted_iota(I32, z.shape, 0)
        out = cb_ref[...] + cw_ref[0:1, :] * jnp.where(rowi >= CONV_K - 1, pltpu.roll(z, CONV_K - 1, 0), 0.0)
        for j in range(1, CONV_K - 1):
            shift = CONV_K - 1 - j
            out = out + cw_ref[j:j + 1, :] * jnp.where(rowi >= shift, pltpu.roll(z, shift, 0), 0.0)
        out = out + cw_ref[CONV_K - 1:CONV_K, :] * z
        return out * jax.nn.sigmoid(out)

    q_s[...] = conv_silu(zq_ref, cwq_ref, cbq_ref).astype(BF16)
    k_s[...] = conv_silu(zk_ref, cwk_ref, cbk_ref) * (QK_DIM ** -0.5)

    lane = lax.broadcasted_iota(I32, (chunk, LANES), 1)
    sub = lax.broadcasted_iota(I32, (MLSTM_HEADS, chunk), 0)
    ti = lax.broadcasted_iota(I32, (chunk, chunk), 0)
    si = lax.broadcasted_iota(I32, (chunk, chunk), 1)
    causal = ti >= si
    ones_cols = jnp.ones((chunk, LANES), BF16)

    @pl.when(hgroup == 0)
    def _():
        tri = jnp.where(causal, 1.0, 0.0)
        bias = bg_ref[...]
        for c in range(seq // chunk):
            rs = slice(c * chunk, (c + 1) * chunk)
            gb = gt_ref[rs, :] + bias
            log_f = -(jnp.maximum(-gb, 0.0) + jnp.log1p(jnp.exp(-jnp.abs(gb))))
            bcum = _dot(tri, log_f, precision=HIGHEST)
            gb_s[rs, :] = gb
            bc_s[rs, :] = bcum
            gbt_s[:, rs] = gb.T
            bct_s[:, rs] = bcum.T

    def pick_col(a, idx):
        return jnp.sum(jnp.where(lane == idx, a, 0.0), axis=1, keepdims=True)

    def pick_row(a, idx):
        return jnp.sum(jnp.where(sub == idx, a, 0.0), axis=0, keepdims=True)

    heads = [hgroup * n_heads + i for i in range(n_heads)]
    qcols = [slice(i * QK_DIM, (i + 1) * QK_DIM) for i in range(n_heads)]
    vcols = [slice(i * V_DIM, (i + 1) * V_DIM) for i in range(n_heads)]
    hr = range(n_heads)
    c_state = [jnp.zeros((QK_DIM, V_DIM + LANES), F32) for _ in hr]
    m_state = [jnp.full((1, 1), NEG, F32) for _ in hr]
    for c in range(seq // chunk):
        rs = slice(c * chunk, (c + 1) * chunk)
        gb_c, bc_c = gb_s[rs, :], bc_s[rs, :]
        gbt_c, bct_c = gbt_s[0:MLSTM_HEADS, rs], bct_s[MLSTM_HEADS:2 * MLSTM_HEADS, rs]
        li_col = [pick_col(gb_c, hd) for hd in heads]
        bc_col = [pick_col(bc_c, hd + MLSTM_HEADS) for hd in heads]
        li_row = [pick_row(gbt_c, hd) for hd in heads]
        bc_row = [pick_row(bct_c, hd) for hd in heads]
        dmat = [jnp.where(causal, bc_col[i] + (li_row[i] - bc_row[i]), NEG) for i in hr]
        inter = [bc_col[i] + m_state[i] for i in hr]
        m_t = [jnp.maximum(inter[i], jnp.max(dmat[i], axis=1, keepdims=True)) for i in hr]
        qc = [q_s[rs, qcols[i]] for i in hr]
        kf = [k_s[rs, qcols[i]] for i in hr]
        vc = [v_ref[rs, vcols[i]] for i in hr]
        a = [_dot_nt(qc[i], kf[i].astype(BF16)) * jnp.exp(dmat[i] - m_t[i]) for i in hr]
        w_inter = [jnp.exp(inter[i] - m_t[i]) for i in hr]
        q_state = [_dot(qc[i], c_state[i].astype(BF16)) for i in hr]
        num = [_dot(a[i].astype(BF16), vc[i]) + w_inter[i] * q_state[i][:, :V_DIM] for i in hr]
        den = [jnp.sum(a[i], axis=1, keepdims=True) + w_inter[i] * q_state[i][:, V_DIM:V_DIM + 1] for i in hr]
        h_out = [num[i] / jnp.maximum(jnp.abs(den[i]), jnp.exp(-m_t[i])) for i in hr]
        mu = [jnp.mean(h_out[i], axis=1, keepdims=True) for i in hr]
        hc = [h_out[i] - mu[i] for i in hr]
        var = [jnp.mean(hc[i] * hc[i], axis=1, keepdims=True) for i in hr]
        for i in hr:
            hn = hc[i] * lax.rsqrt(var[i] + LN_EPS) * ng_ref[:, vcols[i]]
            y_ref[rs, vcols[i]] = (hn * jax.nn.sigmoid(op_ref[rs, vcols[i]].astype(F32))).astype(y_ref.dtype)
        b_last = [bc_col[i][chunk - 1:chunk, :] for i in hr]
        g = [b_last[i] - bc_col[i] + li_col[i] for i in hr]
        m_new = [jnp.maximum(b_last[i] + m_state[i], jnp.max(g[i], axis=0, keepdims=True)) for i in hr]
        wk = [jnp.exp(g[i] - m_new[i]) for i in hr]
        decay = [jnp.exp(b_last[i] + m_state[i] - m_new[i]) for i in hr]
        kw_t = [(wk[i] * kf[i]).T.astype(BF16) for i in hr]
        upd = [_dot(kw_t[i], jnp.concatenate([vc[i], ones_cols], axis=1)) for i in hr]
        c_state = [decay[i] * c_state[i] + upd[i] for i in hr]
        m_state = m_new


def _mlstm_cell(z, gates, b_gates_pad, conv_w, conv_b, norm_g, batch):
    t = z.shape[0]
    nh = MLSTM_HEADS_PER_STEP
    groups = MLSTM_HEADS // nh
    qw, vw = nh * QK_DIM, nh * V_DIM
    k_blk0 = (MLSTM_QK_COLS // 2) // qw
    v_blk0 = MLSTM_QK_COLS // vw
    o_blk0 = (MLSTM_QK_COLS + MLSTM_V_COLS) // vw
    return pl.pallas_call(
        functools.partial(_mlstm_cell_kernel, chunk=MLSTM_CHUNK, n_heads=nh),
        grid=(batch, groups),
        in_specs=[
            pl.BlockSpec((SEQ, qw), lambda b, h: (b, h)),
            pl.BlockSpec((SEQ, qw), lambda b, h: (b, k_blk0 + h)),
            pl.BlockSpec((SEQ, vw), lambda b, h: (b, v_blk0 + h)),
            pl.BlockSpec((SEQ, vw), lambda b, h: (b, o_blk0 + h)),
            pl.BlockSpec((SEQ, LANES), lambda b, h: (b, 0)),
            pl.BlockSpec((1, LANES), lambda b, h: (0, 0)),
            pl.BlockSpec((CONV_K, qw), lambda b, h: (0, h)),
            pl.BlockSpec((1, qw), lambda b, h: (0, h)),
            pl.BlockSpec((CONV_K, qw), lambda b, h: (0, k_blk0 + h)),
            pl.BlockSpec((1, qw), lambda b, h: (0, k_blk0 + h)),
            pl.BlockSpec((1, vw), lambda b, h: (0, h)),
        ],
        out_specs=pl.BlockSpec((SEQ, vw), lambda b, h: (b, h)),
        out_shape=jax.ShapeDtypeStruct((t, MLSTM_V_COLS), BF16),
        scratch_shapes=[pltpu.VMEM((SEQ, qw), BF16), pltpu.VMEM((SEQ, qw), F32),
                        pltpu.VMEM((SEQ, LANES), F32), pltpu.VMEM((SEQ, LANES), F32),
                        pltpu.VMEM((LANES, SEQ), F32), pltpu.VMEM((LANES, SEQ), F32)],
        compiler_params=pltpu.CompilerParams(
            dimension_semantics=("parallel", "arbitrary"), vmem_limit_bytes=56 * MIB),
        name="mlstm_cell",
    )(z, z, z, z, gates, b_gates_pad, conv_w, conv_b.reshape(1, -1), conv_w, conv_b.reshape(1, -1),
      norm_g.reshape(1, -1))


def kernel(x, attn_w_qkv, attn_w_o, mlstm_w_in, mlstm_b_gates, mlstm_conv_w, mlstm_conv_b, mlstm_norm_g,
           mlstm_w_out, ln_mix_g, ln_mix_b, ln_ffn_g, ln_ffn_b, router_w, router_b, moe_w_gate, moe_w_up,
           moe_w_down):
    batch, seq, d = x.shape
    assert (seq, d) == (SEQ, D_MODEL)
    t = batch * seq
    x2 = x.reshape(t, d)

    wr_f32 = jnp.zeros((D_MODEL, LANES), F32).at[:, :N_EXPERTS].set(router_w)
    wr_hi = lax.bitcast_convert_type(lax.bitcast_convert_type(wr_f32, jnp.uint32) & jnp.uint32(0xFFFF0000), F32)
    wr_pad = jnp.concatenate([wr_hi.astype(BF16), (wr_f32 - wr_hi).astype(BF16)], axis=1)
    br_pad = jnp.zeros((1, LANES), F32).at[0, :N_EXPERTS].set(router_b)

    w_in = mlstm_w_in[0]
    cm_views, w_qkv, (w_o, w_main, w_out) = _class_major_views(
        x, attn_w_qkv[0], (attn_w_o[0], w_in, mlstm_w_out[0]))
    views = [x] + list(cm_views)
    outs, lses, moe_bf = [], [], []
    for group, (dil, moe_w) in enumerate(zip(DILATIONS, (moe_w_gate, moe_w_up, moe_w_down))):
        o, lse, w_bf = _attn_group(views[group], w_qkv, _rope_table(dil), group, dil, moe_w)
        outs.append(o)
        lses.append(lse)
        moe_bf.append(w_bf)
    wg_bf, wu_bf, wd_bf = moe_bf
    x2, *routing = _attn_out(outs, lses, x2, w_o, ln_mix_g[0], ln_mix_b[0], wr_pad, br_pad)
    x2 = _moe_layer(x2, routing, 0, wg_bf, wu_bf, wd_bf, ln_ffn_g[0], ln_ffn_b[0])

    w_gate = jnp.zeros((D_MODEL, LANES), F32).at[:, :2 * MLSTM_HEADS].set(w_in[:, MLSTM_MAIN_COLS:]).astype(BF16)
    bg_pad = jnp.zeros((1, LANES), F32).at[0, :2 * MLSTM_HEADS].set(mlstm_b_gates[0])
    z, gates = _inproj(x2, w_main, w_gate)
    y = _mlstm_cell(z, gates, bg_pad, mlstm_conv_w[0], mlstm_conv_b[0], mlstm_norm_g[0], batch)
    x2, *routing = _mix_out(y, x2, w_out, ln_mix_g[1], ln_mix_b[1], wr_pad, br_pad)
    x2 = _moe_layer(x2, routing, 1, wg_bf, wu_bf, wd_bf, ln_ffn_g[1], ln_ffn_b[1])
    return x2.reshape(batch, seq, d)
```

```python
import functools

import numpy as np
import jax
import jax.numpy as jnp
from jax import lax
from jax.experimental import pallas as pl
from jax.experimental.pallas import tpu as pltpu

F32 = jnp.float32
BF16 = jnp.bfloat16
I32 = jnp.int32
HIGHEST = lax.Precision.HIGHEST

D_MODEL = 2048
SEQ = 2048
DEPTH = 2
DILATIONS = (1, 4, 16)
N_BACK = 128
ATTN_HEADS = 8
HEAD_DIM = 128
ROT_DIM = HEAD_DIM // 4
ROPE_THETA = 500000.0
ATTN_BLOCK = 128
ATTN_COLS = ATTN_HEADS * HEAD_DIM

MLSTM_HEADS = 8
QK_DIM = 128
V_DIM = D_MODEL // MLSTM_HEADS
CONV_K = 4
MLSTM_QK_COLS = 2 * MLSTM_HEADS * QK_DIM
MLSTM_V_COLS = MLSTM_HEADS * V_DIM
MLSTM_MAIN_COLS = MLSTM_QK_COLS + 2 * MLSTM_V_COLS

N_EXPERTS = 16
N_EXPERT_GROUPS = 4
EXPERTS_PER_GROUP = 4
D_EXPERT = 768
PAIRS = ((0, 1), (0, 2), (0, 3), (1, 2), (1, 3), (2, 3))
N_CLASSES = N_EXPERT_GROUPS * len(PAIRS)

ALPHA = (2 * DEPTH) ** 0.25
LN_EPS = 1e-5
NEG = -1e30

LANES = 128
N_DMA_QUEUES = 2
MIB = 1024 * 1024

ATTN_STEP_ROWS = 512
ATTN_HEAD_SET = 8
ROPE_PAIR_SHIFT = 64
ROW_TILE = 512
INPROJ_TM = 1024
INPROJ_TN = 1024
MLSTM_CHUNK = 128
MLSTM_HEADS_PER_STEP = 4
MOE_BLOCK = 256
MOE_GATHER_AHEAD = 2
GATHER_TM = 512


def _dot(a, b, **kw):
    return jnp.dot(a, b, preferred_element_type=F32, **kw)


def _dot_nt(a, b):
    return lax.dot_general(a, b, (((1,), (1,)), ((), ())), preferred_element_type=F32)


def _layer_norm_rows(y, g, b):
    mu = jnp.mean(y, axis=-1, keepdims=True)
    yc = y - mu
    var = jnp.mean(yc * yc, axis=-1, keepdims=True)
    return yc * lax.rsqrt(var + LN_EPS) * g + b


def _attn_group_kernel(x_ref, w_ref, tab_ref, cast_in_ref, o_ref, lse_ref, cast_out_ref, q_s, k_s, v_s,
                       *, n_cls, lc, carry):
    cast_out_ref[...] = cast_in_ref[...].astype(BF16)
    step = pl.program_id(1)
    rows_total = n_cls * lc
    blocks_per_class = lc // ATTN_BLOCK

    if n_cls == 1:
        xs = x_ref[0]
    else:
        xs = jnp.concatenate([x_ref[0, :, c * D_MODEL:(c + 1) * D_MODEL] for c in range(n_cls)], axis=0)
    qkv = _dot(xs.astype(BF16), w_ref[...])

    if carry:
        @pl.when(step == 0)
        def _():
            k_s[0:ATTN_BLOCK, :] = jnp.zeros((ATTN_BLOCK, ATTN_COLS), BF16)
            v_s[0:ATTN_BLOCK, :] = jnp.zeros((ATTN_BLOCK, ATTN_COLS), BF16)

        @pl.when(step > 0)
        def _():
            k_s[0:ATTN_BLOCK, :] = k_s[rows_total:rows_total + ATTN_BLOCK, :]
            v_s[0:ATTN_BLOCK, :] = v_s[rows_total:rows_total + ATTN_BLOCK, :]

    cosf = tab_ref[:, 0:LANES]
    sinr = tab_ref[:, LANES:2 * LANES]

    def rope(t):
        return t * cosf + pltpu.roll(t, ROPE_PAIR_SHIFT, 1) * sinr

    for h in range(ATTN_HEADS):
        cs = slice(h * HEAD_DIM, (h + 1) * HEAD_DIM)
        q_s[:, cs] = rope(qkv[:, h * HEAD_DIM:(h + 1) * HEAD_DIM]).astype(BF16)
        k_s[ATTN_BLOCK:, cs] = rope(qkv[:, ATTN_COLS + h * HEAD_DIM:ATTN_COLS + (h + 1) * HEAD_DIM]).astype(BF16)
    v_s[ATTN_BLOCK:, :] = qkv[:, 2 * ATTN_COLS:3 * ATTN_COLS].astype(BF16)

    row = lax.broadcasted_iota(I32, (ATTN_BLOCK, ATTN_BLOCK), 0)
    col = lax.broadcasted_iota(I32, (ATTN_BLOCK, ATTN_BLOCK), 1)
    row2 = lax.broadcasted_iota(I32, (ATTN_BLOCK, 2 * ATTN_BLOCK), 0)
    col2 = lax.broadcasted_iota(I32, (ATTN_BLOCK, 2 * ATTN_BLOCK), 1)
    lane = lax.broadcasted_iota(I32, (ATTN_BLOCK, LANES), 1)
    mask_cur = col <= row
    mask_both = jnp.logical_and(col2 >= row2, col2 <= row2 + N_BACK)
    if carry:
        mask_first = jnp.logical_and(mask_both, jnp.logical_or(col2 >= ATTN_BLOCK, step > 0))
    scale = HEAD_DIM ** -0.5

    for c in range(n_cls):
        for bi in range(blocks_per_class):
            j = c * blocks_per_class + bi
            ors = slice(bi * ATTN_BLOCK, (bi + 1) * ATTN_BLOCK)
            with_prev = bi > 0 or carry
            if with_prev:
                krows = slice(j * ATTN_BLOCK, (j + 2) * ATTN_BLOCK)
                mask = mask_first if bi == 0 else mask_both
            else:
                krows = slice((j + 1) * ATTN_BLOCK, (j + 2) * ATTN_BLOCK)
                mask = mask_cur
            lse_tile = jnp.zeros((ATTN_BLOCK, LANES), F32)
            for h0 in range(0, ATTN_HEADS, ATTN_HEAD_SET):
                heads = range(h0, h0 + ATTN_HEAD_SET)
                cols = [slice(h * HEAD_DIM, (h + 1) * HEAD_DIM) for h in heads]
                scores = [jnp.where(mask, _dot_nt(q_s[j * ATTN_BLOCK:(j + 1) * ATTN_BLOCK, cs], k_s[krows, cs])
                                    * scale, NEG) for cs in cols]
                maxes = [jnp.max(s, axis=1, keepdims=True) for s in scores]
                probs = [jnp.exp(s - m) for s, m in zip(scores, maxes)]
                dens = [jnp.sum(p, axis=1, keepdims=True) for p in probs]
                accs = [_dot(p.astype(BF16), v_s[krows, cs]) for p, cs in zip(probs, cols)]
                for h, acc, den, m in zip(heads, accs, dens, maxes):
                    o_ref[0, ors, c * ATTN_COLS + h * HEAD_DIM:c * ATTN_COLS + (h + 1) * HEAD_DIM] = (
                        acc / den).astype(o_ref.dtype)
                    lse_tile = jnp.where(lane == h, m + jnp.log(den), lse_tile)
            lse_ref[0, ors, c * LANES:(c + 1) * LANES] = lse_tile


def _class_major_perm(dil):
    width = ATTN_STEP_ROWS // dil
    perm = np.zeros((ATTN_STEP_ROWS, ATTN_STEP_ROWS), np.float32)
    for r in range(dil):
        for m in range(width):
            perm[r * width + m, m * dil + r] = 1.0
    return perm


def _class_major_kernel(x_ref, p1_ref, p2_ref, swap_ref, wqkv_ref, *refs):
    n_plain = (len(refs) - len(DILATIONS)) // 2
    plain_in = refs[:n_plain]
    view_out = refs[n_plain:n_plain + len(DILATIONS) - 1]
    wqkv_out = refs[n_plain + len(DILATIONS) - 1]
    plain_out = refs[n_plain + len(DILATIONS):]

    xb = x_ref[0].astype(BF16)
    for p_ref, out_ref, dil in zip((p1_ref, p2_ref), view_out, DILATIONS[1:]):
        width = ATTN_STEP_ROWS // dil
        rows = _dot(p_ref[...], xb).astype(BF16)
        for r in range(dil):
            out_ref[0, :, r * D_MODEL:(r + 1) * D_MODEL] = rows[r * width:(r + 1) * width, :]

    for src_ref, dst_ref in zip(plain_in, plain_out):
        dst_ref[...] = src_ref[...].astype(BF16)
    for head in range(len(DILATIONS) * 3 * ATTN_HEADS):
        cs = slice(head * HEAD_DIM, (head + 1) * HEAD_DIM)
        piece = wqkv_ref[:, cs].astype(BF16)
        if (head // ATTN_HEADS) % 3 < 2:
            piece = _dot(piece, swap_ref[...]).astype(BF16)
        wqkv_out[:, cs] = piece


def _class_major_views(x3, w_qkv, plain_weights):
    batch = x3.shape[0]
    n_steps = batch * (SEQ // ATTN_STEP_ROWS)
    step_map = lambda b, s: (b * (SEQ // ATTN_STEP_ROWS) + s, 0)
    const_map = lambda b, s: (0, 0)
    slab = lambda w: pl.BlockSpec((w.shape[0] // n_steps, w.shape[1]), step_map)
    perms = [jnp.asarray(_class_major_perm(d), BF16) for d in DILATIONS[1:]]
    swap = jnp.asarray(_rope_swap_matrix(), BF16)
    outs = pl.pallas_call(
        _class_major_kernel,
        grid=(batch, SEQ // ATTN_STEP_ROWS),
        in_specs=[pl.BlockSpec((1, ATTN_STEP_ROWS, D_MODEL), lambda b, s: (b, s, 0))] + [
            pl.BlockSpec((ATTN_STEP_ROWS, ATTN_STEP_ROWS), const_map) for _ in perms] + [
            pl.BlockSpec((HEAD_DIM, HEAD_DIM), const_map), slab(w_qkv)] + [slab(w) for w in plain_weights],
        out_specs=[pl.BlockSpec((1, ATTN_STEP_ROWS // d, d * D_MODEL), lambda b, s: (b, s, 0)) for d in DILATIONS[1:]]
        + [slab(w_qkv)] + [slab(w) for w in plain_weights],
        out_shape=[jax.ShapeDtypeStruct((batch, SEQ // d, d * D_MODEL), BF16) for d in DILATIONS[1:]]
        + [jax.ShapeDtypeStruct(w.shape, BF16) for w in [w_qkv] + list(plain_weights)],
        compiler_params=pltpu.CompilerParams(dimension_semantics=("parallel", "parallel"), vmem_limit_bytes=40 * MIB),
        name="class_major_views",
    )(x3, *perms, swap, w_qkv, *plain_weights)
    n_views = len(DILATIONS) - 1
    return outs[:n_views], outs[n_views], outs[n_views + 1:]


def _attn_group(xg, w_qkv_bf16, tab, group, dil, moe_w):
    batch = xg.shape[0]
    cast_cols = moe_w.shape[-1]
    moe_w2 = moe_w.reshape(-1, cast_cols)
    cast_rows = moe_w2.shape[0] // (batch * (SEQ // ATTN_STEP_ROWS))
    cast_spec = pl.BlockSpec((cast_rows, cast_cols), lambda b, s: (b * (SEQ // ATTN_STEP_ROWS) + s, 0))
    per_class = SEQ // dil
    lc = min(ATTN_STEP_ROWS, per_class)
    n_cls = ATTN_STEP_ROWS // lc
    steps = SEQ // ATTN_STEP_ROWS
    carry = dil == 1
    if dil == 1:
        imap = lambda b, s: (b, s, 0)
    else:
        imap = lambda b, s: (b, 0, s)
    kern = functools.partial(_attn_group_kernel, n_cls=n_cls, lc=lc, carry=carry)
    o, lse, moe_w_bf16 = pl.pallas_call(
        kern,
        grid=(batch, steps),
        in_specs=[
            pl.BlockSpec((1, lc, n_cls * D_MODEL), imap),
            pl.BlockSpec((D_MODEL, 3 * ATTN_COLS), lambda b, s: (0, group), pipeline_mode=pl.Buffered(1)),
            pl.BlockSpec((ATTN_STEP_ROWS, 2 * LANES), lambda b, s: (s, 0)),
            cast_spec,
        ],
        out_specs=[
            pl.BlockSpec((1, lc, n_cls * ATTN_COLS), imap),
            pl.BlockSpec((1, lc, n_cls * LANES), imap),
            cast_spec,
        ],
        out_shape=[
            jax.ShapeDtypeStruct((batch, per_class, dil * ATTN_COLS), BF16),
            jax.ShapeDtypeStruct((batch, per_class, dil * LANES), F32),
            jax.ShapeDtypeStruct(moe_w2.shape, BF16),
        ],
        scratch_shapes=[
            pltpu.VMEM((ATTN_STEP_ROWS, ATTN_COLS), BF16),
            pltpu.VMEM((ATTN_BLOCK + ATTN_STEP_ROWS, ATTN_COLS), BF16),
            pltpu.VMEM((ATTN_BLOCK + ATTN_STEP_ROWS, ATTN_COLS), BF16),
        ],
        compiler_params=pltpu.CompilerParams(
            dimension_semantics=("parallel", "arbitrary"), vmem_limit_bytes=56 * MIB),
        name=f"attn_group{group}",
    )(xg, w_qkv_bf16, tab, moe_w2)
    return o, lse, moe_w_bf16.reshape(moe_w.shape)


def _rope_swap_matrix():
    half = ROT_DIM // 2
    src = np.arange(HEAD_DIM)
    src[half:ROT_DIM] = np.arange(ROPE_PAIR_SHIFT, ROPE_PAIR_SHIFT + half)
    src[ROPE_PAIR_SHIFT:ROPE_PAIR_SHIFT + half] = np.arange(half, ROT_DIM)
    swap = np.zeros((HEAD_DIM, HEAD_DIM), np.float32)
    swap[src, np.arange(HEAD_DIM)] = 1.0
    return swap


def _rope_table(dil):
    inv_freq = ROPE_THETA ** (-np.arange(0, ROT_DIM, 2, dtype=np.float64) / ROT_DIM)
    ang = np.arange(SEQ, dtype=np.float64)[:, None] * inv_freq[None, :]
    ang = np.concatenate([ang, ang], -1)
    cos, sin = np.cos(ang), np.sin(ang)
    half = ROT_DIM // 2
    gap = ROPE_PAIR_SHIFT - half
    tail = LANES - ROPE_PAIR_SHIFT - half
    cosf = np.concatenate([cos[:, :half], np.ones((SEQ, gap)), cos[:, half:], np.ones((SEQ, tail))], 1)
    sinr = np.concatenate([-sin[:, :half], np.zeros((SEQ, gap)), sin[:, half:], np.zeros((SEQ, tail))], 1)
    tab = np.concatenate([cosf, sinr], 1)
    tab = tab.reshape(SEQ // dil, dil, 2 * LANES).transpose(1, 0, 2).reshape(SEQ, 2 * LANES)
    return jnp.asarray(tab.astype(np.float32))


def _attn_out_kernel(o0_ref, o1_ref, o2_ref, l0_ref, l1_ref, l2_ref, pt1_ref, pt2_ref, x_ref, w_ref, g_ref, b_ref,
                     wr_ref, br_ref, out_ref, mi_ref, mf_ref, cnt_ref, l1_s, l2_s, carry_s):
    o_nat = [None]
    for o_ref, l_ref, pt_ref, l_s, dil in ((o1_ref, l1_ref, pt1_ref, l1_s, DILATIONS[1]),
                                          (o2_ref, l2_ref, pt2_ref, l2_s, DILATIONS[2])):
        width = ATTN_STEP_ROWS // dil
        o_cm = jnp.concatenate([o_ref[0, :, r * ATTN_COLS:(r + 1) * ATTN_COLS] for r in range(dil)], axis=0)
        o_nat.append(_dot(pt_ref[...], o_cm))
        for r in range(dil):
            l_s[pl.ds(r, width, stride=dil), :] = l_ref[0, :, r * LANES:(r + 1) * LANES]
    ls = [l0_ref[0], l1_s[...], l2_s[...]]
    mx = jnp.maximum(jnp.maximum(ls[0], ls[1]), ls[2])
    es = [jnp.exp(l - mx) for l in ls]
    den = es[0] + es[1] + es[2]
    ws = [e / den for e in es]
    parts = []
    for h in range(ATTN_HEADS):
        cs = slice(h * HEAD_DIM, (h + 1) * HEAD_DIM)
        acc = ws[0][:, h:h + 1] * o0_ref[0, :, cs].astype(F32)
        acc = acc + ws[1][:, h:h + 1] * o_nat[1][:, cs]
        acc = acc + ws[2][:, h:h + 1] * o_nat[2][:, cs]
        parts.append(acc)
    mixed_in = jnp.concatenate(parts, axis=1).astype(BF16)
    y = ALPHA * x_ref[...] + _dot(mixed_in, w_ref[...])
    out = _layer_norm_rows(y, g_ref[...], b_ref[...])
    out_ref[...] = out
    _route_tile(out, wr_ref, br_ref, mi_ref, mf_ref, cnt_ref, carry_s)


def _attn_out(outs, lses, x2, w_bf16, g, b, wr_pad, br_pad):
    t = x2.shape[0]
    steps = SEQ // ATTN_STEP_ROWS
    r_in, r_out, r_shape, r_scratch = _route_specs(t, ATTN_STEP_ROWS)
    view_map = lambda i: (i // steps, i % steps, 0)
    view_spec = lambda dil, width: pl.BlockSpec((1, ATTN_STEP_ROWS // dil, dil * width), view_map)
    const_spec = lambda shape: pl.BlockSpec(shape, lambda i: (0, 0))
    row_spec = pl.BlockSpec((ATTN_STEP_ROWS, D_MODEL), lambda i: (i, 0))
    perms_t = [jnp.asarray(_class_major_perm(d).T, BF16) for d in DILATIONS[1:]]
    return pl.pallas_call(
        _attn_out_kernel,
        grid=(t // ATTN_STEP_ROWS,),
        in_specs=[view_spec(d, ATTN_COLS) for d in DILATIONS] + [view_spec(d, LANES) for d in DILATIONS] + [
            const_spec((ATTN_STEP_ROWS, ATTN_STEP_ROWS)) for _ in perms_t] + [
            row_spec, pl.BlockSpec((ATTN_COLS, D_MODEL), lambda i: (0, 0), pipeline_mode=pl.Buffered(1)),
            const_spec((1, D_MODEL)), const_spec((1, D_MODEL))] + r_in,
        out_specs=[row_spec] + r_out,
        out_shape=[jax.ShapeDtypeStruct((t, D_MODEL), F32)] + r_shape,
        scratch_shapes=[pltpu.VMEM((ATTN_STEP_ROWS, LANES), F32), pltpu.VMEM((ATTN_STEP_ROWS, LANES), F32)] + r_scratch,
        compiler_params=pltpu.CompilerParams(dimension_semantics=("arbitrary",), vmem_limit_bytes=48 * MIB),
        name="attn_out",
    )(*outs, *lses, *perms_t, x2, w_bf16, g.reshape(1, D_MODEL), b.reshape(1, D_MODEL), wr_pad, br_pad)


def _mix_out_kernel(y_ref, x_ref, w_ref, g_ref, b_ref, wr_ref, br_ref, out_ref, mi_ref, mf_ref, cnt_ref, carry_s):
    y = ALPHA * x_ref[...] + _dot(y_ref[...], w_ref[...])
    out = _layer_norm_rows(y, g_ref[...], b_ref[...])
    out_ref[...] = out
    _route_tile(out, wr_ref, br_ref, mi_ref, mf_ref, cnt_ref, carry_s)


def _mix_out(mixer_y, x2, w_bf16, g, b, wr_pad, br_pad):
    t = x2.shape[0]
    k = w_bf16.shape[0]
    row_spec = lambda width: pl.BlockSpec((ROW_TILE, width), lambda i: (i, 0))
    const_spec = lambda shape: pl.BlockSpec(shape, lambda i: (0, 0))
    r_in, r_out, r_shape, r_scratch = _route_specs(t, ROW_TILE)
    return pl.pallas_call(
        _mix_out_kernel,
        grid=(t // ROW_TILE,),
        in_specs=[row_spec(k), row_spec(D_MODEL),
                  pl.BlockSpec((k, D_MODEL), lambda i: (0, 0), pipeline_mode=pl.Buffered(1)),
                  const_spec((1, D_MODEL)), const_spec((1, D_MODEL))] + r_in,
        out_specs=[row_spec(D_MODEL)] + r_out,
        out_shape=[jax.ShapeDtypeStruct((t, D_MODEL), F32)] + r_shape,
        scratch_shapes=r_scratch,
        compiler_params=pltpu.CompilerParams(dimension_semantics=("arbitrary",), vmem_limit_bytes=48 * MIB),
        name="mix_out",
    )(mixer_y, x2, w_bf16, g.reshape(1, D_MODEL), b.reshape(1, D_MODEL), wr_pad, br_pad)


def _route_tile(x, wr_ref, br_ref, mi_ref, mf_ref, cnt_ref, carry_s):
    tm = x.shape[0]

    @pl.when(pl.program_id(0) == 0)
    def _():
        carry_s[...] = jnp.zeros_like(carry_s)

    x_hi = x.astype(BF16)
    x_lo = (x - x_hi.astype(F32)).astype(BF16)
    hi_terms = _dot(x_hi, wr_ref[...])
    logits = (hi_terms[:, :LANES] + _dot(x_lo, wr_ref[:, :LANES])) + hi_terms[:, LANES:] + br_ref[...]
    lt = logits.T
    l = [lt[e:e + 1, :] for e in range(N_EXPERTS)]
    mx = l[0]
    for e in range(1, N_EXPERTS):
        mx = jnp.maximum(mx, l[e])
    ex = [jnp.exp(v - mx) for v in l]
    tot = ex[0]
    for e in range(1, N_EXPERTS):
        tot = tot + ex[e]
    p = [v / tot for v in ex]

    def first_index_of(vals, target):
        idx = jnp.full_like(target, len(vals) - 1).astype(I32)
        for k in range(len(vals) - 2, -1, -1):
            idx = jnp.where(vals[k] == target, k, idx)
        return idx

    best = None
    for g in range(N_EXPERT_GROUPS):
        pg = p[g * EXPERTS_PER_GROUP:(g + 1) * EXPERTS_PER_GROUP]
        top1 = jnp.maximum(jnp.maximum(pg[0], pg[1]), jnp.maximum(pg[2], pg[3]))
        i1 = first_index_of(pg, top1)
        rest = [jnp.where(i1 == k, -1.0, pg[k]) for k in range(EXPERTS_PER_GROUP)]
        top2 = jnp.maximum(jnp.maximum(rest[0], rest[1]), jnp.maximum(rest[2], rest[3]))
        i2 = first_index_of(rest, top2)
        score = top1 + top2
        if best is None:
            best = (score, jnp.zeros_like(i1), top1, top2, i1, i2)
        else:
            better = score > best[0]
            cand = (score, jnp.full_like(i1, g), top1, top2, i1, i2)
            best = tuple(jnp.where(better, cv, bv) for cv, bv in zip(cand, best))
    _, g_sel, p1, p2, i1, i2 = best
    psum = p1 + p2
    gate1, gate2 = p1 / psum, p2 / psum
    first_low = i1 < i2
    lo = jnp.where(first_low, i1, i2)
    hi = jnp.where(first_low, i2, i1)
    gate_lo = jnp.where(first_low, gate1, gate2)
    gate_hi = jnp.where(first_low, gate2, gate1)
    pair = jnp.where(lo == 0, 0, jnp.where(lo == 1, 3, 5)) + hi - lo - 1
    cls = g_sel * len(PAIRS) + pair

    n_rows = carry_s.shape[0]
    sub = lax.broadcasted_iota(I32, (n_rows, tm), 0)
    onehot = sub == cls
    oh = jnp.where(onehot, 1.0, 0.0)
    upper = (lax.broadcasted_iota(I32, (tm, tm), 0) <= lax.broadcasted_iota(I32, (tm, tm), 1))
    cum = _dot(oh.astype(BF16), jnp.where(upper, 1.0, 0.0).astype(BF16))
    carry = carry_s[:, 0:1]
    rank = jnp.sum(jnp.where(onehot, cum - 1.0 + carry, 0.0), axis=0, keepdims=True)
    carry_new = carry + jnp.sum(oh, axis=1, keepdims=True)
    carry_s[...] = jnp.broadcast_to(carry_new, carry_s.shape)
    cnt_ref[...] = jnp.broadcast_to(carry_new, cnt_ref.shape)

    sub8 = lax.broadcasted_iota(I32, (8, tm), 0)
    mi_ref[...] = jnp.where(sub8 == 0, cls, jnp.where(sub8 == 1, rank.astype(I32), 0))
    mf_ref[...] = jnp.where(sub8 == 0, gate_lo, jnp.where(sub8 == 1, gate_hi, 0.0))


def _route_specs(t, tm):
    in_specs = [pl.BlockSpec((D_MODEL, 2 * LANES), lambda i: (0, 0)), pl.BlockSpec((1, LANES), lambda i: (0, 0))]
    out_specs = [pl.BlockSpec((8, tm), lambda i: (0, i)), pl.BlockSpec((8, tm), lambda i: (0, i)),
                 pl.BlockSpec((32, LANES), lambda i: (0, 0))]
    out_shape = [jax.ShapeDtypeStruct((8, t), I32), jax.ShapeDtypeStruct((8, t), F32),
                 jax.ShapeDtypeStruct((32, LANES), F32)]
    return in_specs, out_specs, out_shape, [pltpu.VMEM((32, LANES), F32)]


def _expert_kernel(elo_ref, ehi_ref, valid_ref, src0_ref, src1_ref, src_ahead_ref, x_hbm, gs_ref,
                   wg0, wu0, wd0, wg1, wu1, wd1, out_ref, rows_s, sems):
    del elo_ref, ehi_ref
    i = pl.program_id(0)
    n_slots = MOE_GATHER_AHEAD + 1
    slot = i % n_slots

    def start_row(idx_ref, t, to_slot, priority):
        pltpu.make_async_copy(x_hbm.at[pl.ds(idx_ref[0, 0, t], 1)], rows_s.at[to_slot, pl.ds(t, 1)],
                              sems.at[to_slot]).start(priority=priority)

    def wait_rows(of_slot):
        pltpu.make_async_copy(x_hbm.at[pl.ds(0, MOE_BLOCK)], rows_s.at[of_slot], sems.at[of_slot]).wait()

    def prime(idx_ref, to_slot):
        def issue(t, c):
            start_row(idx_ref, t, to_slot, 0)
            return c

        lax.fori_loop(0, MOE_BLOCK, issue, 0, unroll=8)

    @pl.when(i == 0)
    def _():
        prime(src0_ref, 0)

    @pl.when(jnp.logical_and(i == 0, valid_ref[1] > 0))
    def _():
        prime(src1_ref, 1)

    def used_block(cur):
        wait_rows(cur)
        xb = rows_s[cur].astype(BF16)

        def ffn(wg, wu, wd):
            hg = _dot(xb, wg[0, 0])
            hu = _dot(xb, wu[0, 0])
            hidden = (hg * jax.nn.sigmoid(hg)) * hu
            return _dot(hidden.astype(BF16), wd[0, 0])

        gs = gs_ref[...]
        out_ref[...] = gs[:, 0:1] * ffn(wg0, wu0, wd0) + gs[:, 1:2] * ffn(wg1, wu1, wd1)

        @pl.when(valid_ref[i + MOE_GATHER_AHEAD] > 0)
        def _():
            for t in range(MOE_BLOCK):
                start_row(src_ahead_ref, t, (cur + MOE_GATHER_AHEAD) % n_slots, priority=t % N_DMA_QUEUES)

    for cur in range(n_slots):
        pl.when(jnp.logical_and(valid_ref[i] > 0, slot == cur))(functools.partial(used_block, cur))

    @pl.when(valid_ref[i] == 0)
    def _():
        out_ref[...] = jnp.zeros_like(out_ref)


def _experts(x2, src_rows, gates_sorted, blk_elo, blk_ehi, blk_valid, layer, wg, wu, wd):
    assert MOE_GATHER_AHEAD == 2
    nb = src_rows.shape[0] - MOE_GATHER_AHEAD
    n_rows = nb * MOE_BLOCK
    lo_map = lambda b, elo, ehi, valid: (layer, elo[b], 0, 0)
    hi_map = lambda b, elo, ehi, valid: (layer, ehi[b], 0, 0)
    row_map = lambda b, elo, ehi, valid: (b, 0)
    up_shape = (1, 1, D_MODEL, D_EXPERT)
    down_shape = (1, 1, D_EXPERT, D_MODEL)
    grid_spec = pltpu.PrefetchScalarGridSpec(
        num_scalar_prefetch=3,
        grid=(nb,),
        in_specs=[
            pl.BlockSpec((1, 1, MOE_BLOCK), lambda b, elo, ehi, valid: (0, 0, 0), memory_space=pltpu.SMEM),
            pl.BlockSpec((1, 1, MOE_BLOCK), lambda b, elo, ehi, valid: (1, 0, 0), memory_space=pltpu.SMEM),
            pl.BlockSpec((1, 1, MOE_BLOCK), lambda b, elo, ehi, valid: (b + MOE_GATHER_AHEAD, 0, 0),
                         memory_space=pltpu.SMEM),
            pl.BlockSpec(memory_space=pl.ANY),
            pl.BlockSpec((MOE_BLOCK, LANES), row_map),
            pl.BlockSpec(up_shape, lo_map), pl.BlockSpec(up_shape, lo_map), pl.BlockSpec(down_shape, lo_map),
            pl.BlockSpec(up_shape, hi_map), pl.BlockSpec(up_shape, hi_map), pl.BlockSpec(down_shape, hi_map),
        ],
        out_specs=pl.BlockSpec((MOE_BLOCK, D_MODEL), row_map),
        scratch_shapes=[pltpu.VMEM((MOE_GATHER_AHEAD + 1, MOE_BLOCK, D_MODEL), F32),
                        pltpu.SemaphoreType.DMA((MOE_GATHER_AHEAD + 1,))],
    )
    return pl.pallas_call(
        _expert_kernel,
        grid_spec=grid_spec,
        out_shape=jax.ShapeDtypeStruct((n_rows, D_MODEL), F32),
        compiler_params=pltpu.CompilerParams(dimension_semantics=("arbitrary",), vmem_limit_bytes=58 * MIB),
        name="moe_experts",
    )(blk_elo, blk_ehi, blk_valid, src_rows, src_rows, src_rows, x2, gates_sorted, wg, wu, wd, wg, wu, wd)


def _gather_ln_kernel(dest_ref, dest_next_ref, x_ref, src_hbm, g_ref, b_ref, y_ref, rows_s, sems, *, tm):
    i = pl.program_id(0)
    slot = i % 2

    def issue_rows(idx_ref, to_slot):
        for t in range(tm):
            pltpu.make_async_copy(src_hbm.at[pl.ds(idx_ref[0, 0, t], 1)], rows_s.at[to_slot, pl.ds(t, 1)],
                                  sems.at[to_slot]).start(priority=t % N_DMA_QUEUES)

    @pl.when(i == 0)
    def _():
        issue_rows(dest_ref, slot)

    @pl.when(i + 1 < pl.num_programs(0))
    def _():
        issue_rows(dest_next_ref, 1 - slot)

    pltpu.make_async_copy(src_hbm.at[pl.ds(0, tm)], rows_s.at[slot], sems.at[slot]).wait()
    y = ALPHA * x_ref[...] + rows_s[slot]
    y_ref[...] = _layer_norm_rows(y, g_ref[...], b_ref[...])


def _gather_ln(x2, expert_out, dest, g, b):
    t = x2.shape[0]
    tm = GATHER_TM
    n_tiles = t // tm
    dest3 = dest.reshape(n_tiles, 1, tm)
    return pl.pallas_call(
        functools.partial(_gather_ln_kernel, tm=tm),
        grid=(n_tiles,),
        in_specs=[
            pl.BlockSpec((1, 1, tm), lambda i: (i, 0, 0), memory_space=pltpu.SMEM),
            pl.BlockSpec((1, 1, tm), lambda i: (jnp.minimum(i + 1, n_tiles - 1), 0, 0), memory_space=pltpu.SMEM),
            pl.BlockSpec((tm, D_MODEL), lambda i: (i, 0)),
            pl.BlockSpec(memory_space=pl.ANY),
            pl.BlockSpec((1, D_MODEL), lambda i: (0, 0)),
            pl.BlockSpec((1, D_MODEL), lambda i: (0, 0)),
        ],
        out_specs=pl.BlockSpec((tm, D_MODEL), lambda i: (i, 0)),
        out_shape=jax.ShapeDtypeStruct((t, D_MODEL), F32),
        scratch_shapes=[pltpu.VMEM((2, tm, D_MODEL), F32), pltpu.SemaphoreType.DMA((2,))],
        compiler_params=pltpu.CompilerParams(dimension_semantics=("arbitrary",), vmem_limit_bytes=32 * MIB),
        name="moe_gather_ln",
    )(dest3, dest3, x2, expert_out, g.reshape(1, D_MODEL), b.reshape(1, D_MODEL))


_CLASS_LO = np.array([4 * (c // 6) + PAIRS[c % 6][0] for c in range(N_CLASSES)], np.int32)
_CLASS_HI = np.array([4 * (c // 6) + PAIRS[c % 6][1] for c in range(N_CLASSES)], np.int32)


def _moe_layer(x2, routing, layer, wg, wu, wd, ln_g, ln_b):
    t = x2.shape[0]
    nb = -(-(t + N_CLASSES * (MOE_BLOCK - 1)) // MOE_BLOCK)
    n_rows = nb * MOE_BLOCK
    meta_i, meta_f, cnt = routing
    cls, rank = meta_i[0], meta_i[1]
    counts = cnt[:N_CLASSES, 0].astype(I32)
    padded = (counts + MOE_BLOCK - 1) // MOE_BLOCK * MOE_BLOCK
    ends = jnp.cumsum(padded)
    starts = ends - padded
    dest = starts[cls] + rank
    blk_start = jnp.arange(nb, dtype=I32) * MOE_BLOCK
    nb_ext = nb + MOE_GATHER_AHEAD
    blk_valid = (jnp.arange(nb_ext, dtype=I32) * MOE_BLOCK < ends[-1]).astype(I32)
    n_valid = jnp.sum(blk_valid)
    blk_cls = jnp.minimum(jnp.sum((ends[None, :] <= blk_start[:, None]).astype(I32), axis=1), N_CLASSES - 1)
    blk_cls = blk_cls[jnp.minimum(jnp.arange(nb), n_valid - 1)]
    blk_elo = jnp.asarray(_CLASS_LO)[blk_cls]
    blk_ehi = jnp.asarray(_CLASS_HI)[blk_cls]
    tok_meta = jnp.concatenate([meta_f[:2].T, jnp.arange(t, dtype=F32)[:, None], jnp.zeros((t, LANES - 3), F32)], 1)
    sorted_meta = jnp.zeros((nb_ext * MOE_BLOCK, LANES), F32).at[dest].set(tok_meta)
    src_rows = sorted_meta[:, 2].astype(I32).reshape(nb_ext, 1, MOE_BLOCK)
    out = _experts(x2, src_rows, sorted_meta, blk_elo, blk_ehi, blk_valid, layer, wg, wu, wd)
    return _gather_ln(x2, out, dest, ln_g, ln_b)


def _inproj_kernel(x_ref, w_ref, wgate_ref, z_ref, gates_ref, xb_s):
    @pl.when(pl.program_id(1) == 0)
    def _():
        xb_s[...] = x_ref[...].astype(BF16)
        gates_ref[...] = _dot(xb_s[...], wgate_ref[...])

    z_ref[...] = _dot(xb_s[...], w_ref[...]).astype(z_ref.dtype)


def _inproj(x2, w_in_bf16, w_gate_bf16):
    t = x2.shape[0]
    tm, tn = INPROJ_TM, INPROJ_TN
    n_main = MLSTM_MAIN_COLS
    return pl.pallas_call(
        _inproj_kernel,
        grid=(t // tm, n_main // tn),
        in_specs=[
            pl.BlockSpec((tm, D_MODEL), lambda m, n: (m, 0)),
            pl.BlockSpec((D_MODEL, tn), lambda m, n: (0, n)),
            pl.BlockSpec((D_MODEL, LANES), lambda m, n: (0, 0)),
        ],
        out_specs=[
            pl.BlockSpec((tm, tn), lambda m, n: (m, n)),
            pl.BlockSpec((tm, LANES), lambda m, n: (m, 0)),
        ],
        out_shape=[
            jax.ShapeDtypeStruct((t, n_main), BF16),
            jax.ShapeDtypeStruct((t, LANES), F32),
        ],
        scratch_shapes=[pltpu.VMEM((tm, D_MODEL), BF16)],
        compiler_params=pltpu.CompilerParams(
            dimension_semantics=("parallel", "arbitrary"), vmem_limit_bytes=48 * MIB),
        name="mlstm_inproj",
    )(x2, w_in_bf16, w_gate_bf16)


def _mlstm_cell_kernel(zq_ref, zk_ref, v_ref, op_ref, gt_ref, bg_ref, cwq_ref, cbq_ref, cwk_ref, cbk_ref,
                       ng_ref, y_ref, q_s, k_s, gb_s, bc_s, gbt_s, bct_s, *, chunk, n_heads):
    hgroup = pl.program_id(1)
    seq = zq_ref.shape[0]

    def conv_silu(z_ref, cw_ref, cb_ref):
        z = z_ref[...].astype(F32)
        rowi = lax.broadcasted_iota(I32, z.shape, 0)
        out = cb_ref[...] + cw_ref[0:1, :] * jnp.where(rowi >= CONV_K - 1, pltpu.roll(z, CONV_K - 1, 0), 0.0)
        for j in range(1, CONV_K - 1):
            shift = CONV_K - 1 - j
            out = out + cw_ref[j:j + 1, :] * jnp.where(rowi >= shift, pltpu.roll(z, shift, 0), 0.0)
        out = out + cw_ref[CONV_K - 1:CONV_K, :] * z
        return out * jax.nn.sigmoid(out)

    q_s[...] = conv_silu(zq_ref, cwq_ref, cbq_ref).astype(BF16)
    k_s[...] = conv_silu(zk_ref, cwk_ref, cbk_ref) * (QK_DIM ** -0.5)

    lane = lax.broadcasted_iota(I32, (chunk, LANES), 1)
    sub = lax.broadcasted_iota(I32, (MLSTM_HEADS, chunk), 0)
    ti = lax.broadcasted_iota(I32, (chunk, chunk), 0)
    si = lax.broadcasted_iota(I32, (chunk, chunk), 1)
    causal = ti >= si
    ones_cols = jnp.ones((chunk, LANES), BF16)

    @pl.when(hgroup == 0)
    def _():
        tri = jnp.where(causal, 1.0, 0.0)
        bias = bg_ref[...]
        for c in range(seq // chunk):
            rs = slice(c * chunk, (c + 1) * chunk)
            gb = gt_ref[rs, :] + bias
            log_f = -(jnp.maximum(-gb, 0.0) + jnp.log1p(jnp.exp(-jnp.abs(gb))))
            bcum = _dot(tri, log_f, precision=HIGHEST)
            gb_s[rs, :] = gb
            bc_s[rs, :] = bcum
            gbt_s[:, rs] = gb.T
            bct_s[:, rs] = bcum.T

    def pick_col(a, idx):
        return jnp.sum(jnp.where(lane == idx, a, 0.0), axis=1, keepdims=True)

    def pick_row(a, idx):
        return jnp.sum(jnp.where(sub == idx, a, 0.0), axis=0, keepdims=True)

    heads = [hgroup * n_heads + i for i in range(n_heads)]
    qcols = [slice(i * QK_DIM, (i + 1) * QK_DIM) for i in range(n_heads)]
    vcols = [slice(i * V_DIM, (i + 1) * V_DIM) for i in range(n_heads)]
    hr = range(n_heads)
    c_state = [jnp.zeros((QK_DIM, V_DIM + LANES), F32) for _ in hr]
    m_state = [jnp.full((1, 1), NEG, F32) for _ in hr]
    for c in range(seq // chunk):
        rs = slice(c * chunk, (c + 1) * chunk)
        gb_c, bc_c = gb_s[rs, :], bc_s[rs, :]
        gbt_c, bct_c = gbt_s[0:MLSTM_HEADS, rs], bct_s[MLSTM_HEADS:2 * MLSTM_HEADS, rs]
        li_col = [pick_col(gb_c, hd) for hd in heads]
        bc_col = [pick_col(bc_c, hd + MLSTM_HEADS) for hd in heads]
        li_row = [pick_row(gbt_c, hd) for hd in heads]
        bc_row = [pick_row(bct_c, hd) for hd in heads]
        dmat = [jnp.where(causal, bc_col[i] + (li_row[i] - bc_row[i]), NEG) for i in hr]
        inter = [bc_col[i] + m_state[i] for i in hr]
        m_t = [jnp.maximum(inter[i], jnp.max(dmat[i], axis=1, keepdims=True)) for i in hr]
        qc = [q_s[rs, qcols[i]] for i in hr]
        kf = [k_s[rs, qcols[i]] for i in hr]
        vc = [v_ref[rs, vcols[i]] for i in hr]
        a = [_dot_nt(qc[i], kf[i].astype(BF16)) * jnp.exp(dmat[i] - m_t[i]) for i in hr]
        w_inter = [jnp.exp(inter[i] - m_t[i]) for i in hr]
        q_state = [_dot(qc[i], c_state[i].astype(BF16)) for i in hr]
        num = [_dot(a[i].astype(BF16), vc[i]) + w_inter[i] * q_state[i][:, :V_DIM] for i in hr]
        den = [jnp.sum(a[i], axis=1, keepdims=True) + w_inter[i] * q_state[i][:, V_DIM:V_DIM + 1] for i in hr]
        h_out = [num[i] / jnp.maximum(jnp.abs(den[i]), jnp.exp(-m_t[i])) for i in hr]
        mu = [jnp.mean(h_out[i], axis=1, keepdims=True) for i in hr]
        hc = [h_out[i] - mu[i] for i in hr]
        var = [jnp.mean(hc[i] * hc[i], axis=1, keepdims=True) for i in hr]
        for i in hr:
            hn = hc[i] * lax.rsqrt(var[i] + LN_EPS) * ng_ref[:, vcols[i]]
            y_ref[rs, vcols[i]] = (hn * jax.nn.sigmoid(op_ref[rs, vcols[i]].astype(F32))).astype(y_ref.dtype)
        b_last = [bc_col[i][chunk - 1:chunk, :] for i in hr]
        g = [b_last[i] - bc_col[i] + li_col[i] for i in hr]
        m_new = [jnp.maximum(b_last[i] + m_state[i], jnp.max(g[i], axis=0, keepdims=True)) for i in hr]
        wk = [jnp.exp(g[i] - m_new[i]) for i in hr]
        decay = [jnp.exp(b_last[i] + m_state[i] - m_new[i]) for i in hr]
        kw_t = [(wk[i] * kf[i]).T.astype(BF16) for i in hr]
        upd = [_dot(kw_t[i], jnp.concatenate([vc[i], ones_cols], axis=1)) for i in hr]
        c_state = [decay[i] * c_state[i] + upd[i] for i in hr]
        m_state = m_new


def _mlstm_cell(z, gates, b_gates_pad, conv_w, conv_b, norm_g, batch):
    t = z.shape[0]
    nh = MLSTM_HEADS_PER_STEP
    groups = MLSTM_HEADS // nh
    qw, vw = nh * QK_DIM, nh * V_DIM
    k_blk0 = (MLSTM_QK_COLS // 2) // qw
    v_blk0 = MLSTM_QK_COLS // vw
    o_blk0 = (MLSTM_QK_COLS + MLSTM_V_COLS) // vw
    return pl.pallas_call(
        functools.partial(_mlstm_cell_kernel, chunk=MLSTM_CHUNK, n_heads=nh),
        grid=(batch, groups),
        in_specs=[
            pl.BlockSpec((SEQ, qw), lambda b, h: (b, h)),
            pl.BlockSpec((SEQ, qw), lambda b, h: (b, k_blk0 + h)),
            pl.BlockSpec((SEQ, vw), lambda b, h: (b, v_blk0 + h)),
            pl.BlockSpec((SEQ, vw), lambda b, h: (b, o_blk0 + h)),
            pl.BlockSpec((SEQ, LANES), lambda b, h: (b, 0)),
            pl.BlockSpec((1, LANES), lambda b, h: (0, 0)),
            pl.BlockSpec((CONV_K, qw), lambda b, h: (0, h)),
            pl.BlockSpec((1, qw), lambda b, h: (0, h)),
            pl.BlockSpec((CONV_K, qw), lambda b, h: (0, k_blk0 + h)),
            pl.BlockSpec((1, qw), lambda b, h: (0, k_blk0 + h)),
            pl.BlockSpec((1, vw), lambda b, h: (0, h)),
        ],
        out_specs=pl.BlockSpec((SEQ, vw), lambda b, h: (b, h)),
        out_shape=jax.ShapeDtypeStruct((t, MLSTM_V_COLS), BF16),
        scratch_shapes=[pltpu.VMEM((SEQ, qw), BF16), pltpu.VMEM((SEQ, qw), F32),
                        pltpu.VMEM((SEQ, LANES), F32), pltpu.VMEM((SEQ, LANES), F32),
                        pltpu.VMEM((LANES, SEQ), F32), pltpu.VMEM((LANES, SEQ), F32)],
        compiler_params=pltpu.CompilerParams(
            dimension_semantics=("parallel", "arbitrary"), vmem_limit_bytes=56 * MIB),
        name="mlstm_cell",
    )(z, z, z, z, gates, b_gates_pad, conv_w, conv_b.reshape(1, -1), conv_w, conv_b.reshape(1, -1),
      norm_g.reshape(1, -1))


def kernel(x, attn_w_qkv, attn_w_o, mlstm_w_in, mlstm_b_gates, mlstm_conv_w, mlstm_conv_b, mlstm_norm_g,
           mlstm_w_out, ln_mix_g, ln_mix_b, ln_ffn_g, ln_ffn_b, router_w, router_b, moe_w_gate, moe_w_up,
           moe_w_down):
    batch, seq, d = x.shape
    assert (seq, d) == (SEQ, D_MODEL)
    t = batch * seq
    x2 = x.reshape(t, d)

    wr_f32 = jnp.zeros((D_MODEL, LANES), F32).at[:, :N_EXPERTS].set(router_w)
    wr_hi = lax.bitcast_convert_type(lax.bitcast_convert_type(wr_f32, jnp.uint32) & jnp.uint32(0xFFFF0000), F32)
    wr_pad = jnp.concatenate([wr_hi.astype(BF16), (wr_f32 - wr_hi).astype(BF16)], axis=1)
    br_pad = jnp.zeros((1, LANES), F32).at[0, :N_EXPERTS].set(router_b)

    w_in = mlstm_w_in[0]
    cm_views, w_qkv, (w_o, w_main, w_out) = _class_major_views(
        x, attn_w_qkv[0], (attn_w_o[0], w_in, mlstm_w_out[0]))
    views = [x] + list(cm_views)
    outs, lses, moe_bf = [], [], []
    for group, (dil, moe_w) in enumerate(zip(DILATIONS, (moe_w_gate, moe_w_up, moe_w_down))):
        o, lse, w_bf = _attn_group(views[group], w_qkv, _rope_table(dil), group, dil, moe_w)
        outs.append(o)
        lses.append(lse)
        moe_bf.append(w_bf)
    wg_bf, wu_bf, wd_bf = moe_bf
    x2, *routing = _attn_out(outs, lses, x2, w_o, ln_mix_g[0], ln_mix_b[0], wr_pad, br_pad)
    x2 = _moe_layer(x2, routing, 0, wg_bf, wu_bf, wd_bf, ln_ffn_g[0], ln_ffn_b[0])

    w_gate = jnp.zeros((D_MODEL, LANES), F32).at[:, :2 * MLSTM_HEADS].set(w_in[:, MLSTM_MAIN_COLS:]).astype(BF16)
    bg_pad = jnp.zeros((1, LANES), F32).at[0, :2 * MLSTM_HEADS].set(mlstm_b_gates[0])
    z, gates = _inproj(x2, w_main, w_gate)
    y = _mlstm_cell(z, gates, bg_pad, mlstm_conv_w[0], mlstm_conv_b[0], mlstm_norm_g[0], batch)
    x2, *routing = _mix_out(y, x2, w_out, ln_mix_g[1], ln_mix_b[1], wr_pad, br_pad)
    x2 = _moe_layer(x2, routing, 1, wg_bf, wu_bf, wd_bf, ln_ffn_g[1], ln_ffn_b[1])
    return x2.reshape(batch, seq, d)
```

```python
import functools

import numpy as np
import jax
import jax.numpy as jnp
from jax import lax
from jax.experimental import pallas as pl
from jax.experimental.pallas import tpu as pltpu

F32 = jnp.float32
BF16 = jnp.bfloat16
I32 = jnp.int32
HIGHEST = lax.Precision.HIGHEST

D_MODEL = 2048
SEQ = 2048
DEPTH = 2
DILATIONS = (1, 4, 16)
N_BACK = 128
ATTN_HEADS = 8
HEAD_DIM = 128
ROT_DIM = HEAD_DIM // 4
ROPE_THETA = 500000.0
ATTN_BLOCK = 128
ATTN_COLS = ATTN_HEADS * HEAD_DIM

MLSTM_HEADS = 8
QK_DIM = 128
V_DIM = D_MODEL // MLSTM_HEADS
CONV_K = 4
MLSTM_QK_COLS = 2 * MLSTM_HEADS * QK_DIM
MLSTM_V_COLS = MLSTM_HEADS * V_DIM
MLSTM_MAIN_COLS = MLSTM_QK_COLS + 2 * MLSTM_V_COLS

N_EXPERTS = 16
N_EXPERT_GROUPS = 4
EXPERTS_PER_GROUP = 4
D_EXPERT = 768
PAIRS = ((0, 1), (0, 2), (0, 3), (1, 2), (1, 3), (2, 3))
N_CLASSES = N_EXPERT_GROUPS * len(PAIRS)

ALPHA = (2 * DEPTH) ** 0.25
LN_EPS = 1e-5
NEG = -1e30

LANES = 128
N_DMA_QUEUES = 2
MIB = 1024 * 1024

ATTN_STEP_ROWS = 512
ATTN_HEAD_SET = 8
ROPE_PAIR_SHIFT = 64
ROW_TILE = 512
INPROJ_TM = 1024
INPROJ_TN = 1536
MLSTM_CHUNK = 128
MLSTM_HEADS_PER_STEP = 4
MOE_BLOCK = 256
MOE_GATHER_AHEAD = 2
GATHER_TM = 512


def _dot(a, b, **kw):
    return jnp.dot(a, b, preferred_element_type=F32, **kw)


def _dot_nt(a, b):
    return lax.dot_general(a, b, (((1,), (1,)), ((), ())), preferred_element_type=F32)


def _layer_norm_rows(y, g, b):
    mu = jnp.mean(y, axis=-1, keepdims=True)
    yc = y - mu
    var = jnp.mean(yc * yc, axis=-1, keepdims=True)
    return yc * lax.rsqrt(var + LN_EPS) * g + b


def _attn_group_kernel(x_ref, w_ref, tab_ref, cast_in_ref, o_ref, lse_ref, cast_out_ref, q_s, k_s, v_s,
                       *, n_cls, lc, carry):
    cast_out_ref[...] = cast_in_ref[...].astype(BF16)
    step = pl.program_id(1)
    rows_total = n_cls * lc
    blocks_per_class = lc // ATTN_BLOCK

    if n_cls == 1:
        xs = x_ref[0]
    else:
        xs = jnp.concatenate([x_ref[0, :, c * D_MODEL:(c + 1) * D_MODEL] for c in range(n_cls)], axis=0)
    qkv = _dot(xs.astype(BF16), w_ref[...])

    if carry:
        @pl.when(step == 0)
        def _():
            k_s[0:ATTN_BLOCK, :] = jnp.zeros((ATTN_BLOCK, ATTN_COLS), BF16)
            v_s[0:ATTN_BLOCK, :] = jnp.zeros((ATTN_BLOCK, ATTN_COLS), BF16)

        @pl.when(step > 0)
        def _():
            k_s[0:ATTN_BLOCK, :] = k_s[rows_total:rows_total + ATTN_BLOCK, :]
            v_s[0:ATTN_BLOCK, :] = v_s[rows_total:rows_total + ATTN_BLOCK, :]

    cosf = tab_ref[:, 0:LANES]
    sinr = tab_ref[:, LANES:2 * LANES]

    def rope(t):
        return t * cosf + pltpu.roll(t, ROPE_PAIR_SHIFT, 1) * sinr

    for h in range(ATTN_HEADS):
        cs = slice(h * HEAD_DIM, (h + 1) * HEAD_DIM)
        q_s[:, cs] = rope(qkv[:, h * HEAD_DIM:(h + 1) * HEAD_DIM]).astype(BF16)
        k_s[ATTN_BLOCK:, cs] = rope(qkv[:, ATTN_COLS + h * HEAD_DIM:ATTN_COLS + (h + 1) * HEAD_DIM]).astype(BF16)
    v_s[ATTN_BLOCK:, :] = qkv[:, 2 * ATTN_COLS:3 * ATTN_COLS].astype(BF16)

    row = lax.broadcasted_iota(I32, (ATTN_BLOCK, ATTN_BLOCK), 0)
    col = lax.broadcasted_iota(I32, (ATTN_BLOCK, ATTN_BLOCK), 1)
    row2 = lax.broadcasted_iota(I32, (ATTN_BLOCK, 2 * ATTN_BLOCK), 0)
    col2 = lax.broadcasted_iota(I32, (ATTN_BLOCK, 2 * ATTN_BLOCK), 1)
    lane = lax.broadcasted_iota(I32, (ATTN_BLOCK, LANES), 1)
    mask_cur = col <= row
    mask_both = jnp.logical_and(col2 >= row2, col2 <= row2 + N_BACK)
    if carry:
        mask_first = jnp.logical_and(mask_both, jnp.logical_or(col2 >= ATTN_BLOCK, step > 0))
    scale = HEAD_DIM ** -0.5

    for c in range(n_cls):
        for bi in range(blocks_per_class):
            j = c * blocks_per_class + bi
            ors = slice(bi * ATTN_BLOCK, (bi + 1) * ATTN_BLOCK)
            with_prev = bi > 0 or carry
            if with_prev:
                krows = slice(j * ATTN_BLOCK, (j + 2) * ATTN_BLOCK)
                mask = mask_first if bi == 0 else mask_both
            else:
                krows = slice((j + 1) * ATTN_BLOCK, (j + 2) * ATTN_BLOCK)
                mask = mask_cur
            lse_tile = jnp.zeros((ATTN_BLOCK, LANES), F32)
            for h0 in range(0, ATTN_HEADS, ATTN_HEAD_SET):
                heads = range(h0, h0 + ATTN_HEAD_SET)
                cols = [slice(h * HEAD_DIM, (h + 1) * HEAD_DIM) for h in heads]
                scores = [jnp.where(mask, _dot_nt(q_s[j * ATTN_BLOCK:(j + 1) * ATTN_BLOCK, cs], k_s[krows, cs])
                                    * scale, NEG) for cs in cols]
                maxes = [jnp.max(s, axis=1, keepdims=True) for s in scores]
                probs = [jnp.exp(s - m) for s, m in zip(scores, maxes)]
                dens = [jnp.sum(p, axis=1, keepdims=True) for p in probs]
                accs = [_dot(p.astype(BF16), v_s[krows, cs]) for p, cs in zip(probs, cols)]
                for h, acc, den, m in zip(heads, accs, dens, maxes):
                    o_ref[0, ors, c * ATTN_COLS + h * HEAD_DIM:c * ATTN_COLS + (h + 1) * HEAD_DIM] = (
                        acc / den).astype(o_ref.dtype)
                    lse_tile = jnp.where(lane == h, m + jnp.log(den), lse_tile)
            lse_ref[0, ors, c * LANES:(c + 1) * LANES] = lse_tile


def _class_major_perm(dil):
    width = ATTN_STEP_ROWS // dil
    perm = np.zeros((ATTN_STEP_ROWS, ATTN_STEP_ROWS), np.float32)
    for r in range(dil):
        for m in range(width):
            perm[r * width + m, m * dil + r] = 1.0
    return perm


def _class_major_kernel(x_ref, p1_ref, p2_ref, swap_ref, wqkv_ref, *refs):
    n_plain = (len(refs) - len(DILATIONS)) // 2
    plain_in = refs[:n_plain]
    view_out = refs[n_plain:n_plain + len(DILATIONS) - 1]
    wqkv_out = refs[n_plain + len(DILATIONS) - 1]
    plain_out = refs[n_plain + len(DILATIONS):]

    xb = x_ref[0].astype(BF16)
    for p_ref, out_ref, dil in zip((p1_ref, p2_ref), view_out, DILATIONS[1:]):
        width = ATTN_STEP_ROWS // dil
        rows = _dot(p_ref[...], xb).astype(BF16)
        for r in range(dil):
            out_ref[0, :, r * D_MODEL:(r + 1) * D_MODEL] = rows[r * width:(r + 1) * width, :]

    for src_ref, dst_ref in zip(plain_in, plain_out):
        dst_ref[...] = src_ref[...].astype(BF16)
    for head in range(len(DILATIONS) * 3 * ATTN_HEADS):
        cs = slice(head * HEAD_DIM, (head + 1) * HEAD_DIM)
        piece = wqkv_ref[:, cs].astype(BF16)
        if (head // ATTN_HEADS) % 3 < 2:
            piece = _dot(piece, swap_ref[...]).astype(BF16)
        wqkv_out[:, cs] = piece


def _class_major_views(x3, w_qkv, plain_weights):
    batch = x3.shape[0]
    n_steps = batch * (SEQ // ATTN_STEP_ROWS)
    step_map = lambda b, s: (b * (SEQ // ATTN_STEP_ROWS) + s, 0)
    const_map = lambda b, s: (0, 0)
    slab = lambda w: pl.BlockSpec((w.shape[0] // n_steps, w.shape[1]), step_map)
    perms = [jnp.asarray(_class_major_perm(d), BF16) for d in DILATIONS[1:]]
    swap = jnp.asarray(_rope_swap_matrix(), BF16)
    outs = pl.pallas_call(
        _class_major_kernel,
        grid=(batch, SEQ // ATTN_STEP_ROWS),
        in_specs=[pl.BlockSpec((1, ATTN_STEP_ROWS, D_MODEL), lambda b, s: (b, s, 0))] + [
            pl.BlockSpec((ATTN_STEP_ROWS, ATTN_STEP_ROWS), const_map) for _ in perms] + [
            pl.BlockSpec((HEAD_DIM, HEAD_DIM), const_map), slab(w_qkv)] + [slab(w) for w in plain_weights],
        out_specs=[pl.BlockSpec((1, ATTN_STEP_ROWS // d, d * D_MODEL), lambda b, s: (b, s, 0)) for d in DILATIONS[1:]]
        + [slab(w_qkv)] + [slab(w) for w in plain_weights],
        out_shape=[jax.ShapeDtypeStruct((batch, SEQ // d, d * D_MODEL), BF16) for d in DILATIONS[1:]]
        + [jax.ShapeDtypeStruct(w.shape, BF16) for w in [w_qkv] + list(plain_weights)],
        compiler_params=pltpu.CompilerParams(dimension_semantics=("parallel", "parallel"), vmem_limit_bytes=40 * MIB),
        name="class_major_views",
    )(x3, *perms, swap, w_qkv, *plain_weights)
    n_views = len(DILATIONS) - 1
    return outs[:n_views], outs[n_views], outs[n_views + 1:]


def _attn_group(xg, w_qkv_bf16, tab, group, dil, moe_w):
    batch = xg.shape[0]
    cast_cols = moe_w.shape[-1]
    moe_w2 = moe_w.reshape(-1, cast_cols)
    cast_rows = moe_w2.shape[0] // (batch * (SEQ // ATTN_STEP_ROWS))
    cast_spec = pl.BlockSpec((cast_rows, cast_cols), lambda b, s: (b * (SEQ // ATTN_STEP_ROWS) + s, 0))
    per_class = SEQ // dil
    lc = min(ATTN_STEP_ROWS, per_class)
    n_cls = ATTN_STEP_ROWS // lc
    steps = SEQ // ATTN_STEP_ROWS
    carry = dil == 1
    if dil == 1:
        imap = lambda b, s: (b, s, 0)
    else:
        imap = lambda b, s: (b, 0, s)
    kern = functools.partial(_attn_group_kernel, n_cls=n_cls, lc=lc, carry=carry)
    o, lse, moe_w_bf16 = pl.pallas_call(
        kern,
        grid=(batch, steps),
        in_specs=[
            pl.BlockSpec((1, lc, n_cls * D_MODEL), imap),
            pl.BlockSpec((D_MODEL, 3 * ATTN_COLS), lambda b, s: (0, group), pipeline_mode=pl.Buffered(1)),
            pl.BlockSpec((ATTN_STEP_ROWS, 2 * LANES), lambda b, s: (s, 0)),
            cast_spec,
        ],
        out_specs=[
            pl.BlockSpec((1, lc, n_cls * ATTN_COLS), imap),
            pl.BlockSpec((1, lc, n_cls * LANES), imap),
            cast_spec,
        ],
        out_shape=[
            jax.ShapeDtypeStruct((batch, per_class, dil * ATTN_COLS), BF16),
            jax.ShapeDtypeStruct((batch, per_class, dil * LANES), F32),
            jax.ShapeDtypeStruct(moe_w2.shape, BF16),
        ],
        scratch_shapes=[
            pltpu.VMEM((ATTN_STEP_ROWS, ATTN_COLS), BF16),
            pltpu.VMEM((ATTN_BLOCK + ATTN_STEP_ROWS, ATTN_COLS), BF16),
            pltpu.VMEM((ATTN_BLOCK + ATTN_STEP_ROWS, ATTN_COLS), BF16),
        ],
        compiler_params=pltpu.CompilerParams(
            dimension_semantics=("parallel", "arbitrary"), vmem_limit_bytes=56 * MIB),
        name=f"attn_group{group}",
    )(xg, w_qkv_bf16, tab, moe_w2)
    return o, lse, moe_w_bf16.reshape(moe_w.shape)


def _rope_swap_matrix():
    half = ROT_DIM // 2
    src = np.arange(HEAD_DIM)
    src[half:ROT_DIM] = np.arange(ROPE_PAIR_SHIFT, ROPE_PAIR_SHIFT + half)
    src[ROPE_PAIR_SHIFT:ROPE_PAIR_SHIFT + half] = np.arange(half, ROT_DIM)
    swap = np.zeros((HEAD_DIM, HEAD_DIM), np.float32)
    swap[src, np.arange(HEAD_DIM)] = 1.0
    return swap


def _rope_table(dil):
    inv_freq = ROPE_THETA ** (-np.arange(0, ROT_DIM, 2, dtype=np.float64) / ROT_DIM)
    ang = np.arange(SEQ, dtype=np.float64)[:, None] * inv_freq[None, :]
    ang = np.concatenate([ang, ang], -1)
    cos, sin = np.cos(ang), np.sin(ang)
    half = ROT_DIM // 2
    gap = ROPE_PAIR_SHIFT - half
    tail = LANES - ROPE_PAIR_SHIFT - half
    cosf = np.concatenate([cos[:, :half], np.ones((SEQ, gap)), cos[:, half:], np.ones((SEQ, tail))], 1)
    sinr = np.concatenate([-sin[:, :half], np.zeros((SEQ, gap)), sin[:, half:], np.zeros((SEQ, tail))], 1)
    tab = np.concatenate([cosf, sinr], 1)
    tab = tab.reshape(SEQ // dil, dil, 2 * LANES).transpose(1, 0, 2).reshape(SEQ, 2 * LANES)
    return jnp.asarray(tab.astype(np.float32))


def _attn_out_kernel(o0_ref, o1_ref, o2_ref, l0_ref, l1_ref, l2_ref, pt1_ref, pt2_ref, x_ref, w_ref, g_ref, b_ref,
                     wr_ref, br_ref, out_ref, mi_ref, mf_ref, cnt_ref, l1_s, l2_s, carry_s):
    o_nat = [None]
    for o_ref, l_ref, pt_ref, l_s, dil in ((o1_ref, l1_ref, pt1_ref, l1_s, DILATIONS[1]),
                                          (o2_ref, l2_ref, pt2_ref, l2_s, DILATIONS[2])):
        width = ATTN_STEP_ROWS // dil
        o_cm = jnp.concatenate([o_ref[0, :, r * ATTN_COLS:(r + 1) * ATTN_COLS] for r in range(dil)], axis=0)
        o_nat.append(_dot(pt_ref[...], o_cm))
        for r in range(dil):
            l_s[pl.ds(r, width, stride=dil), :] = l_ref[0, :, r * LANES:(r + 1) * LANES]
    ls = [l0_ref[0], l1_s[...], l2_s[...]]
    mx = jnp.maximum(jnp.maximum(ls[0], ls[1]), ls[2])
    es = [jnp.exp(l - mx) for l in ls]
    den = es[0] + es[1] + es[2]
    ws = [e / den for e in es]
    parts = []
    for h in range(ATTN_HEADS):
        cs = slice(h * HEAD_DIM, (h + 1) * HEAD_DIM)
        acc = ws[0][:, h:h + 1] * o0_ref[0, :, cs].astype(F32)
        acc = acc + ws[1][:, h:h + 1] * o_nat[1][:, cs]
        acc = acc + ws[2][:, h:h + 1] * o_nat[2][:, cs]
        parts.append(acc)
    mixed_in = jnp.concatenate(parts, axis=1).astype(BF16)
    y = ALPHA * x_ref[...] + _dot(mixed_in, w_ref[...])
    out = _layer_norm_rows(y, g_ref[...], b_ref[...])
    out_ref[...] = out
    _route_tile(out, wr_ref, br_ref, mi_ref, mf_ref, cnt_ref, carry_s)


def _attn_out(outs, lses, x2, w_bf16, g, b, wr_pad, br_pad):
    t = x2.shape[0]
    steps = SEQ // ATTN_STEP_ROWS
    r_in, r_out, r_shape, r_scratch = _route_specs(t, ATTN_STEP_ROWS)
    view_map = lambda i: (i // steps, i % steps, 0)
    view_spec = lambda dil, width: pl.BlockSpec((1, ATTN_STEP_ROWS // dil, dil * width), view_map)
    const_spec = lambda shape: pl.BlockSpec(shape, lambda i: (0, 0))
    row_spec = pl.BlockSpec((ATTN_STEP_ROWS, D_MODEL), lambda i: (i, 0))
    perms_t = [jnp.asarray(_class_major_perm(d).T, BF16) for d in DILATIONS[1:]]
    return pl.pallas_call(
        _attn_out_kernel,
        grid=(t // ATTN_STEP_ROWS,),
        in_specs=[view_spec(d, ATTN_COLS) for d in DILATIONS] + [view_spec(d, LANES) for d in DILATIONS] + [
            const_spec((ATTN_STEP_ROWS, ATTN_STEP_ROWS)) for _ in perms_t] + [
            row_spec, pl.BlockSpec((ATTN_COLS, D_MODEL), lambda i: (0, 0), pipeline_mode=pl.Buffered(1)),
            const_spec((1, D_MODEL)), const_spec((1, D_MODEL))] + r_in,
        out_specs=[row_spec] + r_out,
        out_shape=[jax.ShapeDtypeStruct((t, D_MODEL), F32)] + r_shape,
        scratch_shapes=[pltpu.VMEM((ATTN_STEP_ROWS, LANES), F32), pltpu.VMEM((ATTN_STEP_ROWS, LANES), F32)] + r_scratch,
        compiler_params=pltpu.CompilerParams(dimension_semantics=("arbitrary",), vmem_limit_bytes=48 * MIB),
        name="attn_out",
    )(*outs, *lses, *perms_t, x2, w_bf16, g.reshape(1, D_MODEL), b.reshape(1, D_MODEL), wr_pad, br_pad)


def _mix_out_kernel(y_ref, x_ref, w_ref, g_ref, b_ref, wr_ref, br_ref, out_ref, mi_ref, mf_ref, cnt_ref, carry_s):
    y = ALPHA * x_ref[...] + _dot(y_ref[...], w_ref[...])
    out = _layer_norm_rows(y, g_ref[...], b_ref[...])
    out_ref[...] = out
    _route_tile(out, wr_ref, br_ref, mi_ref, mf_ref, cnt_ref, carry_s)


def _mix_out(mixer_y, x2, w_bf16, g, b, wr_pad, br_pad):
    t = x2.shape[0]
    k = w_bf16.shape[0]
    row_spec = lambda width: pl.BlockSpec((ROW_TILE, width), lambda i: (i, 0))
    const_spec = lambda shape: pl.BlockSpec(shape, lambda i: (0, 0))
    r_in, r_out, r_shape, r_scratch = _route_specs(t, ROW_TILE)
    return pl.pallas_call(
        _mix_out_kernel,
        grid=(t // ROW_TILE,),
        in_specs=[row_spec(k), row_spec(D_MODEL),
                  pl.BlockSpec((k, D_MODEL), lambda i: (0, 0), pipeline_mode=pl.Buffered(1)),
                  const_spec((1, D_MODEL)), const_spec((1, D_MODEL))] + r_in,
        out_specs=[row_spec(D_MODEL)] + r_out,
        out_shape=[jax.ShapeDtypeStruct((t, D_MODEL), F32)] + r_shape,
        scratch_shapes=r_scratch,
        compiler_params=pltpu.CompilerParams(dimension_semantics=("arbitrary",), vmem_limit_bytes=48 * MIB),
        name="mix_out",
    )(mixer_y, x2, w_bf16, g.reshape(1, D_MODEL), b.reshape(1, D_MODEL), wr_pad, br_pad)


def _route_tile(x, wr_ref, br_ref, mi_ref, mf_ref, cnt_ref, carry_s):
    tm = x.shape[0]

    @pl.when(pl.program_id(0) == 0)
    def _():
        carry_s[...] = jnp.zeros_like(carry_s)

    x_hi = x.astype(BF16)
    x_lo = (x - x_hi.astype(F32)).astype(BF16)
    hi_terms = _dot(x_hi, wr_ref[...])
    logits = (hi_terms[:, :LANES] + _dot(x_lo, wr_ref[:, :LANES])) + hi_terms[:, LANES:] + br_ref[...]
    lt = logits.T
    l = [lt[e:e + 1, :] for e in range(N_EXPERTS)]
    mx = l[0]
    for e in range(1, N_EXPERTS):
        mx = jnp.maximum(mx, l[e])
    ex = [jnp.exp(v - mx) for v in l]
    tot = ex[0]
    for e in range(1, N_EXPERTS):
        tot = tot + ex[e]
    p = [v / tot for v in ex]

    def first_index_of(vals, target):
        idx = jnp.full_like(target, len(vals) - 1).astype(I32)
        for k in range(len(vals) - 2, -1, -1):
            idx = jnp.where(vals[k] == target, k, idx)
        return idx

    best = None
    for g in range(N_EXPERT_GROUPS):
        pg = p[g * EXPERTS_PER_GROUP:(g + 1) * EXPERTS_PER_GROUP]
        top1 = jnp.maximum(jnp.maximum(pg[0], pg[1]), jnp.maximum(pg[2], pg[3]))
        i1 = first_index_of(pg, top1)
        rest = [jnp.where(i1 == k, -1.0, pg[k]) for k in range(EXPERTS_PER_GROUP)]
        top2 = jnp.maximum(jnp.maximum(rest[0], rest[1]), jnp.maximum(rest[2], rest[3]))
        i2 = first_index_of(rest, top2)
        score = top1 + top2
        if best is None:
            best = (score, jnp.zeros_like(i1), top1, top2, i1, i2)
        else:
            better = score > best[0]
            cand = (score, jnp.full_like(i1, g), top1, top2, i1, i2)
            best = tuple(jnp.where(better, cv, bv) for cv, bv in zip(cand, best))
    _, g_sel, p1, p2, i1, i2 = best
    psum = p1 + p2
    gate1, gate2 = p1 / psum, p2 / psum
    first_low = i1 < i2
    lo = jnp.where(first_low, i1, i2)
    hi = jnp.where(first_low, i2, i1)
    gate_lo = jnp.where(first_low, gate1, gate2)
    gate_hi = jnp.where(first_low, gate2, gate1)
    pair = jnp.where(lo == 0, 0, jnp.where(lo == 1, 3, 5)) + hi - lo - 1
    cls = g_sel * len(PAIRS) + pair

    n_rows = carry_s.shape[0]
    sub = lax.broadcasted_iota(I32, (n_rows, tm), 0)
    onehot = sub == cls
    oh = jnp.where(onehot, 1.0, 0.0)
    upper = (lax.broadcasted_iota(I32, (tm, tm), 0) <= lax.broadcasted_iota(I32, (tm, tm), 1))
    cum = _dot(oh.astype(BF16), jnp.where(upper, 1.0, 0.0).astype(BF16))
    carry = carry_s[:, 0:1]
    rank = jnp.sum(jnp.where(onehot, cum - 1.0 + carry, 0.0), axis=0, keepdims=True)
    carry_new = carry + jnp.sum(oh, axis=1, keepdims=True)
    carry_s[...] = jnp.broadcast_to(carry_new, carry_s.shape)
    cnt_ref[...] = jnp.broadcast_to(carry_new, cnt_ref.shape)

    sub8 = lax.broadcasted_iota(I32, (8, tm), 0)
    mi_ref[...] = jnp.where(sub8 == 0, cls, jnp.where(sub8 == 1, rank.astype(I32), 0))
    mf_ref[...] = jnp.where(sub8 == 0, gate_lo, jnp.where(sub8 == 1, gate_hi, 0.0))


def _route_specs(t, tm):
    in_specs = [pl.BlockSpec((D_MODEL, 2 * LANES), lambda i: (0, 0)), pl.BlockSpec((1, LANES), lambda i: (0, 0))]
    out_specs = [pl.BlockSpec((8, tm), lambda i: (0, i)), pl.BlockSpec((8, tm), lambda i: (0, i)),
                 pl.BlockSpec((32, LANES), lambda i: (0, 0))]
    out_shape = [jax.ShapeDtypeStruct((8, t), I32), jax.ShapeDtypeStruct((8, t), F32),
                 jax.ShapeDtypeStruct((32, LANES), F32)]
    return in_specs, out_specs, out_shape, [pltpu.VMEM((32, LANES), F32)]


def _expert_kernel(elo_ref, ehi_ref, valid_ref, src0_ref, src1_ref, src_ahead_ref, x_hbm, gs_ref,
                   wg0, wu0, wd0, wg1, wu1, wd1, out_ref, rows_s, sems):
    del elo_ref, ehi_ref
    i = pl.program_id(0)
    n_slots = MOE_GATHER_AHEAD + 1
    slot = i % n_slots

    def start_row(idx_ref, t, to_slot, priority):
        pltpu.make_async_copy(x_hbm.at[pl.ds(idx_ref[0, 0, t], 1)], rows_s.at[to_slot, pl.ds(t, 1)],
                              sems.at[to_slot]).start(priority=priority)

    def wait_rows(of_slot):
        pltpu.make_async_copy(x_hbm.at[pl.ds(0, MOE_BLOCK)], rows_s.at[of_slot], sems.at[of_slot]).wait()

    def prime(idx_ref, to_slot):
        def issue(t, c):
            start_row(idx_ref, t, to_slot, 0)
            return c

        lax.fori_loop(0, MOE_BLOCK, issue, 0, unroll=8)

    @pl.when(i == 0)
    def _():
        prime(src0_ref, 0)

    @pl.when(jnp.logical_and(i == 0, valid_ref[1] > 0))
    def _():
        prime(src1_ref, 1)

    def used_block(cur):
        wait_rows(cur)
        xb = rows_s[cur].astype(BF16)

        def ffn(wg, wu, wd):
            hg = _dot(xb, wg[0, 0])
            hu = _dot(xb, wu[0, 0])
            hidden = (hg * jax.nn.sigmoid(hg)) * hu
            return _dot(hidden.astype(BF16), wd[0, 0])

        gs = gs_ref[...]
        out_ref[...] = gs[:, 0:1] * ffn(wg0, wu0, wd0) + gs[:, 1:2] * ffn(wg1, wu1, wd1)

        @pl.when(valid_ref[i + MOE_GATHER_AHEAD] > 0)
        def _():
            for t in range(MOE_BLOCK):
                start_row(src_ahead_ref, t, (cur + MOE_GATHER_AHEAD) % n_slots, priority=t % N_DMA_QUEUES)

    for cur in range(n_slots):
        pl.when(jnp.logical_and(valid_ref[i] > 0, slot == cur))(functools.partial(used_block, cur))

    @pl.when(valid_ref[i] == 0)
    def _():
        out_ref[...] = jnp.zeros_like(out_ref)


def _experts(x2, src_rows, gates_sorted, blk_elo, blk_ehi, blk_valid, layer, wg, wu, wd):
    assert MOE_GATHER_AHEAD == 2
    nb = src_rows.shape[0] - MOE_GATHER_AHEAD
    n_rows = nb * MOE_BLOCK
    lo_map = lambda b, elo, ehi, valid: (layer, elo[b], 0, 0)
    hi_map = lambda b, elo, ehi, valid: (layer, ehi[b], 0, 0)
    row_map = lambda b, elo, ehi, valid: (b, 0)
    up_shape = (1, 1, D_MODEL, D_EXPERT)
    down_shape = (1, 1, D_EXPERT, D_MODEL)
    grid_spec = pltpu.PrefetchScalarGridSpec(
        num_scalar_prefetch=3,
        grid=(nb,),
        in_specs=[
            pl.BlockSpec((1, 1, MOE_BLOCK), lambda b, elo, ehi, valid: (0, 0, 0), memory_space=pltpu.SMEM),
            pl.BlockSpec((1, 1, MOE_BLOCK), lambda b, elo, ehi, valid: (1, 0, 0), memory_space=pltpu.SMEM),
            pl.BlockSpec((1, 1, MOE_BLOCK), lambda b, elo, ehi, valid: (b + MOE_GATHER_AHEAD, 0, 0),
                         memory_space=pltpu.SMEM),
            pl.BlockSpec(memory_space=pl.ANY),
            pl.BlockSpec((MOE_BLOCK, LANES), row_map),
            pl.BlockSpec(up_shape, lo_map), pl.BlockSpec(up_shape, lo_map), pl.BlockSpec(down_shape, lo_map),
            pl.BlockSpec(up_shape, hi_map), pl.BlockSpec(up_shape, hi_map), pl.BlockSpec(down_shape, hi_map),
        ],
        out_specs=pl.BlockSpec((MOE_BLOCK, D_MODEL), row_map),
        scratch_shapes=[pltpu.VMEM((MOE_GATHER_AHEAD + 1, MOE_BLOCK, D_MODEL), F32),
                        pltpu.SemaphoreType.DMA((MOE_GATHER_AHEAD + 1,))],
    )
    return pl.pallas_call(
        _expert_kernel,
        grid_spec=grid_spec,
        out_shape=jax.ShapeDtypeStruct((n_rows, D_MODEL), F32),
        compiler_params=pltpu.CompilerParams(dimension_semantics=("arbitrary",), vmem_limit_bytes=58 * MIB),
        name="moe_experts",
    )(blk_elo, blk_ehi, blk_valid, src_rows, src_rows, src_rows, x2, gates_sorted, wg, wu, wd, wg, wu, wd)


def _gather_ln_kernel(dest_ref, dest_next_ref, x_ref, src_hbm, g_ref, b_ref, y_ref, rows_s, sems, *, tm):
    i = pl.program_id(0)
    slot = i % 2

    def issue_rows(idx_ref, to_slot):
        for t in range(tm):
            pltpu.make_async_copy(src_hbm.at[pl.ds(idx_ref[0, 0, t], 1)], rows_s.at[to_slot, pl.ds(t, 1)],
                                  sems.at[to_slot]).start(priority=t % N_DMA_QUEUES)

    @pl.when(i == 0)
    def _():
        issue_rows(dest_ref, slot)

    @pl.when(i + 1 < pl.num_programs(0))
    def _():
        issue_rows(dest_next_ref, 1 - slot)

    pltpu.make_async_copy(src_hbm.at[pl.ds(0, tm)], rows_s.at[slot], sems.at[slot]).wait()
    y = ALPHA * x_ref[...] + rows_s[slot]
    y_ref[...] = _layer_norm_rows(y, g_ref[...], b_ref[...])


def _gather_ln(x2, expert_out, dest, g, b):
    t = x2.shape[0]
    tm = GATHER_TM
    n_tiles = t // tm
    dest3 = dest.reshape(n_tiles, 1, tm)
    return pl.pallas_call(
        functools.partial(_gather_ln_kernel, tm=tm),
        grid=(n_tiles,),
        in_specs=[
            pl.BlockSpec((1, 1, tm), lambda i: (i, 0, 0), memory_space=pltpu.SMEM),
            pl.BlockSpec((1, 1, tm), lambda i: (jnp.minimum(i + 1, n_tiles - 1), 0, 0), memory_space=pltpu.SMEM),
            pl.BlockSpec((tm, D_MODEL), lambda i: (i, 0)),
            pl.BlockSpec(memory_space=pl.ANY),
            pl.BlockSpec((1, D_MODEL), lambda i: (0, 0)),
            pl.BlockSpec((1, D_MODEL), lambda i: (0, 0)),
        ],
        out_specs=pl.BlockSpec((tm, D_MODEL), lambda i: (i, 0)),
        out_shape=jax.ShapeDtypeStruct((t, D_MODEL), F32),
        scratch_shapes=[pltpu.VMEM((2, tm, D_MODEL), F32), pltpu.SemaphoreType.DMA((2,))],
        compiler_params=pltpu.CompilerParams(dimension_semantics=("arbitrary",), vmem_limit_bytes=32 * MIB),
        name="moe_gather_ln",
    )(dest3, dest3, x2, expert_out, g.reshape(1, D_MODEL), b.reshape(1, D_MODEL))


_CLASS_LO = np.array([4 * (c // 6) + PAIRS[c % 6][0] for c in range(N_CLASSES)], np.int32)
_CLASS_HI = np.array([4 * (c // 6) + PAIRS[c % 6][1] for c in range(N_CLASSES)], np.int32)


def _moe_layer(x2, routing, layer, wg, wu, wd, ln_g, ln_b):
    t = x2.shape[0]
    nb = -(-(t + N_CLASSES * (MOE_BLOCK - 1)) // MOE_BLOCK)
    n_rows = nb * MOE_BLOCK
    meta_i, meta_f, cnt = routing
    cls, rank = meta_i[0], meta_i[1]
    counts = cnt[:N_CLASSES, 0].astype(I32)
    padded = (counts + MOE_BLOCK - 1) // MOE_BLOCK * MOE_BLOCK
    ends = jnp.cumsum(padded)
    starts = ends - padded
    dest = starts[cls] + rank
    blk_start = jnp.arange(nb, dtype=I32) * MOE_BLOCK
    nb_ext = nb + MOE_GATHER_AHEAD
    blk_valid = (jnp.arange(nb_ext, dtype=I32) * MOE_BLOCK < ends[-1]).astype(I32)
    n_valid = jnp.sum(blk_valid)
    blk_cls = jnp.minimum(jnp.sum((ends[None, :] <= blk_start[:, None]).astype(I32), axis=1), N_CLASSES - 1)
    blk_cls = blk_cls[jnp.minimum(jnp.arange(nb), n_valid - 1)]
    blk_elo = jnp.asarray(_CLASS_LO)[blk_cls]
    blk_ehi = jnp.asarray(_CLASS_HI)[blk_cls]
    tok_meta = jnp.concatenate([meta_f[:2].T, jnp.arange(t, dtype=F32)[:, None], jnp.zeros((t, LANES - 3), F32)], 1)
    sorted_meta = jnp.zeros((nb_ext * MOE_BLOCK, LANES), F32).at[dest].set(tok_meta)
    src_rows = sorted_meta[:, 2].astype(I32).reshape(nb_ext, 1, MOE_BLOCK)
    out = _experts(x2, src_rows, sorted_meta, blk_elo, blk_ehi, blk_valid, layer, wg, wu, wd)
    return _gather_ln(x2, out, dest, ln_g, ln_b)


def _inproj_kernel(x_ref, w_ref, wgate_ref, z_ref, gates_ref, xb_s):
    @pl.when(pl.program_id(1) == 0)
    def _():
        xb_s[...] = x_ref[...].astype(BF16)
        gates_ref[...] = _dot(xb_s[...], wgate_ref[...])

    z_ref[...] = _dot(xb_s[...], w_ref[...]).astype(z_ref.dtype)


def _inproj(x2, w_in_bf16, w_gate_bf16):
    t = x2.shape[0]
    tm, tn = INPROJ_TM, INPROJ_TN
    n_main = MLSTM_MAIN_COLS
    return pl.pallas_call(
        _inproj_kernel,
        grid=(t // tm, n_main // tn),
        in_specs=[
            pl.BlockSpec((tm, D_MODEL), lambda m, n: (m, 0)),
            pl.BlockSpec((D_MODEL, tn), lambda m, n: (0, n)),
            pl.BlockSpec((D_MODEL, LANES), lambda m, n: (0, 0)),
        ],
        out_specs=[
            pl.BlockSpec((tm, tn), lambda m, n: (m, n)),
            pl.BlockSpec((tm, LANES), lambda m, n: (m, 0)),
        ],
        out_shape=[
            jax.ShapeDtypeStruct((t, n_main), BF16),
            jax.ShapeDtypeStruct((t, LANES), F32),
        ],
        scratch_shapes=[pltpu.VMEM((tm, D_MODEL), BF16)],
        compiler_params=pltpu.CompilerParams(
            dimension_semantics=("parallel", "arbitrary"), vmem_limit_bytes=48 * MIB),
        name="mlstm_inproj",
    )(x2, w_in_bf16, w_gate_bf16)


def _mlstm_cell_kernel(zq_ref, zk_ref, v_ref, op_ref, gt_ref, bg_ref, cwq_ref, cbq_ref, cwk_ref, cbk_ref,
                       ng_ref, y_ref, q_s, k_s, gb_s, bc_s, gbt_s, bct_s, *, chunk, n_heads):
    hgroup = pl.program_id(1)
    seq = zq_ref.shape[0]

    def conv_silu(z_ref, cw_ref, cb_ref):
        z = z_ref[...].astype(F32)
        rowi = lax.broadcasted_iota(I32, z.shape, 0)
        out = cb_ref[...] + cw_ref[0:1, :] * jnp.where(rowi >= CONV_K - 1, pltpu.roll(z, CONV_K - 1, 0), 0.0)
        for j in range(1, CONV_K - 1):
            shift = CONV_K - 1 - j
            out = out + cw_ref[j:j + 1, :] * jnp.where(rowi >= shift, pltpu.roll(z, shift, 0), 0.0)
        out = out + cw_ref[CONV_K - 1:CONV_K, :] * z
        return out * jax.nn.sigmoid(out)

    q_s[...] = conv_silu(zq_ref, cwq_ref, cbq_ref).astype(BF16)
    k_s[...] = conv_silu(zk_ref, cwk_ref, cbk_ref) * (QK_DIM ** -0.5)

    lane = lax.broadcasted_iota(I32, (chunk, LANES), 1)
    sub = lax.broadcasted_iota(I32, (MLSTM_HEADS, chunk), 0)
    ti = lax.broadcasted_iota(I32, (chunk, chunk), 0)
    si = lax.broadcasted_iota(I32, (chunk, chunk), 1)
    causal = ti >= si
    ones_cols = jnp.ones((chunk, LANES), BF16)

    @pl.when(hgroup == 0)
    def _():
        tri = jnp.where(causal, 1.0, 0.0)
        bias = bg_ref[...]
        for c in range(seq // chunk):
            rs = slice(c * chunk, (c + 1) * chunk)
            gb = gt_ref[rs, :] + bias
            log_f = -(jnp.maximum(-gb, 0.0) + jnp.log1p(jnp.exp(-jnp.abs(gb))))
            bcum = _dot(tri, log_f, precision=HIGHEST)
            gb_s[rs, :] = gb
            bc_s[rs, :] = bcum
            gbt_s[:, rs] = gb.T
            bct_s[:, rs] = bcum.T

    def pick_col(a, idx):
        return jnp.sum(jnp.where(lane == idx, a, 0.0), axis=1, keepdims=True)

    def pick_row(a, idx):
        return jnp.sum(jnp.where(sub == idx, a, 0.0), axis=0, keepdims=True)

    heads = [hgroup * n_heads + i for i in range(n_heads)]
    qcols = [slice(i * QK_DIM, (i + 1) * QK_DIM) for i in range(n_heads)]
    vcols = [slice(i * V_DIM, (i + 1) * V_DIM) for i in range(n_heads)]
    hr = range(n_heads)
    c_state = [jnp.zeros((QK_DIM, V_DIM + LANES), F32) for _ in hr]
    m_state = [jnp.full((1, 1), NEG, F32) for _ in hr]
    for c in range(seq // chunk):
        rs = slice(c * chunk, (c + 1) * chunk)
        gb_c, bc_c = gb_s[rs, :], bc_s[rs, :]
        gbt_c, bct_c = gbt_s[0:MLSTM_HEADS, rs], bct_s[MLSTM_HEADS:2 * MLSTM_HEADS, rs]
        li_col = [pick_col(gb_c, hd) for hd in heads]
        bc_col = [pick_col(bc_c, hd + MLSTM_HEADS) for hd in heads]
        li_row = [pick_row(gbt_c, hd) for hd in heads]
        bc_row = [pick_row(bct_c, hd) for hd in heads]
        dmat = [jnp.where(causal, bc_col[i] + (li_row[i] - bc_row[i]), NEG) for i in hr]
        inter = [bc_col[i] + m_state[i] for i in hr]
        m_t = [jnp.maximum(inter[i], jnp.max(dmat[i], axis=1, keepdims=True)) for i in hr]
        qc = [q_s[rs, qcols[i]] for i in hr]
        kf = [k_s[rs, qcols[i]] for i in hr]
        vc = [v_ref[rs, vcols[i]] for i in hr]
        a = [_dot_nt(qc[i], kf[i].astype(BF16)) * jnp.exp(dmat[i] - m_t[i]) for i in hr]
        w_inter = [jnp.exp(inter[i] - m_t[i]) for i in hr]
        q_state = [_dot(qc[i], c_state[i].astype(BF16)) for i in hr]
        num = [_dot(a[i].astype(BF16), vc[i]) + w_inter[i] * q_state[i][:, :V_DIM] for i in hr]
        den = [jnp.sum(a[i], axis=1, keepdims=True) + w_inter[i] * q_state[i][:, V_DIM:V_DIM + 1] for i in hr]
        h_out = [num[i] / jnp.maximum(jnp.abs(den[i]), jnp.exp(-m_t[i])) for i in hr]
        mu = [jnp.mean(h_out[i], axis=1, keepdims=True) for i in hr]
        hc = [h_out[i] - mu[i] for i in hr]
        var = [jnp.mean(hc[i] * hc[i], axis=1, keepdims=True) for i in hr]
        for i in hr:
            hn = hc[i] * lax.rsqrt(var[i] + LN_EPS) * ng_ref[:, vcols[i]]
            y_ref[rs, vcols[i]] = (hn * jax.nn.sigmoid(op_ref[rs, vcols[i]].astype(F32))).astype(y_ref.dtype)
        b_last = [bc_col[i][chunk - 1:chunk, :] for i in hr]
        g = [b_last[i] - bc_col[i] + li_col[i] for i in hr]
        m_new = [jnp.maximum(b_last[i] + m_state[i], jnp.max(g[i], axis=0, keepdims=True)) for i in hr]
        wk = [jnp.exp(g[i] - m_new[i]) for i in hr]
        decay = [jnp.exp(b_last[i] + m_state[i] - m_new[i]) for i in hr]
        kw_t = [(wk[i] * kf[i]).T.astype(BF16) for i in hr]
        upd = [_dot(kw_t[i], jnp.concatenate([vc[i], ones_cols], axis=1)) for i in hr]
        c_state = [decay[i] * c_state[i] + upd[i] for i in hr]
        m_state = m_new


def _mlstm_cell(z, gates, b_gates_pad, conv_w, conv_b, norm_g, batch):
    t = z.shape[0]
    nh = MLSTM_HEADS_PER_STEP
    groups = MLSTM_HEADS // nh
    qw, vw = nh * QK_DIM, nh * V_DIM
    k_blk0 = (MLSTM_QK_COLS // 2) // qw
    v_blk0 = MLSTM_QK_COLS // vw
    o_blk0 = (MLSTM_QK_COLS + MLSTM_V_COLS) // vw
    return pl.pallas_call(
        functools.partial(_mlstm_cell_kernel, chunk=MLSTM_CHUNK, n_heads=nh),
        grid=(batch, groups),
        in_specs=[
            pl.BlockSpec((SEQ, qw), lambda b, h: (b, h)),
            pl.BlockSpec((SEQ, qw), lambda b, h: (b, k_blk0 + h)),
            pl.BlockSpec((SEQ, vw), lambda b, h: (b, v_blk0 + h)),
            pl.BlockSpec((SEQ, vw), lambda b, h: (b, o_blk0 + h)),
            pl.BlockSpec((SEQ, LANES), lambda b, h: (b, 0)),
            pl.BlockSpec((1, LANES), lambda b, h: (0, 0)),
            pl.BlockSpec((CONV_K, qw), lambda b, h: (0, h)),
            pl.BlockSpec((1, qw), lambda b, h: (0, h)),
            pl.BlockSpec((CONV_K, qw), lambda b, h: (0, k_blk0 + h)),
            pl.BlockSpec((1, qw), lambda b, h: (0, k_blk0 + h)),
            pl.BlockSpec((1, vw), lambda b, h: (0, h)),
        ],
        out_specs=pl.BlockSpec((SEQ, vw), lambda b, h: (b, h)),
        out_shape=jax.ShapeDtypeStruct((t, MLSTM_V_COLS), BF16),
        scratch_shapes=[pltpu.VMEM((SEQ, qw), BF16), pltpu.VMEM((SEQ, qw), F32),
                        pltpu.VMEM((SEQ, LANES), F32), pltpu.VMEM((SEQ, LANES), F32),
                        pltpu.VMEM((LANES, SEQ), F32), pltpu.VMEM((LANES, SEQ), F32)],
        compiler_params=pltpu.CompilerParams(
            dimension_semantics=("parallel", "arbitrary"), vmem_limit_bytes=56 * MIB),
        name="mlstm_cell",
    )(z, z, z, z, gates, b_gates_pad, conv_w, conv_b.reshape(1, -1), conv_w, conv_b.reshape(1, -1),
      norm_g.reshape(1, -1))


def kernel(x, attn_w_qkv, attn_w_o, mlstm_w_in, mlstm_b_gates, mlstm_conv_w, mlstm_conv_b, mlstm_norm_g,
           mlstm_w_out, ln_mix_g, ln_mix_b, ln_ffn_g, ln_ffn_b, router_w, router_b, moe_w_gate, moe_w_up,
           moe_w_down):
    batch, seq, d = x.shape
    assert (seq, d) == (SEQ, D_MODEL)
    t = batch * seq
    x2 = x.reshape(t, d)

    wr_f32 = jnp.zeros((D_MODEL, LANES), F32).at[:, :N_EXPERTS].set(router_w)
    wr_hi = lax.bitcast_convert_type(lax.bitcast_convert_type(wr_f32, jnp.uint32) & jnp.uint32(0xFFFF0000), F32)
    wr_pad = jnp.concatenate([wr_hi.astype(BF16), (wr_f32 - wr_hi).astype(BF16)], axis=1)
    br_pad = jnp.zeros((1, LANES), F32).at[0, :N_EXPERTS].set(router_b)

    w_in = mlstm_w_in[0]
    cm_views, w_qkv, (w_o, w_main, w_out) = _class_major_views(
        x, attn_w_qkv[0], (attn_w_o[0], w_in, mlstm_w_out[0]))
    views = [x] + list(cm_views)
    outs, lses, moe_bf = [], [], []
    for group, (dil, moe_w) in enumerate(zip(DILATIONS, (moe_w_gate, moe_w_up, moe_w_down))):
        o, lse, w_bf = _attn_group(views[group], w_qkv, _rope_table(dil), group, dil, moe_w)
        outs.append(o)
        lses.append(lse)
        moe_bf.append(w_bf)
    wg_bf, wu_bf, wd_bf = moe_bf
    x2, *routing = _attn_out(outs, lses, x2, w_o, ln_mix_g[0], ln_mix_b[0], wr_pad, br_pad)
    x2 = _moe_layer(x2, routing, 0, wg_bf, wu_bf, wd_bf, ln_ffn_g[0], ln_ffn_b[0])

    w_gate = jnp.zeros((D_MODEL, LANES), F32).at[:, :2 * MLSTM_HEADS].set(w_in[:, MLSTM_MAIN_COLS:]).astype(BF16)
    bg_pad = jnp.zeros((1, LANES), F32).at[0, :2 * MLSTM_HEADS].set(mlstm_b_gates[0])
    z, gates = _inproj(x2, w_main, w_gate)
    y = _mlstm_cell(z, gates, bg_pad, mlstm_conv_w[0], mlstm_conv_b[0], mlstm_norm_g[0], batch)
    x2, *routing = _mix_out(y, x2, w_out, ln_mix_g[1], ln_mix_b[1], wr_pad, br_pad)
    x2 = _moe_layer(x2, routing, 1, wg_bf, wu_bf, wd_bf, ln_ffn_g[1], ln_ffn_b[1])
    return x2.reshape(batch, seq, d)
```
